```python
import jax
import jax.numpy as jnp
from jax import lax
import numpy as np

D_MODEL = 1024
BATCH = 8
SEQ = 2048
DEPTH = 2

GRID_W = 64
CTX_LEN = 256
HEAD_DIM = 64
H_ATT = 6
H_KV = 2
H_RET = 4
H_SSD = 6
SSD_GROUPS = 2
SSD_STATE = 128
SSD_CONV = 5
D_ATT = H_ATT * HEAD_DIM
D_KV = H_KV * HEAD_DIM
D_RET = H_RET * HEAD_DIM
D_SSD = H_SSD * HEAD_DIM
D_XBC = D_SSD + 2 * SSD_GROUPS * SSD_STATE
D_MIX = D_ATT + D_RET + D_SSD
D_FF = 4 * D_MODEL
IN_SPLITS = (D_ATT, D_KV, D_KV, D_RET, D_RET, D_RET, D_RET, D_SSD, D_XBC, 2 * H_SSD)
D_IN = sum(IN_SPLITS)
Q_BLOCK = 128
CHUNK = 128
ROPE_THETA = 10000.0
EPS = 1e-6
F32 = jnp.float32

kernel_name = 'hybrid_parallel_heads_dit_block'


def rmsnorm(x, g):
    xf = x.astype(F32)
    y = xf * lax.rsqrt(jnp.mean(xf * xf, axis=-1, keepdims=True) + EPS)
    return (y * g.astype(F32)).astype(x.dtype)


def grid_rope(n):
    rows = n // GRID_W
    row = jnp.broadcast_to(jnp.arange(rows)[:, None], (rows, GRID_W)).reshape(n)
    col = jnp.broadcast_to(jnp.arange(GRID_W)[None, :], (rows, GRID_W)).reshape(n)
    half = HEAD_DIM // 2
    inv_freq = ROPE_THETA ** (-jnp.arange(0, half, 2, dtype=F32) / half)
    ang = jnp.stack([row, col], axis=-1).astype(F32)[:, :, None] * inv_freq
    ang = jnp.concatenate([ang, ang], axis=-1)
    return jnp.cos(ang), jnp.sin(ang)


def apply_rope(x, cos, sin):
    b, n, h, d = x.shape
    xr = x.astype(F32).reshape(b, n, h, 2, d // 2)
    x1, x2 = jnp.split(xr, 2, axis=-1)
    rot = jnp.concatenate([-x2, x1], axis=-1)
    out = xr * cos[:, None] + rot * sin[:, None]
    return out.reshape(b, n, h, d).astype(x.dtype)


def depthwise_conv(x, w, b):
    y = lax.conv_general_dilated(x, w[:, None, :].astype(x.dtype), window_strides=(1,),
                                 padding=[(SSD_CONV // 2, SSD_CONV // 2)],
                                 dimension_numbers=('NWC', 'WIO', 'NWC'),
                                 feature_group_count=x.shape[-1])
    return y + b.astype(x.dtype)


def chunked_recurrence(q, k, v, log_a, s0, strict):
    b, h, l, n = q.shape
    p = v.shape[-1]
    nc = l // CHUNK
    qc = q.astype(F32).reshape(b, h, nc, CHUNK, n)
    kc = k.astype(F32).reshape(b, h, nc, CHUNK, n)
    vc = v.astype(F32).reshape(b, h, nc, CHUNK, p)
    cum = jnp.cumsum(log_a.astype(F32).reshape(b, h, nc, CHUNK), axis=-1)
    idx = jnp.arange(CHUNK)
    mask = (idx[:, None] > idx[None, :]) if strict else (idx[:, None] >= idx[None, :])
    decay = jnp.exp(jnp.where(mask, cum[..., :, None] - cum[..., None, :], -jnp.inf))
    scores = jnp.einsum('bhcin,bhcjn->bhcij', qc, kc) * decay
    y_intra = jnp.einsum('bhcij,bhcjp->bhcip', scores, vc)
    last = cum[..., -1:]
    d_state = jnp.einsum('bhcj,bhcjn,bhcjp->bhcnp', jnp.exp(last - cum), kc, vc)
    chunk_decay = jnp.exp(last[..., 0])

    def step(s, inp):
        dec, ds = inp
        return dec[..., None, None] * s + ds, s

    s_final, s_in = lax.scan(step, s0.astype(F32),
                             (jnp.moveaxis(chunk_decay, 2, 0), jnp.moveaxis(d_state, 2, 0)))
    s_in = jnp.moveaxis(s_in, 0, 2)
    y_inter = jnp.exp(cum)[..., None] * jnp.einsum('bhcin,bhcnp->bhcip', qc, s_in)
    return (y_intra + y_inter).reshape(b, h, l, p), s_final


def final_state(k, v, log_a):
    cum = jnp.cumsum(log_a.astype(F32), axis=-1)
    w = jnp.exp(cum[..., -1:] - cum)
    return jnp.einsum('bhl,bhln,bhlp->bhnp', w, k.astype(F32), v.astype(F32))


def bidirectional_recurrence(q, ks, v, las, qc, kcs, vc, lacs, need_ctx_out):
    bsz, h, _, n_state = q.shape
    p = v.shape[-1]
    y_dirs = []
    yc_dirs = []
    for direction in range(2):
        rev = direction == 1
        fl = (lambda t: jnp.flip(t, axis=2)) if rev else (lambda t: t)
        if need_ctx_out:
            yc_d, s_ctx = chunked_recurrence(fl(qc), fl(kcs[direction]), fl(vc), fl(lacs[direction]),
                                             jnp.zeros((bsz, h, n_state, p), F32), rev)
            yc_dirs.append(fl(yc_d))
        else:
            s_ctx = final_state(fl(kcs[direction]), fl(vc), fl(lacs[direction]))
        y_d, _ = chunked_recurrence(fl(q), fl(ks[direction]), fl(v), fl(las[direction]), s_ctx, rev)
        y_dirs.append(fl(y_d))
    y = (y_dirs[0] + y_dirs[1]).astype(v.dtype)
    if not need_ctx_out:
        return y, None
    return y, (yc_dirs[0] + yc_dirs[1]).astype(vc.dtype)


def sdpa(qh, keys, vals):
    b, lq = qh.shape[:2]
    qg = qh.reshape(b, lq, H_KV, H_ATT // H_KV, HEAD_DIM)
    s = jnp.einsum('bqkgd,bskd->bkgqs', qg, keys).astype(F32) * (HEAD_DIM ** -0.5)
    pr = jax.nn.softmax(s, axis=-1).astype(vals.dtype)
    o = jnp.einsum('bkgqs,bskd->bqkgd', pr, vals)
    return o.reshape(b, lq, D_ATT)


def attention_mixer(q, k, v, qc, kc, vc, qn_g, kn_g, cos, sin, need_ctx_out):
    b, n, _ = q.shape
    lc = kc.shape[1]
    q = apply_rope(rmsnorm(q.reshape(b, n, H_ATT, HEAD_DIM), qn_g), cos, sin)
    k = apply_rope(rmsnorm(k.reshape(b, n, H_KV, HEAD_DIM), kn_g), cos, sin)
    kc = rmsnorm(kc.reshape(b, lc, H_KV, HEAD_DIM), kn_g)
    vc = vc.reshape(b, lc, H_KV, HEAD_DIM)
    keys = jnp.concatenate([k, kc], axis=1)
    vals = jnp.concatenate([v.reshape(b, n, H_KV, HEAD_DIM), vc], axis=1)
    nb = n // Q_BLOCK
    q_blocks = jnp.moveaxis(q.reshape(b, nb, Q_BLOCK, H_ATT, HEAD_DIM), 1, 0)
    out = lax.map(lambda qb: sdpa(qb, keys, vals), q_blocks)
    out = jnp.moveaxis(out, 0, 1).reshape(b, n, D_ATT)
    if not need_ctx_out:
        return out, None
    out_c = sdpa(rmsnorm(qc.reshape(b, lc, H_ATT, HEAD_DIM), qn_g), kc, vc)
    return out, out_c


def head_groupnorm(y, gain, bias):
    mu = jnp.mean(y, axis=-1, keepdims=True)
    var = jnp.mean(jnp.square(y - mu), axis=-1, keepdims=True)
    yn = (y - mu) * lax.rsqrt(var + EPS)
    b, h, l, p = y.shape
    yn = jnp.transpose(yn, (0, 2, 1, 3)).reshape(b, l, h * p)
    return yn * gain.astype(F32) + bias.astype(F32)


def retention_mixer(q, k, v, g, qc, kc, vc, gc, decay_logit, gn_g, gn_b, cos, sin, need_ctx_out):
    b, n, _ = q.shape
    lc = qc.shape[1]
    heads = lambda t: t.reshape(t.shape[0], t.shape[1], H_RET, HEAD_DIM)
    bhld = lambda t: jnp.transpose(t, (0, 2, 1, 3))
    kscale = HEAD_DIM ** -0.5
    q = bhld(apply_rope(heads(q), cos, sin))
    k = bhld(apply_rope(heads(k), cos, sin)) * kscale
    v = bhld(heads(v))
    qc = bhld(heads(qc))
    kc = bhld(heads(kc)) * kscale
    vc = bhld(heads(vc))
    log_gamma = jax.nn.log_sigmoid(decay_logit.astype(F32))
    las = tuple(jnp.broadcast_to(log_gamma[d][None, :, None], (b, H_RET, n)) for d in range(2))
    lacs = tuple(jnp.broadcast_to(log_gamma[d][None, :, None], (b, H_RET, lc)) for d in range(2))
    y, yc = bidirectional_recurrence(q, (k, k), v, las, qc, (kc, kc), vc, lacs, need_ctx_out)
    out = (head_groupnorm(y.astype(F32), gn_g, gn_b) * jax.nn.silu(g.astype(F32))).astype(g.dtype)
    if not need_ctx_out:
        return out, None
    out_c = (head_groupnorm(yc.astype(F32), gn_g, gn_b) * jax.nn.silu(gc.astype(F32))).astype(gc.dtype)
    return out, out_c


def ssd_prep(xbc, dt_raw, conv_w, conv_b, dt_bias, a_log):
    b, l, _ = xbc.shape
    xbc = jax.nn.silu(depthwise_conv(xbc, conv_w, conv_b))
    xs, bm, cm = jnp.split(xbc, [D_SSD, D_SSD + SSD_GROUPS * SSD_STATE], axis=-1)
    rep = H_SSD // SSD_GROUPS
    xs = jnp.transpose(xs.reshape(b, l, H_SSD, HEAD_DIM), (0, 2, 1, 3))
    bm = jnp.repeat(jnp.transpose(bm.reshape(b, l, SSD_GROUPS, SSD_STATE), (0, 2, 1, 3)), rep, axis=1)
    cm = jnp.repeat(jnp.transpose(cm.reshape(b, l, SSD_GROUPS, SSD_STATE), (0, 2, 1, 3)), rep, axis=1)
    dt = jax.nn.softplus(dt_raw.astype(F32).reshape(b, l, 2, H_SSD) + dt_bias.astype(F32))
    dt = jnp.transpose(dt, (2, 0, 3, 1))
    a = -jnp.exp(a_log.astype(F32))
    las = tuple(dt[d] * a[d][None, :, None] for d in range(2))
    ks = tuple(bm.astype(F32) * dt[d][..., None] for d in range(2))
    return xs, cm, ks, las


def ssd_finish(y, xs, z, d_skip, norm_g):
    y = y + d_skip.astype(y.dtype)[None, :, None, None] * xs
    b, _, l, _ = y.shape
    y = jnp.transpose(y, (0, 2, 1, 3)).reshape(b, l, D_SSD)
    return rmsnorm(y * jax.nn.silu(z), norm_g)


def ssd_mixer(z, xbc, dt_raw, zc, xbcc, dtc_raw, conv_w, conv_b, dt_bias, a_log, d_skip, norm_g, need_ctx_out):
    xs, cm, ks, las = ssd_prep(xbc, dt_raw, conv_w, conv_b, dt_bias, a_log)
    xsc, cmc, ksc, lasc = ssd_prep(xbcc, dtc_raw, conv_w, conv_b, dt_bias, a_log)
    y, yc = bidirectional_recurrence(cm, ks, xs, las, cmc, ksc, xsc, lasc, need_ctx_out)
    out = ssd_finish(y, xs, z, d_skip, norm_g)
    if not need_ctx_out:
        return out, None
    return out, ssd_finish(yc, xsc, zc, d_skip, norm_g)


def squared_relu_mlp(h, w1, w2):
    return jnp.square(jax.nn.relu(h @ w1)) @ w2


def trunk_layer(x, xc, mod, mod_c, norm_g, w_in, w_out, qn_g, kn_g, ret_decay, ret_gn_g, ret_gn_b,
                conv_w, conv_b, dt_bias, a_log, d_skip, ssd_norm_g, w_ff1, w_ff2, cos, sin, need_ctx_out):
    sh1, sc1, g1, sh2, sc2, g2 = jnp.split(mod, 6, axis=-1)
    csh1, csc1, cg1, csh2, csc2, cg2 = jnp.split(mod_c, 6, axis=-1)
    split_idx = np.cumsum(IN_SPLITS)[:-1].tolist()
    h = rmsnorm(x, norm_g[0]) * (1 + sc1[:, None]) + sh1[:, None]
    hc = rmsnorm(xc, norm_g[0]) * (1 + csc1) + csh1
    qa, ka, va, qr, kr, vr, gr, z, xbc, dt = jnp.split(h @ w_in, split_idx, axis=-1)
    qac, kac, vac, qrc, krc, vrc, grc, zc, xbcc, dtc = jnp.split(hc @ w_in, split_idx, axis=-1)
    att, att_c = attention_mixer(qa, ka, va, qac, kac, vac, qn_g, kn_g, cos, sin, need_ctx_out)
    ret, ret_c = retention_mixer(qr, kr, vr, gr, qrc, krc, vrc, grc, ret_decay, ret_gn_g, ret_gn_b,
                                 cos, sin, need_ctx_out)
    ssd, ssd_c = ssd_mixer(z, xbc, dt, zc, xbcc, dtc, conv_w, conv_b, dt_bias, a_log, d_skip, ssd_norm_g,
                           need_ctx_out)
    o = jnp.concatenate([att, ret, ssd], axis=-1) @ w_out
    x = x + g1[:, None] * rmsnorm(o, norm_g[1])
    h2 = rmsnorm(x, norm_g[2]) * (1 + sc2[:, None]) + sh2[:, None]
    x = x + g2[:, None] * rmsnorm(squared_relu_mlp(h2, w_ff1, w_ff2), norm_g[3])
    if need_ctx_out:
        oc = jnp.concatenate([att_c, ret_c, ssd_c], axis=-1) @ w_out
        xc = xc + cg1 * rmsnorm(oc, norm_g[1])
        h2c = rmsnorm(xc, norm_g[2]) * (1 + csc2) + csh2
        xc = xc + cg2 * rmsnorm(squared_relu_mlp(h2c, w_ff1, w_ff2), norm_g[3])
    return x, xc


def setup_inputs(seed: int = 0) -> dict:
    key = jax.random.key(seed)
    ks = jax.random.split(key, 24)
    nrm = lambda k, shape, s: jax.random.normal(k, shape, F32) * s
    gamma = 1.0 - 2.0 ** (-5.0 - jnp.arange(H_RET, dtype=F32))
    ret_logit = jnp.log(gamma) - jnp.log1p(-gamma)
    dt0 = jnp.exp(jax.random.uniform(ks[15], (DEPTH, 2, H_SSD), F32, jnp.log(1e-3), jnp.log(1e-1)))
    return {
        'x': nrm(ks[0], (BATCH, SEQ, D_MODEL), 1.0),
        'c': nrm(ks[1], (BATCH, D_MODEL), 1.0),
        'ctx': nrm(ks[2], (BATCH, CTX_LEN, D_MODEL), 1.0),
        'c_ctx': nrm(ks[3], (D_MODEL,), 1.0),
        'w_mod': nrm(ks[4], (DEPTH, D_MODEL, 6 * D_MODEL), 0.5 * D_MODEL ** -0.5),
        'b_mod': nrm(ks[5], (DEPTH, 6 * D_MODEL), 0.01),
        'norm_g': 1.0 + nrm(ks[6], (DEPTH, 4, D_MODEL), 0.05),
        'w_in': nrm(ks[7], (DEPTH, D_MODEL, D_IN), D_MODEL ** -0.5),
        'w_out': nrm(ks[8], (DEPTH, D_MIX, D_MODEL), D_MIX ** -0.5),
        'q_norm_g': 1.0 + nrm(ks[9], (DEPTH, HEAD_DIM), 0.05),
        'k_norm_g': 1.0 + nrm(ks[10], (DEPTH, HEAD_DIM), 0.05),
        'ret_decay_logit': ret_logit + nrm(ks[11], (DEPTH, 2, H_RET), 0.05),
        'ret_gn_g': 1.0 + nrm(ks[12], (DEPTH, D_RET), 0.05),
        'ret_gn_b': nrm(ks[13], (DEPTH, D_RET), 0.01),
        'ssd_conv_w': nrm(ks[14], (DEPTH, SSD_CONV, D_XBC), SSD_CONV ** -0.5),
        'ssd_conv_b': nrm(ks[16], (DEPTH, D_XBC), 0.01),
        'ssd_dt_bias': dt0 + jnp.log(-jnp.expm1(-dt0)),
        'ssd_a_log': jnp.log(jax.random.uniform(ks[17], (DEPTH, 2, H_SSD), F32, 1.0, 16.0)),
        'ssd_d': 1.0 + nrm(ks[18], (DEPTH, H_SSD), 0.1),
        'ssd_norm_g': 1.0 + nrm(ks[19], (DEPTH, D_SSD), 0.05),
        'w_ff1': nrm(ks[20], (DEPTH, D_MODEL, D_FF), D_MODEL ** -0.5),
        'w_ff2': nrm(ks[21], (DEPTH, D_FF, D_MODEL), D_FF ** -0.5),
    }


def reference(x, c, ctx, c_ctx, w_mod, b_mod, norm_g, w_in, w_out, q_norm_g, k_norm_g, ret_decay_logit,
              ret_gn_g, ret_gn_b, ssd_conv_w, ssd_conv_b, ssd_dt_bias, ssd_a_log, ssd_d, ssd_norm_g,
              w_ff1, w_ff2):
    n = x.shape[1]
    cos, sin = grid_rope(n)
    sc = jax.nn.silu(c)
    scc = jax.nn.silu(c_ctx)
    xc = ctx
    for layer in range(DEPTH):
        mod = sc @ w_mod[layer] + b_mod[layer]
        mod_c = scc @ w_mod[layer] + b_mod[layer]
        x, xc = trunk_layer(x, xc, mod, mod_c, norm_g[layer], w_in[layer], w_out[layer],
                            q_norm_g[layer], k_norm_g[layer], ret_decay_logit[layer],
                            ret_gn_g[layer], ret_gn_b[layer], ssd_conv_w[layer], ssd_conv_b[layer],
                            ssd_dt_bias[layer], ssd_a_log[layer], ssd_d[layer], ssd_norm_g[layer],
                            w_ff1[layer], w_ff2[layer], cos, sin, layer < DEPTH - 1)
    return x
```

```python
import functools

import jax
import jax.numpy as jnp
from jax import lax
from jax.experimental import pallas as pl
from jax.experimental.pallas import tpu as pltpu

F32 = jnp.float32
BF16 = jnp.bfloat16

HEAD_DIM = 64
H_ATT = 6
H_KV = 2
H_RET = 4
H_SSD = 6
SSD_GROUPS = 2
SSD_STATE = 128
SSD_CONV = 5
GRID_W = 64
ROPE_THETA = 10000.0
EPS = 1e-6

D_ATT = H_ATT * HEAD_DIM
D_KV = H_KV * HEAD_DIM
D_RET = H_RET * HEAD_DIM
D_SSD = H_SSD * HEAD_DIM
D_BC = SSD_GROUPS * SSD_STATE
D_XBC = D_SSD + 2 * D_BC
LANES = 128
D_DT = LANES
OFF_Q = 0
OFF_K = OFF_Q + D_ATT
OFF_V = OFF_K + D_KV
OFF_RET = OFF_V + D_KV
OFF_Z = OFF_RET + 4 * D_RET
OFF_XBC = OFF_Z + D_SSD
OFF_DT = OFF_XBC + D_XBC
D_IN_PAD = OFF_DT + D_DT

ROW_TILE = 256
CHUNK = 256
VMEM_LIMIT = 56 * 1024 * 1024


def _silu(x):
    return x * jax.nn.sigmoid(x)


def _split2(a):
    hi = a.astype(BF16)
    lo = (a - hi.astype(F32)).astype(BF16)
    return hi, lo


def _split3(a):
    hi = a.astype(BF16)
    r = a - hi.astype(F32)
    mid = r.astype(BF16)
    lo = (r - mid.astype(F32)).astype(BF16)
    return hi, mid, lo


def _dot(a, b):
    return jnp.dot(a, b, preferred_element_type=F32)


def _dot_nt(a, b):
    return lax.dot_general(a, b, (((1,), (1,)), ((), ())), preferred_element_type=F32)


def _dot2_right(a, m):
    hi, lo = _split2(a)
    return _dot(hi, m) + _dot(lo, m)


def _dot3_right(a, m):
    hi, mid, lo = _split3(a)
    return _dot(hi, m) + _dot(mid, m) + _dot(lo, m)


def _dot3_left(m, a):
    hi, mid, lo = _split3(a)
    return _dot(m, hi) + _dot(m, mid) + _dot(m, lo)


def _rms(x, g):
    ms = jnp.mean(x * x, axis=-1, keepdims=True)
    return x * lax.rsqrt(ms + EPS) * g


def _head_avg_matrix(n):
    r = lax.broadcasted_iota(jnp.int32, (n, n), 0) // HEAD_DIM
    c = lax.broadcasted_iota(jnp.int32, (n, n), 1) // HEAD_DIM
    return jnp.where(r == c, 1.0 / HEAD_DIM, 0.0).astype(BF16)


def _mod_kernel(c_ref, w_ref, b_ref, o_ref):
    sc = _silu(c_ref[...]).astype(BF16)
    o_ref[0] = _dot(sc, w_ref[0].astype(BF16)) + b_ref[0]


def _modulation(cc, w_mod, b_mod):
    depth, d, n = w_mod.shape
    rows = cc.shape[0]
    tn = 1536
    return pl.pallas_call(
        _mod_kernel,
        grid=(depth, n // tn),
        in_specs=[
            pl.BlockSpec((rows, d), lambda l, j: (0, 0)),
            pl.BlockSpec((1, d, tn), lambda l, j: (l, 0, j)),
            pl.BlockSpec((1, 1, tn), lambda l, j: (l, 0, j)),
        ],
        out_specs=pl.BlockSpec((1, rows, tn), lambda l, j: (l, 0, j)),
        out_shape=jax.ShapeDtypeStruct((depth, rows, n), F32),
        compiler_params=pltpu.CompilerParams(
            dimension_semantics=("arbitrary", "arbitrary"), vmem_limit_bytes=VMEM_LIMIT),
        name="modulation",
    )(cc, w_mod, b_mod.reshape(depth, 1, n))


def _rope(x, cos, sin_p, sin_m):
    return x * cos + pltpu.roll(x, 16, 1) * sin_p + pltpu.roll(x, LANES - 16, 1) * sin_m


def _inproj_kernel(x_ref, mod_ref, ng_ref, w_ref, cos_ref, sp_ref, sm_ref, gqk_ref,
                   q_ref, k_ref, v_ref, ret_ref, z_ref, xbc_ref, dt_ref):
    x = x_ref[0]
    h = _rms(x, ng_ref[0:1, :])
    h = h * (1.0 + mod_ref[0, 1:2, :]) + mod_ref[0, 0:1, :]
    acc = _dot(h.astype(BF16), w_ref[...])

    cos = cos_ref[...]
    sin_p = sp_ref[...]
    sin_m = sm_ref[...]

    nqk = D_ATT + D_KV
    qk = acc[:, OFF_Q:OFF_Q + nqk]
    ms = _dot2_right(qk * qk, _head_avg_matrix(nqk))
    qk = qk * lax.rsqrt(ms + EPS) * gqk_ref[...]
    for i in range(D_ATT // LANES):
        sl = slice(i * LANES, (i + 1) * LANES)
        q_ref[0, :, sl] = _rope(qk[:, sl], cos, sin_p, sin_m).astype(BF16)
    k_ref[0] = _rope(qk[:, D_ATT:nqk], cos, sin_p, sin_m).astype(BF16)
    v_ref[0] = acc[:, OFF_V:OFF_V + D_KV].astype(BF16)

    kscale = HEAD_DIM ** -0.5
    for i in range(2 * D_RET // LANES):
        sl = slice(OFF_RET + i * LANES, OFF_RET + (i + 1) * LANES)
        r = _rope(acc[:, sl], cos, sin_p, sin_m)
        if i >= D_RET // LANES:
            r = r * kscale
        ret_ref[0, :, i * LANES:(i + 1) * LANES] = r.astype(BF16)
    ret_ref[0, :, 2 * D_RET:4 * D_RET] = acc[:, OFF_RET + 2 * D_RET:OFF_RET + 4 * D_RET].astype(BF16)
    z_ref[0] = acc[:, OFF_Z:OFF_Z + D_SSD].astype(BF16)
    xbc_ref[0] = acc[:, OFF_XBC:OFF_XBC + D_XBC].astype(BF16)
    dt_ref[0] = acc[:, OFF_DT:OFF_DT + D_DT]


def _inproj(xall, mods, ng, w_in, cos, sin_p, sin_m, gqk, n_ctx_tiles):
    bsz, t, d = xall.shape
    nt = t // ROW_TILE
    ctx_row = mods.shape[0] - 1

    def mod_map(b, i):
        return (jnp.where(i < n_ctx_tiles, ctx_row, b), 0, 0)

    tok = lambda w: pl.BlockSpec((1, ROW_TILE, w), lambda b, i: (b, i, 0))
    rope_spec = pl.BlockSpec((ROW_TILE, LANES), lambda b, i: (i, 0))
    const = lambda shape: pl.BlockSpec(shape, lambda b, i: (0,) * len(shape))
    widths = (D_ATT, D_KV, D_KV, 4 * D_RET, D_SSD, D_XBC)
    return pl.pallas_call(
        _inproj_kernel,
        grid=(bsz, nt),
        in_specs=[
            tok(d),
            pl.BlockSpec((1, 6, d), mod_map),
            const((4, d)),
            const((d, D_IN_PAD)),
            rope_spec, rope_spec, rope_spec,
            const((1, D_ATT + D_KV)),
        ],
        out_specs=[tok(w) for w in widths] + [tok(D_DT)],
        out_shape=[jax.ShapeDtypeStruct((bsz, t, w), BF16) for w in widths]
        + [jax.ShapeDtypeStruct((bsz, t, D_DT), F32)],
        compiler_params=pltpu.CompilerParams(
            dimension_semantics=("arbitrary", "arbitrary"), vmem_limit_bytes=VMEM_LIMIT),
        name="inproj",
    )(xall, mods, ng, w_in, cos, sin_p, sin_m, gqk)


def _attn_kernel(q_ref, k_ref, v_ref, o_ref, km_ref, vt_ref, *, n_ctx, with_ctx):
    qi = pl.program_id(1)

    @pl.when(qi == 0)
    def _():
        k = k_ref[0]
        lane = lax.broadcasted_iota(jnp.int32, k.shape, 1)
        zero = jnp.zeros_like(k)
        km_ref[0] = jnp.where(lane < HEAD_DIM, k, zero)
        km_ref[1] = jnp.where(lane >= HEAD_DIM, k, zero)
        vt_ref[...] = v_ref[0].astype(F32).T.astype(BF16)

    def attend(nk):
        for j in range(D_ATT // LANES):
            qj = q_ref[0, :, j * LANES:(j + 1) * LANES]
            halves = []
            for g in range(H_KV):
                s = _dot_nt(km_ref[g, 0:nk, :], qj)
                m = jnp.max(s, axis=0, keepdims=True)
                p = jnp.exp(s - m)
                l = jnp.sum(p, axis=0, keepdims=True)
                o = _dot(vt_ref[g * HEAD_DIM:(g + 1) * HEAD_DIM, 0:nk], p.astype(BF16))
                halves.append(o * (1.0 / l))
            ot = jnp.concatenate(halves, axis=0)
            o_ref[0, :, j * LANES:(j + 1) * LANES] = ot.T.astype(BF16)

    n_all = k_ref.shape[1]
    if with_ctx:
        @pl.when(qi == 0)
        def _():
            attend(n_ctx)

        @pl.when(qi > 0)
        def _():
            attend(n_all)
    else:
        attend(n_all)


def _attention(q, k, v, n_ctx, with_ctx):
    bsz, t, _ = q.shape
    tq = ROW_TILE
    assert n_ctx == tq
    first = 0 if with_ctx else n_ctx // tq
    nq = t // tq - first
    return pl.pallas_call(
        functools.partial(_attn_kernel, n_ctx=n_ctx, with_ctx=with_ctx),
        grid=(bsz, nq),
        in_specs=[
            pl.BlockSpec((1, tq, D_ATT), lambda b, i: (b, i + first, 0)),
            pl.BlockSpec((1, t, D_KV), lambda b, i: (b, 0, 0)),
            pl.BlockSpec((1, t, D_KV), lambda b, i: (b, 0, 0)),
        ],
        out_specs=pl.BlockSpec((1, tq, D_ATT), lambda b, i: (b, i + first, 0)),
        out_shape=jax.ShapeDtypeStruct((bsz, t, D_ATT), BF16),
        scratch_shapes=[pltpu.VMEM((H_KV, t, D_KV), BF16), pltpu.VMEM((D_KV, t), BF16)],
        compiler_params=pltpu.CompilerParams(
            dimension_semantics=("arbitrary", "arbitrary"), vmem_limit_bytes=VMEM_LIMIT),
        name="attention",
    )(q, k, v)


def _log_sigmoid(x):
    return jnp.minimum(x, 0.0) - jnp.log1p(jnp.exp(-jnp.abs(x)))


def _ret_kernel(r_ref, lg_ref, gng_ref, gnb_ref, o_ref, y_ref, sf_ref, sb_ref, *, n_ctx_chunks):
    c_len = CHUNK
    t = r_ref.shape[1]
    n_chunks = t // c_len
    lg = _log_sigmoid(lg_ref[...])
    lgf = lg[0:1, :]
    lgb = lg[1:2, :]
    tcol = lax.broadcasted_iota(jnp.int32, (c_len, 1), 0).astype(F32)
    wq_f = jnp.exp((tcol + 1.0) * lgf)
    wq_b = jnp.exp((c_len - tcol) * lgb)
    wk_f = jnp.exp((c_len - 1.0 - tcol) * lgf)
    wk_b = jnp.exp(tcol * lgb)
    dec_f = jnp.exp(c_len * lgf)
    dec_b = jnp.exp(c_len * lgb)

    ti = lax.broadcasted_iota(jnp.int32, (c_len, c_len), 0)
    si = lax.broadcasted_iota(jnp.int32, (c_len, c_len), 1)
    diff = (ti - si).astype(F32)
    lane = lax.broadcasted_iota(jnp.int32, (c_len, D_RET), 1)
    srow = lax.broadcasted_iota(jnp.int32, (D_RET, D_RET), 0) // HEAD_DIM
    scol = lax.broadcasted_iota(jnp.int32, (D_RET, D_RET), 1) // HEAD_DIM
    smask = srow == scol
    avg = _head_avg_matrix(D_RET)

    def chunk_rows(c):
        return pl.ds(pl.multiple_of(c * c_len, c_len), c_len)

    def state_delta(k, v, wk):
        kw = (k.astype(F32) * wk).T.astype(BF16)
        return jnp.where(smask, _dot(kw, v), 0.0)

    sf_ref[...] = jnp.zeros_like(sf_ref)
    sb_ref[...] = jnp.zeros_like(sb_ref)

    def fwd(c, carry):
        rows = chunk_rows(c)
        q = r_ref[0, rows, 0:D_RET]
        k = r_ref[0, rows, D_RET:2 * D_RET]
        v = r_ref[0, rows, 2 * D_RET:3 * D_RET]
        zero = jnp.zeros_like(q)
        y = wq_f * _dot(q, sf_ref[...].astype(BF16))
        for h in range(H_RET):
            hm = (lane >= h * HEAD_DIM) & (lane < (h + 1) * HEAD_DIM)
            lf = lgf[:, h * HEAD_DIM:h * HEAD_DIM + 1]
            lb = lgb[:, h * HEAD_DIM:h * HEAD_DIM + 1]
            dmat = jnp.exp(jnp.where(diff >= 0, diff * lf, -diff * lb))
            s = _dot_nt(jnp.where(hm, q, zero), k) * dmat
            y = y + _dot(s.astype(BF16), jnp.where(hm, v, zero))
        y_ref[rows, :] = y
        sf_ref[...] = dec_f * sf_ref[...] + state_delta(k, v, wk_f)
        return carry

    lax.fori_loop(0, n_chunks, fwd, 0)

    def bwd(c):
        rows = chunk_rows(c)
        q = r_ref[0, rows, 0:D_RET]
        k = r_ref[0, rows, D_RET:2 * D_RET]
        v = r_ref[0, rows, 2 * D_RET:3 * D_RET]
        g = r_ref[0, rows, 3 * D_RET:4 * D_RET].astype(F32)
        y = y_ref[rows, :] + wq_b * _dot(q, sb_ref[...].astype(BF16))
        sb_ref[...] = dec_b * sb_ref[...] + state_delta(k, v, wk_b)
        mu = _dot2_right(y, avg)
        d = y - mu
        var = _dot2_right(d * d, avg)
        yn = d * lax.rsqrt(var + EPS) * gng_ref[...] + gnb_ref[...]
        o_ref[0, rows, :] = (yn * _silu(g)).astype(BF16)

    def bwd_ctx(i, carry):
        bwd(n_ctx_chunks - 1 - i)
        return carry

    def bwd_lat(i, carry):
        bwd(n_chunks - 1 - i)
        return carry

    lax.fori_loop(0, n_ctx_chunks, bwd_ctx, 0)
    lax.fori_loop(0, n_chunks - n_ctx_chunks, bwd_lat, 0)


def _retention(ret, lg, gng, gnb, n_ctx):
    bsz, t, _ = ret.shape
    const = lambda shape: pl.BlockSpec(shape, lambda b: (0,) * len(shape))
    return pl.pallas_call(
        functools.partial(_ret_kernel, n_ctx_chunks=n_ctx // CHUNK),
        grid=(bsz,),
        in_specs=[
            pl.BlockSpec((1, t, 4 * D_RET), lambda b: (b, 0, 0)),
            const((2, D_RET)), const((1, D_RET)), const((1, D_RET)),
        ],
        out_specs=pl.BlockSpec((1, t, D_RET), lambda b: (b, 0, 0)),
        out_shape=jax.ShapeDtypeStruct((bsz, t, D_RET), BF16),
        scratch_shapes=[pltpu.VMEM((t, D_RET), F32), pltpu.VMEM((D_RET, D_RET), F32),
                        pltpu.VMEM((D_RET, D_RET), F32)],
        compiler_params=pltpu.CompilerParams(
            dimension_semantics=("arbitrary",), vmem_limit_bytes=VMEM_LIMIT),
        name="retention",
    )(ret, lg, gng, gnb)


def _softplus(x):
    return jnp.maximum(x, 0.0) + jnp.log1p(jnp.exp(-jnp.abs(x)))


def _ssd_kernel(z_ref, xbc_ref, dt_ref, cw_ref, cb_ref, dtb_ref, alog_ref, dskip_ref, ng_ref,
                o_ref, xpad_ref, xc_ref, y_ref, eb_ref, dsb_ref, sf_ref, sb_ref, *, n_ctx):
    c_len = CHUNK
    t = xbc_ref.shape[1]
    n_chunks = t // c_len
    n_ctx_chunks = n_ctx // c_len
    pad = 8
    half = SSD_CONV // 2

    zpad = jnp.zeros((pad, D_XBC), F32)
    xpad_ref[0:pad, :] = zpad
    xpad_ref[pad + n_ctx:2 * pad + n_ctx, :] = zpad
    xpad_ref[2 * pad + t:3 * pad + t, :] = zpad

    def fill(c, carry):
        src = pl.ds(pl.multiple_of(c * c_len, c_len), c_len)
        off = jnp.where(c < n_ctx_chunks, pad, 2 * pad)
        dst = pl.ds(pl.multiple_of(c * c_len + off, pad), c_len)
        xpad_ref[dst, :] = xbc_ref[0, src, :].astype(F32)
        return carry

    lax.fori_loop(0, n_chunks, fill, 0)

    cb = cb_ref[...]
    for c in range(n_chunks):
        base = c * c_len + (pad if c < n_ctx_chunks else 2 * pad)
        acc = jnp.broadcast_to(cb, (c_len, D_XBC))
        for j in range(SSD_CONV):
            acc = acc + cw_ref[j:j + 1, :] * xpad_ref[base + j - half:base + j - half + c_len, :]
        xc_ref[c * c_len:(c + 1) * c_len, :] = _silu(acc).astype(BF16)

    nd = 2 * H_SSD
    lane_dt = lax.broadcasted_iota(jnp.int32, (1, D_DT), 1)
    a_vec = jnp.where(lane_dt < nd, -jnp.exp(alog_ref[...]), 0.0)
    is_f = lane_dt < H_SSD
    ti = lax.broadcasted_iota(jnp.int32, (c_len, c_len), 0)
    si = lax.broadcasted_iota(jnp.int32, (c_len, c_len), 1)
    causal = si <= ti
    tri_l = jnp.where(causal, 1.0, 0.0).astype(BF16)
    tri_u = jnp.where(si >= ti, 1.0, 0.0).astype(BF16)
    er = lax.broadcasted_iota(jnp.int32, (D_DT, 2 * D_SSD), 0)
    ec = lax.broadcasted_iota(jnp.int32, (D_DT, 2 * D_SSD), 1) // HEAD_DIM
    expand = jnp.where(er == ec, 1.0, 0.0).astype(BF16)
    lane_x = lax.broadcasted_iota(jnp.int32, (c_len, D_SSD), 1)
    lane_c = lax.broadcasted_iota(jnp.int32, (c_len, D_BC), 1)
    srow = lax.broadcasted_iota(jnp.int32, (D_BC, D_SSD), 0) // SSD_STATE
    scol = lax.broadcasted_iota(jnp.int32, (D_BC, D_SSD), 1) // (D_SSD // SSD_GROUPS)
    smask = srow == scol
    heads_per_group = H_SSD // SSD_GROUPS

    def chunk_rows(c):
        return pl.ds(pl.multiple_of(c * c_len, c_len), c_len)

    sf_ref[...] = jnp.zeros_like(sf_ref)
    sb_ref[...] = jnp.zeros_like(sb_ref)

    def fwd(c, carry):
        rows = chunk_rows(c)
        xs = xc_ref[rows, 0:D_SSD]
        bm = xc_ref[rows, D_SSD:D_SSD + D_BC]
        cm = xc_ref[rows, D_SSD + D_BC:D_XBC]
        xs_f = xs.astype(F32)
        dt = _softplus(dt_ref[0, rows, :] + dtb_ref[...])
        la = dt * a_vec
        cum = jnp.where(is_f, _dot3_left(tri_l, la), _dot3_left(tri_u, la))
        cum_t = cum.T
        dt_t = dt.T
        edge = jnp.where(is_f, cum[c_len - 1:c_len, :], cum[0:1, :])
        wk = jnp.exp(edge - cum) * dt
        cum_e = _dot3_right(cum, expand)
        wk_e = _dot3_right(wk, expand)
        e_f = jnp.exp(cum_e[:, 0:D_SSD])
        e_b = jnp.exp(cum_e[:, D_SSD:2 * D_SSD])
        eb_ref[rows, :] = e_b

        y = e_f * _dot(cm, sf_ref[...].astype(BF16))
        zc = jnp.zeros_like(cm)
        zx = jnp.zeros_like(xs)
        for g in range(SSD_GROUPS):
            gm = (lane_c >= g * SSD_STATE) & (lane_c < (g + 1) * SSD_STATE)
            gmat = _dot_nt(jnp.where(gm, cm, zc), bm)
            for hh in range(heads_per_group):
                h = g * heads_per_group + hh
                d_f = cum[:, h:h + 1] - cum_t[h:h + 1, :]
                d_b = cum[:, H_SSD + h:H_SSD + h + 1] - cum_t[H_SSD + h:H_SSD + h + 1, :]
                w = jnp.where(causal, dt_t[h:h + 1, :], dt_t[H_SSD + h:H_SSD + h + 1, :])
                m = jnp.exp(jnp.where(causal, d_f, d_b)) * w
                hm = (lane_x >= h * HEAD_DIM) & (lane_x < (h + 1) * HEAD_DIM)
                y = y + _dot((gmat * m).astype(BF16), jnp.where(hm, xs, zx))
        y_ref[rows, :] = y

        bm_t = bm.astype(F32).T.astype(BF16)
        ds_f = _dot(bm_t, (xs_f * wk_e[:, 0:D_SSD]).astype(BF16))
        ds_b = _dot(bm_t, (xs_f * wk_e[:, D_SSD:2 * D_SSD]).astype(BF16))
        sf_ref[...] = e_f[c_len - 1:c_len, :] * sf_ref[...] + jnp.where(smask, ds_f, 0.0)
        dsb_ref[c] = jnp.where(smask, ds_b, 0.0)
        return carry

    lax.fori_loop(0, n_chunks, fwd, 0)

    def bwd(c):
        rows = chunk_rows(c)
        xs_f = xc_ref[rows, 0:D_SSD].astype(F32)
        cm = xc_ref[rows, D_SSD + D_BC:D_XBC]
        e_b = eb_ref[rows, :]
        y = y_ref[rows, :] + e_b * _dot(cm, sb_ref[...].astype(BF16))
        sb_ref[...] = e_b[0:1, :] * sb_ref[...] + dsb_ref[c]
        y = y + dskip_ref[...] * xs_f
        u = y * _silu(z_ref[0, rows, :].astype(F32))
        o_ref[0, rows, :] = _rms(u, ng_ref[...]).astype(BF16)

    def bwd_ctx(i, carry):
        bwd(n_ctx_chunks - 1 - i)
        return carry

    def bwd_lat(i, carry):
        bwd(n_chunks - 1 - i)
        return carry

    lax.fori_loop(0, n_ctx_chunks, bwd_ctx, 0)
    lax.fori_loop(0, n_chunks - n_ctx_chunks, bwd_lat, 0)


def _ssd(z, xbc, dt, conv_w, conv_b, dt_bias, a_log, d_skip, norm_g, n_ctx):
    bsz, t, _ = xbc.shape
    n_chunks = t // CHUNK
    const = lambda shape: pl.BlockSpec(shape, lambda b: (0,) * len(shape))
    seq = lambda w: pl.BlockSpec((1, t, w), lambda b: (b, 0, 0))
    return pl.pallas_call(
        functools.partial(_ssd_kernel, n_ctx=n_ctx),
        grid=(bsz,),
        in_specs=[seq(D_SSD), seq(D_XBC), seq(D_DT),
                  const((8, D_XBC)), const((1, D_XBC)), const((1, D_DT)), const((1, D_DT)),
                  const((1, D_SSD)), const((1, D_SSD))],
        out_specs=seq(D_SSD),
        out_shape=jax.ShapeDtypeStruct((bsz, t, D_SSD), BF16),
        scratch_shapes=[
            pltpu.VMEM((t + 24, D_XBC), F32),
            pltpu.VMEM((t, D_XBC), BF16),
            pltpu.VMEM((t, D_SSD), F32),
            pltpu.VMEM((t, D_SSD), F32),
            pltpu.VMEM((n_chunks, D_BC, D_SSD), F32),
            pltpu.VMEM((D_BC, D_SSD), F32),
            pltpu.VMEM((D_BC, D_SSD), F32),
        ],
        compiler_params=pltpu.CompilerParams(
            dimension_semantics=("arbitrary",), vmem_limit_bytes=VMEM_LIMIT),
        name="ssd",
    )(z, xbc, dt, conv_w, conv_b, dt_bias, a_log, d_skip, norm_g)


def _post_kernel(att_ref, ret_ref, ssd_ref, x_ref, mod_ref, ng_ref, wo_ref, w1_ref, w2_ref, o_ref,
                 *, ff_chunk):
    mix = jnp.concatenate([att_ref[0], ret_ref[0], ssd_ref[0]], axis=-1)
    o = _dot(mix, wo_ref[...])
    x1 = x_ref[0] + mod_ref[0, 2:3, :] * _rms(o, ng_ref[1:2, :])
    h2 = _rms(x1, ng_ref[2:3, :]) * (1.0 + mod_ref[0, 4:5, :]) + mod_ref[0, 3:4, :]
    h2 = h2.astype(BF16)
    d_ff = w1_ref.shape[1]
    acc = jnp.zeros(x1.shape, F32)
    for j in range(d_ff // ff_chunk):
        sl = slice(j * ff_chunk, (j + 1) * ff_chunk)
        a = jnp.maximum(_dot(h2, w1_ref[:, sl]), 0.0)
        acc = acc + _dot((a * a).astype(BF16), w2_ref[sl, :])
    o_ref[0] = x1 + mod_ref[0, 5:6, :] * _rms(acc, ng_ref[3:4, :])


def _post(att, ret, ssd, xall, mods, ng, w_out, w_ff1, w_ff2, n_ctx_tiles, with_ctx):
    bsz, t, d = xall.shape
    d_ff = w_ff1.shape[1]
    first = 0 if with_ctx else n_ctx_tiles
    nt = t // ROW_TILE - first
    ctx_row = mods.shape[0] - 1

    def mod_map(b, i):
        return (jnp.where(i + first < n_ctx_tiles, ctx_row, b), 0, 0)

    tok = lambda w: pl.BlockSpec((1, ROW_TILE, w), lambda b, i: (b, i + first, 0))
    const = lambda shape: pl.BlockSpec(shape, lambda b, i: (0,) * len(shape),
                                       pipeline_mode=pl.Buffered(1))
    return pl.pallas_call(
        functools.partial(_post_kernel, ff_chunk=1024),
        grid=(bsz, nt),
        in_specs=[tok(D_ATT), tok(D_RET), tok(D_SSD), tok(d),
                  pl.BlockSpec((1, 6, d), mod_map),
                  const((4, d)), const((d, d)), const((d, d_ff)), const((d_ff, d))],
        out_specs=pl.BlockSpec((1, ROW_TILE, d), lambda b, i: (b, i, 0)),
        out_shape=jax.ShapeDtypeStruct((bsz, nt * ROW_TILE, d), F32),
        compiler_params=pltpu.CompilerParams(
            dimension_semantics=("arbitrary", "arbitrary"), vmem_limit_bytes=VMEM_LIMIT),
        name="post",
    )(att, ret, ssd, xall, mods, ng, w_out, w_ff1, w_ff2)


def _rope_tables(n, n_ctx):
    rows = n // GRID_W
    row = jnp.broadcast_to(jnp.arange(rows)[:, None], (rows, GRID_W)).reshape(n)
    col = jnp.broadcast_to(jnp.arange(GRID_W)[None, :], (rows, GRID_W)).reshape(n)
    half = HEAD_DIM // 2
    inv_freq = ROPE_THETA ** (-jnp.arange(0, half, 2, dtype=F32) / half)
    ang = jnp.stack([row, col], axis=-1).astype(F32)[:, :, None] * inv_freq
    ang = jnp.concatenate([ang, ang], axis=-1).reshape(n, HEAD_DIM)
    ang = jnp.tile(ang, (1, LANES // HEAD_DIM))
    cos = jnp.cos(ang)
    sin = jnp.sin(ang)
    upper = (jnp.arange(LANES) % half) >= half // 2
    sin_p = jnp.where(upper, sin, 0.0)
    sin_m = jnp.where(upper, 0.0, -sin)
    ident = lambda v, a: jnp.concatenate([jnp.full((n_ctx, LANES), v, F32), a], axis=0)
    return ident(1.0, cos), ident(0.0, sin_p), ident(0.0, sin_m)


def _q_head_order():
    per = H_ATT // H_KV
    order = []
    for j in range(per):
        for g in range(H_KV):
            order.append(g * per + j)
    return order


def kernel(x, c, ctx, c_ctx, w_mod, b_mod, norm_g, w_in, w_out, q_norm_g, k_norm_g, ret_decay_logit,
           ret_gn_g, ret_gn_b, ssd_conv_w, ssd_conv_b, ssd_dt_bias, ssd_a_log, ssd_d, ssd_norm_g,
           w_ff1, w_ff2):
    bsz, n, d = x.shape
    n_ctx = ctx.shape[1]
    depth = w_mod.shape[0]
    assert n % ROW_TILE == 0 and n_ctx % ROW_TILE == 0 and n_ctx % CHUNK == 0 and n % CHUNK == 0
    n_ctx_tiles = n_ctx // ROW_TILE

    n_rows = -(-(bsz + 1) // 8) * 8
    cc = jnp.concatenate([c, c_ctx[None, :], jnp.zeros((n_rows - bsz - 1, d), F32)], axis=0)
    mods = _modulation(cc, w_mod, b_mod)[:, :bsz + 1].reshape(depth, bsz + 1, 6, d)

    cos, sin_p, sin_m = _rope_tables(n, n_ctx)
    order = _q_head_order()
    n_dt = 2 * H_SSD
    pad_dt = lambda a: jnp.pad(a.reshape(1, n_dt), ((0, 0), (0, D_DT - n_dt)))

    xall = jnp.concatenate([ctx, x], axis=1)
    for layer in range(depth):
        last = layer == depth - 1
        wi = w_in[layer]
        wi = jnp.concatenate(
            [wi[:, h * HEAD_DIM:(h + 1) * HEAD_DIM] for h in order]
            + [wi[:, D_ATT:], jnp.zeros((d, D_IN_PAD - wi.shape[1]), F32)], axis=1).astype(BF16)
        wo = w_out[layer]
        wo = jnp.concatenate(
            [wo[h * HEAD_DIM:(h + 1) * HEAD_DIM] for h in order] + [wo[D_ATT:]], axis=0).astype(BF16)
        gqk = jnp.concatenate([jnp.tile(q_norm_g[layer], H_ATT) * HEAD_DIM ** -0.5,
                               jnp.tile(k_norm_g[layer], H_KV)])[None, :]

        q, k, v, ret, z, xbc, dt = _inproj(xall, mods[layer], norm_g[layer], wi, cos, sin_p, sin_m,
                                           gqk, n_ctx_tiles)
        att = _attention(q, k, v, n_ctx, with_ctx=not last)
        lg = jnp.repeat(ret_decay_logit[layer], HEAD_DIM, axis=1)
        ret_o = _retention(ret, lg, ret_gn_g[layer][None, :], ret_gn_b[layer][None, :], n_ctx)
        conv_w = jnp.pad(ssd_conv_w[layer], ((0, 8 - SSD_CONV), (0, 0)))
        ssd_o = _ssd(z, xbc, dt, conv_w, ssd_conv_b[layer][None, :], pad_dt(ssd_dt_bias[layer]),
                     pad_dt(ssd_a_log[layer]), jnp.repeat(ssd_d[layer], HEAD_DIM)[None, :],
                     ssd_norm_g[layer][None, :], n_ctx)
        xall = _post(att, ret_o, ssd_o, xall, mods[layer], norm_g[layer], wo,
                     w_ff1[layer].astype(BF16), w_ff2[layer].astype(BF16), n_ctx_tiles,
                     with_ctx=not last)
    return xall
```

```python
import functools

import jax
import jax.numpy as jnp
from jax import lax
from jax.experimental import pallas as pl
from jax.experimental.pallas import tpu as pltpu

F32 = jnp.float32
BF16 = jnp.bfloat16

HEAD_DIM = 64
H_ATT = 6
H_KV = 2
H_RET = 4
H_SSD = 6
SSD_GROUPS = 2
SSD_STATE = 128
SSD_CONV = 5
GRID_W = 64
ROPE_THETA = 10000.0
EPS = 1e-6

D_ATT = H_ATT * HEAD_DIM
D_KV = H_KV * HEAD_DIM
D_RET = H_RET * HEAD_DIM
D_SSD = H_SSD * HEAD_DIM
D_BC = SSD_GROUPS * SSD_STATE
D_XBC = D_SSD + 2 * D_BC
LANES = 128
D_DT = LANES
OFF_Q = 0
OFF_K = OFF_Q + D_ATT
OFF_V = OFF_K + D_KV
OFF_RET = OFF_V + D_KV
OFF_Z = OFF_RET + 4 * D_RET
OFF_XBC = OFF_Z + D_SSD
OFF_DT = OFF_XBC + D_XBC
D_IN_PAD = OFF_DT + D_DT

ROW_TILE = 256
CHUNK = 256
VMEM_LIMIT = 56 * 1024 * 1024


def _silu(x):
    return x * jax.nn.sigmoid(x)


def _split2(a):
    hi = a.astype(BF16)
    lo = (a - hi.astype(F32)).astype(BF16)
    return hi, lo


def _split3(a):
    hi = a.astype(BF16)
    r = a - hi.astype(F32)
    mid = r.astype(BF16)
    lo = (r - mid.astype(F32)).astype(BF16)
    return hi, mid, lo


def _dot(a, b):
    return jnp.dot(a, b, preferred_element_type=F32)


def _dot_nt(a, b):
    return lax.dot_general(a, b, (((1,), (1,)), ((), ())), preferred_element_type=F32)


def _dot2_right(a, m):
    hi, lo = _split2(a)
    return _dot(hi, m) + _dot(lo, m)


def _dot3_right(a, m):
    hi, mid, lo = _split3(a)
    return _dot(hi, m) + _dot(mid, m) + _dot(lo, m)


def _dot3_left(m, a):
    hi, mid, lo = _split3(a)
    return _dot(m, hi) + _dot(m, mid) + _dot(m, lo)


def _rms(x, g):
    ms = jnp.mean(x * x, axis=-1, keepdims=True)
    return x * lax.rsqrt(ms + EPS) * g


def _head_avg_matrix(n):
    r = lax.broadcasted_iota(jnp.int32, (n, n), 0) // HEAD_DIM
    c = lax.broadcasted_iota(jnp.int32, (n, n), 1) // HEAD_DIM
    return jnp.where(r == c, 1.0 / HEAD_DIM, 0.0).astype(BF16)


def _mod_kernel(c_ref, w_ref, b_ref, o_ref):
    sc = _silu(c_ref[...]).astype(BF16)
    o_ref[0] = _dot(sc, w_ref[0].astype(BF16)) + b_ref[0]


def _modulation(cc, w_mod, b_mod):
    depth, d, n = w_mod.shape
    rows = cc.shape[0]
    tn = 1536
    return pl.pallas_call(
        _mod_kernel,
        grid=(depth, n // tn),
        in_specs=[
            pl.BlockSpec((rows, d), lambda l, j: (0, 0)),
            pl.BlockSpec((1, d, tn), lambda l, j: (l, 0, j)),
            pl.BlockSpec((1, 1, tn), lambda l, j: (l, 0, j)),
        ],
        out_specs=pl.BlockSpec((1, rows, tn), lambda l, j: (l, 0, j)),
        out_shape=jax.ShapeDtypeStruct((depth, rows, n), F32),
        compiler_params=pltpu.CompilerParams(
            dimension_semantics=("arbitrary", "arbitrary"), vmem_limit_bytes=VMEM_LIMIT),
        name="modulation",
    )(cc, w_mod, b_mod.reshape(depth, 1, n))


def _rope(x, cos, sin_p, sin_m):
    return x * cos + pltpu.roll(x, 16, 1) * sin_p + pltpu.roll(x, LANES - 16, 1) * sin_m


def _inproj_kernel(xc_ref, xl_ref, mod_ref, ng_ref, w_ref, cos_ref, sp_ref, sm_ref, gqk_ref,
                   q_ref, k_ref, v_ref, ret_ref, z_ref, xbc_ref, dt_ref, *, n_ctx_tiles):
    x = jnp.where(pl.program_id(1) < n_ctx_tiles, xc_ref[0], xl_ref[0])
    h = _rms(x, ng_ref[0:1, :])
    h = h * (1.0 + mod_ref[0, 1:2, :]) + mod_ref[0, 0:1, :]
    acc = _dot(h.astype(BF16), w_ref[...])

    cos = cos_ref[...]
    sin_p = sp_ref[...]
    sin_m = sm_ref[...]

    nqk = D_ATT + D_KV
    qk = acc[:, OFF_Q:OFF_Q + nqk]
    ms = _dot2_right(qk * qk, _head_avg_matrix(nqk))
    qk = qk * lax.rsqrt(ms + EPS) * gqk_ref[...]
    for i in range(D_ATT // LANES):
        sl = slice(i * LANES, (i + 1) * LANES)
        q_ref[0, :, sl] = _rope(qk[:, sl], cos, sin_p, sin_m).astype(BF16)
    k_ref[0] = _rope(qk[:, D_ATT:nqk], cos, sin_p, sin_m).astype(BF16)
    v_ref[0] = acc[:, OFF_V:OFF_V + D_KV].astype(BF16)

    kscale = HEAD_DIM ** -0.5
    for i in range(2 * D_RET // LANES):
        sl = slice(OFF_RET + i * LANES, OFF_RET + (i + 1) * LANES)
        r = _rope(acc[:, sl], cos, sin_p, sin_m)
        if i >= D_RET // LANES:
            r = r * kscale
        ret_ref[0, :, i * LANES:(i + 1) * LANES] = r.astype(BF16)
    ret_ref[0, :, 2 * D_RET:4 * D_RET] = acc[:, OFF_RET + 2 * D_RET:OFF_RET + 4 * D_RET].astype(BF16)
    z_ref[0] = acc[:, OFF_Z:OFF_Z + D_SSD].astype(BF16)
    xbc_ref[0] = acc[:, OFF_XBC:OFF_XBC + D_XBC].astype(BF16)
    dt_ref[0] = acc[:, OFF_DT:OFF_DT + D_DT]


def _token_specs(d, n_ctx_tiles, lat_off, first=0):
    ctx_spec = pl.BlockSpec((1, ROW_TILE, d),
                            lambda b, i: (b, jnp.minimum(i + first, n_ctx_tiles - 1), 0))
    lat_spec = pl.BlockSpec((1, ROW_TILE, d),
                            lambda b, i: (b, jnp.maximum(i + first, n_ctx_tiles) - lat_off, 0))
    return ctx_spec, lat_spec


def _inproj(x_ctx, x_lat, lat_off, t, mods, ng, w_in, cos, sin_p, sin_m, gqk, n_ctx_tiles):
    bsz, _, d = x_lat.shape
    nt = t // ROW_TILE
    ctx_row = mods.shape[0] - 1

    def mod_map(b, i):
        return (jnp.where(i < n_ctx_tiles, ctx_row, b), 0, 0)

    tok = lambda w: pl.BlockSpec((1, ROW_TILE, w), lambda b, i: (b, i, 0))
    rope_spec = pl.BlockSpec((ROW_TILE, LANES), lambda b, i: (i, 0))
    const = lambda shape: pl.BlockSpec(shape, lambda b, i: (0,) * len(shape))
    widths = (D_ATT, D_KV, D_KV, 4 * D_RET, D_SSD, D_XBC)
    return pl.pallas_call(
        functools.partial(_inproj_kernel, n_ctx_tiles=n_ctx_tiles),
        grid=(bsz, nt),
        in_specs=[
            *_token_specs(d, n_ctx_tiles, lat_off),
            pl.BlockSpec((1, 6, d), mod_map),
            const((4, d)),
            const((d, D_IN_PAD)),
            rope_spec, rope_spec, rope_spec,
            const((1, D_ATT + D_KV)),
        ],
        out_specs=[tok(w) for w in widths] + [tok(D_DT)],
        out_shape=[jax.ShapeDtypeStruct((bsz, t, w), BF16) for w in widths]
        + [jax.ShapeDtypeStruct((bsz, t, D_DT), F32)],
        compiler_params=pltpu.CompilerParams(
            dimension_semantics=("arbitrary", "arbitrary"), vmem_limit_bytes=VMEM_LIMIT),
        name="inproj",
    )(x_ctx, x_lat, mods, ng, w_in, cos, sin_p, sin_m, gqk)


def _attn_kernel(q_ref, k_ref, v_ref, o_ref, km_ref, vt_ref, *, n_ctx, with_ctx):
    qi = pl.program_id(1)

    @pl.when(qi == 0)
    def _():
        k = k_ref[0].astype(F32)
        kr = pltpu.roll(k, HEAD_DIM, 1)
        low = lax.broadcasted_iota(jnp.int32, k.shape, 1) < HEAD_DIM
        km_ref[0] = jnp.where(low, k, 0.0).astype(BF16)
        km_ref[1] = jnp.where(low, 0.0, kr).astype(BF16)
        km_ref[2] = jnp.where(low, kr, 0.0).astype(BF16)
        km_ref[3] = jnp.where(low, 0.0, k).astype(BF16)
        vt_ref[...] = v_ref[0].astype(F32).T.astype(BF16)

    def attend(nk):
        for j in range(D_ATT // LANES):
            qj = q_ref[0, :, j * LANES:(j + 1) * LANES]
            halves = []
            for half in range(LANES // HEAD_DIM):
                g = (j * (LANES // HEAD_DIM) + half) // (H_ATT // H_KV)
                s = _dot_nt(km_ref[2 * g + half, 0:nk, :], qj)
                m = jnp.max(s, axis=0, keepdims=True)
                p = jnp.exp(s - m)
                l = jnp.sum(p, axis=0, keepdims=True)
                o = _dot(vt_ref[g * HEAD_DIM:(g + 1) * HEAD_DIM, 0:nk], p.astype(BF16))
                halves.append(o * (1.0 / l))
            ot = jnp.concatenate(halves, axis=0)
            o_ref[0, :, j * LANES:(j + 1) * LANES] = ot.T.astype(BF16)

    n_all = k_ref.shape[1]
    if with_ctx:
        @pl.when(qi == 0)
        def _():
            attend(n_ctx)

        @pl.when(qi > 0)
        def _():
            attend(n_all)
    else:
        attend(n_all)


def _attention(q, k, v, n_ctx, with_ctx):
    bsz, t, _ = q.shape
    tq = ROW_TILE
    assert n_ctx == tq
    first = 0 if with_ctx else n_ctx // tq
    nq = t // tq - first
    return pl.pallas_call(
        functools.partial(_attn_kernel, n_ctx=n_ctx, with_ctx=with_ctx),
        grid=(bsz, nq),
        in_specs=[
            pl.BlockSpec((1, tq, D_ATT), lambda b, i: (b, i + first, 0)),
            pl.BlockSpec((1, t, D_KV), lambda b, i: (b, 0, 0)),
            pl.BlockSpec((1, t, D_KV), lambda b, i: (b, 0, 0)),
        ],
        out_specs=pl.BlockSpec((1, tq, D_ATT), lambda b, i: (b, i + first, 0)),
        out_shape=jax.ShapeDtypeStruct((bsz, t, D_ATT), BF16),
        scratch_shapes=[pltpu.VMEM((2 * H_KV, t, D_KV), BF16), pltpu.VMEM((D_KV, t), BF16)],
        compiler_params=pltpu.CompilerParams(
            dimension_semantics=("arbitrary", "arbitrary"), vmem_limit_bytes=VMEM_LIMIT),
        name="attention",
    )(q, k, v)


def _log_sigmoid(x):
    return jnp.minimum(x, 0.0) - jnp.log1p(jnp.exp(-jnp.abs(x)))


def _ret_kernel(r_ref, lg_ref, gng_ref, gnb_ref, o_ref, y_ref, sf_ref, sb_ref, *, n_ctx_chunks):
    c_len = CHUNK
    t = r_ref.shape[1]
    n_chunks = t // c_len
    lg = _log_sigmoid(lg_ref[...])
    lgf = lg[0:1, :]
    lgb = lg[1:2, :]
    tcol = lax.broadcasted_iota(jnp.int32, (c_len, 1), 0).astype(F32)
    wq_f = jnp.exp((tcol + 1.0) * lgf)
    wq_b = jnp.exp((c_len - tcol) * lgb)
    wk_f = jnp.exp((c_len - 1.0 - tcol) * lgf)
    wk_b = jnp.exp(tcol * lgb)
    dec_f = jnp.exp(c_len * lgf)
    dec_b = jnp.exp(c_len * lgb)

    ti = lax.broadcasted_iota(jnp.int32, (c_len, c_len), 0)
    si = lax.broadcasted_iota(jnp.int32, (c_len, c_len), 1)
    diff = (ti - si).astype(F32)
    lane = lax.broadcasted_iota(jnp.int32, (c_len, D_RET), 1)
    srow = lax.broadcasted_iota(jnp.int32, (D_RET, D_RET), 0) // HEAD_DIM
    scol = lax.broadcasted_iota(jnp.int32, (D_RET, D_RET), 1) // HEAD_DIM
    smask = srow == scol
    avg = _head_avg_matrix(D_RET)

    def chunk_rows(c):
        return pl.ds(pl.multiple_of(c * c_len, c_len), c_len)

    def state_delta(k, v, wk):
        kw = (k.astype(F32) * wk).T.astype(BF16)
        return jnp.where(smask, _dot(kw, v), 0.0)

    sf_ref[...] = jnp.zeros_like(sf_ref)
    sb_ref[...] = jnp.zeros_like(sb_ref)

    def fwd(c, carry):
        rows = chunk_rows(c)
        q = r_ref[0, rows, 0:D_RET]
        k = r_ref[0, rows, D_RET:2 * D_RET]
        v = r_ref[0, rows, 2 * D_RET:3 * D_RET]
        zero = jnp.zeros_like(q)
        y = wq_f * _dot(q, sf_ref[...].astype(BF16))
        for h in range(H_RET):
            hm = (lane >= h * HEAD_DIM) & (lane < (h + 1) * HEAD_DIM)
            lf = lgf[:, h * HEAD_DIM:h * HEAD_DIM + 1]
            lb = lgb[:, h * HEAD_DIM:h * HEAD_DIM + 1]
            dmat = jnp.exp(jnp.where(diff >= 0, diff * lf, -diff * lb))
            s = _dot_nt(jnp.where(hm, q, zero), k) * dmat
            y = y + _dot(s.astype(BF16), jnp.where(hm, v, zero))
        y_ref[rows, :] = y
        sf_ref[...] = dec_f * sf_ref[...] + state_delta(k, v, wk_f)
        return carry

    lax.fori_loop(0, n_chunks, fwd, 0)

    def bwd(c):
        rows = chunk_rows(c)
        q = r_ref[0, rows, 0:D_RET]
        k = r_ref[0, rows, D_RET:2 * D_RET]
        v = r_ref[0, rows, 2 * D_RET:3 * D_RET]
        g = r_ref[0, rows, 3 * D_RET:4 * D_RET].astype(F32)
        y = y_ref[rows, :] + wq_b * _dot(q, sb_ref[...].astype(BF16))
        sb_ref[...] = dec_b * sb_ref[...] + state_delta(k, v, wk_b)
        mu = _dot2_right(y, avg)
        d = y - mu
        var = _dot2_right(d * d, avg)
        yn = d * lax.rsqrt(var + EPS) * gng_ref[...] + gnb_ref[...]
        o_ref[0, rows, :] = (yn * _silu(g)).astype(BF16)

    def bwd_ctx(i, carry):
        bwd(n_ctx_chunks - 1 - i)
        return carry

    def bwd_lat(i, carry):
        bwd(n_chunks - 1 - i)
        return carry

    lax.fori_loop(0, n_ctx_chunks, bwd_ctx, 0)
    lax.fori_loop(0, n_chunks - n_ctx_chunks, bwd_lat, 0)


def _retention(ret, lg, gng, gnb, n_ctx):
    bsz, t, _ = ret.shape
    const = lambda shape: pl.BlockSpec(shape, lambda b: (0,) * len(shape))
    return pl.pallas_call(
        functools.partial(_ret_kernel, n_ctx_chunks=n_ctx // CHUNK),
        grid=(bsz,),
        in_specs=[
            pl.BlockSpec((1, t, 4 * D_RET), lambda b: (b, 0, 0)),
            const((2, D_RET)), const((1, D_RET)), const((1, D_RET)),
        ],
        out_specs=pl.BlockSpec((1, t, D_RET), lambda b: (b, 0, 0)),
        out_shape=jax.ShapeDtypeStruct((bsz, t, D_RET), BF16),
        scratch_shapes=[pltpu.VMEM((t, D_RET), F32), pltpu.VMEM((D_RET, D_RET), F32),
                        pltpu.VMEM((D_RET, D_RET), F32)],
        compiler_params=pltpu.CompilerParams(
            dimension_semantics=("arbitrary",), vmem_limit_bytes=VMEM_LIMIT),
        name="retention",
    )(ret, lg, gng, gnb)


def _softplus(x):
    return jnp.maximum(x, 0.0) + jnp.log1p(jnp.exp(-jnp.abs(x)))


def _ssd_kernel(z_ref, xbc_ref, dt_ref, cw_ref, cb_ref, dtb_ref, alog_ref, dskip_ref, ng_ref,
                o_ref, xpad_ref, xc_ref, y_ref, eb_ref, dsb_ref, sf_ref, sb_ref, *, n_ctx):
    c_len = CHUNK
    t = xbc_ref.shape[1]
    n_chunks = t // c_len
    n_ctx_chunks = n_ctx // c_len
    pad = 8
    half = SSD_CONV // 2

    zpad = jnp.zeros((pad, D_XBC), F32)
    xpad_ref[0:pad, :] = zpad
    xpad_ref[pad + n_ctx:2 * pad + n_ctx, :] = zpad
    xpad_ref[2 * pad + t:3 * pad + t, :] = zpad

    def fill(c, carry):
        src = pl.ds(pl.multiple_of(c * c_len, c_len), c_len)
        off = jnp.where(c < n_ctx_chunks, pad, 2 * pad)
        dst = pl.ds(pl.multiple_of(c * c_len + off, pad), c_len)
        xpad_ref[dst, :] = xbc_ref[0, src, :].astype(F32)
        return carry

    lax.fori_loop(0, n_chunks, fill, 0)

    cb = cb_ref[...]
    for c in range(n_chunks):
        base = c * c_len + (pad if c < n_ctx_chunks else 2 * pad)
        acc = jnp.broadcast_to(cb, (c_len, D_XBC))
        for j in range(SSD_CONV):
            acc = acc + cw_ref[j:j + 1, :] * xpad_ref[base + j - half:base + j - half + c_len, :]
        xc_ref[c * c_len:(c + 1) * c_len, :] = _silu(acc).astype(BF16)

    nd = 2 * H_SSD
    lane_dt = lax.broadcasted_iota(jnp.int32, (1, D_DT), 1)
    a_vec = jnp.where(lane_dt < nd, -jnp.exp(alog_ref[...]), 0.0)
    is_f = lane_dt < H_SSD
    ti = lax.broadcasted_iota(jnp.int32, (c_len, c_len), 0)
    si = lax.broadcasted_iota(jnp.int32, (c_len, c_len), 1)
    causal = si <= ti
    tri_l = jnp.where(causal, 1.0, 0.0).astype(BF16)
    tri_u = jnp.where(si >= ti, 1.0, 0.0).astype(BF16)
    er = lax.broadcasted_iota(jnp.int32, (D_DT, 2 * D_SSD), 0)
    ec = lax.broadcasted_iota(jnp.int32, (D_DT, 2 * D_SSD), 1) // HEAD_DIM
    expand = jnp.where(er == ec, 1.0, 0.0).astype(BF16)
    lane_x = lax.broadcasted_iota(jnp.int32, (c_len, D_SSD), 1)
    lane_c = lax.broadcasted_iota(jnp.int32, (c_len, D_BC), 1)
    srow = lax.broadcasted_iota(jnp.int32, (D_BC, D_SSD), 0) // SSD_STATE
    scol = lax.broadcasted_iota(jnp.int32, (D_BC, D_SSD), 1) // (D_SSD // SSD_GROUPS)
    smask = srow == scol
    heads_per_group = H_SSD // SSD_GROUPS

    def chunk_rows(c):
        return pl.ds(pl.multiple_of(c * c_len, c_len), c_len)

    sf_ref[...] = jnp.zeros_like(sf_ref)
    sb_ref[...] = jnp.zeros_like(sb_ref)

    def fwd(c, carry):
        rows = chunk_rows(c)
        xs = xc_ref[rows, 0:D_SSD]
        bm = xc_ref[rows, D_SSD:D_SSD + D_BC]
        cm = xc_ref[rows, D_SSD + D_BC:D_XBC]
        xs_f = xs.astype(F32)
        dt = _softplus(dt_ref[0, rows, :] + dtb_ref[...])
        la = dt * a_vec
        cum = jnp.where(is_f, _dot3_left(tri_l, la), _dot3_left(tri_u, la))
        cum_t = cum.T
        dt_t = dt.T
        edge = jnp.where(is_f, cum[c_len - 1:c_len, :], cum[0:1, :])
        wk = jnp.exp(edge - cum) * dt
        cum_e = _dot3_right(cum, expand)
        wk_e = _dot3_right(wk, expand)
        e_f = jnp.exp(cum_e[:, 0:D_SSD])
        e_b = jnp.exp(cum_e[:, D_SSD:2 * D_SSD])
        eb_ref[rows, :] = e_b

        y = e_f * _dot(cm, sf_ref[...].astype(BF16))
        zc = jnp.zeros_like(cm)
        zx = jnp.zeros_like(xs)
        for g in range(SSD_GROUPS):
            gm = (lane_c >= g * SSD_STATE) & (lane_c < (g + 1) * SSD_STATE)
            gmat = _dot_nt(jnp.where(gm, cm, zc), bm)
            for hh in range(heads_per_group):
                h = g * heads_per_group + hh
                d_f = cum[:, h:h + 1] - cum_t[h:h + 1, :]
                d_b = cum[:, H_SSD + h:H_SSD + h + 1] - cum_t[H_SSD + h:H_SSD + h + 1, :]
                w = jnp.where(causal, dt_t[h:h + 1, :], dt_t[H_SSD + h:H_SSD + h + 1, :])
                m = jnp.exp(jnp.where(causal, d_f, d_b)) * w
                hm = (lane_x >= h * HEAD_DIM) & (lane_x < (h + 1) * HEAD_DIM)
                y = y + _dot((gmat * m).astype(BF16), jnp.where(hm, xs, zx))
        y_ref[rows, :] = y

        bm_t = bm.astype(F32).T.astype(BF16)
        ds_f = _dot(bm_t, (xs_f * wk_e[:, 0:D_SSD]).astype(BF16))
        ds_b = _dot(bm_t, (xs_f * wk_e[:, D_SSD:2 * D_SSD]).astype(BF16))
        sf_ref[...] = e_f[c_len - 1:c_len, :] * sf_ref[...] + jnp.where(smask, ds_f, 0.0)
        dsb_ref[c] = jnp.where(smask, ds_b, 0.0)
        return carry

    lax.fori_loop(0, n_chunks, fwd, 0)

    def bwd(c):
        rows = chunk_rows(c)
        xs_f = xc_ref[rows, 0:D_SSD].astype(F32)
        cm = xc_ref[rows, D_SSD + D_BC:D_XBC]
        e_b = eb_ref[rows, :]
        y = y_ref[rows, :] + e_b * _dot(cm, sb_ref[...].astype(BF16))
        sb_ref[...] = e_b[0:1, :] * sb_ref[...] + dsb_ref[c]
        y = y + dskip_ref[...] * xs_f
        u = y * _silu(z_ref[0, rows, :].astype(F32))
        o_ref[0, rows, :] = _rms(u, ng_ref[...]).astype(BF16)

    def bwd_ctx(i, carry):
        bwd(n_ctx_chunks - 1 - i)
        return carry

    def bwd_lat(i, carry):
        bwd(n_chunks - 1 - i)
        return carry

    lax.fori_loop(0, n_ctx_chunks, bwd_ctx, 0)
    lax.fori_loop(0, n_chunks - n_ctx_chunks, bwd_lat, 0)


def _ssd(z, xbc, dt, conv_w, conv_b, dt_bias, a_log, d_skip, norm_g, n_ctx):
    bsz, t, _ = xbc.shape
    n_chunks = t // CHUNK
    const = lambda shape: pl.BlockSpec(shape, lambda b: (0,) * len(shape))
    seq = lambda w: pl.BlockSpec((1, t, w), lambda b: (b, 0, 0))
    return pl.pallas_call(
        functools.partial(_ssd_kernel, n_ctx=n_ctx),
        grid=(bsz,),
        in_specs=[seq(D_SSD), seq(D_XBC), seq(D_DT),
                  const((8, D_XBC)), const((1, D_XBC)), const((1, D_DT)), const((1, D_DT)),
                  const((1, D_SSD)), const((1, D_SSD))],
        out_specs=seq(D_SSD),
        out_shape=jax.ShapeDtypeStruct((bsz, t, D_SSD), BF16),
        scratch_shapes=[
            pltpu.VMEM((t + 24, D_XBC), F32),
            pltpu.VMEM((t, D_XBC), BF16),
            pltpu.VMEM((t, D_SSD), F32),
            pltpu.VMEM((t, D_SSD), F32),
            pltpu.VMEM((n_chunks, D_BC, D_SSD), F32),
            pltpu.VMEM((D_BC, D_SSD), F32),
            pltpu.VMEM((D_BC, D_SSD), F32),
        ],
        compiler_params=pltpu.CompilerParams(
            dimension_semantics=("arbitrary",), vmem_limit_bytes=VMEM_LIMIT),
        name="ssd",
    )(z, xbc, dt, conv_w, conv_b, dt_bias, a_log, d_skip, norm_g)


def _post_kernel(att_ref, ret_ref, ssd_ref, xc_ref, xl_ref, mod_ref, ng_ref, wo_ref, w1_ref, w2_ref,
                 o_ref, *, ff_chunk, n_ctx_tiles, first):
    x = jnp.where(pl.program_id(1) + first < n_ctx_tiles, xc_ref[0], xl_ref[0])
    mix = jnp.concatenate([att_ref[0], ret_ref[0], ssd_ref[0]], axis=-1)
    o = _dot(mix, wo_ref[...])
    x1 = x + mod_ref[0, 2:3, :] * _rms(o, ng_ref[1:2, :])
    h2 = _rms(x1, ng_ref[2:3, :]) * (1.0 + mod_ref[0, 4:5, :]) + mod_ref[0, 3:4, :]
    h2 = h2.astype(BF16)
    d_ff = w1_ref.shape[1]
    acc = jnp.zeros(x1.shape, F32)
    for j in range(d_ff // ff_chunk):
        sl = slice(j * ff_chunk, (j + 1) * ff_chunk)
        a = jnp.maximum(_dot(h2, w1_ref[:, sl]), 0.0)
        acc = acc + _dot((a * a).astype(BF16), w2_ref[sl, :])
    o_ref[0] = x1 + mod_ref[0, 5:6, :] * _rms(acc, ng_ref[3:4, :])


def _post(att, ret, ssd, x_ctx, x_lat, lat_off, mods, ng, w_out, w_ff1, w_ff2, n_ctx_tiles, with_ctx):
    bsz, t, _ = att.shape
    d = x_lat.shape[2]
    d_ff = w_ff1.shape[1]
    first = 0 if with_ctx else n_ctx_tiles
    nt = t // ROW_TILE - first
    ctx_row = mods.shape[0] - 1

    def mod_map(b, i):
        return (jnp.where(i + first < n_ctx_tiles, ctx_row, b), 0, 0)

    tok = lambda w: pl.BlockSpec((1, ROW_TILE, w), lambda b, i: (b, i + first, 0))
    const = lambda shape: pl.BlockSpec(shape, lambda b, i: (0,) * len(shape),
                                       pipeline_mode=pl.Buffered(1))
    return pl.pallas_call(
        functools.partial(_post_kernel, ff_chunk=1024, n_ctx_tiles=n_ctx_tiles, first=first),
        grid=(bsz, nt),
        in_specs=[tok(D_ATT), tok(D_RET), tok(D_SSD),
                  *_token_specs(d, n_ctx_tiles, lat_off, first),
                  pl.BlockSpec((1, 6, d), mod_map),
                  const((4, d)), const((d, d)), const((d, d_ff)), const((d_ff, d))],
        out_specs=pl.BlockSpec((1, ROW_TILE, d), lambda b, i: (b, i, 0)),
        out_shape=jax.ShapeDtypeStruct((bsz, nt * ROW_TILE, d), F32),
        compiler_params=pltpu.CompilerParams(
            dimension_semantics=("arbitrary", "arbitrary"), vmem_limit_bytes=VMEM_LIMIT),
        name="post",
    )(att, ret, ssd, x_ctx, x_lat, mods, ng, w_out, w_ff1, w_ff2)


def _rope_tables(n, n_ctx):
    rows = n // GRID_W
    row = jnp.broadcast_to(jnp.arange(rows)[:, None], (rows, GRID_W)).reshape(n)
    col = jnp.broadcast_to(jnp.arange(GRID_W)[None, :], (rows, GRID_W)).reshape(n)
    half = HEAD_DIM // 2
    inv_freq = ROPE_THETA ** (-jnp.arange(0, half, 2, dtype=F32) / half)
    ang = jnp.stack([row, col], axis=-1).astype(F32)[:, :, None] * inv_freq
    ang = jnp.concatenate([ang, ang], axis=-1).reshape(n, HEAD_DIM)
    ang = jnp.tile(ang, (1, LANES // HEAD_DIM))
    cos = jnp.cos(ang)
    sin = jnp.sin(ang)
    upper = (jnp.arange(LANES) % half) >= half // 2
    sin_p = jnp.where(upper, sin, 0.0)
    sin_m = jnp.where(upper, 0.0, -sin)
    ident = lambda v, a: jnp.concatenate([jnp.full((n_ctx, LANES), v, F32), a], axis=0)
    return ident(1.0, cos), ident(0.0, sin_p), ident(0.0, sin_m)


def kernel(x, c, ctx, c_ctx, w_mod, b_mod, norm_g, w_in, w_out, q_norm_g, k_norm_g, ret_decay_logit,
           ret_gn_g, ret_gn_b, ssd_conv_w, ssd_conv_b, ssd_dt_bias, ssd_a_log, ssd_d, ssd_norm_g,
           w_ff1, w_ff2):
    bsz, n, d = x.shape
    n_ctx = ctx.shape[1]
    depth = w_mod.shape[0]
    assert n % ROW_TILE == 0 and n_ctx % ROW_TILE == 0 and n_ctx % CHUNK == 0 and n % CHUNK == 0
    n_ctx_tiles = n_ctx // ROW_TILE

    n_rows = -(-(bsz + 1) // 8) * 8
    cc = jnp.concatenate([c, c_ctx[None, :], jnp.zeros((n_rows - bsz - 1, d), F32)], axis=0)
    mods = _modulation(cc, w_mod, b_mod)[:, :bsz + 1].reshape(depth, bsz + 1, 6, d)

    cos, sin_p, sin_m = _rope_tables(n, n_ctx)
    n_dt = 2 * H_SSD
    pad_dt = lambda a: jnp.pad(a.reshape(1, n_dt), ((0, 0), (0, D_DT - n_dt)))

    x_ctx, x_lat, lat_off = ctx, x, n_ctx_tiles
    for layer in range(depth):
        last = layer == depth - 1
        wi = jnp.pad(w_in[layer], ((0, 0), (0, D_IN_PAD - w_in.shape[2]))).astype(BF16)
        wo = w_out[layer].astype(BF16)
        gqk = jnp.concatenate([jnp.tile(q_norm_g[layer], H_ATT) * HEAD_DIM ** -0.5,
                               jnp.tile(k_norm_g[layer], H_KV)])[None, :]

        q, k, v, ret, z, xbc, dt = _inproj(x_ctx, x_lat, lat_off, n_ctx + n, mods[layer],
                                           norm_g[layer], wi, cos, sin_p, sin_m, gqk, n_ctx_tiles)
        att = _attention(q, k, v, n_ctx, with_ctx=not last)
        lg = jnp.repeat(ret_decay_logit[layer], HEAD_DIM, axis=1)
        ret_o = _retention(ret, lg, ret_gn_g[layer][None, :], ret_gn_b[layer][None, :], n_ctx)
        conv_w = jnp.pad(ssd_conv_w[layer], ((0, 8 - SSD_CONV), (0, 0)))
        ssd_o = _ssd(z, xbc, dt, conv_w, ssd_conv_b[layer][None, :], pad_dt(ssd_dt_bias[layer]),
                     pad_dt(ssd_a_log[layer]), jnp.repeat(ssd_d[layer], HEAD_DIM)[None, :],
                     ssd_norm_g[layer][None, :], n_ctx)
        xall = _post(att, ret_o, ssd_o, x_ctx, x_lat, lat_off, mods[layer], norm_g[layer], wo,
                     w_ff1[layer].astype(BF16), w_ff2[layer].astype(BF16), n_ctx_tiles,
                     with_ctx=not last)
        x_ctx, x_lat, lat_off = xall, xall, 0
    return xall
```

```python
import functools

import jax
import jax.numpy as jnp
from jax import lax
from jax.experimental import pallas as pl
from jax.experimental.pallas import tpu as pltpu

F32 = jnp.float32
BF16 = jnp.bfloat16

HEAD_DIM = 64
H_ATT = 6
H_KV = 2
H_RET = 4
H_SSD = 6
SSD_GROUPS = 2
SSD_STATE = 128
SSD_CONV = 5
GRID_W = 64
ROPE_THETA = 10000.0
EPS = 1e-6
LOG2_E = 1.4426950408889634

D_ATT = H_ATT * HEAD_DIM
D_KV = H_KV * HEAD_DIM
D_RET = H_RET * HEAD_DIM
D_SSD = H_SSD * HEAD_DIM
D_BC = SSD_GROUPS * SSD_STATE
D_XBC = D_SSD + 2 * D_BC
LANES = 128
D_DT = LANES
OFF_Q = 0
OFF_K = OFF_Q + D_ATT
OFF_V = OFF_K + D_KV
OFF_RET = OFF_V + D_KV
OFF_Z = OFF_RET + 4 * D_RET
OFF_XBC = OFF_Z + D_SSD
OFF_DT = OFF_XBC + D_XBC
D_IN_PAD = OFF_DT + D_DT

ROW_TILE = 256
CHUNK = 256
VMEM_LIMIT = 56 * 1024 * 1024


def _silu(x):
    return x * jax.nn.sigmoid(x)


def _split2(a):
    hi = a.astype(BF16)
    lo = (a - hi.astype(F32)).astype(BF16)
    return hi, lo


def _split3(a):
    hi = a.astype(BF16)
    r = a - hi.astype(F32)
    mid = r.astype(BF16)
    lo = (r - mid.astype(F32)).astype(BF16)
    return hi, mid, lo


def _dot(a, b):
    return jnp.dot(a, b, preferred_element_type=F32)


def _dot_nt(a, b):
    return lax.dot_general(a, b, (((1,), (1,)), ((), ())), preferred_element_type=F32)


def _dot2_right(a, m):
    hi, lo = _split2(a)
    return _dot(hi, m) + _dot(lo, m)


def _dot3_right(a, m):
    hi, mid, lo = _split3(a)
    return _dot(hi, m) + _dot(mid, m) + _dot(lo, m)


def _dot3_left(m, a):
    hi, mid, lo = _split3(a)
    return _dot(m, hi) + _dot(m, mid) + _dot(m, lo)


def _rms(x, g):
    ms = jnp.mean(x * x, axis=-1, keepdims=True)
    return x * lax.rsqrt(ms + EPS) * g


def _head_avg_matrix(n):
    r = lax.broadcasted_iota(jnp.int32, (n, n), 0) // HEAD_DIM
    c = lax.broadcasted_iota(jnp.int32, (n, n), 1) // HEAD_DIM
    return jnp.where(r == c, 1.0 / HEAD_DIM, 0.0).astype(BF16)


def _mod_kernel(c_ref, w_ref, b_ref, o_ref):
    sc = _silu(c_ref[...]).astype(BF16)
    o_ref[0] = _dot(sc, w_ref[0].astype(BF16)) + b_ref[0]


def _modulation(cc, w_mod, b_mod):
    depth, d, n = w_mod.shape
    rows = cc.shape[0]
    tn = 1536
    return pl.pallas_call(
        _mod_kernel,
        grid=(depth, n // tn),
        in_specs=[
            pl.BlockSpec((rows, d), lambda l, j: (0, 0)),
            pl.BlockSpec((1, d, tn), lambda l, j: (l, 0, j)),
            pl.BlockSpec((1, 1, tn), lambda l, j: (l, 0, j)),
        ],
        out_specs=pl.BlockSpec((1, rows, tn), lambda l, j: (l, 0, j)),
        out_shape=jax.ShapeDtypeStruct((depth, rows, n), F32),
        compiler_params=pltpu.CompilerParams(
            dimension_semantics=("arbitrary", "arbitrary"), vmem_limit_bytes=VMEM_LIMIT),
        name="modulation",
    )(cc, w_mod, b_mod.reshape(depth, 1, n))


def _rope(x, cos, sin_p, sin_m):
    return x * cos + pltpu.roll(x, 16, 1) * sin_p + pltpu.roll(x, LANES - 16, 1) * sin_m


def _inproj_kernel(xc_ref, xl_ref, mod_ref, ng_ref, w_ref, cos_ref, sp_ref, sm_ref, gqk_ref,
                   qt_ref, k_ref, v_ref, ret_ref, z_ref, xbc_ref, dt_ref, *, n_ctx_tiles):
    x = jnp.where(pl.program_id(1) < n_ctx_tiles, xc_ref[0], xl_ref[0])
    h = _rms(x, ng_ref[0:1, :])
    h = h * (1.0 + mod_ref[0, 1:2, :]) + mod_ref[0, 0:1, :]
    acc = _dot(h.astype(BF16), w_ref[...])

    cos = cos_ref[...]
    sin_p = sp_ref[...]
    sin_m = sm_ref[...]

    nqk = D_ATT + D_KV
    qk = acc[:, OFF_Q:OFF_Q + nqk]
    ms = _dot2_right(qk * qk, _head_avg_matrix(nqk))
    qk = qk * lax.rsqrt(ms + EPS) * gqk_ref[...]
    for i in range(D_ATT // LANES):
        sl = slice(i * LANES, (i + 1) * LANES)
        qt_ref[0, sl, :] = _rope(qk[:, sl], cos, sin_p, sin_m).T.astype(BF16)
    k_ref[0] = _rope(qk[:, D_ATT:nqk], cos, sin_p, sin_m).astype(BF16)
    v_ref[0] = acc[:, OFF_V:OFF_V + D_KV].astype(BF16)

    kscale = HEAD_DIM ** -0.5
    for i in range(2 * D_RET // LANES):
        sl = slice(OFF_RET + i * LANES, OFF_RET + (i + 1) * LANES)
        r = _rope(acc[:, sl], cos, sin_p, sin_m)
        if i >= D_RET // LANES:
            r = r * kscale
        ret_ref[0, :, i * LANES:(i + 1) * LANES] = r.astype(BF16)
    ret_ref[0, :, 2 * D_RET:4 * D_RET] = acc[:, OFF_RET + 2 * D_RET:OFF_RET + 4 * D_RET].astype(BF16)
    z_ref[0] = acc[:, OFF_Z:OFF_Z + D_SSD].astype(BF16)
    xbc_ref[0] = acc[:, OFF_XBC:OFF_XBC + D_XBC].astype(BF16)
    dt_ref[0] = acc[:, OFF_DT:OFF_DT + D_DT]


def _token_specs(d, n_ctx_tiles, lat_off, first=0):
    ctx_spec = pl.BlockSpec((1, ROW_TILE, d),
                            lambda b, i: (b, jnp.minimum(i + first, n_ctx_tiles - 1), 0))
    lat_spec = pl.BlockSpec((1, ROW_TILE, d),
                            lambda b, i: (b, jnp.maximum(i + first, n_ctx_tiles) - lat_off, 0))
    return ctx_spec, lat_spec


def _inproj(x_ctx, x_lat, lat_off, t, mods, ng, w_in, cos, sin_p, sin_m, gqk, n_ctx_tiles):
    bsz, _, d = x_lat.shape
    nt = t // ROW_TILE
    ctx_row = mods.shape[0] - 1

    def mod_map(b, i):
        return (jnp.where(i < n_ctx_tiles, ctx_row, b), 0, 0)

    tok = lambda w: pl.BlockSpec((1, ROW_TILE, w), lambda b, i: (b, i, 0))
    rope_spec = pl.BlockSpec((ROW_TILE, LANES), lambda b, i: (i, 0))
    const = lambda shape: pl.BlockSpec(shape, lambda b, i: (0,) * len(shape))
    widths = (D_KV, D_KV, 4 * D_RET, D_SSD, D_XBC)
    qt_spec = pl.BlockSpec((1, D_ATT, ROW_TILE), lambda b, i: (b, 0, i))
    return pl.pallas_call(
        functools.partial(_inproj_kernel, n_ctx_tiles=n_ctx_tiles),
        grid=(bsz, nt),
        in_specs=[
            *_token_specs(d, n_ctx_tiles, lat_off),
            pl.BlockSpec((1, 6, d), mod_map),
            const((4, d)),
            const((d, D_IN_PAD)),
            rope_spec, rope_spec, rope_spec,
            const((1, D_ATT + D_KV)),
        ],
        out_specs=[qt_spec] + [tok(w) for w in widths] + [tok(D_DT)],
        out_shape=[jax.ShapeDtypeStruct((bsz, D_ATT, t), BF16)]
        + [jax.ShapeDtypeStruct((bsz, t, w), BF16) for w in widths]
        + [jax.ShapeDtypeStruct((bsz, t, D_DT), F32)],
        compiler_params=pltpu.CompilerParams(
            dimension_semantics=("arbitrary", "arbitrary"), vmem_limit_bytes=VMEM_LIMIT),
        name="inproj",
    )(x_ctx, x_lat, mods, ng, w_in, cos, sin_p, sin_m, gqk)


V_ROWS = 80
KEY_BLOCK = 256
NEG_BIG = -1e30


def _attn_kernel(qt_ref, k_ref, v_ref, o_ref, km_ref, vt_ref, acc_ref, m_ref, alpha_ref, s_ref, p_ref,
                 *, n_ctx, with_ctx):
    qi = pl.program_id(1)
    per_tile = LANES // HEAD_DIM
    n_tiles = D_ATT // LANES

    @pl.when(qi == 0)
    def _():
        k = k_ref[0].astype(F32)
        kr = pltpu.roll(k, HEAD_DIM, 1)
        low = lax.broadcasted_iota(jnp.int32, k.shape, 1) < HEAD_DIM
        km_ref[0] = jnp.where(low, k, 0.0).astype(BF16)
        km_ref[1] = jnp.where(low, 0.0, kr).astype(BF16)
        km_ref[2] = jnp.where(low, kr, 0.0).astype(BF16)
        km_ref[3] = jnp.where(low, 0.0, k).astype(BF16)
        vt = v_ref[0].astype(F32).T
        row = lax.broadcasted_iota(jnp.int32, (V_ROWS - HEAD_DIM, vt.shape[1]), 0)
        tail = jnp.where(row == 0, 1.0, 0.0)
        for g in range(H_KV):
            vg = jnp.concatenate([vt[g * HEAD_DIM:(g + 1) * HEAD_DIM], tail], axis=0).astype(BF16)
            for blk in range(vt.shape[1] // KEY_BLOCK):
                vt_ref[g, blk] = vg[:, blk * KEY_BLOCK:(blk + 1) * KEY_BLOCK]

    kv_of = lambda h: h // (H_ATT // H_KV)

    def scores(i):
        for j in range(n_tiles):
            ks = jnp.concatenate(
                [km_ref[2 * kv_of(j * per_tile + half) + half, i * KEY_BLOCK:(i + 1) * KEY_BLOCK, :]
                 for half in range(per_tile)], axis=0)
            s_ref[i % 2, j] = _dot(ks, qt_ref[0, j * LANES:(j + 1) * LANES, :])

    def exponentials(i):
        for h in range(H_ATT):
            j, half = divmod(h, per_tile)
            s = s_ref[i % 2, j, half * KEY_BLOCK:(half + 1) * KEY_BLOCK, :]
            m_old = m_ref[h]
            m_new = jnp.maximum(m_old, jnp.max(s, axis=0, keepdims=True))
            m_ref[h] = m_new
            alpha_ref[i % 2, h] = jnp.exp2(m_old - m_new)
            p_ref[i % 2, h] = jnp.exp2(s - m_new).astype(BF16)

    def weighted_values(i):
        for h in range(H_ATT):
            pv = _dot(vt_ref[kv_of(h), i], p_ref[i % 2, h])
            acc_ref[h] = acc_ref[h] * alpha_ref[i % 2, h] + pv

    def run(n_blocks):
        acc_ref[...] = jnp.zeros_like(acc_ref)
        m_ref[...] = jnp.full(m_ref.shape, NEG_BIG, F32)
        for it in range(n_blocks + 2):
            if it >= 2:
                weighted_values(it - 2)
            if it < n_blocks:
                scores(it)
            if 1 <= it <= n_blocks:
                exponentials(it - 1)
        for j in range(n_tiles):
            halves = []
            for half in range(per_tile):
                acc = acc_ref[j * per_tile + half]
                halves.append(acc[0:HEAD_DIM] * (1.0 / acc[HEAD_DIM:HEAD_DIM + 1]))
            ot = jnp.concatenate(halves, axis=0)
            o_ref[0, :, j * LANES:(j + 1) * LANES] = ot.T.astype(BF16)

    n_all = k_ref.shape[1] // KEY_BLOCK
    if with_ctx:
        @pl.when(qi == 0)
        def _():
            run(n_ctx // KEY_BLOCK)

        @pl.when(qi > 0)
        def _():
            run(n_all)
    else:
        run(n_all)


def _attention(qt, k, v, n_ctx, with_ctx):
    bsz, _, t = qt.shape
    tq = ROW_TILE
    assert n_ctx == tq and n_ctx % KEY_BLOCK == 0 and t % KEY_BLOCK == 0
    first = 0 if with_ctx else n_ctx // tq
    nq = t // tq - first
    return pl.pallas_call(
        functools.partial(_attn_kernel, n_ctx=n_ctx, with_ctx=with_ctx),
        grid=(bsz, nq),
        in_specs=[
            pl.BlockSpec((1, D_ATT, tq), lambda b, i: (b, 0, i + first)),
            pl.BlockSpec((1, t, D_KV), lambda b, i: (b, 0, 0)),
            pl.BlockSpec((1, t, D_KV), lambda b, i: (b, 0, 0)),
        ],
        out_specs=pl.BlockSpec((1, tq, D_ATT), lambda b, i: (b, i, 0)),
        out_shape=jax.ShapeDtypeStruct((bsz, nq * tq, D_ATT), BF16),
        scratch_shapes=[pltpu.VMEM((2 * H_KV, t, D_KV), BF16),
                        pltpu.VMEM((H_KV, t // KEY_BLOCK, V_ROWS, KEY_BLOCK), BF16),
                        pltpu.VMEM((H_ATT, V_ROWS, tq), F32),
                        pltpu.VMEM((H_ATT, 1, tq), F32),
                        pltpu.VMEM((2, H_ATT, 1, tq), F32),
                        pltpu.VMEM((2, D_ATT // LANES, (LANES // HEAD_DIM) * KEY_BLOCK, tq), F32),
                        pltpu.VMEM((2, H_ATT, KEY_BLOCK, tq), BF16)],
        compiler_params=pltpu.CompilerParams(
            dimension_semantics=("arbitrary", "arbitrary"), vmem_limit_bytes=VMEM_LIMIT),
        name="attention",
    )(qt, k, v)


def _log_sigmoid(x):
    return jnp.minimum(x, 0.0) - jnp.log1p(jnp.exp(-jnp.abs(x)))


def _ret_kernel(r_ref, lg_ref, gng_ref, gnb_ref, o_ref, y_ref, sf_ref, sb_ref, *, n_ctx_chunks):
    c_len = CHUNK
    t = r_ref.shape[1]
    n_chunks = t // c_len
    lg = _log_sigmoid(lg_ref[...])
    lgf = lg[0:1, :]
    lgb = lg[1:2, :]
    tcol = lax.broadcasted_iota(jnp.int32, (c_len, 1), 0).astype(F32)
    wq_f = jnp.exp((tcol + 1.0) * lgf)
    wq_b = jnp.exp((c_len - tcol) * lgb)
    wk_f = jnp.exp((c_len - 1.0 - tcol) * lgf)
    wk_b = jnp.exp(tcol * lgb)
    dec_f = jnp.exp(c_len * lgf)
    dec_b = jnp.exp(c_len * lgb)

    ti = lax.broadcasted_iota(jnp.int32, (c_len, c_len), 0)
    si = lax.broadcasted_iota(jnp.int32, (c_len, c_len), 1)
    diff = (ti - si).astype(F32)
    lane = lax.broadcasted_iota(jnp.int32, (c_len, D_RET), 1)
    srow = lax.broadcasted_iota(jnp.int32, (D_RET, D_RET), 0) // HEAD_DIM
    scol = lax.broadcasted_iota(jnp.int32, (D_RET, D_RET), 1) // HEAD_DIM
    smask = srow == scol
    avg = _head_avg_matrix(D_RET)

    def chunk_rows(c):
        return pl.ds(pl.multiple_of(c * c_len, c_len), c_len)

    def state_delta(k, v, wk):
        kw = (k.astype(F32) * wk).T.astype(BF16)
        return jnp.where(smask, _dot(kw, v), 0.0)

    sf_ref[...] = jnp.zeros_like(sf_ref)
    sb_ref[...] = jnp.zeros_like(sb_ref)

    def fwd(c, carry):
        rows = chunk_rows(c)
        q = r_ref[0, rows, 0:D_RET]
        k = r_ref[0, rows, D_RET:2 * D_RET]
        v = r_ref[0, rows, 2 * D_RET:3 * D_RET]
        zero = jnp.zeros_like(q)
        y = wq_f * _dot(q, sf_ref[...].astype(BF16))
        for h in range(H_RET):
            hm = (lane >= h * HEAD_DIM) & (lane < (h + 1) * HEAD_DIM)
            lf = lgf[:, h * HEAD_DIM:h * HEAD_DIM + 1]
            lb = lgb[:, h * HEAD_DIM:h * HEAD_DIM + 1]
            dmat = jnp.exp(jnp.where(diff >= 0, diff * lf, -diff * lb))
            s = _dot_nt(jnp.where(hm, q, zero), k) * dmat
            y = y + _dot(s.astype(BF16), jnp.where(hm, v, zero))
        y_ref[rows, :] = y
        sf_ref[...] = dec_f * sf_ref[...] + state_delta(k, v, wk_f)
        return carry

    lax.fori_loop(0, n_chunks, fwd, 0)

    def bwd(c):
        rows = chunk_rows(c)
        q = r_ref[0, rows, 0:D_RET]
        k = r_ref[0, rows, D_RET:2 * D_RET]
        v = r_ref[0, rows, 2 * D_RET:3 * D_RET]
        g = r_ref[0, rows, 3 * D_RET:4 * D_RET].astype(F32)
        y = y_ref[rows, :] + wq_b * _dot(q, sb_ref[...].astype(BF16))
        sb_ref[...] = dec_b * sb_ref[...] + state_delta(k, v, wk_b)
        mu = _dot2_right(y, avg)
        d = y - mu
        var = _dot2_right(d * d, avg)
        yn = d * lax.rsqrt(var + EPS) * gng_ref[...] + gnb_ref[...]
        o_ref[0, rows, :] = (yn * _silu(g)).astype(BF16)

    def bwd_ctx(i, carry):
        bwd(n_ctx_chunks - 1 - i)
        return carry

    def bwd_lat(i, carry):
        bwd(n_chunks - 1 - i)
        return carry

    lax.fori_loop(0, n_ctx_chunks, bwd_ctx, 0)
    lax.fori_loop(0, n_chunks - n_ctx_chunks, bwd_lat, 0)


def _retention(ret, lg, gng, gnb, n_ctx):
    bsz, t, _ = ret.shape
    const = lambda shape: pl.BlockSpec(shape, lambda b: (0,) * len(shape))
    return pl.pallas_call(
        functools.partial(_ret_kernel, n_ctx_chunks=n_ctx // CHUNK),
        grid=(bsz,),
        in_specs=[
            pl.BlockSpec((1, t, 4 * D_RET), lambda b: (b, 0, 0)),
            const((2, D_RET)), const((1, D_RET)), const((1, D_RET)),
        ],
        out_specs=pl.BlockSpec((1, t, D_RET), lambda b: (b, 0, 0)),
        out_shape=jax.ShapeDtypeStruct((bsz, t, D_RET), BF16),
        scratch_shapes=[pltpu.VMEM((t, D_RET), F32), pltpu.VMEM((D_RET, D_RET), F32),
                        pltpu.VMEM((D_RET, D_RET), F32)],
        compiler_params=pltpu.CompilerParams(
            dimension_semantics=("arbitrary",), vmem_limit_bytes=VMEM_LIMIT),
        name="retention",
    )(ret, lg, gng, gnb)


def _softplus(x):
    return jnp.maximum(x, 0.0) + jnp.log1p(jnp.exp(-jnp.abs(x)))


def _ssd_kernel(z_ref, xbc_ref, dt_ref, cw_ref, cb_ref, dtb_ref, alog_ref, dskip_ref, ng_ref,
                o_ref, xpad_ref, xc_ref, y_ref, eb_ref, dsb_ref, sf_ref, sb_ref, *, n_ctx):
    c_len = CHUNK
    t = xbc_ref.shape[1]
    n_chunks = t // c_len
    n_ctx_chunks = n_ctx // c_len
    pad = 8
    half = SSD_CONV // 2

    zpad = jnp.zeros((pad, D_XBC), F32)
    xpad_ref[0:pad, :] = zpad
    xpad_ref[pad + n_ctx:2 * pad + n_ctx, :] = zpad
    xpad_ref[2 * pad + t:3 * pad + t, :] = zpad

    def fill(c, carry):
        src = pl.ds(pl.multiple_of(c * c_len, c_len), c_len)
        off = jnp.where(c < n_ctx_chunks, pad, 2 * pad)
        dst = pl.ds(pl.multiple_of(c * c_len + off, pad), c_len)
        xpad_ref[dst, :] = xbc_ref[0, src, :].astype(F32)
        return carry

    lax.fori_loop(0, n_chunks, fill, 0)

    cb = cb_ref[...]
    for c in range(n_chunks):
        base = c * c_len + (pad if c < n_ctx_chunks else 2 * pad)
        acc = jnp.broadcast_to(cb, (c_len, D_XBC))
        for j in range(SSD_CONV):
            acc = acc + cw_ref[j:j + 1, :] * xpad_ref[base + j - half:base + j - half + c_len, :]
        xc_ref[c * c_len:(c + 1) * c_len, :] = _silu(acc).astype(BF16)

    nd = 2 * H_SSD
    lane_dt = lax.broadcasted_iota(jnp.int32, (1, D_DT), 1)
    a_vec = jnp.where(lane_dt < nd, -jnp.exp(alog_ref[...]), 0.0)
    is_f = lane_dt < H_SSD
    ti = lax.broadcasted_iota(jnp.int32, (c_len, c_len), 0)
    si = lax.broadcasted_iota(jnp.int32, (c_len, c_len), 1)
    causal = si <= ti
    tri_l = jnp.where(causal, 1.0, 0.0).astype(BF16)
    tri_u = jnp.where(si >= ti, 1.0, 0.0).astype(BF16)
    er = lax.broadcasted_iota(jnp.int32, (D_DT, 2 * D_SSD), 0)
    ec = lax.broadcasted_iota(jnp.int32, (D_DT, 2 * D_SSD), 1) // HEAD_DIM
    expand = jnp.where(er == ec, 1.0, 0.0).astype(BF16)
    lane_x = lax.broadcasted_iota(jnp.int32, (c_len, D_SSD), 1)
    lane_c = lax.broadcasted_iota(jnp.int32, (c_len, D_BC), 1)
    srow = lax.broadcasted_iota(jnp.int32, (D_BC, D_SSD), 0) // SSD_STATE
    scol = lax.broadcasted_iota(jnp.int32, (D_BC, D_SSD), 1) // (D_SSD // SSD_GROUPS)
    smask = srow == scol
    heads_per_group = H_SSD // SSD_GROUPS

    def chunk_rows(c):
        return pl.ds(pl.multiple_of(c * c_len, c_len), c_len)

    sf_ref[...] = jnp.zeros_like(sf_ref)
    sb_ref[...] = jnp.zeros_like(sb_ref)

    def fwd(c, carry):
        rows = chunk_rows(c)
        xs = xc_ref[rows, 0:D_SSD]
        bm = xc_ref[rows, D_SSD:D_SSD + D_BC]
        cm = xc_ref[rows, D_SSD + D_BC:D_XBC]
        xs_f = xs.astype(F32)
        dt = _softplus(dt_ref[0, rows, :] + dtb_ref[...])
        la = dt * a_vec
        cum = jnp.where(is_f, _dot3_left(tri_l, la), _dot3_left(tri_u, la))
        cum_t = cum.T
        dt_t = dt.T
        edge = jnp.where(is_f, cum[c_len - 1:c_len, :], cum[0:1, :])
        wk = jnp.exp(edge - cum) * dt
        cum_e = _dot3_right(cum, expand)
        wk_e = _dot3_right(wk, expand)
        e_f = jnp.exp(cum_e[:, 0:D_SSD])
        e_b = jnp.exp(cum_e[:, D_SSD:2 * D_SSD])
        eb_ref[rows, :] = e_b

        y = e_f * _dot(cm, sf_ref[...].astype(BF16))
        zc = jnp.zeros_like(cm)
        zx = jnp.zeros_like(xs)
        for g in range(SSD_GROUPS):
            gm = (lane_c >= g * SSD_STATE) & (lane_c < (g + 1) * SSD_STATE)
            gmat = _dot_nt(jnp.where(gm, cm, zc), bm)
            for hh in range(heads_per_group):
                h = g * heads_per_group + hh
                d_f = cum[:, h:h + 1] - cum_t[h:h + 1, :]
                d_b = cum[:, H_SSD + h:H_SSD + h + 1] - cum_t[H_SSD + h:H_SSD + h + 1, :]
                w = jnp.where(causal, dt_t[h:h + 1, :], dt_t[H_SSD + h:H_SSD + h + 1, :])
                m = jnp.exp(jnp.where(causal, d_f, d_b)) * w
                hm = (lane_x >= h * HEAD_DIM) & (lane_x < (h + 1) * HEAD_DIM)
                y = y + _dot((gmat * m).astype(BF16), jnp.where(hm, xs, zx))
        y_ref[rows, :] = y

        bm_t = bm.astype(F32).T.astype(BF16)
        ds_f = _dot(bm_t, (xs_f * wk_e[:, 0:D_SSD]).astype(BF16))
        ds_b = _dot(bm_t, (xs_f * wk_e[:, D_SSD:2 * D_SSD]).astype(BF16))
        sf_ref[...] = e_f[c_len - 1:c_len, :] * sf_ref[...] + jnp.where(smask, ds_f, 0.0)
        dsb_ref[c] = jnp.where(smask, ds_b, 0.0)
        return carry

    lax.fori_loop(0, n_chunks, fwd, 0)

    def bwd(c):
        rows = chunk_rows(c)
        xs_f = xc_ref[rows, 0:D_SSD].astype(F32)
        cm = xc_ref[rows, D_SSD + D_BC:D_XBC]
        e_b = eb_ref[rows, :]
        y = y_ref[rows, :] + e_b * _dot(cm, sb_ref[...].astype(BF16))
        sb_ref[...] = e_b[0:1, :] * sb_ref[...] + dsb_ref[c]
        y = y + dskip_ref[...] * xs_f
        u = y * _silu(z_ref[0, rows, :].astype(F32))
        o_ref[0, rows, :] = _rms(u, ng_ref[...]).astype(BF16)

    def bwd_ctx(i, carry):
        bwd(n_ctx_chunks - 1 - i)
        return carry

    def bwd_lat(i, carry):
        bwd(n_chunks - 1 - i)
        return carry

    lax.fori_loop(0, n_ctx_chunks, bwd_ctx, 0)
    lax.fori_loop(0, n_chunks - n_ctx_chunks, bwd_lat, 0)


def _ssd(z, xbc, dt, conv_w, conv_b, dt_bias, a_log, d_skip, norm_g, n_ctx):
    bsz, t, _ = xbc.shape
    n_chunks = t // CHUNK
    const = lambda shape: pl.BlockSpec(shape, lambda b: (0,) * len(shape))
    seq = lambda w: pl.BlockSpec((1, t, w), lambda b: (b, 0, 0))
    return pl.pallas_call(
        functools.partial(_ssd_kernel, n_ctx=n_ctx),
        grid=(bsz,),
        in_specs=[seq(D_SSD), seq(D_XBC), seq(D_DT),
                  const((8, D_XBC)), const((1, D_XBC)), const((1, D_DT)), const((1, D_DT)),
                  const((1, D_SSD)), const((1, D_SSD))],
        out_specs=seq(D_SSD),
        out_shape=jax.ShapeDtypeStruct((bsz, t, D_SSD), BF16),
        scratch_shapes=[
            pltpu.VMEM((t + 24, D_XBC), F32),
            pltpu.VMEM((t, D_XBC), BF16),
            pltpu.VMEM((t, D_SSD), F32),
            pltpu.VMEM((t, D_SSD), F32),
            pltpu.VMEM((n_chunks, D_BC, D_SSD), F32),
            pltpu.VMEM((D_BC, D_SSD), F32),
            pltpu.VMEM((D_BC, D_SSD), F32),
        ],
        compiler_params=pltpu.CompilerParams(
            dimension_semantics=("arbitrary",), vmem_limit_bytes=VMEM_LIMIT),
        name="ssd",
    )(z, xbc, dt, conv_w, conv_b, dt_bias, a_log, d_skip, norm_g)


def _post_kernel(att_ref, ret_ref, ssd_ref, xc_ref, xl_ref, mod_ref, ng_ref, wo_ref, w1_ref, w2_ref,
                 o_ref, *, ff_chunk, n_ctx_tiles, first):
    x = jnp.where(pl.program_id(1) + first < n_ctx_tiles, xc_ref[0], xl_ref[0])
    mix = jnp.concatenate([att_ref[0], ret_ref[0], ssd_ref[0]], axis=-1)
    o = _dot(mix, wo_ref[...])
    x1 = x + mod_ref[0, 2:3, :] * _rms(o, ng_ref[1:2, :])
    h2 = _rms(x1, ng_ref[2:3, :]) * (1.0 + mod_ref[0, 4:5, :]) + mod_ref[0, 3:4, :]
    h2 = h2.astype(BF16)
    d_ff = w1_ref.shape[1]
    acc = jnp.zeros(x1.shape, F32)
    for j in range(d_ff // ff_chunk):
        sl = slice(j * ff_chunk, (j + 1) * ff_chunk)
        a = jnp.maximum(_dot(h2, w1_ref[:, sl]), 0.0)
        acc = acc + _dot((a * a).astype(BF16), w2_ref[sl, :])
    o_ref[0] = x1 + mod_ref[0, 5:6, :] * _rms(acc, ng_ref[3:4, :])


def _post(att, ret, ssd, x_ctx, x_lat, lat_off, mods, ng, w_out, w_ff1, w_ff2, n_ctx_tiles, with_ctx):
    bsz, t, _ = ret.shape
    d = x_lat.shape[2]
    d_ff = w_ff1.shape[1]
    first = 0 if with_ctx else n_ctx_tiles
    nt = t // ROW_TILE - first
    assert att.shape[1] == nt * ROW_TILE
    ctx_row = mods.shape[0] - 1

    def mod_map(b, i):
        return (jnp.where(i + first < n_ctx_tiles, ctx_row, b), 0, 0)

    tok = lambda w: pl.BlockSpec((1, ROW_TILE, w), lambda b, i: (b, i + first, 0))
    const = lambda shape: pl.BlockSpec(shape, lambda b, i: (0,) * len(shape),
                                       pipeline_mode=pl.Buffered(1))
    return pl.pallas_call(
        functools.partial(_post_kernel, ff_chunk=1024, n_ctx_tiles=n_ctx_tiles, first=first),
        grid=(bsz, nt),
        in_specs=[pl.BlockSpec((1, ROW_TILE, D_ATT), lambda b, i: (b, i, 0)), tok(D_RET), tok(D_SSD),
                  *_token_specs(d, n_ctx_tiles, lat_off, first),
                  pl.BlockSpec((1, 6, d), mod_map),
                  const((4, d)), const((d, d)), const((d, d_ff)), const((d_ff, d))],
        out_specs=pl.BlockSpec((1, ROW_TILE, d), lambda b, i: (b, i, 0)),
        out_shape=jax.ShapeDtypeStruct((bsz, nt * ROW_TILE, d), F32),
        compiler_params=pltpu.CompilerParams(
            dimension_semantics=("arbitrary", "arbitrary"), vmem_limit_bytes=VMEM_LIMIT),
        name="post",
    )(att, ret, ssd, x_ctx, x_lat, mods, ng, w_out, w_ff1, w_ff2)


def _rope_tables(n, n_ctx):
    rows = n // GRID_W
    row = jnp.broadcast_to(jnp.arange(rows)[:, None], (rows, GRID_W)).reshape(n)
    col = jnp.broadcast_to(jnp.arange(GRID_W)[None, :], (rows, GRID_W)).reshape(n)
    half = HEAD_DIM // 2
    inv_freq = ROPE_THETA ** (-jnp.arange(0, half, 2, dtype=F32) / half)
    ang = jnp.stack([row, col], axis=-1).astype(F32)[:, :, None] * inv_freq
    ang = jnp.concatenate([ang, ang], axis=-1).reshape(n, HEAD_DIM)
    ang = jnp.tile(ang, (1, LANES // HEAD_DIM))
    cos = jnp.cos(ang)
    sin = jnp.sin(ang)
    upper = (jnp.arange(LANES) % half) >= half // 2
    sin_p = jnp.where(upper, sin, 0.0)
    sin_m = jnp.where(upper, 0.0, -sin)
    ident = lambda v, a: jnp.concatenate([jnp.full((n_ctx, LANES), v, F32), a], axis=0)
    return ident(1.0, cos), ident(0.0, sin_p), ident(0.0, sin_m)


def kernel(x, c, ctx, c_ctx, w_mod, b_mod, norm_g, w_in, w_out, q_norm_g, k_norm_g, ret_decay_logit,
           ret_gn_g, ret_gn_b, ssd_conv_w, ssd_conv_b, ssd_dt_bias, ssd_a_log, ssd_d, ssd_norm_g,
           w_ff1, w_ff2):
    bsz, n, d = x.shape
    n_ctx = ctx.shape[1]
    depth = w_mod.shape[0]
    assert n % ROW_TILE == 0 and n_ctx % ROW_TILE == 0 and n_ctx % CHUNK == 0 and n % CHUNK == 0
    n_ctx_tiles = n_ctx // ROW_TILE

    n_rows = -(-(bsz + 1) // 8) * 8
    cc = jnp.concatenate([c, c_ctx[None, :], jnp.zeros((n_rows - bsz - 1, d), F32)], axis=0)
    mods = _modulation(cc, w_mod, b_mod)[:, :bsz + 1].reshape(depth, bsz + 1, 6, d)

    cos, sin_p, sin_m = _rope_tables(n, n_ctx)
    n_dt = 2 * H_SSD
    pad_dt = lambda a: jnp.pad(a.reshape(1, n_dt), ((0, 0), (0, D_DT - n_dt)))

    x_ctx, x_lat, lat_off = ctx, x, n_ctx_tiles
    for layer in range(depth):
        last = layer == depth - 1
        wi = jnp.pad(w_in[layer], ((0, 0), (0, D_IN_PAD - w_in.shape[2]))).astype(BF16)
        wo = w_out[layer].astype(BF16)
        gqk = jnp.concatenate([jnp.tile(q_norm_g[layer], H_ATT) * (HEAD_DIM ** -0.5 * LOG2_E),
                               jnp.tile(k_norm_g[layer], H_KV)])[None, :]

        q, k, v, ret, z, xbc, dt = _inproj(x_ctx, x_lat, lat_off, n_ctx + n, mods[layer],
                                           norm_g[layer], wi, cos, sin_p, sin_m, gqk, n_ctx_tiles)
        att = _attention(q, k, v, n_ctx, with_ctx=not last)
        lg = jnp.repeat(ret_decay_logit[layer], HEAD_DIM, axis=1)
        ret_o = _retention(ret, lg, ret_gn_g[layer][None, :], ret_gn_b[layer][None, :], n_ctx)
        conv_w = jnp.pad(ssd_conv_w[layer], ((0, 8 - SSD_CONV), (0, 0)))
        ssd_o = _ssd(z, xbc, dt, conv_w, ssd_conv_b[layer][None, :], pad_dt(ssd_dt_bias[layer]),
                     pad_dt(ssd_a_log[layer]), jnp.repeat(ssd_d[layer], HEAD_DIM)[None, :],
                     ssd_norm_g[layer][None, :], n_ctx)
        xall = _post(att, ret_o, ssd_o, x_ctx, x_lat, lat_off, mods[layer], norm_g[layer], wo,
                     w_ff1[layer].astype(BF16), w_ff2[layer].astype(BF16), n_ctx_tiles,
                     with_ctx=not last)
        x_ctx, x_lat, lat_off = xall, xall, 0
    return xall
```

```python
import functools

import jax
import jax.numpy as jnp
from jax import lax
from jax.experimental import pallas as pl
from jax.experimental.pallas import tpu as pltpu

F32 = jnp.float32
BF16 = jnp.bfloat16

HEAD_DIM = 64
H_ATT = 6
H_KV = 2
H_RET = 4
H_SSD = 6
SSD_GROUPS = 2
SSD_STATE = 128
SSD_CONV = 5
GRID_W = 64
ROPE_THETA = 10000.0
EPS = 1e-6
LOG2_E = 1.4426950408889634

D_ATT = H_ATT * HEAD_DIM
D_KV = H_KV * HEAD_DIM
D_RET = H_RET * HEAD_DIM
D_SSD = H_SSD * HEAD_DIM
D_BC = SSD_GROUPS * SSD_STATE
D_XBC = D_SSD + 2 * D_BC
LANES = 128
D_DT = LANES
OFF_Q = 0
OFF_K = OFF_Q + D_ATT
OFF_V = OFF_K + D_KV
OFF_RET = OFF_V + D_KV
OFF_Z = OFF_RET + 4 * D_RET
OFF_XBC = OFF_Z + D_SSD
OFF_DT = OFF_XBC + D_XBC
D_IN_PAD = OFF_DT + D_DT

ROW_TILE = 256
CHUNK = 256
VMEM_LIMIT = 56 * 1024 * 1024


def _silu(x):
    return x * jax.nn.sigmoid(x)


def _split2(a):
    hi = a.astype(BF16)
    lo = (a - hi.astype(F32)).astype(BF16)
    return hi, lo


def _dot(a, b):
    return jnp.dot(a, b, preferred_element_type=F32)


def _dot_nt(a, b):
    return lax.dot_general(a, b, (((1,), (1,)), ((), ())), preferred_element_type=F32)


def _dot2_right(a, m):
    hi, lo = _split2(a)
    return _dot(hi, m) + _dot(lo, m)


def _dot2_left(m, a):
    hi, lo = _split2(a)
    return _dot(m, hi) + _dot(m, lo)


def _rms(x, g):
    ms = jnp.mean(x * x, axis=-1, keepdims=True)
    return x * lax.rsqrt(ms + EPS) * g


def _head_avg_matrix(n):
    r = lax.broadcasted_iota(jnp.int32, (n, n), 0) // HEAD_DIM
    c = lax.broadcasted_iota(jnp.int32, (n, n), 1) // HEAD_DIM
    return jnp.where(r == c, 1.0 / HEAD_DIM, 0.0).astype(BF16)


def _mod_kernel(c_ref, w_ref, b_ref, o_ref):
    sc = _silu(c_ref[...]).astype(BF16)
    o_ref[0] = _dot(sc, w_ref[0].astype(BF16)) + b_ref[0]


def _modulation(cc, w_mod, b_mod):
    depth, d, n = w_mod.shape
    rows = cc.shape[0]
    tn = 1536
    return pl.pallas_call(
        _mod_kernel,
        grid=(depth, n // tn),
        in_specs=[
            pl.BlockSpec((rows, d), lambda l, j: (0, 0)),
            pl.BlockSpec((1, d, tn), lambda l, j: (l, 0, j)),
            pl.BlockSpec((1, 1, tn), lambda l, j: (l, 0, j)),
        ],
        out_specs=pl.BlockSpec((1, rows, tn), lambda l, j: (l, 0, j)),
        out_shape=jax.ShapeDtypeStruct((depth, rows, n), F32),
        compiler_params=pltpu.CompilerParams(
            dimension_semantics=("arbitrary", "arbitrary"), vmem_limit_bytes=VMEM_LIMIT),
        name="modulation",
    )(cc, w_mod, b_mod.reshape(depth, 1, n))


def _mod_chunk(mod_ref, row, k):
    d = mod_ref.shape[1] // 6
    return mod_ref[pl.ds(row, 1), k * d:(k + 1) * d]


def _rope(x, cos, sin_p, sin_m):
    return x * cos + pltpu.roll(x, 16, 1) * sin_p + pltpu.roll(x, LANES - 16, 1) * sin_m


def _inproj_kernel(xc_ref, xl_ref, mod_ref, ng_ref, w_ref, cos_ref, sp_ref, sm_ref, gqk_ref,
                   qt_ref, k_ref, v_ref, ret_ref, z_ref, xbc_ref, dt_ref, *, n_ctx_tiles, ctx_row):
    is_ctx = pl.program_id(1) < n_ctx_tiles
    x = jnp.where(is_ctx, xc_ref[0], xl_ref[0])
    mod = functools.partial(_mod_chunk, mod_ref, jnp.where(is_ctx, ctx_row, pl.program_id(0)))
    h = _rms(x, ng_ref[0:1, :])
    h = h * (1.0 + mod(1)) + mod(0)
    acc = _dot(h.astype(BF16), w_ref[...])

    cos = cos_ref[...]
    sin_p = sp_ref[...]
    sin_m = sm_ref[...]

    nqk = D_ATT + D_KV
    qk = acc[:, OFF_Q:OFF_Q + nqk]
    ms = _dot2_right(qk * qk, _head_avg_matrix(nqk))
    qk = qk * lax.rsqrt(ms + EPS) * gqk_ref[...]
    for i in range(D_ATT // LANES):
        sl = slice(i * LANES, (i + 1) * LANES)
        qt_ref[0, sl, :] = _rope(qk[:, sl], cos, sin_p, sin_m).T.astype(BF16)
    k_ref[0] = _rope(qk[:, D_ATT:nqk], cos, sin_p, sin_m).astype(BF16)
    v_ref[0] = acc[:, OFF_V:OFF_V + D_KV].astype(BF16)

    kscale = HEAD_DIM ** -0.5
    for i in range(2 * D_RET // LANES):
        sl = slice(OFF_RET + i * LANES, OFF_RET + (i + 1) * LANES)
        r = _rope(acc[:, sl], cos, sin_p, sin_m)
        if i >= D_RET // LANES:
            r = r * kscale
        ret_ref[0, :, i * LANES:(i + 1) * LANES] = r.astype(BF16)
    ret_ref[0, :, 2 * D_RET:4 * D_RET] = acc[:, OFF_RET + 2 * D_RET:OFF_RET + 4 * D_RET].astype(BF16)
    z_ref[0] = acc[:, OFF_Z:OFF_Z + D_SSD].astype(BF16)
    xbc_ref[0] = acc[:, OFF_XBC:OFF_XBC + D_XBC].astype(BF16)
    dt_ref[0] = acc[:, OFF_DT:OFF_DT + D_DT]


def _token_specs(d, n_ctx_tiles, lat_off, first=0):
    ctx_spec = pl.BlockSpec((1, ROW_TILE, d),
                            lambda b, i: (b, jnp.minimum(i + first, n_ctx_tiles - 1), 0))
    lat_spec = pl.BlockSpec((1, ROW_TILE, d),
                            lambda b, i: (b, jnp.maximum(i + first, n_ctx_tiles) - lat_off, 0))
    return ctx_spec, lat_spec


def _inproj(x_ctx, x_lat, lat_off, t, mods, layer, ng, w_in, cos, sin_p, sin_m, gqk, n_ctx_tiles):
    bsz, _, d = x_lat.shape
    nt = t // ROW_TILE
    mod_spec = pl.BlockSpec((None,) + mods.shape[1:], lambda b, i: (layer, 0, 0))

    tok = lambda w: pl.BlockSpec((1, ROW_TILE, w), lambda b, i: (b, i, 0))
    rope_spec = pl.BlockSpec((ROW_TILE, LANES), lambda b, i: (i, 0))
    const = lambda shape: pl.BlockSpec(shape, lambda b, i: (0,) * len(shape))
    widths = (D_KV, D_KV, 4 * D_RET, D_SSD, D_XBC)
    qt_spec = pl.BlockSpec((1, D_ATT, ROW_TILE), lambda b, i: (b, 0, i))
    return pl.pallas_call(
        functools.partial(_inproj_kernel, n_ctx_tiles=n_ctx_tiles, ctx_row=bsz),
        grid=(bsz, nt),
        in_specs=[
            *_token_specs(d, n_ctx_tiles, lat_off),
            mod_spec,
            const((4, d)),
            const((d, D_IN_PAD)),
            rope_spec, rope_spec, rope_spec,
            const((1, D_ATT + D_KV)),
        ],
        out_specs=[qt_spec] + [tok(w) for w in widths] + [tok(D_DT)],
        out_shape=[jax.ShapeDtypeStruct((bsz, D_ATT, t), BF16)]
        + [jax.ShapeDtypeStruct((bsz, t, w), BF16) for w in widths]
        + [jax.ShapeDtypeStruct((bsz, t, D_DT), F32)],
        compiler_params=pltpu.CompilerParams(
            dimension_semantics=("arbitrary", "arbitrary"), vmem_limit_bytes=VMEM_LIMIT),
        name="inproj",
    )(x_ctx, x_lat, mods, ng, w_in, cos, sin_p, sin_m, gqk)


V_ROWS = 80
KEY_BLOCK = 256
NEG_BIG = -1e30


def _attn_kernel(qt_ref, k_ref, v_ref, o_ref, km_ref, vt_ref, acc_ref, m_ref, alpha_ref, s_ref, p_ref,
                 *, n_ctx, with_ctx):
    qi = pl.program_id(1)
    per_tile = LANES // HEAD_DIM
    n_tiles = D_ATT // LANES

    @pl.when(qi == 0)
    def _():
        k = k_ref[0].astype(F32)
        kr = pltpu.roll(k, HEAD_DIM, 1)
        low = lax.broadcasted_iota(jnp.int32, k.shape, 1) < HEAD_DIM
        km_ref[0] = jnp.where(low, k, 0.0).astype(BF16)
        km_ref[1] = jnp.where(low, 0.0, kr).astype(BF16)
        km_ref[2] = jnp.where(low, kr, 0.0).astype(BF16)
        km_ref[3] = jnp.where(low, 0.0, k).astype(BF16)
        vt = v_ref[0].astype(F32).T
        row = lax.broadcasted_iota(jnp.int32, (V_ROWS - HEAD_DIM, vt.shape[1]), 0)
        tail = jnp.where(row == 0, 1.0, 0.0)
        for g in range(H_KV):
            vg = jnp.concatenate([vt[g * HEAD_DIM:(g + 1) * HEAD_DIM], tail], axis=0).astype(BF16)
            for blk in range(vt.shape[1] // KEY_BLOCK):
                vt_ref[g, blk] = vg[:, blk * KEY_BLOCK:(blk + 1) * KEY_BLOCK]

    kv_of = lambda h: h // (H_ATT // H_KV)

    def scores(i):
        for j in range(n_tiles):
            ks = jnp.concatenate(
                [km_ref[2 * kv_of(j * per_tile + half) + half, i * KEY_BLOCK:(i + 1) * KEY_BLOCK, :]
                 for half in range(per_tile)], axis=0)
            s_ref[i % 2, j] = _dot(ks, qt_ref[0, j * LANES:(j + 1) * LANES, :])

    def exponentials(i):
        for h in range(H_ATT):
            j, half = divmod(h, per_tile)
            s = s_ref[i % 2, j, half * KEY_BLOCK:(half + 1) * KEY_BLOCK, :]
            m_old = m_ref[h]
            m_new = jnp.maximum(m_old, jnp.max(s, axis=0, keepdims=True))
            m_ref[h] = m_new
            alpha_ref[i % 2, h] = jnp.exp2(m_old - m_new)
            p_ref[i % 2, h] = jnp.exp2(s - m_new).astype(BF16)

    def weighted_values(i):
        for h in range(H_ATT):
            pv = _dot(vt_ref[kv_of(h), i], p_ref[i % 2, h])
            acc_ref[h] = acc_ref[h] * alpha_ref[i % 2, h] + pv

    def run(n_blocks):
        acc_ref[...] = jnp.zeros_like(acc_ref)
        m_ref[...] = jnp.full(m_ref.shape, NEG_BIG, F32)
        for it in range(n_blocks + 2):
            if it >= 2:
                weighted_values(it - 2)
            if it < n_blocks:
                scores(it)
            if 1 <= it <= n_blocks:
                exponentials(it - 1)
        for j in range(n_tiles):
            halves = []
            for half in range(per_tile):
                acc = acc_ref[j * per_tile + half]
                halves.append(acc[0:HEAD_DIM] * (1.0 / acc[HEAD_DIM:HEAD_DIM + 1]))
            ot = jnp.concatenate(halves, axis=0)
            o_ref[0, :, j * LANES:(j + 1) * LANES] = ot.T.astype(BF16)

    n_all = k_ref.shape[1] // KEY_BLOCK
    if with_ctx:
        @pl.when(qi == 0)
        def _():
            run(n_ctx // KEY_BLOCK)

        @pl.when(qi > 0)
        def _():
            run(n_all)
    else:
        run(n_all)


def _attention(qt, k, v, n_ctx, with_ctx):
    bsz, _, t = qt.shape
    tq = ROW_TILE
    assert n_ctx == tq and n_ctx % KEY_BLOCK == 0 and t % KEY_BLOCK == 0
    first = 0 if with_ctx else n_ctx // tq
    nq = t // tq - first
    return pl.pallas_call(
        functools.partial(_attn_kernel, n_ctx=n_ctx, with_ctx=with_ctx),
        grid=(bsz, nq),
        in_specs=[
            pl.BlockSpec((1, D_ATT, tq), lambda b, i: (b, 0, i + first)),
            pl.BlockSpec((1, t, D_KV), lambda b, i: (b, 0, 0)),
            pl.BlockSpec((1, t, D_KV), lambda b, i: (b, 0, 0)),
        ],
        out_specs=pl.BlockSpec((1, tq, D_ATT), lambda b, i: (b, i, 0)),
        out_shape=jax.ShapeDtypeStruct((bsz, nq * tq, D_ATT), BF16),
        scratch_shapes=[pltpu.VMEM((2 * H_KV, t, D_KV), BF16),
                        pltpu.VMEM((H_KV, t // KEY_BLOCK, V_ROWS, KEY_BLOCK), BF16),
                        pltpu.VMEM((H_ATT, V_ROWS, tq), F32),
                        pltpu.VMEM((H_ATT, 1, tq), F32),
                        pltpu.VMEM((2, H_ATT, 1, tq), F32),
                        pltpu.VMEM((2, D_ATT // LANES, (LANES // HEAD_DIM) * KEY_BLOCK, tq), F32),
                        pltpu.VMEM((2, H_ATT, KEY_BLOCK, tq), BF16)],
        compiler_params=pltpu.CompilerParams(
            dimension_semantics=("arbitrary", "arbitrary"), vmem_limit_bytes=VMEM_LIMIT),
        name="attention",
    )(qt, k, v)


def _log_sigmoid(x):
    return jnp.minimum(x, 0.0) - jnp.log1p(jnp.exp(-jnp.abs(x)))


def _ret_kernel(r_ref, lg_ref, gng_ref, gnb_ref, o_ref, y_ref, sf_ref, sb_ref, w_ref, dm_ref,
                *, n_ctx_chunks):
    c_len = CHUNK
    t = r_ref.shape[1]
    n_chunks = t // c_len
    lg = _log_sigmoid(lg_ref[...])
    lgf = lg[0:1, :]
    lgb = lg[1:2, :]
    dec_f = jnp.exp(c_len * lgf)
    dec_b = jnp.exp(c_len * lgb)

    @pl.when(pl.program_id(0) == 0)
    def _():
        tcol = lax.broadcasted_iota(jnp.int32, (c_len, 1), 0).astype(F32)
        w_ref[0] = jnp.exp((tcol + 1.0) * lgf)
        w_ref[1] = jnp.exp((c_len - tcol) * lgb)
        w_ref[2] = jnp.exp((c_len - 1.0 - tcol) * lgf)
        w_ref[3] = jnp.exp(tcol * lgb)
        ti = lax.broadcasted_iota(jnp.int32, (c_len, c_len), 0)
        si = lax.broadcasted_iota(jnp.int32, (c_len, c_len), 1)
        diff = (ti - si).astype(F32)
        for h in range(H_RET):
            lf = lgf[:, h * HEAD_DIM:h * HEAD_DIM + 1]
            lb = lgb[:, h * HEAD_DIM:h * HEAD_DIM + 1]
            dm_ref[h] = jnp.exp(jnp.where(diff >= 0, diff * lf, -diff * lb))

    lane = lax.broadcasted_iota(jnp.int32, (c_len, D_RET), 1)
    srow = lax.broadcasted_iota(jnp.int32, (D_RET, D_RET), 0) // HEAD_DIM
    scol = lax.broadcasted_iota(jnp.int32, (D_RET, D_RET), 1) // HEAD_DIM
    smask = srow == scol
    avg = _head_avg_matrix(D_RET)

    def chunk_rows(c):
        return pl.ds(pl.multiple_of(c * c_len, c_len), c_len)

    def state_delta(k, v, wk):
        kw = (k.astype(F32) * wk).T.astype(BF16)
        return jnp.where(smask, _dot(kw, v), 0.0)

    sf_ref[...] = jnp.zeros_like(sf_ref)
    sb_ref[...] = jnp.zeros_like(sb_ref)

    def fwd(c, carry):
        rows = chunk_rows(c)
        q = r_ref[0, rows, 0:D_RET]
        k = r_ref[0, rows, D_RET:2 * D_RET]
        v = r_ref[0, rows, 2 * D_RET:3 * D_RET]
        zero = jnp.zeros_like(q)
        y = w_ref[0] * _dot(q, sf_ref[...].astype(BF16))
        for h in range(H_RET):
            hm = (lane >= h * HEAD_DIM) & (lane < (h + 1) * HEAD_DIM)
            s = _dot_nt(jnp.where(hm, q, zero), k) * dm_ref[h]
            y = y + _dot(s.astype(BF16), jnp.where(hm, v, zero))
        y_ref[rows, :] = y
        sf_ref[...] = dec_f * sf_ref[...] + state_delta(k, v, w_ref[2])
        return carry

    lax.fori_loop(0, n_chunks, fwd, 0)

    def bwd(c):
        rows = chunk_rows(c)
        q = r_ref[0, rows, 0:D_RET]
        k = r_ref[0, rows, D_RET:2 * D_RET]
        v = r_ref[0, rows, 2 * D_RET:3 * D_RET]
        g = r_ref[0, rows, 3 * D_RET:4 * D_RET].astype(F32)
        y = y_ref[rows, :] + w_ref[1] * _dot(q, sb_ref[...].astype(BF16))
        sb_ref[...] = dec_b * sb_ref[...] + state_delta(k, v, w_ref[3])
        mu = _dot2_right(y, avg)
        d = y - mu
        var = _dot2_right(d * d, avg)
        yn = d * lax.rsqrt(var + EPS) * gng_ref[...] + gnb_ref[...]
        o_ref[0, rows, :] = (yn * _silu(g)).astype(BF16)

    def bwd_ctx(i, carry):
        bwd(n_ctx_chunks - 1 - i)
        return carry

    def bwd_lat(i, carry):
        bwd(n_chunks - 1 - i)
        return carry

    lax.fori_loop(0, n_ctx_chunks, bwd_ctx, 0)
    lax.fori_loop(0, n_chunks - n_ctx_chunks, bwd_lat, 0)


def _retention(ret, lg, gng, gnb, n_ctx):
    bsz, t, _ = ret.shape
    const = lambda shape: pl.BlockSpec(shape, lambda b: (0,) * len(shape))
    return pl.pallas_call(
        functools.partial(_ret_kernel, n_ctx_chunks=n_ctx // CHUNK),
        grid=(bsz,),
        in_specs=[
            pl.BlockSpec((1, t, 4 * D_RET), lambda b: (b, 0, 0)),
            const((2, D_RET)), const((1, D_RET)), const((1, D_RET)),
        ],
        out_specs=pl.BlockSpec((1, t, D_RET), lambda b: (b, 0, 0)),
        out_shape=jax.ShapeDtypeStruct((bsz, t, D_RET), BF16),
        scratch_shapes=[pltpu.VMEM((t, D_RET), F32), pltpu.VMEM((D_RET, D_RET), F32),
                        pltpu.VMEM((D_RET, D_RET), F32), pltpu.VMEM((4, CHUNK, D_RET), F32),
                        pltpu.VMEM((H_RET, CHUNK, CHUNK), F32)],
        compiler_params=pltpu.CompilerParams(
            dimension_semantics=("arbitrary",), vmem_limit_bytes=VMEM_LIMIT),
        name="retention",
    )(ret, lg, gng, gnb)


def _softplus(x):
    return jnp.maximum(x, 0.0) + jnp.log1p(jnp.exp(-jnp.abs(x)))


def _ssd_kernel(z_ref, xbc_ref, dt_ref, cw_ref, cb_ref, dtb_ref, alog_ref, dskip_ref, ng_ref,
                o_ref, xpad_ref, xc_ref, y_ref, eb_ref, dsb_ref, sf_ref, sb_ref, *, n_ctx):
    c_len = CHUNK
    t = xbc_ref.shape[1]
    n_chunks = t // c_len
    n_ctx_chunks = n_ctx // c_len
    pad = 8
    half = SSD_CONV // 2

    zpad = jnp.zeros((pad, D_XBC), F32)
    xpad_ref[0:pad, :] = zpad
    xpad_ref[pad + n_ctx:2 * pad + n_ctx, :] = zpad
    xpad_ref[2 * pad + t:3 * pad + t, :] = zpad

    def fill(c, carry):
        src = pl.ds(pl.multiple_of(c * c_len, c_len), c_len)
        off = jnp.where(c < n_ctx_chunks, pad, 2 * pad)
        dst = pl.ds(pl.multiple_of(c * c_len + off, pad), c_len)
        xpad_ref[dst, :] = xbc_ref[0, src, :].astype(F32)
        return carry

    lax.fori_loop(0, n_chunks, fill, 0)

    def conv_chunk(c):
        c = jnp.asarray(c, jnp.int32)
        off = jnp.where(c < n_ctx_chunks, pad, 2 * pad)
        base = pl.multiple_of(c * c_len + off, pad)
        rows = pl.ds(pl.multiple_of(c * c_len, c_len), c_len)
        for i in range(D_XBC // LANES):
            ln = slice(i * LANES, (i + 1) * LANES)
            win = xpad_ref[pl.ds(base - pad, c_len + 2 * pad), ln]
            acc = jnp.broadcast_to(cb_ref[:, ln], (c_len, LANES))
            for j in range(SSD_CONV):
                acc = acc + cw_ref[j:j + 1, ln] * win[pad + j - half:pad + j - half + c_len]
            xc_ref[rows, ln] = _silu(acc).astype(BF16)

    conv_chunk(0)

    nd = 2 * H_SSD
    lane_dt = lax.broadcasted_iota(jnp.int32, (1, D_DT), 1)
    a_vec = jnp.where(lane_dt < nd, -jnp.exp(alog_ref[...]), 0.0)
    is_f = lane_dt < H_SSD
    ti = lax.broadcasted_iota(jnp.int32, (c_len, c_len), 0)
    si = lax.broadcasted_iota(jnp.int32, (c_len, c_len), 1)
    causal = si <= ti
    tri_l = jnp.where(causal, 1.0, 0.0).astype(BF16)
    tri_u = jnp.where(si >= ti, 1.0, 0.0).astype(BF16)
    er = lax.broadcasted_iota(jnp.int32, (D_DT, 2 * D_SSD), 0)
    ec = lax.broadcasted_iota(jnp.int32, (D_DT, 2 * D_SSD), 1) // HEAD_DIM
    expand = jnp.where(er == ec, 1.0, 0.0).astype(BF16)
    lane_x = lax.broadcasted_iota(jnp.int32, (c_len, D_SSD), 1)
    lane_c = lax.broadcasted_iota(jnp.int32, (c_len, D_BC), 1)
    srow = lax.broadcasted_iota(jnp.int32, (D_BC, D_SSD), 0) // SSD_STATE
    scol = lax.broadcasted_iota(jnp.int32, (D_BC, D_SSD), 1) // (D_SSD // SSD_GROUPS)
    smask = srow == scol
    heads_per_group = H_SSD // SSD_GROUPS

    def chunk_rows(c):
        return pl.ds(pl.multiple_of(c * c_len, c_len), c_len)

    sf_ref[...] = jnp.zeros_like(sf_ref)
    sb_ref[...] = jnp.zeros_like(sb_ref)

    def fwd(c, carry):
        conv_chunk(jnp.minimum(c + 1, n_chunks - 1))
        rows = chunk_rows(c)
        xs = xc_ref[rows, 0:D_SSD]
        bm = xc_ref[rows, D_SSD:D_SSD + D_BC]
        cm = xc_ref[rows, D_SSD + D_BC:D_XBC]
        xs_f = xs.astype(F32)
        dt = _softplus(dt_ref[0, rows, :] + dtb_ref[...])
        la = dt * a_vec
        cum = jnp.where(is_f, _dot2_left(tri_l, la), _dot2_left(tri_u, la))
        cum_t = cum.T
        dt_t = dt.T
        edge = jnp.where(is_f, cum[c_len - 1:c_len, :], cum[0:1, :])
        wk = jnp.exp(edge - cum) * dt
        cum_e = _dot2_right(cum, expand)
        wk_e = _dot2_right(wk, expand)
        e_f = jnp.exp(cum_e[:, 0:D_SSD])
        e_b = jnp.exp(cum_e[:, D_SSD:2 * D_SSD])
        eb_ref[rows, :] = e_b

        y = e_f * _dot(cm, sf_ref[...].astype(BF16))
        zc = jnp.zeros_like(cm)
        zx = jnp.zeros_like(xs)
        for g in range(SSD_GROUPS):
            gm = (lane_c >= g * SSD_STATE) & (lane_c < (g + 1) * SSD_STATE)
            gmat = _dot_nt(jnp.where(gm, cm, zc), bm)
            for hh in range(heads_per_group):
                h = g * heads_per_group + hh
                d_f = cum[:, h:h + 1] - cum_t[h:h + 1, :]
                d_b = cum[:, H_SSD + h:H_SSD + h + 1] - cum_t[H_SSD + h:H_SSD + h + 1, :]
                w = jnp.where(causal, dt_t[h:h + 1, :], dt_t[H_SSD + h:H_SSD + h + 1, :])
                m = jnp.exp(jnp.where(causal, d_f, d_b)) * w
                hm = (lane_x >= h * HEAD_DIM) & (lane_x < (h + 1) * HEAD_DIM)
                y = y + _dot((gmat * m).astype(BF16), jnp.where(hm, xs, zx))
        y_ref[rows, :] = y

        bm_t = bm.astype(F32).T.astype(BF16)
        ds_f = _dot(bm_t, (xs_f * wk_e[:, 0:D_SSD]).astype(BF16))
        ds_b = _dot(bm_t, (xs_f * wk_e[:, D_SSD:2 * D_SSD]).astype(BF16))
        sf_ref[...] = e_f[c_len - 1:c_len, :] * sf_ref[...] + jnp.where(smask, ds_f, 0.0)
        dsb_ref[c] = jnp.where(smask, ds_b, 0.0)
        return carry

    lax.fori_loop(0, n_chunks, fwd, 0)

    def bwd(c):
        rows = chunk_rows(c)
        xs_f = xc_ref[rows, 0:D_SSD].astype(F32)
        cm = xc_ref[rows, D_SSD + D_BC:D_XBC]
        e_b = eb_ref[rows, :]
        y = y_ref[rows, :] + e_b * _dot(cm, sb_ref[...].astype(BF16))
        sb_ref[...] = e_b[0:1, :] * sb_ref[...] + dsb_ref[c]
        y = y + dskip_ref[...] * xs_f
        u = y * _silu(z_ref[0, rows, :].astype(F32))
        o_ref[0, rows, :] = _rms(u, ng_ref[...]).astype(BF16)

    def bwd_ctx(i, carry):
        bwd(n_ctx_chunks - 1 - i)
        return carry

    def bwd_lat(i, carry):
        bwd(n_chunks - 1 - i)
        return carry

    lax.fori_loop(0, n_ctx_chunks, bwd_ctx, 0)
    lax.fori_loop(0, n_chunks - n_ctx_chunks, bwd_lat, 0)


def _ssd(z, xbc, dt, conv_w, conv_b, dt_bias, a_log, d_skip, norm_g, n_ctx):
    bsz, t, _ = xbc.shape
    n_chunks = t // CHUNK
    const = lambda shape: pl.BlockSpec(shape, lambda b: (0,) * len(shape))
    seq = lambda w: pl.BlockSpec((1, t, w), lambda b: (b, 0, 0))
    return pl.pallas_call(
        functools.partial(_ssd_kernel, n_ctx=n_ctx),
        grid=(bsz,),
        in_specs=[seq(D_SSD), seq(D_XBC), seq(D_DT),
                  const((8, D_XBC)), const((1, D_XBC)), const((1, D_DT)), const((1, D_DT)),
                  const((1, D_SSD)), const((1, D_SSD))],
        out_specs=seq(D_SSD),
        out_shape=jax.ShapeDtypeStruct((bsz, t, D_SSD), BF16),
        scratch_shapes=[
            pltpu.VMEM((t + 24, D_XBC), F32),
            pltpu.VMEM((t, D_XBC), BF16),
            pltpu.VMEM((t, D_SSD), F32),
            pltpu.VMEM((t, D_SSD), F32),
            pltpu.VMEM((n_chunks, D_BC, D_SSD), F32),
            pltpu.VMEM((D_BC, D_SSD), F32),
            pltpu.VMEM((D_BC, D_SSD), F32),
        ],
        compiler_params=pltpu.CompilerParams(
            dimension_semantics=("arbitrary",), vmem_limit_bytes=VMEM_LIMIT),
        name="ssd",
    )(z, xbc, dt, conv_w, conv_b, dt_bias, a_log, d_skip, norm_g)


def _post_kernel(att_ref, ret_ref, ssd_ref, xc_ref, xl_ref, mod_ref, ng_ref, wo_ref, w1_ref, w2_ref,
                 o_ref, *, ff_chunk, n_ctx_tiles, first, ctx_row):
    is_ctx = pl.program_id(1) + first < n_ctx_tiles
    x = jnp.where(is_ctx, xc_ref[0], xl_ref[0])
    mod = functools.partial(_mod_chunk, mod_ref, jnp.where(is_ctx, ctx_row, pl.program_id(0)))
    mix = jnp.concatenate([att_ref[0], ret_ref[0], ssd_ref[0]], axis=-1)
    o = _dot(mix, wo_ref[...])
    x1 = x + mod(2) * _rms(o, ng_ref[1:2, :])
    h2 = _rms(x1, ng_ref[2:3, :]) * (1.0 + mod(4)) + mod(3)
    h2 = h2.astype(BF16)
    d_ff = w1_ref.shape[1]
    acc = jnp.zeros(x1.shape, F32)
    for j in range(d_ff // ff_chunk):
        sl = slice(j * ff_chunk, (j + 1) * ff_chunk)
        a = jnp.maximum(_dot(h2, w1_ref[:, sl]), 0.0)
        acc = acc + _dot((a * a).astype(BF16), w2_ref[sl, :])
    o_ref[0] = x1 + mod(5) * _rms(acc, ng_ref[3:4, :])


def _post(att, ret, ssd, x_ctx, x_lat, lat_off, mods, layer, ng, w_out, w_ff1, w_ff2, n_ctx_tiles,
          with_ctx):
    bsz, t, _ = ret.shape
    d = x_lat.shape[2]
    d_ff = w_ff1.shape[1]
    first = 0 if with_ctx else n_ctx_tiles
    nt = t // ROW_TILE - first
    assert att.shape[1] == nt * ROW_TILE
    mod_spec = pl.BlockSpec((None,) + mods.shape[1:], lambda b, i: (layer, 0, 0))
    tok = lambda w: pl.BlockSpec((1, ROW_TILE, w), lambda b, i: (b, i + first, 0))
    const = lambda shape: pl.BlockSpec(shape, lambda b, i: (0,) * len(shape),
                                       pipeline_mode=pl.Buffered(1))
    return pl.pallas_call(
        functools.partial(_post_kernel, ff_chunk=1024, n_ctx_tiles=n_ctx_tiles, first=first,
                          ctx_row=bsz),
        grid=(bsz, nt),
        in_specs=[pl.BlockSpec((1, ROW_TILE, D_ATT), lambda b, i: (b, i, 0)), tok(D_RET), tok(D_SSD),
                  *_token_specs(d, n_ctx_tiles, lat_off, first),
                  mod_spec,
                  const((4, d)), const((d, d)), const((d, d_ff)), const((d_ff, d))],
        out_specs=pl.BlockSpec((1, ROW_TILE, d), lambda b, i: (b, i, 0)),
        out_shape=jax.ShapeDtypeStruct((bsz, nt * ROW_TILE, d), F32),
        compiler_params=pltpu.CompilerParams(
            dimension_semantics=("arbitrary", "arbitrary"), vmem_limit_bytes=VMEM_LIMIT),
        name="post",
    )(att, ret, ssd, x_ctx, x_lat, mods, ng, w_out, w_ff1, w_ff2)


def _rope_tables(n, n_ctx):
    rows = n // GRID_W
    row = jnp.broadcast_to(jnp.arange(rows)[:, None], (rows, GRID_W)).reshape(n)
    col = jnp.broadcast_to(jnp.arange(GRID_W)[None, :], (rows, GRID_W)).reshape(n)
    half = HEAD_DIM // 2
    inv_freq = ROPE_THETA ** (-jnp.arange(0, half, 2, dtype=F32) / half)
    ang = jnp.stack([row, col], axis=-1).astype(F32)[:, :, None] * inv_freq
    ang = jnp.concatenate([ang, ang], axis=-1).reshape(n, HEAD_DIM)
    ang = jnp.tile(ang, (1, LANES // HEAD_DIM))
    cos = jnp.cos(ang)
    sin = jnp.sin(ang)
    upper = (jnp.arange(LANES) % half) >= half // 2
    sin_p = jnp.where(upper, sin, 0.0)
    sin_m = jnp.where(upper, 0.0, -sin)
    ident = lambda v, a: jnp.concatenate([jnp.full((n_ctx, LANES), v, F32), a], axis=0)
    return ident(1.0, cos), ident(0.0, sin_p), ident(0.0, sin_m)


def kernel(x, c, ctx, c_ctx, w_mod, b_mod, norm_g, w_in, w_out, q_norm_g, k_norm_g, ret_decay_logit,
           ret_gn_g, ret_gn_b, ssd_conv_w, ssd_conv_b, ssd_dt_bias, ssd_a_log, ssd_d, ssd_norm_g,
           w_ff1, w_ff2):
    bsz, n, d = x.shape
    n_ctx = ctx.shape[1]
    depth = w_mod.shape[0]
    assert n % ROW_TILE == 0 and n_ctx % ROW_TILE == 0 and n_ctx % CHUNK == 0 and n % CHUNK == 0
    n_ctx_tiles = n_ctx // ROW_TILE

    n_rows = -(-(bsz + 1) // 8) * 8
    cc = jnp.concatenate([c, c_ctx[None, :], jnp.zeros((n_rows - bsz - 1, d), F32)], axis=0)
    mods = _modulation(cc, w_mod, b_mod)

    cos, sin_p, sin_m = _rope_tables(n, n_ctx)
    n_dt = 2 * H_SSD
    pad_dt = lambda a: jnp.pad(a.reshape(1, n_dt), ((0, 0), (0, D_DT - n_dt)))

    x_ctx, x_lat, lat_off = ctx, x, n_ctx_tiles
    for layer in range(depth):
        last = layer == depth - 1
        wi = jnp.pad(w_in[layer], ((0, 0), (0, D_IN_PAD - w_in.shape[2]))).astype(BF16)
        wo = w_out[layer].astype(BF16)
        gqk = jnp.concatenate([jnp.tile(q_norm_g[layer], H_ATT) * (HEAD_DIM ** -0.5 * LOG2_E),
                               jnp.tile(k_norm_g[layer], H_KV)])[None, :]

        q, k, v, ret, z, xbc, dt = _inproj(x_ctx, x_lat, lat_off, n_ctx + n, mods, layer,
                                           norm_g[layer], wi, cos, sin_p, sin_m, gqk, n_ctx_tiles)
        att = _attention(q, k, v, n_ctx, with_ctx=not last)
        lg = jnp.repeat(ret_decay_logit[layer], HEAD_DIM, axis=1)
        ret_o = _retention(ret, lg, ret_gn_g[layer][None, :], ret_gn_b[layer][None, :], n_ctx)
        conv_w = jnp.pad(ssd_conv_w[layer], ((0, 8 - SSD_CONV), (0, 0)))
        ssd_o = _ssd(z, xbc, dt, conv_w, ssd_conv_b[layer][None, :], pad_dt(ssd_dt_bias[layer]),
                     pad_dt(ssd_a_log[layer]), jnp.repeat(ssd_d[layer], HEAD_DIM)[None, :],
                     ssd_norm_g[layer][None, :], n_ctx)
        xall = _post(att, ret_o, ssd_o, x_ctx, x_lat, lat_off, mods, layer, norm_g[layer], wo,
                     w_ff1[layer].astype(BF16), w_ff2[layer].astype(BF16), n_ctx_tiles,
                     with_ctx=not last)
        x_ctx, x_lat, lat_off = xall, xall, 0
    return xall
```

```python
import functools

import jax
import jax.numpy as jnp
from jax import lax
from jax.experimental import pallas as pl
from jax.experimental.pallas import tpu as pltpu

F32 = jnp.float32
BF16 = jnp.bfloat16

HEAD_DIM = 64
H_ATT = 6
H_KV = 2
H_RET = 4
H_SSD = 6
SSD_GROUPS = 2
SSD_STATE = 128
SSD_CONV = 5
GRID_W = 64
ROPE_THETA = 10000.0
EPS = 1e-6
LOG2_E = 1.4426950408889634

D_ATT = H_ATT * HEAD_DIM
D_KV = H_KV * HEAD_DIM
D_RET = H_RET * HEAD_DIM
D_SSD = H_SSD * HEAD_DIM
D_BC = SSD_GROUPS * SSD_STATE
D_XBC = D_SSD + 2 * D_BC
LANES = 128
D_DT = LANES
OFF_Q = 0
OFF_K = OFF_Q + D_ATT
OFF_V = OFF_K + D_KV
OFF_RET = OFF_V + D_KV
OFF_Z = OFF_RET + 4 * D_RET
OFF_XBC = OFF_Z + D_SSD
OFF_DT = OFF_XBC + D_XBC
D_IN_PAD = OFF_DT + D_DT

ROW_TILE = 256
CHUNK = 256
VMEM_LIMIT = 56 * 1024 * 1024


def _silu(x):
    return x * jax.nn.sigmoid(x)


def _split2(a):
    hi = a.astype(BF16)
    lo = (a - hi.astype(F32)).astype(BF16)
    return hi, lo


def _dot(a, b):
    return jnp.dot(a, b, preferred_element_type=F32)


def _dot_nt(a, b):
    return lax.dot_general(a, b, (((1,), (1,)), ((), ())), preferred_element_type=F32)


def _dot2_right(a, m):
    hi, lo = _split2(a)
    return _dot(hi, m) + _dot(lo, m)


PART_LANES = 16


def _pack3(a, base):
    hi = a.astype(BF16).astype(F32)
    r = a - hi
    mid = r.astype(BF16).astype(F32)
    out = hi + pltpu.roll(mid, PART_LANES, 1) + pltpu.roll(r - mid, 2 * PART_LANES, 1)
    return pltpu.roll(out, base, 1) if base else out


def _unpack3(p):
    return p + pltpu.roll(p, LANES - PART_LANES, 1) + pltpu.roll(p, LANES - 2 * PART_LANES, 1)


def _rms(x, g):
    ms = jnp.mean(x * x, axis=-1, keepdims=True)
    return x * lax.rsqrt(ms + EPS) * g


def _head_avg_matrix(n):
    r = lax.broadcasted_iota(jnp.int32, (n, n), 0) // HEAD_DIM
    c = lax.broadcasted_iota(jnp.int32, (n, n), 1) // HEAD_DIM
    return jnp.where(r == c, 1.0 / HEAD_DIM, 0.0).astype(BF16)


def _mod_kernel(c_ref, w_ref, b_ref, o_ref):
    sc = _silu(c_ref[...]).astype(BF16)
    o_ref[0] = _dot(sc, w_ref[0].astype(BF16)) + b_ref[0]


def _modulation(cc, w_mod, b_mod):
    depth, d, n = w_mod.shape
    rows = cc.shape[0]
    tn = 1536
    return pl.pallas_call(
        _mod_kernel,
        grid=(depth, n // tn),
        in_specs=[
            pl.BlockSpec((rows, d), lambda l, j: (0, 0)),
            pl.BlockSpec((1, d, tn), lambda l, j: (l, 0, j)),
            pl.BlockSpec((1, 1, tn), lambda l, j: (l, 0, j)),
        ],
        out_specs=pl.BlockSpec((1, rows, tn), lambda l, j: (l, 0, j)),
        out_shape=jax.ShapeDtypeStruct((depth, rows, n), F32),
        compiler_params=pltpu.CompilerParams(
            dimension_semantics=("arbitrary", "arbitrary"), vmem_limit_bytes=VMEM_LIMIT),
        name="modulation",
    )(cc, w_mod, b_mod.reshape(depth, 1, n))


def _mod_chunk(mod_ref, row, k):
    d = mod_ref.shape[1] // 6
    return mod_ref[pl.ds(row, 1), k * d:(k + 1) * d]


def _rope(x, cos, sin_p, sin_m):
    return x * cos + pltpu.roll(x, 16, 1) * sin_p + pltpu.roll(x, LANES - 16, 1) * sin_m


def _inproj_kernel(xc_ref, xl_ref, mod_ref, ng_ref, w_ref, cos_ref, sp_ref, sm_ref, gqk_ref,
                   qt_ref, k_ref, v_ref, ret_ref, z_ref, xbc_ref, dt_ref, wbf_ref,
                   *, n_ctx_tiles, ctx_row):
    is_ctx = pl.program_id(1) < n_ctx_tiles
    x = jnp.where(is_ctx, xc_ref[0], xl_ref[0])
    mod = functools.partial(_mod_chunk, mod_ref, jnp.where(is_ctx, ctx_row, pl.program_id(0)))
    h = _rms(x, ng_ref[0:1, :])
    h = h * (1.0 + mod(1)) + mod(0)

    @pl.when((pl.program_id(0) == 0) & (pl.program_id(1) == 0))
    def _():
        n_in = w_ref.shape[1]
        full = n_in // LANES * LANES
        wbf_ref[:, 0:full] = w_ref[:, 0:full].astype(BF16)
        wbf_ref[:, full:D_IN_PAD] = jnp.zeros((w_ref.shape[0], D_IN_PAD - full), BF16)
        wbf_ref[:, full:n_in] = w_ref[:, full:n_in].astype(BF16)

    acc = _dot(h.astype(BF16), wbf_ref[...])

    cos = cos_ref[...]
    sin_p = sp_ref[...]
    sin_m = sm_ref[...]

    nqk = D_ATT + D_KV
    qk = acc[:, OFF_Q:OFF_Q + nqk]
    ms = _dot2_right(qk * qk, _head_avg_matrix(nqk))
    qk = qk * lax.rsqrt(ms + EPS) * gqk_ref[...]
    for i in range(D_ATT // LANES):
        sl = slice(i * LANES, (i + 1) * LANES)
        qt_ref[0, sl, :] = _rope(qk[:, sl], cos, sin_p, sin_m).T.astype(BF16)
    k_ref[0] = _rope(qk[:, D_ATT:nqk], cos, sin_p, sin_m).astype(BF16)
    v_ref[0] = acc[:, OFF_V:OFF_V + D_KV].astype(BF16)

    kscale = HEAD_DIM ** -0.5
    for i in range(2 * D_RET // LANES):
        sl = slice(OFF_RET + i * LANES, OFF_RET + (i + 1) * LANES)
        r = _rope(acc[:, sl], cos, sin_p, sin_m)
        if i >= D_RET // LANES:
            r = r * kscale
        ret_ref[0, :, i * LANES:(i + 1) * LANES] = r.astype(BF16)
    ret_ref[0, :, 2 * D_RET:4 * D_RET] = acc[:, OFF_RET + 2 * D_RET:OFF_RET + 4 * D_RET].astype(BF16)
    z_ref[0] = acc[:, OFF_Z:OFF_Z + D_SSD].astype(BF16)
    xbc_ref[0] = acc[:, OFF_XBC:OFF_XBC + D_XBC].astype(BF16)
    dt_ref[0] = acc[:, OFF_DT:OFF_DT + D_DT]


def _token_specs(d, n_ctx_tiles, lat_off, first=0):
    ctx_spec = pl.BlockSpec((1, ROW_TILE, d),
                            lambda b, i: (b, jnp.minimum(i + first, n_ctx_tiles - 1), 0))
    lat_spec = pl.BlockSpec((1, ROW_TILE, d),
                            lambda b, i: (b, jnp.maximum(i + first, n_ctx_tiles) - lat_off, 0))
    return ctx_spec, lat_spec


def _inproj(x_ctx, x_lat, lat_off, t, mods, layer, ng, w_in, cos, sin_p, sin_m, gqk, n_ctx_tiles):
    bsz, _, d = x_lat.shape
    nt = t // ROW_TILE
    mod_spec = pl.BlockSpec((None,) + mods.shape[1:], lambda b, i: (layer, 0, 0))

    tok = lambda w: pl.BlockSpec((1, ROW_TILE, w), lambda b, i: (b, i, 0))
    rope_spec = pl.BlockSpec((ROW_TILE, LANES), lambda b, i: (i, 0))
    const = lambda shape: pl.BlockSpec(shape, lambda b, i: (0,) * len(shape))
    widths = (D_KV, D_KV, 4 * D_RET, D_SSD, D_XBC)
    qt_spec = pl.BlockSpec((1, D_ATT, ROW_TILE), lambda b, i: (b, 0, i))
    return pl.pallas_call(
        functools.partial(_inproj_kernel, n_ctx_tiles=n_ctx_tiles, ctx_row=bsz),
        grid=(bsz, nt),
        in_specs=[
            *_token_specs(d, n_ctx_tiles, lat_off),
            mod_spec,
            const((4, d)),
            pl.BlockSpec((None,) + w_in.shape[1:], lambda b, i: (layer, 0, 0),
                         pipeline_mode=pl.Buffered(1)),
            rope_spec, rope_spec, rope_spec,
            const((1, D_ATT + D_KV)),
        ],
        out_specs=[qt_spec] + [tok(w) for w in widths] + [tok(D_DT)],
        out_shape=[jax.ShapeDtypeStruct((bsz, D_ATT, t), BF16)]
        + [jax.ShapeDtypeStruct((bsz, t, w), BF16) for w in widths]
        + [jax.ShapeDtypeStruct((bsz, t, D_DT), F32)],
        scratch_shapes=[pltpu.VMEM((d, D_IN_PAD), BF16)],
        compiler_params=pltpu.CompilerParams(
            dimension_semantics=("arbitrary", "arbitrary"), vmem_limit_bytes=VMEM_LIMIT),
        name="inproj",
    )(x_ctx, x_lat, mods, ng, w_in, cos, sin_p, sin_m, gqk)


V_ROWS = 80
KEY_BLOCK = 256
NEG_BIG = -1e30


def _attn_kernel(qt_ref, k_ref, v_ref, o_ref, km_ref, vt_ref, acc_ref, m_ref, alpha_ref, s_ref, p_ref,
                 *, n_ctx, with_ctx):
    qi = pl.program_id(1)
    per_tile = LANES // HEAD_DIM
    n_tiles = D_ATT // LANES

    @pl.when(qi == 0)
    def _():
        k = k_ref[0].astype(F32)
        kr = pltpu.roll(k, HEAD_DIM, 1)
        low = lax.broadcasted_iota(jnp.int32, k.shape, 1) < HEAD_DIM
        km_ref[0] = jnp.where(low, k, 0.0).astype(BF16)
        km_ref[1] = jnp.where(low, 0.0, kr).astype(BF16)
        km_ref[2] = jnp.where(low, kr, 0.0).astype(BF16)
        km_ref[3] = jnp.where(low, 0.0, k).astype(BF16)
        vt = v_ref[0].astype(F32).T
        row = lax.broadcasted_iota(jnp.int32, (V_ROWS - HEAD_DIM, vt.shape[1]), 0)
        tail = jnp.where(row == 0, 1.0, 0.0)
        for g in range(H_KV):
            vg = jnp.concatenate([vt[g * HEAD_DIM:(g + 1) * HEAD_DIM], tail], axis=0).astype(BF16)
            for blk in range(vt.shape[1] // KEY_BLOCK):
                vt_ref[g, blk] = vg[:, blk * KEY_BLOCK:(blk + 1) * KEY_BLOCK]

    kv_of = lambda h: h // (H_ATT // H_KV)

    def scores(i):
        for j in range(n_tiles):
            ks = jnp.concatenate(
                [km_ref[2 * kv_of(j * per_tile + half) + half, i * KEY_BLOCK:(i + 1) * KEY_BLOCK, :]
                 for half in range(per_tile)], axis=0)
            s_ref[i % 2, j] = _dot(ks, qt_ref[0, j * LANES:(j + 1) * LANES, :])

    def exponentials(i):
        for h in range(H_ATT):
            j, half = divmod(h, per_tile)
            s = s_ref[i % 2, j, half * KEY_BLOCK:(half + 1) * KEY_BLOCK, :]
            m_old = m_ref[h]
            m_new = jnp.maximum(m_old, jnp.max(s, axis=0, keepdims=True))
            m_ref[h] = m_new
            alpha_ref[i % 2, h] = jnp.exp2(m_old - m_new)
            p_ref[i % 2, h] = jnp.exp2(s - m_new).astype(BF16)

    def weighted_values(i):
        for h in range(H_ATT):
            pv = _dot(vt_ref[kv_of(h), i], p_ref[i % 2, h])
            acc_ref[h] = acc_ref[h] * alpha_ref[i % 2, h] + pv

    def run(n_blocks):
        acc_ref[...] = jnp.zeros_like(acc_ref)
        m_ref[...] = jnp.full(m_ref.shape, NEG_BIG, F32)
        for it in range(n_blocks + 2):
            if it >= 2:
                weighted_values(it - 2)
            if it < n_blocks:
                scores(it)
            if 1 <= it <= n_blocks:
                exponentials(it - 1)
        for j in range(n_tiles):
            halves = []
            for half in range(per_tile):
                acc = acc_ref[j * per_tile + half]
                halves.append(acc[0:HEAD_DIM] * (1.0 / acc[HEAD_DIM:HEAD_DIM + 1]))
            ot = jnp.concatenate(halves, axis=0)
            o_ref[0, :, j * LANES:(j + 1) * LANES] = ot.T.astype(BF16)

    n_all = k_ref.shape[1] // KEY_BLOCK
    if with_ctx:
        @pl.when(qi == 0)
        def _():
            run(n_ctx // KEY_BLOCK)

        @pl.when(qi > 0)
        def _():
            run(n_all)
    else:
        run(n_all)


def _attention(qt, k, v, n_ctx, with_ctx):
    bsz, _, t = qt.shape
    tq = ROW_TILE
    assert n_ctx == tq and n_ctx % KEY_BLOCK == 0 and t % KEY_BLOCK == 0
    first = 0 if with_ctx else n_ctx // tq
    nq = t // tq - first
    return pl.pallas_call(
        functools.partial(_attn_kernel, n_ctx=n_ctx, with_ctx=with_ctx),
        grid=(bsz, nq),
        in_specs=[
            pl.BlockSpec((1, D_ATT, tq), lambda b, i: (b, 0, i + first)),
            pl.BlockSpec((1, t, D_KV), lambda b, i: (b, 0, 0)),
            pl.BlockSpec((1, t, D_KV), lambda b, i: (b, 0, 0)),
        ],
        out_specs=pl.BlockSpec((1, tq, D_ATT), lambda b, i: (b, i, 0)),
        out_shape=jax.ShapeDtypeStruct((bsz, nq * tq, D_ATT), BF16),
        scratch_shapes=[pltpu.VMEM((2 * H_KV, t, D_KV), BF16),
                        pltpu.VMEM((H_KV, t // KEY_BLOCK, V_ROWS, KEY_BLOCK), BF16),
                        pltpu.VMEM((H_ATT, V_ROWS, tq), F32),
                        pltpu.VMEM((H_ATT, 1, tq), F32),
                        pltpu.VMEM((2, H_ATT, 1, tq), F32),
                        pltpu.VMEM((2, D_ATT // LANES, (LANES // HEAD_DIM) * KEY_BLOCK, tq), F32),
                        pltpu.VMEM((2, H_ATT, KEY_BLOCK, tq), BF16)],
        compiler_params=pltpu.CompilerParams(
            dimension_semantics=("arbitrary", "arbitrary"), vmem_limit_bytes=VMEM_LIMIT),
        name="attention",
    )(qt, k, v)


def _log_sigmoid(x):
    return jnp.minimum(x, 0.0) - jnp.log1p(jnp.exp(-jnp.abs(x)))


def _ret_kernel(r_ref, lg_ref, gng_ref, gnb_ref, o_ref, y_ref, sf_ref, sb_ref, w_ref, dm_ref,
                *, n_ctx_chunks):
    c_len = CHUNK
    t = r_ref.shape[1]
    n_chunks = t // c_len
    lg = _log_sigmoid(lg_ref[...])
    lgf = lg[0:1, :]
    lgb = lg[1:2, :]
    dec_f = jnp.exp(c_len * lgf)
    dec_b = jnp.exp(c_len * lgb)

    @pl.when(pl.program_id(0) == 0)
    def _():
        tcol = lax.broadcasted_iota(jnp.int32, (c_len, 1), 0).astype(F32)
        w_ref[0] = jnp.exp((tcol + 1.0) * lgf)
        w_ref[1] = jnp.exp((c_len - tcol) * lgb)
        w_ref[2] = jnp.exp((c_len - 1.0 - tcol) * lgf)
        w_ref[3] = jnp.exp(tcol * lgb)
        ti = lax.broadcasted_iota(jnp.int32, (c_len, c_len), 0)
        si = lax.broadcasted_iota(jnp.int32, (c_len, c_len), 1)
        diff = (ti - si).astype(F32)
        for h in range(H_RET):
            lf = lgf[:, h * HEAD_DIM:h * HEAD_DIM + 1]
            lb = lgb[:, h * HEAD_DIM:h * HEAD_DIM + 1]
            dm_ref[h] = jnp.exp(jnp.where(diff >= 0, diff * lf, -diff * lb))

    lane = lax.broadcasted_iota(jnp.int32, (c_len, D_RET), 1)
    srow = lax.broadcasted_iota(jnp.int32, (D_RET, D_RET), 0) // HEAD_DIM
    scol = lax.broadcasted_iota(jnp.int32, (D_RET, D_RET), 1) // HEAD_DIM
    smask = srow == scol
    avg = _head_avg_matrix(D_RET)

    def chunk_rows(c):
        return pl.ds(pl.multiple_of(c * c_len, c_len), c_len)

    def state_delta(k, v, wk):
        kw = (k.astype(F32) * wk).T.astype(BF16)
        return jnp.where(smask, _dot(kw, v), 0.0)

    sf_ref[...] = jnp.zeros_like(sf_ref)
    sb_ref[...] = jnp.zeros_like(sb_ref)

    def fwd(c, carry):
        rows = chunk_rows(c)
        q = r_ref[0, rows, 0:D_RET]
        k = r_ref[0, rows, D_RET:2 * D_RET]
        v = r_ref[0, rows, 2 * D_RET:3 * D_RET]
        zero = jnp.zeros_like(q)
        y = w_ref[0] * _dot(q, sf_ref[...].astype(BF16))
        for h in range(H_RET):
            hm = (lane >= h * HEAD_DIM) & (lane < (h + 1) * HEAD_DIM)
            s = _dot_nt(jnp.where(hm, q, zero), k) * dm_ref[h]
            y = y + _dot(s.astype(BF16), jnp.where(hm, v, zero))
        y_ref[rows, :] = y
        sf_ref[...] = dec_f * sf_ref[...] + state_delta(k, v, w_ref[2])
        return carry

    lax.fori_loop(0, n_chunks, fwd, 0)

    def bwd(c):
        rows = chunk_rows(c)
        q = r_ref[0, rows, 0:D_RET]
        k = r_ref[0, rows, D_RET:2 * D_RET]
        v = r_ref[0, rows, 2 * D_RET:3 * D_RET]
        g = r_ref[0, rows, 3 * D_RET:4 * D_RET].astype(F32)
        y = y_ref[rows, :] + w_ref[1] * _dot(q, sb_ref[...].astype(BF16))
        sb_ref[...] = dec_b * sb_ref[...] + state_delta(k, v, w_ref[3])
        mu = _dot2_right(y, avg)
        d = y - mu
        var = _dot2_right(d * d, avg)
        yn = d * lax.rsqrt(var + EPS) * gng_ref[...] + gnb_ref[...]
        o_ref[0, rows, :] = (yn * _silu(g)).astype(BF16)

    def bwd_ctx(i, carry):
        bwd(n_ctx_chunks - 1 - i)
        return carry

    def bwd_lat(i, carry):
        bwd(n_chunks - 1 - i)
        return carry

    lax.fori_loop(0, n_ctx_chunks, bwd_ctx, 0)
    lax.fori_loop(0, n_chunks - n_ctx_chunks, bwd_lat, 0)


def _retention(ret, lg, gng, gnb, n_ctx):
    bsz, t, _ = ret.shape
    const = lambda shape: pl.BlockSpec(shape, lambda b: (0,) * len(shape))
    return pl.pallas_call(
        functools.partial(_ret_kernel, n_ctx_chunks=n_ctx // CHUNK),
        grid=(bsz,),
        in_specs=[
            pl.BlockSpec((1, t, 4 * D_RET), lambda b: (b, 0, 0)),
            const((2, D_RET)), const((1, D_RET)), const((1, D_RET)),
        ],
        out_specs=pl.BlockSpec((1, t, D_RET), lambda b: (b, 0, 0)),
        out_shape=jax.ShapeDtypeStruct((bsz, t, D_RET), BF16),
        scratch_shapes=[pltpu.VMEM((t, D_RET), F32), pltpu.VMEM((D_RET, D_RET), F32),
                        pltpu.VMEM((D_RET, D_RET), F32), pltpu.VMEM((4, CHUNK, D_RET), F32),
                        pltpu.VMEM((H_RET, CHUNK, CHUNK), F32)],
        compiler_params=pltpu.CompilerParams(
            dimension_semantics=("arbitrary",), vmem_limit_bytes=VMEM_LIMIT),
        name="retention",
    )(ret, lg, gng, gnb)


def _softplus(x):
    return jnp.maximum(x, 0.0) + jnp.log1p(jnp.exp(-jnp.abs(x)))


def _ssd_kernel(z_ref, xbc_ref, dt_ref, cw_ref, cb_ref, dtb_ref, alog_ref, dskip_ref, ng_ref,
                o_ref, xpad_ref, xc_ref, y_ref, eb_ref, dsb_ref, sf_ref, sb_ref, ce_all, g_all, sc_all,
                *, n_ctx):
    c_len = CHUNK
    t = xbc_ref.shape[1]
    n_chunks = t // c_len
    n_ctx_chunks = n_ctx // c_len
    pad = 8
    half = SSD_CONV // 2

    zpad = jnp.zeros((pad, D_XBC), F32)
    xpad_ref[0:pad, :] = zpad
    xpad_ref[pad + n_ctx:2 * pad + n_ctx, :] = zpad
    xpad_ref[2 * pad + t:3 * pad + t, :] = zpad

    def fill(c, carry):
        src = pl.ds(pl.multiple_of(c * c_len, c_len), c_len)
        off = jnp.where(c < n_ctx_chunks, pad, 2 * pad)
        dst = pl.ds(pl.multiple_of(c * c_len + off, pad), c_len)
        xpad_ref[dst, :] = xbc_ref[0, src, :].astype(F32)
        return carry

    lax.fori_loop(0, n_chunks, fill, 0)

    for c in range(n_chunks):
        base = c * c_len + (pad if c < n_ctx_chunks else 2 * pad)
        acc = jnp.broadcast_to(cb_ref[...], (c_len, D_XBC))
        for j in range(SSD_CONV):
            acc = acc + cw_ref[j:j + 1, :] * xpad_ref[base + j - half:base + j - half + c_len, :]
        xc_ref[c * c_len:(c + 1) * c_len, :] = _silu(acc).astype(BF16)

    nd = 2 * H_SSD
    lane_dt = lax.broadcasted_iota(jnp.int32, (1, D_DT), 1)
    live = lane_dt < nd
    a_vec = jnp.where(live, -jnp.exp(alog_ref[...]), 0.0)
    is_f = lane_dt < H_SSD
    ti = lax.broadcasted_iota(jnp.int32, (c_len, c_len), 0)
    si = lax.broadcasted_iota(jnp.int32, (c_len, c_len), 1)
    causal = si <= ti
    tri_l = jnp.where(causal, 1.0, 0.0).astype(BF16)
    tri_u = jnp.where(si >= ti, 1.0, 0.0).astype(BF16)
    n_exp = 2 * D_SSD
    er = lax.broadcasted_iota(jnp.int32, (D_DT, 2 * n_exp), 0)
    ej = lax.broadcasted_iota(jnp.int32, (D_DT, 2 * n_exp), 1)
    rel = er - jnp.where(ej >= n_exp, 3 * PART_LANES, 0)
    col = jnp.where(ej >= n_exp, ej - n_exp, ej) // HEAD_DIM
    expand = jnp.where((rel == col) | (rel == col + PART_LANES) | (rel == col + 2 * PART_LANES),
                       1.0, 0.0).astype(BF16)
    lane_w = lax.broadcasted_iota(jnp.int32, (c_len, 2 * LANES), 1)
    lane_c = lax.broadcasted_iota(jnp.int32, (c_len, D_BC), 1)
    srow = lax.broadcasted_iota(jnp.int32, (D_BC, D_SSD), 0) // SSD_STATE
    scol = lax.broadcasted_iota(jnp.int32, (D_BC, D_SSD), 1) // (D_SSD // SSD_GROUPS)
    smask = srow == scol
    heads_per_group = H_SSD // SSD_GROUPS

    def chunk_rows(c):
        return pl.ds(pl.multiple_of(c * c_len, c_len), c_len)

    sf_ref[...] = jnp.zeros_like(sf_ref)
    sb_ref[...] = jnp.zeros_like(sb_ref)

    def fwd_chunk(c, slot):
        ce_ref, g_ref, sc_ref = ce_all.at[slot], g_all.at[slot], sc_all.at[slot]
        rows = chunk_rows(c)
        xs = xc_ref[rows, 0:D_SSD]
        bm = xc_ref[rows, D_SSD:D_SSD + D_BC]
        cm = xc_ref[rows, D_SSD + D_BC:D_XBC]
        dt = _softplus(dt_ref[0, rows, :] + dtb_ref[...])
        la = _pack3(dt * a_vec, 0).astype(BF16)
        cum = jnp.where(is_f, _unpack3(_dot(tri_l, la)), _unpack3(_dot(tri_u, la)))
        cum = jnp.where(live, cum, 0.0)
        edge = jnp.where(is_f, cum[c_len - 1:c_len, :], cum[0:1, :])
        wk = jnp.where(live, jnp.exp(edge - cum) * dt, 0.0)
        packed = (_pack3(cum, 0) + _pack3(wk, 3 * PART_LANES)).astype(BF16)
        ce_ref[...] = _dot(packed, expand)
        key_t = (cum - jnp.log(dt)).T

        e_f = jnp.exp(ce_ref[:, 0:D_SSD])
        eb_ref[rows, :] = jnp.exp(ce_ref[:, D_SSD:n_exp])
        y_ref[rows, :] = e_f * _dot(cm, sf_ref[...].astype(BF16))
        zc = jnp.zeros_like(cm)
        for g in range(SSD_GROUPS):
            gm = (lane_c >= g * SSD_STATE) & (lane_c < (g + 1) * SSD_STATE)
            g_ref[...] = _dot_nt(jnp.where(gm, cm, zc), bm)
            win = slice(g * LANES, (g + 2) * LANES)
            xw = xs[:, win]
            zw = jnp.zeros_like(xw)
            yg = None
            for hh in range(heads_per_group):
                h = g * heads_per_group + hh
                d_f = cum[:, h:h + 1] - key_t[h:h + 1, :]
                d_b = cum[:, H_SSD + h:H_SSD + h + 1] - key_t[H_SSD + h:H_SSD + h + 1, :]
                sc_ref[hh] = (jnp.exp(jnp.where(causal, d_f, d_b)) * g_ref[...]).astype(BF16)
                lo = h * HEAD_DIM - g * LANES
                hm = (lane_w >= lo) & (lane_w < lo + HEAD_DIM)
                part = _dot(sc_ref[hh], jnp.where(hm, xw, zw))
                yg = part if yg is None else yg + part
            y_ref[rows, win] = y_ref[rows, win] + yg

        xs_f = xs.astype(F32)
        bm_t = bm.astype(F32).T.astype(BF16)
        ds_f = _dot(bm_t, (xs_f * ce_ref[:, n_exp:n_exp + D_SSD]).astype(BF16))
        ds_b = _dot(bm_t, (xs_f * ce_ref[:, n_exp + D_SSD:2 * n_exp]).astype(BF16))
        sf_ref[...] = e_f[c_len - 1:c_len, :] * sf_ref[...] + jnp.where(smask, ds_f, 0.0)
        dsb_ref[c] = jnp.where(smask, ds_b, 0.0)

    unroll = ce_all.shape[0]

    def fwd(i, carry):
        for k in range(unroll):
            fwd_chunk(i * unroll + k, k)
        return carry

    lax.fori_loop(0, n_chunks // unroll, fwd, 0)

    def bwd(c):
        rows = chunk_rows(c)
        xs_f = xc_ref[rows, 0:D_SSD].astype(F32)
        cm = xc_ref[rows, D_SSD + D_BC:D_XBC]
        e_b = eb_ref[rows, :]
        y = y_ref[rows, :] + e_b * _dot(cm, sb_ref[...].astype(BF16))
        sb_ref[...] = e_b[0:1, :] * sb_ref[...] + dsb_ref[c]
        y = y + dskip_ref[...] * xs_f
        u = y * _silu(z_ref[0, rows, :].astype(F32))
        o_ref[0, rows, :] = _rms(u, ng_ref[...]).astype(BF16)

    def bwd_ctx(i, carry):
        bwd(n_ctx_chunks - 1 - i)
        return carry

    def bwd_lat(i, carry):
        bwd(n_chunks - 1 - i)
        return carry

    lax.fori_loop(0, n_ctx_chunks, bwd_ctx, 0)
    lax.fori_loop(0, n_chunks - n_ctx_chunks, bwd_lat, 0)


def _ssd(z, xbc, dt, conv_w, conv_b, dt_bias, a_log, d_skip, norm_g, n_ctx):
    bsz, t, _ = xbc.shape
    n_chunks = t // CHUNK
    unroll = next(u for u in (3, 2, 1) if n_chunks % u == 0)
    const = lambda shape: pl.BlockSpec(shape, lambda b: (0,) * len(shape))
    seq = lambda w: pl.BlockSpec((1, t, w), lambda b: (b, 0, 0))
    return pl.pallas_call(
        functools.partial(_ssd_kernel, n_ctx=n_ctx),
        grid=(bsz,),
        in_specs=[seq(D_SSD), seq(D_XBC), seq(D_DT),
                  const((8, D_XBC)), const((1, D_XBC)), const((1, D_DT)), const((1, D_DT)),
                  const((1, D_SSD)), const((1, D_SSD))],
        out_specs=seq(D_SSD),
        out_shape=jax.ShapeDtypeStruct((bsz, t, D_SSD), BF16),
        scratch_shapes=[
            pltpu.VMEM((t + 24, D_XBC), F32),
            pltpu.VMEM((t, D_XBC), BF16),
            pltpu.VMEM((t, D_SSD), F32),
            pltpu.VMEM((t, D_SSD), F32),
            pltpu.VMEM((n_chunks, D_BC, D_SSD), F32),
            pltpu.VMEM((D_BC, D_SSD), F32),
            pltpu.VMEM((D_BC, D_SSD), F32),
            pltpu.VMEM((unroll, CHUNK, 4 * D_SSD), F32),
            pltpu.VMEM((unroll, CHUNK, CHUNK), F32),
            pltpu.VMEM((unroll, H_SSD // SSD_GROUPS, CHUNK, CHUNK), BF16),
        ],
        compiler_params=pltpu.CompilerParams(
            dimension_semantics=("arbitrary",), vmem_limit_bytes=VMEM_LIMIT),
        name="ssd",
    )(z, xbc, dt, conv_w, conv_b, dt_bias, a_log, d_skip, norm_g)


def _post_kernel(att_ref, ret_ref, ssd_ref, xc_ref, xl_ref, mod_ref, ng_ref, wo_ref, w1_ref, w2_ref,
                 o_ref, *, ff_chunk, n_ctx_tiles, first, ctx_row):
    is_ctx = pl.program_id(1) + first < n_ctx_tiles
    x = jnp.where(is_ctx, xc_ref[0], xl_ref[0])
    mod = functools.partial(_mod_chunk, mod_ref, jnp.where(is_ctx, ctx_row, pl.program_id(0)))
    mix = jnp.concatenate([att_ref[0], ret_ref[0], ssd_ref[0]], axis=-1)
    o = _dot(mix, wo_ref[...])
    x1 = x + mod(2) * _rms(o, ng_ref[1:2, :])
    h2 = _rms(x1, ng_ref[2:3, :]) * (1.0 + mod(4)) + mod(3)
    h2 = h2.astype(BF16)
    d_ff = w1_ref.shape[1]
    acc = jnp.zeros(x1.shape, F32)
    for j in range(d_ff // ff_chunk):
        sl = slice(j * ff_chunk, (j + 1) * ff_chunk)
        a = jnp.maximum(_dot(h2, w1_ref[:, sl]), 0.0)
        acc = acc + _dot((a * a).astype(BF16), w2_ref[sl, :])
    o_ref[0] = x1 + mod(5) * _rms(acc, ng_ref[3:4, :])


def _post(att, ret, ssd, x_ctx, x_lat, lat_off, mods, layer, ng, w_out, w_ff1, w_ff2, n_ctx_tiles,
          with_ctx):
    bsz, t, _ = ret.shape
    d = x_lat.shape[2]
    d_ff = w_ff1.shape[1]
    first = 0 if with_ctx else n_ctx_tiles
    nt = t // ROW_TILE - first
    assert att.shape[1] == nt * ROW_TILE
    mod_spec = pl.BlockSpec((None,) + mods.shape[1:], lambda b, i: (layer, 0, 0))
    tok = lambda w: pl.BlockSpec((1, ROW_TILE, w), lambda b, i: (b, i + first, 0))
    const = lambda shape: pl.BlockSpec(shape, lambda b, i: (0,) * len(shape),
                                       pipeline_mode=pl.Buffered(1))
    return pl.pallas_call(
        functools.partial(_post_kernel, ff_chunk=1024, n_ctx_tiles=n_ctx_tiles, first=first,
                          ctx_row=bsz),
        grid=(bsz, nt),
        in_specs=[pl.BlockSpec((1, ROW_TILE, D_ATT), lambda b, i: (b, i, 0)), tok(D_RET), tok(D_SSD),
                  *_token_specs(d, n_ctx_tiles, lat_off, first),
                  mod_spec,
                  const((4, d)), const((d, d)), const((d, d_ff)), const((d_ff, d))],
        out_specs=pl.BlockSpec((1, ROW_TILE, d), lambda b, i: (b, i, 0)),
        out_shape=jax.ShapeDtypeStruct((bsz, nt * ROW_TILE, d), F32),
        compiler_params=pltpu.CompilerParams(
            dimension_semantics=("arbitrary", "arbitrary"), vmem_limit_bytes=VMEM_LIMIT),
        name="post",
    )(att, ret, ssd, x_ctx, x_lat, mods, ng, w_out, w_ff1, w_ff2)


def _rope_tables(n, n_ctx):
    rows = n // GRID_W
    row = jnp.broadcast_to(jnp.arange(rows)[:, None], (rows, GRID_W)).reshape(n)
    col = jnp.broadcast_to(jnp.arange(GRID_W)[None, :], (rows, GRID_W)).reshape(n)
    half = HEAD_DIM // 2
    inv_freq = ROPE_THETA ** (-jnp.arange(0, half, 2, dtype=F32) / half)
    ang = jnp.stack([row, col], axis=-1).astype(F32)[:, :, None] * inv_freq
    ang = jnp.concatenate([ang, ang], axis=-1).reshape(n, HEAD_DIM)
    ang = jnp.tile(ang, (1, LANES // HEAD_DIM))
    cos = jnp.cos(ang)
    sin = jnp.sin(ang)
    upper = (jnp.arange(LANES) % half) >= half // 2
    sin_p = jnp.where(upper, sin, 0.0)
    sin_m = jnp.where(upper, 0.0, -sin)
    ident = lambda v, a: jnp.concatenate([jnp.full((n_ctx, LANES), v, F32), a], axis=0)
    return ident(1.0, cos), ident(0.0, sin_p), ident(0.0, sin_m)


def kernel(x, c, ctx, c_ctx, w_mod, b_mod, norm_g, w_in, w_out, q_norm_g, k_norm_g, ret_decay_logit,
           ret_gn_g, ret_gn_b, ssd_conv_w, ssd_conv_b, ssd_dt_bias, ssd_a_log, ssd_d, ssd_norm_g,
           w_ff1, w_ff2):
    bsz, n, d = x.shape
    n_ctx = ctx.shape[1]
    depth = w_mod.shape[0]
    assert n % ROW_TILE == 0 and n_ctx % ROW_TILE == 0 and n_ctx % CHUNK == 0 and n % CHUNK == 0
    assert w_in.shape[2] == OFF_DT + 2 * H_SSD
    n_ctx_tiles = n_ctx // ROW_TILE

    n_rows = -(-(bsz + 1) // 8) * 8
    cc = jnp.concatenate([c, c_ctx[None, :], jnp.zeros((n_rows - bsz - 1, d), F32)], axis=0)
    mods = _modulation(cc, w_mod, b_mod)

    cos, sin_p, sin_m = _rope_tables(n, n_ctx)
    n_dt = 2 * H_SSD
    pad_dt = lambda a: jnp.pad(a.reshape(1, n_dt), ((0, 0), (0, D_DT - n_dt)))

    x_ctx, x_lat, lat_off = ctx, x, n_ctx_tiles
    for layer in range(depth):
        last = layer == depth - 1
        wo = w_out[layer].astype(BF16)
        gqk = jnp.concatenate([jnp.tile(q_norm_g[layer], H_ATT) * (HEAD_DIM ** -0.5 * LOG2_E),
                               jnp.tile(k_norm_g[layer], H_KV)])[None, :]

        q, k, v, ret, z, xbc, dt = _inproj(x_ctx, x_lat, lat_off, n_ctx + n, mods, layer,
                                           norm_g[layer], w_in, cos, sin_p, sin_m, gqk, n_ctx_tiles)
        att = _attention(q, k, v, n_ctx, with_ctx=not last)
        lg = jnp.repeat(ret_decay_logit[layer], HEAD_DIM, axis=1)
        ret_o = _retention(ret, lg, ret_gn_g[layer][None, :], ret_gn_b[layer][None, :], n_ctx)
        conv_w = jnp.pad(ssd_conv_w[layer], ((0, 8 - SSD_CONV), (0, 0)))
        ssd_o = _ssd(z, xbc, dt, conv_w, ssd_conv_b[layer][None, :], pad_dt(ssd_dt_bias[layer]),
                     pad_dt(ssd_a_log[layer]), jnp.repeat(ssd_d[layer], HEAD_DIM)[None, :],
                     ssd_norm_g[layer][None, :], n_ctx)
        xall = _post(att, ret_o, ssd_o, x_ctx, x_lat, lat_off, mods, layer, norm_g[layer], wo,
                     w_ff1[layer].astype(BF16), w_ff2[layer].astype(BF16), n_ctx_tiles,
                     with_ctx=not last)
        x_ctx, x_lat, lat_off = xall, xall, 0
    return xall
```

```python
import functools

import jax
import jax.numpy as jnp
from jax import lax
from jax.experimental import pallas as pl
from jax.experimental.pallas import tpu as pltpu

F32 = jnp.float32
BF16 = jnp.bfloat16

HEAD_DIM = 64
H_ATT = 6
H_KV = 2
H_RET = 4
H_SSD = 6
SSD_GROUPS = 2
SSD_STATE = 128
SSD_CONV = 5
GRID_W = 64
ROPE_THETA = 10000.0
EPS = 1e-6
LOG2_E = 1.4426950408889634

D_ATT = H_ATT * HEAD_DIM
D_KV = H_KV * HEAD_DIM
D_RET = H_RET * HEAD_DIM
D_SSD = H_SSD * HEAD_DIM
D_BC = SSD_GROUPS * SSD_STATE
D_XBC = D_SSD + 2 * D_BC
LANES = 128
D_DT = LANES
OFF_Q = 0
OFF_K = OFF_Q + D_ATT
OFF_V = OFF_K + D_KV
OFF_RET = OFF_V + D_KV
OFF_Z = OFF_RET + 4 * D_RET
OFF_XBC = OFF_Z + D_SSD
OFF_DT = OFF_XBC + D_XBC
D_IN_PAD = OFF_DT + D_DT

ROW_TILE = 256
CHUNK = 256
VMEM_LIMIT = 56 * 1024 * 1024


def _silu(x):
    return x * jax.nn.sigmoid(x)


def _split2(a):
    hi = a.astype(BF16)
    lo = (a - hi.astype(F32)).astype(BF16)
    return hi, lo


def _dot(a, b):
    return jnp.dot(a, b, preferred_element_type=F32)


def _dot_nt(a, b):
    return lax.dot_general(a, b, (((1,), (1,)), ((), ())), preferred_element_type=F32)


def _dot2_right(a, m):
    hi, lo = _split2(a)
    return _dot(hi, m) + _dot(lo, m)


PART_LANES = 16


def _pack3(a, base):
    hi = a.astype(BF16).astype(F32)
    r = a - hi
    mid = r.astype(BF16).astype(F32)
    out = hi + pltpu.roll(mid, PART_LANES, 1) + pltpu.roll(r - mid, 2 * PART_LANES, 1)
    return pltpu.roll(out, base, 1) if base else out


def _unpack3(p):
    return p + pltpu.roll(p, LANES - PART_LANES, 1) + pltpu.roll(p, LANES - 2 * PART_LANES, 1)


def _rms(x, g):
    ms = jnp.mean(x * x, axis=-1, keepdims=True)
    return x * lax.rsqrt(ms + EPS) * g


def _head_avg_matrix(n):
    r = lax.broadcasted_iota(jnp.int32, (n, n), 0) // HEAD_DIM
    c = lax.broadcasted_iota(jnp.int32, (n, n), 1) // HEAD_DIM
    return jnp.where(r == c, 1.0 / HEAD_DIM, 0.0).astype(BF16)


def _mod_kernel(c_ref, w_ref, b_ref, o_ref):
    sc = _silu(c_ref[...]).astype(BF16)
    o_ref[0] = _dot(sc, w_ref[0].astype(BF16)) + b_ref[0]


def _modulation(cc, w_mod, b_mod):
    depth, d, n = w_mod.shape
    rows = cc.shape[0]
    tn = 1536
    return pl.pallas_call(
        _mod_kernel,
        grid=(depth, n // tn),
        in_specs=[
            pl.BlockSpec((rows, d), lambda l, j: (0, 0)),
            pl.BlockSpec((1, d, tn), lambda l, j: (l, 0, j)),
            pl.BlockSpec((1, 1, tn), lambda l, j: (l, 0, j)),
        ],
        out_specs=pl.BlockSpec((1, rows, tn), lambda l, j: (l, 0, j)),
        out_shape=jax.ShapeDtypeStruct((depth, rows, n), F32),
        compiler_params=pltpu.CompilerParams(
            dimension_semantics=("arbitrary", "arbitrary"), vmem_limit_bytes=VMEM_LIMIT),
        name="modulation",
    )(cc, w_mod, b_mod.reshape(depth, 1, n))


def _mod_chunk(mod_ref, row, k):
    d = mod_ref.shape[1] // 6
    return mod_ref[pl.ds(row, 1), k * d:(k + 1) * d]


def _rope(x, cos, sin_p, sin_m):
    return x * cos + pltpu.roll(x, 16, 1) * sin_p + pltpu.roll(x, LANES - 16, 1) * sin_m


def _inproj_kernel(xc_ref, xl_ref, mod_ref, ng_ref, w_ref, cos_ref, sp_ref, sm_ref, gqk_ref,
                   qt_ref, k_ref, v_ref, ret_ref, z_ref, xbc_ref, dt_ref, *, n_ctx_tiles, ctx_row):
    is_ctx = pl.program_id(1) < n_ctx_tiles
    x = jnp.where(is_ctx, xc_ref[0], xl_ref[0])
    mod = functools.partial(_mod_chunk, mod_ref, jnp.where(is_ctx, ctx_row, pl.program_id(0)))
    h = _rms(x, ng_ref[0:1, :])
    h = h * (1.0 + mod(1)) + mod(0)
    acc = _dot(h.astype(BF16), w_ref[...])

    cos = cos_ref[...]
    sin_p = sp_ref[...]
    sin_m = sm_ref[...]

    nqk = D_ATT + D_KV
    qk = acc[:, OFF_Q:OFF_Q + nqk]
    ms = _dot2_right(qk * qk, _head_avg_matrix(nqk))
    qk = qk * lax.rsqrt(ms + EPS) * gqk_ref[...]
    for i in range(D_ATT // LANES):
        sl = slice(i * LANES, (i + 1) * LANES)
        qt_ref[0, sl, :] = _rope(qk[:, sl], cos, sin_p, sin_m).T.astype(BF16)
    k_ref[0] = _rope(qk[:, D_ATT:nqk], cos, sin_p, sin_m).astype(BF16)
    v_ref[0] = acc[:, OFF_V:OFF_V + D_KV].astype(BF16)

    kscale = HEAD_DIM ** -0.5
    for i in range(2 * D_RET // LANES):
        sl = slice(OFF_RET + i * LANES, OFF_RET + (i + 1) * LANES)
        r = _rope(acc[:, sl], cos, sin_p, sin_m)
        if i >= D_RET // LANES:
            r = r * kscale
        ret_ref[0, :, i * LANES:(i + 1) * LANES] = r.astype(BF16)
    ret_ref[0, :, 2 * D_RET:4 * D_RET] = acc[:, OFF_RET + 2 * D_RET:OFF_RET + 4 * D_RET].astype(BF16)
    z_ref[0] = acc[:, OFF_Z:OFF_Z + D_SSD].astype(BF16)
    xbc_ref[0] = acc[:, OFF_XBC:OFF_XBC + D_XBC].astype(BF16)
    dt_ref[0] = acc[:, OFF_DT:OFF_DT + D_DT]


def _token_specs(d, n_ctx_tiles, lat_off, first=0):
    ctx_spec = pl.BlockSpec((1, ROW_TILE, d),
                            lambda b, i: (b, jnp.minimum(i + first, n_ctx_tiles - 1), 0))
    lat_spec = pl.BlockSpec((1, ROW_TILE, d),
                            lambda b, i: (b, jnp.maximum(i + first, n_ctx_tiles) - lat_off, 0))
    return ctx_spec, lat_spec


def _inproj(x_ctx, x_lat, lat_off, t, mods, layer, ng, w_in, cos, sin_p, sin_m, gqk, n_ctx_tiles):
    bsz, _, d = x_lat.shape
    nt = t // ROW_TILE
    mod_spec = pl.BlockSpec((None,) + mods.shape[1:], lambda b, i: (layer, 0, 0))

    tok = lambda w: pl.BlockSpec((1, ROW_TILE, w), lambda b, i: (b, i, 0))
    rope_spec = pl.BlockSpec((ROW_TILE, LANES), lambda b, i: (i, 0))
    const = lambda shape: pl.BlockSpec(shape, lambda b, i: (0,) * len(shape))
    widths = (D_KV, D_KV, 4 * D_RET, D_SSD, D_XBC)
    qt_spec = pl.BlockSpec((1, D_ATT, ROW_TILE), lambda b, i: (b, 0, i))
    return pl.pallas_call(
        functools.partial(_inproj_kernel, n_ctx_tiles=n_ctx_tiles, ctx_row=bsz),
        grid=(bsz, nt),
        in_specs=[
            *_token_specs(d, n_ctx_tiles, lat_off),
            mod_spec,
            const((4, d)),
            pl.BlockSpec((d, D_IN_PAD), lambda b, i: (0, 0), pipeline_mode=pl.Buffered(1)),
            rope_spec, rope_spec, rope_spec,
            const((1, D_ATT + D_KV)),
        ],
        out_specs=[qt_spec] + [tok(w) for w in widths] + [tok(D_DT)],
        out_shape=[jax.ShapeDtypeStruct((bsz, D_ATT, t), BF16)]
        + [jax.ShapeDtypeStruct((bsz, t, w), BF16) for w in widths]
        + [jax.ShapeDtypeStruct((bsz, t, D_DT), F32)],
        compiler_params=pltpu.CompilerParams(
            dimension_semantics=("arbitrary", "arbitrary"), vmem_limit_bytes=VMEM_LIMIT),
        name="inproj",
    )(x_ctx, x_lat, mods, ng, w_in, cos, sin_p, sin_m, gqk)


V_ROWS = 80
KEY_BLOCK = 256
NEG_BIG = -1e30


def _attn_kernel(qt_ref, k_ref, v_ref, o_ref, km_ref, vt_ref, acc_ref, m_ref, alpha_ref, s_ref, p_ref,
                 *, n_ctx, with_ctx):
    qi = pl.program_id(1)
    per_tile = LANES // HEAD_DIM
    n_tiles = D_ATT // LANES

    @pl.when(qi == 0)
    def _():
        k = k_ref[0].astype(F32)
        kr = pltpu.roll(k, HEAD_DIM, 1)
        low = lax.broadcasted_iota(jnp.int32, k.shape, 1) < HEAD_DIM
        km_ref[0] = jnp.where(low, k, 0.0).astype(BF16)
        km_ref[1] = jnp.where(low, 0.0, kr).astype(BF16)
        km_ref[2] = jnp.where(low, kr, 0.0).astype(BF16)
        km_ref[3] = jnp.where(low, 0.0, k).astype(BF16)
        vt = v_ref[0].astype(F32).T
        row = lax.broadcasted_iota(jnp.int32, (V_ROWS - HEAD_DIM, vt.shape[1]), 0)
        tail = jnp.where(row == 0, 1.0, 0.0)
        for g in range(H_KV):
            vg = jnp.concatenate([vt[g * HEAD_DIM:(g + 1) * HEAD_DIM], tail], axis=0).astype(BF16)
            for blk in range(vt.shape[1] // KEY_BLOCK):
                vt_ref[g, blk] = vg[:, blk * KEY_BLOCK:(blk + 1) * KEY_BLOCK]

    kv_of = lambda h: h // (H_ATT // H_KV)

    def scores(i):
        for j in range(n_tiles):
            ks = jnp.concatenate(
                [km_ref[2 * kv_of(j * per_tile + half) + half, i * KEY_BLOCK:(i + 1) * KEY_BLOCK, :]
                 for half in range(per_tile)], axis=0)
            s_ref[i % 2, j] = _dot(ks, qt_ref[0, j * LANES:(j + 1) * LANES, :])

    def exponentials(i):
        for h in range(H_ATT):
            j, half = divmod(h, per_tile)
            s = s_ref[i % 2, j, half * KEY_BLOCK:(half + 1) * KEY_BLOCK, :]
            m_old = m_ref[h]
            m_new = jnp.maximum(m_old, jnp.max(s, axis=0, keepdims=True))
            m_ref[h] = m_new
            alpha_ref[i % 2, h] = jnp.exp2(m_old - m_new)
            p_ref[i % 2, h] = jnp.exp2(s - m_new).astype(BF16)

    def weighted_values(i):
        for h in range(H_ATT):
            pv = _dot(vt_ref[kv_of(h), i], p_ref[i % 2, h])
            acc_ref[h] = acc_ref[h] * alpha_ref[i % 2, h] + pv

    def run(n_blocks):
        acc_ref[...] = jnp.zeros_like(acc_ref)
        m_ref[...] = jnp.full(m_ref.shape, NEG_BIG, F32)
        for it in range(n_blocks + 2):
            if it >= 2:
                weighted_values(it - 2)
            if it < n_blocks:
                scores(it)
            if 1 <= it <= n_blocks:
                exponentials(it - 1)
        for j in range(n_tiles):
            halves = []
            for half in range(per_tile):
                acc = acc_ref[j * per_tile + half]
                halves.append(acc[0:HEAD_DIM] * (1.0 / acc[HEAD_DIM:HEAD_DIM + 1]))
            ot = jnp.concatenate(halves, axis=0)
            o_ref[0, :, j * LANES:(j + 1) * LANES] = ot.T.astype(BF16)

    n_all = k_ref.shape[1] // KEY_BLOCK
    if with_ctx:
        @pl.when(qi == 0)
        def _():
            run(n_ctx // KEY_BLOCK)

        @pl.when(qi > 0)
        def _():
            run(n_all)
    else:
        run(n_all)


def _attention(qt, k, v, n_ctx, with_ctx):
    bsz, _, t = qt.shape
    tq = ROW_TILE
    assert n_ctx == tq and n_ctx % KEY_BLOCK == 0 and t % KEY_BLOCK == 0
    first = 0 if with_ctx else n_ctx // tq
    nq = t // tq - first
    return pl.pallas_call(
        functools.partial(_attn_kernel, n_ctx=n_ctx, with_ctx=with_ctx),
        grid=(bsz, nq),
        in_specs=[
            pl.BlockSpec((1, D_ATT, tq), lambda b, i: (b, 0, i + first)),
            pl.BlockSpec((1, t, D_KV), lambda b, i: (b, 0, 0)),
            pl.BlockSpec((1, t, D_KV), lambda b, i: (b, 0, 0)),
        ],
        out_specs=pl.BlockSpec((1, tq, D_ATT), lambda b, i: (b, i, 0)),
        out_shape=jax.ShapeDtypeStruct((bsz, nq * tq, D_ATT), BF16),
        scratch_shapes=[pltpu.VMEM((2 * H_KV, t, D_KV), BF16),
                        pltpu.VMEM((H_KV, t // KEY_BLOCK, V_ROWS, KEY_BLOCK), BF16),
                        pltpu.VMEM((H_ATT, V_ROWS, tq), F32),
                        pltpu.VMEM((H_ATT, 1, tq), F32),
                        pltpu.VMEM((2, H_ATT, 1, tq), F32),
                        pltpu.VMEM((2, D_ATT // LANES, (LANES // HEAD_DIM) * KEY_BLOCK, tq), F32),
                        pltpu.VMEM((2, H_ATT, KEY_BLOCK, tq), BF16)],
        compiler_params=pltpu.CompilerParams(
            dimension_semantics=("arbitrary", "arbitrary"), vmem_limit_bytes=VMEM_LIMIT),
        name="attention",
    )(qt, k, v)


def _log_sigmoid(x):
    return jnp.minimum(x, 0.0) - jnp.log1p(jnp.exp(-jnp.abs(x)))


def _ret_kernel(r_ref, lg_ref, gng_ref, gnb_ref, o_ref, y_ref, sf_ref, sb_ref, w_ref, dm_ref,
                *, n_ctx_chunks):
    c_len = CHUNK
    t = r_ref.shape[1]
    n_chunks = t // c_len
    lg = _log_sigmoid(lg_ref[...])
    lgf = lg[0:1, :]
    lgb = lg[1:2, :]
    dec_f = jnp.exp(c_len * lgf)
    dec_b = jnp.exp(c_len * lgb)

    @pl.when(pl.program_id(0) == 0)
    def _():
        tcol = lax.broadcasted_iota(jnp.int32, (c_len, 1), 0).astype(F32)
        w_ref[0] = jnp.exp((tcol + 1.0) * lgf)
        w_ref[1] = jnp.exp((c_len - tcol) * lgb)
        w_ref[2] = jnp.exp((c_len - 1.0 - tcol) * lgf)
        w_ref[3] = jnp.exp(tcol * lgb)
        ti = lax.broadcasted_iota(jnp.int32, (c_len, c_len), 0)
        si = lax.broadcasted_iota(jnp.int32, (c_len, c_len), 1)
        diff = (ti - si).astype(F32)
        for h in range(H_RET):
            lf = lgf[:, h * HEAD_DIM:h * HEAD_DIM + 1]
            lb = lgb[:, h * HEAD_DIM:h * HEAD_DIM + 1]
            dm_ref[h] = jnp.exp(jnp.where(diff >= 0, diff * lf, -diff * lb))

    lane = lax.broadcasted_iota(jnp.int32, (c_len, D_RET), 1)
    srow = lax.broadcasted_iota(jnp.int32, (D_RET, D_RET), 0) // HEAD_DIM
    scol = lax.broadcasted_iota(jnp.int32, (D_RET, D_RET), 1) // HEAD_DIM
    smask = srow == scol
    avg = _head_avg_matrix(D_RET)

    def chunk_rows(c):
        return pl.ds(pl.multiple_of(c * c_len, c_len), c_len)

    def state_delta(k, v, wk):
        kw = (k.astype(F32) * wk).T.astype(BF16)
        return jnp.where(smask, _dot(kw, v), 0.0)

    sf_ref[...] = jnp.zeros_like(sf_ref)
    sb_ref[...] = jnp.zeros_like(sb_ref)

    def fwd(c, carry):
        rows = chunk_rows(c)
        q = r_ref[0, rows, 0:D_RET]
        k = r_ref[0, rows, D_RET:2 * D_RET]
        v = r_ref[0, rows, 2 * D_RET:3 * D_RET]
        zero = jnp.zeros_like(q)
        y = w_ref[0] * _dot(q, sf_ref[...].astype(BF16))
        for h in range(H_RET):
            hm = (lane >= h * HEAD_DIM) & (lane < (h + 1) * HEAD_DIM)
            s = _dot_nt(jnp.where(hm, q, zero), k) * dm_ref[h]
            y = y + _dot(s.astype(BF16), jnp.where(hm, v, zero))
        y_ref[rows, :] = y
        sf_ref[...] = dec_f * sf_ref[...] + state_delta(k, v, w_ref[2])
        return carry

    lax.fori_loop(0, n_chunks, fwd, 0, unroll=3)

    def bwd(c):
        rows = chunk_rows(c)
        q = r_ref[0, rows, 0:D_RET]
        k = r_ref[0, rows, D_RET:2 * D_RET]
        v = r_ref[0, rows, 2 * D_RET:3 * D_RET]
        g = r_ref[0, rows, 3 * D_RET:4 * D_RET].astype(F32)
        y = y_ref[rows, :] + w_ref[1] * _dot(q, sb_ref[...].astype(BF16))
        sb_ref[...] = dec_b * sb_ref[...] + state_delta(k, v, w_ref[3])
        mu = _dot2_right(y, avg)
        d = y - mu
        var = _dot2_right(d * d, avg)
        yn = d * lax.rsqrt(var + EPS) * gng_ref[...] + gnb_ref[...]
        o_ref[0, rows, :] = (yn * _silu(g)).astype(BF16)

    def bwd_ctx(i, carry):
        bwd(n_ctx_chunks - 1 - i)
        return carry

    def bwd_lat(i, carry):
        bwd(n_chunks - 1 - i)
        return carry

    lax.fori_loop(0, n_ctx_chunks, bwd_ctx, 0)
    lax.fori_loop(0, n_chunks - n_ctx_chunks, bwd_lat, 0, unroll=4)


def _retention(ret, lg, gng, gnb, n_ctx):
    bsz, t, _ = ret.shape
    const = lambda shape: pl.BlockSpec(shape, lambda b: (0,) * len(shape))
    return pl.pallas_call(
        functools.partial(_ret_kernel, n_ctx_chunks=n_ctx // CHUNK),
        grid=(bsz,),
        in_specs=[
            pl.BlockSpec((1, t, 4 * D_RET), lambda b: (b, 0, 0)),
            const((2, D_RET)), const((1, D_RET)), const((1, D_RET)),
        ],
        out_specs=pl.BlockSpec((1, t, D_RET), lambda b: (b, 0, 0)),
        out_shape=jax.ShapeDtypeStruct((bsz, t, D_RET), BF16),
        scratch_shapes=[pltpu.VMEM((t, D_RET), F32), pltpu.VMEM((D_RET, D_RET), F32),
                        pltpu.VMEM((D_RET, D_RET), F32), pltpu.VMEM((4, CHUNK, D_RET), F32),
                        pltpu.VMEM((H_RET, CHUNK, CHUNK), F32)],
        compiler_params=pltpu.CompilerParams(
            dimension_semantics=("arbitrary",), vmem_limit_bytes=VMEM_LIMIT),
        name="retention",
    )(ret, lg, gng, gnb)


def _softplus(x):
    return jnp.maximum(x, 0.0) + jnp.log1p(jnp.exp(-jnp.abs(x)))


def _ssd_kernel(z_ref, xbc_ref, dt_ref, cw_ref, cb_ref, dtb_ref, alog_ref, dskip_ref, ng_ref,
                o_ref, xpad_ref, xc_ref, y_ref, eb_ref, dsb_ref, sf_ref, sb_ref, ce_all, g_all, sc_all,
                *, n_ctx):
    c_len = CHUNK
    t = xbc_ref.shape[1]
    n_chunks = t // c_len
    n_ctx_chunks = n_ctx // c_len
    pad = 8
    half = SSD_CONV // 2

    zpad = jnp.zeros((pad, D_XBC), F32)
    xpad_ref[0:pad, :] = zpad
    xpad_ref[pad + n_ctx:2 * pad + n_ctx, :] = zpad
    xpad_ref[2 * pad + t:3 * pad + t, :] = zpad

    def fill(c, carry):
        src = pl.ds(pl.multiple_of(c * c_len, c_len), c_len)
        off = jnp.where(c < n_ctx_chunks, pad, 2 * pad)
        dst = pl.ds(pl.multiple_of(c * c_len + off, pad), c_len)
        xpad_ref[dst, :] = xbc_ref[0, src, :].astype(F32)
        return carry

    lax.fori_loop(0, n_chunks, fill, 0)

    for c in range(n_chunks):
        base = c * c_len + (pad if c < n_ctx_chunks else 2 * pad)
        acc = jnp.broadcast_to(cb_ref[...], (c_len, D_XBC))
        for j in range(SSD_CONV):
            acc = acc + cw_ref[j:j + 1, :] * xpad_ref[base + j - half:base + j - half + c_len, :]
        xc_ref[c * c_len:(c + 1) * c_len, :] = _silu(acc).astype(BF16)

    nd = 2 * H_SSD
    lane_dt = lax.broadcasted_iota(jnp.int32, (1, D_DT), 1)
    live = lane_dt < nd
    a_vec = jnp.where(live, -jnp.exp(alog_ref[...]), 0.0)
    is_f = lane_dt < H_SSD
    ti = lax.broadcasted_iota(jnp.int32, (c_len, c_len), 0)
    si = lax.broadcasted_iota(jnp.int32, (c_len, c_len), 1)
    causal = si <= ti
    tri_l = jnp.where(causal, 1.0, 0.0).astype(BF16)
    tri_u = jnp.where(si >= ti, 1.0, 0.0).astype(BF16)
    n_exp = 2 * D_SSD
    er = lax.broadcasted_iota(jnp.int32, (D_DT, 2 * n_exp), 0)
    ej = lax.broadcasted_iota(jnp.int32, (D_DT, 2 * n_exp), 1)
    rel = er - jnp.where(ej >= n_exp, 3 * PART_LANES, 0)
    col = jnp.where(ej >= n_exp, ej - n_exp, ej) // HEAD_DIM
    expand = jnp.where((rel == col) | (rel == col + PART_LANES) | (rel == col + 2 * PART_LANES),
                       1.0, 0.0).astype(BF16)
    lane_w = lax.broadcasted_iota(jnp.int32, (c_len, 2 * LANES), 1)
    lane_c = lax.broadcasted_iota(jnp.int32, (c_len, D_BC), 1)
    srow = lax.broadcasted_iota(jnp.int32, (D_BC, D_SSD), 0) // SSD_STATE
    scol = lax.broadcasted_iota(jnp.int32, (D_BC, D_SSD), 1) // (D_SSD // SSD_GROUPS)
    smask = srow == scol
    heads_per_group = H_SSD // SSD_GROUPS

    def chunk_rows(c):
        return pl.ds(pl.multiple_of(c * c_len, c_len), c_len)

    sf_ref[...] = jnp.zeros_like(sf_ref)
    sb_ref[...] = jnp.zeros_like(sb_ref)

    def fwd_chunk(c, slot):
        ce_ref, g_ref, sc_ref = ce_all.at[slot], g_all.at[slot], sc_all.at[slot]
        rows = chunk_rows(c)
        xs = xc_ref[rows, 0:D_SSD]
        bm = xc_ref[rows, D_SSD:D_SSD + D_BC]
        cm = xc_ref[rows, D_SSD + D_BC:D_XBC]
        dt = _softplus(dt_ref[0, rows, :] + dtb_ref[...])
        la = _pack3(dt * a_vec, 0).astype(BF16)
        cum = jnp.where(is_f, _unpack3(_dot(tri_l, la)), _unpack3(_dot(tri_u, la)))
        cum = jnp.where(live, cum, 0.0)
        edge = jnp.where(is_f, cum[c_len - 1:c_len, :], cum[0:1, :])
        wk = jnp.where(live, jnp.exp(edge - cum) * dt, 0.0)
        packed = (_pack3(cum, 0) + _pack3(wk, 3 * PART_LANES)).astype(BF16)
        ce_ref[...] = _dot(packed, expand)
        key_t = (cum - jnp.log(dt)).T

        e_f = jnp.exp(ce_ref[:, 0:D_SSD])
        eb_ref[rows, :] = jnp.exp(ce_ref[:, D_SSD:n_exp])
        y_ref[rows, :] = e_f * _dot(cm, sf_ref[...].astype(BF16))
        zc = jnp.zeros_like(cm)
        for g in range(SSD_GROUPS):
            gm = (lane_c >= g * SSD_STATE) & (lane_c < (g + 1) * SSD_STATE)
            g_ref[...] = _dot_nt(jnp.where(gm, cm, zc), bm)
            win = slice(g * LANES, (g + 2) * LANES)
            xw = xs[:, win]
            zw = jnp.zeros_like(xw)
            yg = None
            for hh in range(heads_per_group):
                h = g * heads_per_group + hh
                d_f = cum[:, h:h + 1] - key_t[h:h + 1, :]
                d_b = cum[:, H_SSD + h:H_SSD + h + 1] - key_t[H_SSD + h:H_SSD + h + 1, :]
                sc_ref[hh] = (jnp.exp(jnp.where(causal, d_f, d_b)) * g_ref[...]).astype(BF16)
                lo = h * HEAD_DIM - g * LANES
                hm = (lane_w >= lo) & (lane_w < lo + HEAD_DIM)
                part = _dot(sc_ref[hh], jnp.where(hm, xw, zw))
                yg = part if yg is None else yg + part
            y_ref[rows, win] = y_ref[rows, win] + yg

        xs_f = xs.astype(F32)
        bm_t = bm.astype(F32).T.astype(BF16)
        ds_f = _dot(bm_t, (xs_f * ce_ref[:, n_exp:n_exp + D_SSD]).astype(BF16))
        ds_b = _dot(bm_t, (xs_f * ce_ref[:, n_exp + D_SSD:2 * n_exp]).astype(BF16))
        sf_ref[...] = e_f[c_len - 1:c_len, :] * sf_ref[...] + jnp.where(smask, ds_f, 0.0)
        dsb_ref[c] = jnp.where(smask, ds_b, 0.0)

    unroll = ce_all.shape[0]

    def fwd(i, carry):
        for k in range(unroll):
            fwd_chunk(i * unroll + k, k)
        return carry

    lax.fori_loop(0, n_chunks // unroll, fwd, 0)

    def bwd(c):
        rows = chunk_rows(c)
        xs_f = xc_ref[rows, 0:D_SSD].astype(F32)
        cm = xc_ref[rows, D_SSD + D_BC:D_XBC]
        e_b = eb_ref[rows, :]
        y = y_ref[rows, :] + e_b * _dot(cm, sb_ref[...].astype(BF16))
        sb_ref[...] = e_b[0:1, :] * sb_ref[...] + dsb_ref[c]
        y = y + dskip_ref[...] * xs_f
        u = y * _silu(z_ref[0, rows, :].astype(F32))
        o_ref[0, rows, :] = _rms(u, ng_ref[...]).astype(BF16)

    def bwd_ctx(i, carry):
        bwd(n_ctx_chunks - 1 - i)
        return carry

    def bwd_lat(i, carry):
        bwd(n_chunks - 1 - i)
        return carry

    lax.fori_loop(0, n_ctx_chunks, bwd_ctx, 0)
    lax.fori_loop(0, n_chunks - n_ctx_chunks, bwd_lat, 0, unroll=2)


def _ssd(z, xbc, dt, conv_w, conv_b, dt_bias, a_log, d_skip, norm_g, n_ctx):
    bsz, t, _ = xbc.shape
    n_chunks = t // CHUNK
    unroll = next(u for u in (3, 2, 1) if n_chunks % u == 0)
    const = lambda shape: pl.BlockSpec(shape, lambda b: (0,) * len(shape))
    seq = lambda w: pl.BlockSpec((1, t, w), lambda b: (b, 0, 0))
    return pl.pallas_call(
        functools.partial(_ssd_kernel, n_ctx=n_ctx),
        grid=(bsz,),
        in_specs=[seq(D_SSD), seq(D_XBC), seq(D_DT),
                  const((8, D_XBC)), const((1, D_XBC)), const((1, D_DT)), const((1, D_DT)),
                  const((1, D_SSD)), const((1, D_SSD))],
        out_specs=seq(D_SSD),
        out_shape=jax.ShapeDtypeStruct((bsz, t, D_SSD), BF16),
        scratch_shapes=[
            pltpu.VMEM((t + 24, D_XBC), F32),
            pltpu.VMEM((t, D_XBC), BF16),
            pltpu.VMEM((t, D_SSD), F32),
            pltpu.VMEM((t, D_SSD), F32),
            pltpu.VMEM((n_chunks, D_BC, D_SSD), F32),
            pltpu.VMEM((D_BC, D_SSD), F32),
            pltpu.VMEM((D_BC, D_SSD), F32),
            pltpu.VMEM((unroll, CHUNK, 4 * D_SSD), F32),
            pltpu.VMEM((unroll, CHUNK, CHUNK), F32),
            pltpu.VMEM((unroll, H_SSD // SSD_GROUPS, CHUNK, CHUNK), BF16),
        ],
        compiler_params=pltpu.CompilerParams(
            dimension_semantics=("arbitrary",), vmem_limit_bytes=VMEM_LIMIT),
        name="ssd",
    )(z, xbc, dt, conv_w, conv_b, dt_bias, a_log, d_skip, norm_g)


def _post_kernel(att_ref, ret_ref, ssd_ref, xc_ref, xl_ref, mod_ref, ng_ref, wo_ref, w1_ref, w2_ref,
                 o_ref, *, ff_chunk, n_ctx_tiles, first, ctx_row):
    is_ctx = pl.program_id(1) + first < n_ctx_tiles
    x = jnp.where(is_ctx, xc_ref[0], xl_ref[0])
    mod = functools.partial(_mod_chunk, mod_ref, jnp.where(is_ctx, ctx_row, pl.program_id(0)))
    mix = jnp.concatenate([att_ref[0], ret_ref[0], ssd_ref[0]], axis=-1)
    o = _dot(mix, wo_ref[...])
    x1 = x + mod(2) * _rms(o, ng_ref[1:2, :])
    h2 = _rms(x1, ng_ref[2:3, :]) * (1.0 + mod(4)) + mod(3)
    h2 = h2.astype(BF16)
    d_ff = w1_ref.shape[1]
    acc = jnp.zeros(x1.shape, F32)
    for j in range(d_ff // ff_chunk):
        sl = slice(j * ff_chunk, (j + 1) * ff_chunk)
        a = jnp.maximum(_dot(h2, w1_ref[:, sl]), 0.0)
        acc = acc + _dot((a * a).astype(BF16), w2_ref[sl, :])
    o_ref[0] = x1 + mod(5) * _rms(acc, ng_ref[3:4, :])


def _post(att, ret, ssd, x_ctx, x_lat, lat_off, mods, layer, ng, w_out, w_ff1, w_ff2, n_ctx_tiles,
          with_ctx):
    bsz, t, _ = ret.shape
    d = x_lat.shape[2]
    d_ff = w_ff1.shape[1]
    first = 0 if with_ctx else n_ctx_tiles
    nt = t // ROW_TILE - first
    assert att.shape[1] == nt * ROW_TILE
    mod_spec = pl.BlockSpec((None,) + mods.shape[1:], lambda b, i: (layer, 0, 0))
    tok = lambda w: pl.BlockSpec((1, ROW_TILE, w), lambda b, i: (b, i + first, 0))
    const = lambda shape: pl.BlockSpec(shape, lambda b, i: (0,) * len(shape),
                                       pipeline_mode=pl.Buffered(1))
    return pl.pallas_call(
        functools.partial(_post_kernel, ff_chunk=1024, n_ctx_tiles=n_ctx_tiles, first=first,
                          ctx_row=bsz),
        grid=(bsz, nt),
        in_specs=[pl.BlockSpec((1, ROW_TILE, D_ATT), lambda b, i: (b, i, 0)), tok(D_RET), tok(D_SSD),
                  *_token_specs(d, n_ctx_tiles, lat_off, first),
                  mod_spec,
                  const((4, d)), const((d, d)), const((d, d_ff)), const((d_ff, d))],
        out_specs=pl.BlockSpec((1, ROW_TILE, d), lambda b, i: (b, i, 0)),
        out_shape=jax.ShapeDtypeStruct((bsz, nt * ROW_TILE, d), F32),
        compiler_params=pltpu.CompilerParams(
            dimension_semantics=("arbitrary", "arbitrary"), vmem_limit_bytes=VMEM_LIMIT),
        name="post",
    )(att, ret, ssd, x_ctx, x_lat, mods, ng, w_out, w_ff1, w_ff2)


def _rope_tables(n, n_ctx):
    rows = n // GRID_W
    row = jnp.broadcast_to(jnp.arange(rows)[:, None], (rows, GRID_W)).reshape(n)
    col = jnp.broadcast_to(jnp.arange(GRID_W)[None, :], (rows, GRID_W)).reshape(n)
    half = HEAD_DIM // 2
    inv_freq = ROPE_THETA ** (-jnp.arange(0, half, 2, dtype=F32) / half)
    ang = jnp.stack([row, col], axis=-1).astype(F32)[:, :, None] * inv_freq
    ang = jnp.concatenate([ang, ang], axis=-1).reshape(n, HEAD_DIM)
    ang = jnp.tile(ang, (1, LANES // HEAD_DIM))
    cos = jnp.cos(ang)
    sin = jnp.sin(ang)
    upper = (jnp.arange(LANES) % half) >= half // 2
    sin_p = jnp.where(upper, sin, 0.0)
    sin_m = jnp.where(upper, 0.0, -sin)
    ident = lambda v, a: jnp.concatenate([jnp.full((n_ctx, LANES), v, F32), a], axis=0)
    return ident(1.0, cos), ident(0.0, sin_p), ident(0.0, sin_m)


def kernel(x, c, ctx, c_ctx, w_mod, b_mod, norm_g, w_in, w_out, q_norm_g, k_norm_g, ret_decay_logit,
           ret_gn_g, ret_gn_b, ssd_conv_w, ssd_conv_b, ssd_dt_bias, ssd_a_log, ssd_d, ssd_norm_g,
           w_ff1, w_ff2):
    bsz, n, d = x.shape
    n_ctx = ctx.shape[1]
    depth = w_mod.shape[0]
    assert n % ROW_TILE == 0 and n_ctx % ROW_TILE == 0 and n_ctx % CHUNK == 0 and n % CHUNK == 0
    assert w_in.shape[2] == OFF_DT + 2 * H_SSD
    n_ctx_tiles = n_ctx // ROW_TILE

    n_rows = -(-(bsz + 1) // 8) * 8
    cc = jnp.concatenate([c, c_ctx[None, :], jnp.zeros((n_rows - bsz - 1, d), F32)], axis=0)
    mods = _modulation(cc, w_mod, b_mod)

    cos, sin_p, sin_m = _rope_tables(n, n_ctx)
    n_dt = 2 * H_SSD
    pad_dt = lambda a: jnp.pad(a.reshape(1, n_dt), ((0, 0), (0, D_DT - n_dt)))

    x_ctx, x_lat, lat_off = ctx, x, n_ctx_tiles
    for layer in range(depth):
        last = layer == depth - 1
        wi = jnp.pad(w_in[layer], ((0, 0), (0, D_IN_PAD - w_in.shape[2]))).astype(BF16)
        wo = w_out[layer].astype(BF16)
        gqk = jnp.concatenate([jnp.tile(q_norm_g[layer], H_ATT) * (HEAD_DIM ** -0.5 * LOG2_E),
                               jnp.tile(k_norm_g[layer], H_KV)])[None, :]

        q, k, v, ret, z, xbc, dt = _inproj(x_ctx, x_lat, lat_off, n_ctx + n, mods, layer,
                                           norm_g[layer], wi, cos, sin_p, sin_m, gqk, n_ctx_tiles)
        att = _attention(q, k, v, n_ctx, with_ctx=not last)
        lg = jnp.repeat(ret_decay_logit[layer], HEAD_DIM, axis=1)
        ret_o = _retention(ret, lg, ret_gn_g[layer][None, :], ret_gn_b[layer][None, :], n_ctx)
        conv_w = jnp.pad(ssd_conv_w[layer], ((0, 8 - SSD_CONV), (0, 0)))
        ssd_o = _ssd(z, xbc, dt, conv_w, ssd_conv_b[layer][None, :], pad_dt(ssd_dt_bias[layer]),
                     pad_dt(ssd_a_log[layer]), jnp.repeat(ssd_d[layer], HEAD_DIM)[None, :],
                     ssd_norm_g[layer][None, :], n_ctx)
        xall = _post(att, ret_o, ssd_o, x_ctx, x_lat, lat_off, mods, layer, norm_g[layer], wo,
                     w_ff1[layer].astype(BF16), w_ff2[layer].astype(BF16), n_ctx_tiles,
                     with_ctx=not last)
        x_ctx, x_lat, lat_off = xall, xall, 0
    return xall
```

```python
import functools

import jax
import jax.numpy as jnp
from jax import lax
from jax.experimental import pallas as pl
from jax.experimental.pallas import tpu as pltpu

F32 = jnp.float32
BF16 = jnp.bfloat16

HEAD_DIM = 64
H_ATT = 6
H_KV = 2
H_RET = 4
H_SSD = 6
SSD_GROUPS = 2
SSD_STATE = 128
SSD_CONV = 5
GRID_W = 64
ROPE_THETA = 10000.0
EPS = 1e-6
LOG2_E = 1.4426950408889634

D_ATT = H_ATT * HEAD_DIM
D_KV = H_KV * HEAD_DIM
D_RET = H_RET * HEAD_DIM
D_SSD = H_SSD * HEAD_DIM
D_BC = SSD_GROUPS * SSD_STATE
D_XBC = D_SSD + 2 * D_BC
LANES = 128
D_DT = LANES
OFF_Q = 0
OFF_K = OFF_Q + D_ATT
OFF_V = OFF_K + D_KV
OFF_RET = OFF_V + D_KV
OFF_Z = OFF_RET + 4 * D_RET
OFF_XBC = OFF_Z + D_SSD
OFF_DT = OFF_XBC + D_XBC
D_IN_PAD = OFF_DT + D_DT

ROW_TILE = 256
CHUNK = 256
VMEM_LIMIT = 56 * 1024 * 1024


def _silu(x):
    return x * jax.nn.sigmoid(x)


def _split2(a):
    hi = a.astype(BF16)
    lo = (a - hi.astype(F32)).astype(BF16)
    return hi, lo


def _dot(a, b):
    return jnp.dot(a, b, preferred_element_type=F32)


def _dot_nt(a, b):
    return lax.dot_general(a, b, (((1,), (1,)), ((), ())), preferred_element_type=F32)


def _dot2_right(a, m):
    hi, lo = _split2(a)
    return _dot(hi, m) + _dot(lo, m)


PART_LANES = 16


def _pack3(a, base):
    hi = a.astype(BF16).astype(F32)
    r = a - hi
    mid = r.astype(BF16).astype(F32)
    out = hi + pltpu.roll(mid, PART_LANES, 1) + pltpu.roll(r - mid, 2 * PART_LANES, 1)
    return pltpu.roll(out, base, 1) if base else out


def _unpack3(p):
    return p + pltpu.roll(p, LANES - PART_LANES, 1) + pltpu.roll(p, LANES - 2 * PART_LANES, 1)


def _rms(x, g):
    ms = jnp.mean(x * x, axis=-1, keepdims=True)
    return x * lax.rsqrt(ms + EPS) * g


def _head_avg_matrix(n):
    r = lax.broadcasted_iota(jnp.int32, (n, n), 0) // HEAD_DIM
    c = lax.broadcasted_iota(jnp.int32, (n, n), 1) // HEAD_DIM
    return jnp.where(r == c, 1.0 / HEAD_DIM, 0.0).astype(BF16)


def _mod_kernel(c_ref, w_ref, b_ref, o_ref):
    sc = _silu(c_ref[...]).astype(BF16)
    o_ref[0] = _dot(sc, w_ref[0].astype(BF16)) + b_ref[0]


def _modulation(cc, w_mod, b_mod):
    depth, d, n = w_mod.shape
    rows = cc.shape[0]
    tn = 1536
    return pl.pallas_call(
        _mod_kernel,
        grid=(depth, n // tn),
        in_specs=[
            pl.BlockSpec((rows, d), lambda l, j: (0, 0)),
            pl.BlockSpec((1, d, tn), lambda l, j: (l, 0, j)),
            pl.BlockSpec((1, 1, tn), lambda l, j: (l, 0, j)),
        ],
        out_specs=pl.BlockSpec((1, rows, tn), lambda l, j: (l, 0, j)),
        out_shape=jax.ShapeDtypeStruct((depth, rows, n), F32),
        compiler_params=pltpu.CompilerParams(
            dimension_semantics=("arbitrary", "arbitrary"), vmem_limit_bytes=VMEM_LIMIT),
        name="modulation",
    )(cc, w_mod, b_mod.reshape(depth, 1, n))


def _mod_chunk(mod_ref, row, k):
    d = mod_ref.shape[1] // 6
    return mod_ref[pl.ds(row, 1), k * d:(k + 1) * d]


def _rope(x, cos, sin_p, sin_m):
    return x * cos + pltpu.roll(x, 16, 1) * sin_p + pltpu.roll(x, LANES - 16, 1) * sin_m


def _inproj_kernel(xc_ref, xl_ref, mod_ref, ng_ref, w_ref, cos_ref, sp_ref, sm_ref, gqk_ref,
                   qt_ref, k_ref, v_ref, ret_ref, z_ref, xbc_ref, dt_ref, *, n_ctx_tiles, ctx_row):
    is_ctx = pl.program_id(1) < n_ctx_tiles
    x = jnp.where(is_ctx, xc_ref[0], xl_ref[0])
    mod = functools.partial(_mod_chunk, mod_ref, jnp.where(is_ctx, ctx_row, pl.program_id(0)))
    h = _rms(x, ng_ref[0:1, :])
    h = h * (1.0 + mod(1)) + mod(0)
    acc = _dot(h.astype(BF16), w_ref[...])

    cos = cos_ref[...]
    sin_p = sp_ref[...]
    sin_m = sm_ref[...]

    nqk = D_ATT + D_KV
    qk = acc[:, OFF_Q:OFF_Q + nqk]
    ms = _dot2_right(qk * qk, _head_avg_matrix(nqk))
    qk = qk * lax.rsqrt(ms + EPS) * gqk_ref[...]
    for i in range(D_ATT // LANES):
        sl = slice(i * LANES, (i + 1) * LANES)
        qt_ref[0, sl, :] = _rope(qk[:, sl], cos, sin_p, sin_m).T.astype(BF16)
    k_ref[0] = _rope(qk[:, D_ATT:nqk], cos, sin_p, sin_m).astype(BF16)
    v_ref[0] = acc[:, OFF_V:OFF_V + D_KV].astype(BF16)

    kscale = HEAD_DIM ** -0.5
    for i in range(2 * D_RET // LANES):
        sl = slice(OFF_RET + i * LANES, OFF_RET + (i + 1) * LANES)
        r = _rope(acc[:, sl], cos, sin_p, sin_m)
        if i >= D_RET // LANES:
            r = r * kscale
        ret_ref[0, :, i * LANES:(i + 1) * LANES] = r.astype(BF16)
    ret_ref[0, :, 2 * D_RET:4 * D_RET] = acc[:, OFF_RET + 2 * D_RET:OFF_RET + 4 * D_RET].astype(BF16)
    z_ref[0] = acc[:, OFF_Z:OFF_Z + D_SSD].astype(BF16)
    xbc_ref[0] = acc[:, OFF_XBC:OFF_XBC + D_XBC].astype(BF16)
    dt_ref[0] = acc[:, OFF_DT:OFF_DT + D_DT]


def _token_specs(d, n_ctx_tiles, lat_off, first=0):
    ctx_spec = pl.BlockSpec((1, ROW_TILE, d),
                            lambda b, i: (b, jnp.minimum(i + first, n_ctx_tiles - 1), 0))
    lat_spec = pl.BlockSpec((1, ROW_TILE, d),
                            lambda b, i: (b, jnp.maximum(i + first, n_ctx_tiles) - lat_off, 0))
    return ctx_spec, lat_spec


def _inproj(x_ctx, x_lat, lat_off, t, mods, layer, ng, w_in, cos, sin_p, sin_m, gqk, n_ctx_tiles):
    bsz, _, d = x_lat.shape
    nt = t // ROW_TILE
    mod_spec = pl.BlockSpec((None,) + mods.shape[1:], lambda b, i: (layer, 0, 0))

    tok = lambda w: pl.BlockSpec((1, ROW_TILE, w), lambda b, i: (b, i, 0))
    rope_spec = pl.BlockSpec((ROW_TILE, LANES), lambda b, i: (i, 0))
    const = lambda shape: pl.BlockSpec(shape, lambda b, i: (0,) * len(shape))
    widths = (D_KV, D_KV, 4 * D_RET, D_SSD, D_XBC)
    qt_spec = pl.BlockSpec((1, D_ATT, ROW_TILE), lambda b, i: (b, 0, i))
    return pl.pallas_call(
        functools.partial(_inproj_kernel, n_ctx_tiles=n_ctx_tiles, ctx_row=bsz),
        grid=(bsz, nt),
        in_specs=[
            *_token_specs(d, n_ctx_tiles, lat_off),
            mod_spec,
            const((4, d)),
            pl.BlockSpec((None, d, D_IN_PAD), lambda b, i: (layer, 0, 0),
                         pipeline_mode=pl.Buffered(1)),
            rope_spec, rope_spec, rope_spec,
            const((1, D_ATT + D_KV)),
        ],
        out_specs=[qt_spec] + [tok(w) for w in widths] + [tok(D_DT)],
        out_shape=[jax.ShapeDtypeStruct((bsz, D_ATT, t), BF16)]
        + [jax.ShapeDtypeStruct((bsz, t, w), BF16) for w in widths]
        + [jax.ShapeDtypeStruct((bsz, t, D_DT), F32)],
        compiler_params=pltpu.CompilerParams(
            dimension_semantics=("arbitrary", "arbitrary"), vmem_limit_bytes=VMEM_LIMIT),
        name="inproj",
    )(x_ctx, x_lat, mods, ng, w_in, cos, sin_p, sin_m, gqk)


V_ROWS = 80
KEY_BLOCK = 256
NEG_BIG = -1e30


def _attn_kernel(qt_ref, k_ref, v_ref, o_ref, km_ref, vt_ref, acc_ref, m_ref, alpha_ref, s_ref, p_ref,
                 *, n_ctx, with_ctx):
    qi = pl.program_id(1)
    per_tile = LANES // HEAD_DIM
    n_tiles = D_ATT // LANES

    @pl.when(qi == 0)
    def _():
        k = k_ref[0].astype(F32)
        kr = pltpu.roll(k, HEAD_DIM, 1)
        low = lax.broadcasted_iota(jnp.int32, k.shape, 1) < HEAD_DIM
        km_ref[0] = jnp.where(low, k, 0.0).astype(BF16)
        km_ref[1] = jnp.where(low, 0.0, kr).astype(BF16)
        km_ref[2] = jnp.where(low, kr, 0.0).astype(BF16)
        km_ref[3] = jnp.where(low, 0.0, k).astype(BF16)
        vt = v_ref[0].astype(F32).T
        row = lax.broadcasted_iota(jnp.int32, (V_ROWS - HEAD_DIM, vt.shape[1]), 0)
        tail = jnp.where(row == 0, 1.0, 0.0)
        for g in range(H_KV):
            vg = jnp.concatenate([vt[g * HEAD_DIM:(g + 1) * HEAD_DIM], tail], axis=0).astype(BF16)
            for blk in range(vt.shape[1] // KEY_BLOCK):
                vt_ref[g, blk] = vg[:, blk * KEY_BLOCK:(blk + 1) * KEY_BLOCK]

    kv_of = lambda h: h // (H_ATT // H_KV)

    def scores(i):
        for j in range(n_tiles):
            ks = jnp.concatenate(
                [km_ref[2 * kv_of(j * per_tile + half) + half, i * KEY_BLOCK:(i + 1) * KEY_BLOCK, :]
                 for half in range(per_tile)], axis=0)
            s_ref[i % 2, j] = _dot(ks, qt_ref[0, j * LANES:(j + 1) * LANES, :])

    def exponentials(i):
        for h in range(H_ATT):
            j, half = divmod(h, per_tile)
            s = s_ref[i % 2, j, half * KEY_BLOCK:(half + 1) * KEY_BLOCK, :]
            m_old = m_ref[h]
            m_new = jnp.maximum(m_old, jnp.max(s, axis=0, keepdims=True))
            m_ref[h] = m_new
            alpha_ref[i % 2, h] = jnp.exp2(m_old - m_new)
            p_ref[i % 2, h] = jnp.exp2(s - m_new).astype(BF16)

    def weighted_values(i):
        for h in range(H_ATT):
            pv = _dot(vt_ref[kv_of(h), i], p_ref[i % 2, h])
            acc_ref[h] = acc_ref[h] * alpha_ref[i % 2, h] + pv

    def run(n_blocks):
        acc_ref[...] = jnp.zeros_like(acc_ref)
        m_ref[...] = jnp.full(m_ref.shape, NEG_BIG, F32)
        for it in range(n_blocks + 2):
            if it >= 2:
                weighted_values(it - 2)
            if it < n_blocks:
                scores(it)
            if 1 <= it <= n_blocks:
                exponentials(it - 1)
        for j in range(n_tiles):
            halves = []
            for half in range(per_tile):
                acc = acc_ref[j * per_tile + half]
                halves.append(acc[0:HEAD_DIM] * (1.0 / acc[HEAD_DIM:HEAD_DIM + 1]))
            ot = jnp.concatenate(halves, axis=0)
            o_ref[0, :, j * LANES:(j + 1) * LANES] = ot.T.astype(BF16)

    n_all = k_ref.shape[1] // KEY_BLOCK
    if with_ctx:
        @pl.when(qi == 0)
        def _():
            run(n_ctx // KEY_BLOCK)

        @pl.when(qi > 0)
        def _():
            run(n_all)
    else:
        run(n_all)


def _attention(qt, k, v, n_ctx, with_ctx):
    bsz, _, t = qt.shape
    tq = ROW_TILE
    assert n_ctx == tq and n_ctx % KEY_BLOCK == 0 and t % KEY_BLOCK == 0
    first = 0 if with_ctx else n_ctx // tq
    nq = t // tq - first
    return pl.pallas_call(
        functools.partial(_attn_kernel, n_ctx=n_ctx, with_ctx=with_ctx),
        grid=(bsz, nq),
        in_specs=[
            pl.BlockSpec((1, D_ATT, tq), lambda b, i: (b, 0, i + first)),
            pl.BlockSpec((1, t, D_KV), lambda b, i: (b, 0, 0)),
            pl.BlockSpec((1, t, D_KV), lambda b, i: (b, 0, 0)),
        ],
        out_specs=pl.BlockSpec((1, tq, D_ATT), lambda b, i: (b, i, 0)),
        out_shape=jax.ShapeDtypeStruct((bsz, nq * tq, D_ATT), BF16),
        scratch_shapes=[pltpu.VMEM((2 * H_KV, t, D_KV), BF16),
                        pltpu.VMEM((H_KV, t // KEY_BLOCK, V_ROWS, KEY_BLOCK), BF16),
                        pltpu.VMEM((H_ATT, V_ROWS, tq), F32),
                        pltpu.VMEM((H_ATT, 1, tq), F32),
                        pltpu.VMEM((2, H_ATT, 1, tq), F32),
                        pltpu.VMEM((2, D_ATT // LANES, (LANES // HEAD_DIM) * KEY_BLOCK, tq), F32),
                        pltpu.VMEM((2, H_ATT, KEY_BLOCK, tq), BF16)],
        compiler_params=pltpu.CompilerParams(
            dimension_semantics=("arbitrary", "arbitrary"), vmem_limit_bytes=VMEM_LIMIT),
        name="attention",
    )(qt, k, v)


def _log_sigmoid(x):
    return jnp.minimum(x, 0.0) - jnp.log1p(jnp.exp(-jnp.abs(x)))


def _ret_kernel(r_ref, lg_ref, gng_ref, gnb_ref, o_ref, y_ref, sf_ref, sb_ref, w_ref, dm_ref,
                *, n_ctx_chunks):
    c_len = CHUNK
    t = r_ref.shape[1]
    n_chunks = t // c_len
    lg = _log_sigmoid(lg_ref[...])
    lgf = lg[0:1, :]
    lgb = lg[1:2, :]
    dec_f = jnp.exp(c_len * lgf)
    dec_b = jnp.exp(c_len * lgb)

    @pl.when(pl.program_id(0) == 0)
    def _():
        tcol = lax.broadcasted_iota(jnp.int32, (c_len, 1), 0).astype(F32)
        w_ref[0] = jnp.exp((tcol + 1.0) * lgf)
        w_ref[1] = jnp.exp((c_len - tcol) * lgb)
        w_ref[2] = jnp.exp((c_len - 1.0 - tcol) * lgf)
        w_ref[3] = jnp.exp(tcol * lgb)
        ti = lax.broadcasted_iota(jnp.int32, (c_len, c_len), 0)
        si = lax.broadcasted_iota(jnp.int32, (c_len, c_len), 1)
        diff = (ti - si).astype(F32)
        for h in range(H_RET):
            lf = lgf[:, h * HEAD_DIM:h * HEAD_DIM + 1]
            lb = lgb[:, h * HEAD_DIM:h * HEAD_DIM + 1]
            dm_ref[h] = jnp.exp(jnp.where(diff >= 0, diff * lf, -diff * lb))

    lane = lax.broadcasted_iota(jnp.int32, (c_len, D_RET), 1)
    srow = lax.broadcasted_iota(jnp.int32, (D_RET, D_RET), 0) // HEAD_DIM
    scol = lax.broadcasted_iota(jnp.int32, (D_RET, D_RET), 1) // HEAD_DIM
    smask = srow == scol
    avg = _head_avg_matrix(D_RET)

    def chunk_rows(c):
        return pl.ds(pl.multiple_of(c * c_len, c_len), c_len)

    def state_delta(k, v, wk):
        kw = (k.astype(F32) * wk).T.astype(BF16)
        return jnp.where(smask, _dot(kw, v), 0.0)

    sf_ref[...] = jnp.zeros_like(sf_ref)
    sb_ref[...] = jnp.zeros_like(sb_ref)

    def fwd(c, carry):
        rows = chunk_rows(c)
        q = r_ref[0, rows, 0:D_RET]
        k = r_ref[0, rows, D_RET:2 * D_RET]
        v = r_ref[0, rows, 2 * D_RET:3 * D_RET]
        zero = jnp.zeros_like(q)
        y = w_ref[0] * _dot(q, sf_ref[...].astype(BF16))
        for h in range(H_RET):
            hm = (lane >= h * HEAD_DIM) & (lane < (h + 1) * HEAD_DIM)
            s = _dot_nt(jnp.where(hm, q, zero), k) * dm_ref[h]
            y = y + _dot(s.astype(BF16), jnp.where(hm, v, zero))
        y_ref[rows, :] = y
        sf_ref[...] = dec_f * sf_ref[...] + state_delta(k, v, w_ref[2])
        return carry

    lax.fori_loop(0, n_chunks, fwd, 0, unroll=3)

    def bwd(c):
        rows = chunk_rows(c)
        q = r_ref[0, rows, 0:D_RET]
        k = r_ref[0, rows, D_RET:2 * D_RET]
        v = r_ref[0, rows, 2 * D_RET:3 * D_RET]
        g = r_ref[0, rows, 3 * D_RET:4 * D_RET].astype(F32)
        y = y_ref[rows, :] + w_ref[1] * _dot(q, sb_ref[...].astype(BF16))
        sb_ref[...] = dec_b * sb_ref[...] + state_delta(k, v, w_ref[3])
        mu = _dot2_right(y, avg)
        d = y - mu
        var = _dot2_right(d * d, avg)
        yn = d * lax.rsqrt(var + EPS) * gng_ref[...] + gnb_ref[...]
        o_ref[0, rows, :] = (yn * _silu(g)).astype(BF16)

    def bwd_ctx(i, carry):
        bwd(n_ctx_chunks - 1 - i)
        return carry

    def bwd_lat(i, carry):
        bwd(n_chunks - 1 - i)
        return carry

    lax.fori_loop(0, n_ctx_chunks, bwd_ctx, 0)
    lax.fori_loop(0, n_chunks - n_ctx_chunks, bwd_lat, 0, unroll=4)


def _retention(ret, lg, gng, gnb, n_ctx):
    bsz, t, _ = ret.shape
    const = lambda shape: pl.BlockSpec(shape, lambda b: (0,) * len(shape))
    return pl.pallas_call(
        functools.partial(_ret_kernel, n_ctx_chunks=n_ctx // CHUNK),
        grid=(bsz,),
        in_specs=[
            pl.BlockSpec((1, t, 4 * D_RET), lambda b: (b, 0, 0)),
            const((2, D_RET)), const((1, D_RET)), const((1, D_RET)),
        ],
        out_specs=pl.BlockSpec((1, t, D_RET), lambda b: (b, 0, 0)),
        out_shape=jax.ShapeDtypeStruct((bsz, t, D_RET), BF16),
        scratch_shapes=[pltpu.VMEM((t, D_RET), F32), pltpu.VMEM((D_RET, D_RET), F32),
                        pltpu.VMEM((D_RET, D_RET), F32), pltpu.VMEM((4, CHUNK, D_RET), F32),
                        pltpu.VMEM((H_RET, CHUNK, CHUNK), F32)],
        compiler_params=pltpu.CompilerParams(
            dimension_semantics=("arbitrary",), vmem_limit_bytes=VMEM_LIMIT),
        name="retention",
    )(ret, lg, gng, gnb)


def _softplus(x):
    return jnp.maximum(x, 0.0) + jnp.log1p(jnp.exp(-jnp.abs(x)))


def _ssd_kernel(z_ref, xbc_ref, dt_ref, cw_ref, cb_ref, dtb_ref, alog_ref, dskip_ref, ng_ref,
                o_ref, xpad_ref, xc_ref, y_ref, eb_ref, dsb_ref, sf_ref, sb_ref, ce_all, g_all, sc_all,
                *, n_ctx):
    c_len = CHUNK
    t = xbc_ref.shape[1]
    n_chunks = t // c_len
    n_ctx_chunks = n_ctx // c_len
    pad = 8
    half = SSD_CONV // 2

    zpad = jnp.zeros((pad, D_XBC), F32)
    xpad_ref[0:pad, :] = zpad
    xpad_ref[pad + n_ctx:2 * pad + n_ctx, :] = zpad
    xpad_ref[2 * pad + t:3 * pad + t, :] = zpad

    def fill(c, carry):
        src = pl.ds(pl.multiple_of(c * c_len, c_len), c_len)
        off = jnp.where(c < n_ctx_chunks, pad, 2 * pad)
        dst = pl.ds(pl.multiple_of(c * c_len + off, pad), c_len)
        xpad_ref[dst, :] = xbc_ref[0, src, :].astype(F32)
        return carry

    lax.fori_loop(0, n_chunks, fill, 0)

    for c in range(n_chunks):
        base = c * c_len + (pad if c < n_ctx_chunks else 2 * pad)
        acc = jnp.broadcast_to(cb_ref[...], (c_len, D_XBC))
        for j in range(SSD_CONV):
            acc = acc + cw_ref[j:j + 1, :] * xpad_ref[base + j - half:base + j - half + c_len, :]
        xc_ref[c * c_len:(c + 1) * c_len, :] = _silu(acc).astype(BF16)

    nd = 2 * H_SSD
    lane_dt = lax.broadcasted_iota(jnp.int32, (1, D_DT), 1)
    live = lane_dt < nd
    a_vec = jnp.where(live, -jnp.exp(alog_ref[...]), 0.0)
    is_f = lane_dt < H_SSD
    ti = lax.broadcasted_iota(jnp.int32, (c_len, c_len), 0)
    si = lax.broadcasted_iota(jnp.int32, (c_len, c_len), 1)
    causal = si <= ti
    tri_l = jnp.where(causal, 1.0, 0.0).astype(BF16)
    tri_u = jnp.where(si >= ti, 1.0, 0.0).astype(BF16)
    n_exp = 2 * D_SSD
    er = lax.broadcasted_iota(jnp.int32, (D_DT, 2 * n_exp), 0)
    ej = lax.broadcasted_iota(jnp.int32, (D_DT, 2 * n_exp), 1)
    rel = er - jnp.where(ej >= n_exp, 3 * PART_LANES, 0)
    col = jnp.where(ej >= n_exp, ej - n_exp, ej) // HEAD_DIM
    expand = jnp.where((rel == col) | (rel == col + PART_LANES) | (rel == col + 2 * PART_LANES),
                       1.0, 0.0).astype(BF16)
    lane_w = lax.broadcasted_iota(jnp.int32, (c_len, 2 * LANES), 1)
    lane_c = lax.broadcasted_iota(jnp.int32, (c_len, D_BC), 1)
    srow = lax.broadcasted_iota(jnp.int32, (D_BC, D_SSD), 0) // SSD_STATE
    scol = lax.broadcasted_iota(jnp.int32, (D_BC, D_SSD), 1) // (D_SSD // SSD_GROUPS)
    smask = srow == scol
    heads_per_group = H_SSD // SSD_GROUPS

    def chunk_rows(c):
        return pl.ds(pl.multiple_of(c * c_len, c_len), c_len)

    sf_ref[...] = jnp.zeros_like(sf_ref)
    sb_ref[...] = jnp.zeros_like(sb_ref)

    def fwd_chunk(c, slot):
        ce_ref, g_ref, sc_ref = ce_all.at[slot], g_all.at[slot], sc_all.at[slot]
        rows = chunk_rows(c)
        xs = xc_ref[rows, 0:D_SSD]
        bm = xc_ref[rows, D_SSD:D_SSD + D_BC]
        cm = xc_ref[rows, D_SSD + D_BC:D_XBC]
        dt = _softplus(dt_ref[0, rows, :] + dtb_ref[...])
        la = _pack3(dt * a_vec, 0).astype(BF16)
        cum = jnp.where(is_f, _unpack3(_dot(tri_l, la)), _unpack3(_dot(tri_u, la)))
        cum = jnp.where(live, cum, 0.0)
        edge = jnp.where(is_f, cum[c_len - 1:c_len, :], cum[0:1, :])
        wk = jnp.where(live, jnp.exp(edge - cum) * dt, 0.0)
        packed = (_pack3(cum, 0) + _pack3(wk, 3 * PART_LANES)).astype(BF16)
        ce_ref[...] = _dot(packed, expand)
        key_t = (cum - jnp.log(dt)).T

        e_f = jnp.exp(ce_ref[:, 0:D_SSD])
        eb_ref[rows, :] = jnp.exp(ce_ref[:, D_SSD:n_exp])
        y_ref[rows, :] = e_f * _dot(cm, sf_ref[...].astype(BF16))
        zc = jnp.zeros_like(cm)
        for g in range(SSD_GROUPS):
            gm = (lane_c >= g * SSD_STATE) & (lane_c < (g + 1) * SSD_STATE)
            g_ref[...] = _dot_nt(jnp.where(gm, cm, zc), bm)
            win = slice(g * LANES, (g + 2) * LANES)
            xw = xs[:, win]
            zw = jnp.zeros_like(xw)
            yg = None
            for hh in range(heads_per_group):
                h = g * heads_per_group + hh
                d_f = cum[:, h:h + 1] - key_t[h:h + 1, :]
                d_b = cum[:, H_SSD + h:H_SSD + h + 1] - key_t[H_SSD + h:H_SSD + h + 1, :]
                sc_ref[hh] = (jnp.exp(jnp.where(causal, d_f, d_b)) * g_ref[...]).astype(BF16)
                lo = h * HEAD_DIM - g * LANES
                hm = (lane_w >= lo) & (lane_w < lo + HEAD_DIM)
                part = _dot(sc_ref[hh], jnp.where(hm, xw, zw))
                yg = part if yg is None else yg + part
            y_ref[rows, win] = y_ref[rows, win] + yg

        xs_f = xs.astype(F32)
        bm_t = bm.astype(F32).T.astype(BF16)
        ds_f = _dot(bm_t, (xs_f * ce_ref[:, n_exp:n_exp + D_SSD]).astype(BF16))
        ds_b = _dot(bm_t, (xs_f * ce_ref[:, n_exp + D_SSD:2 * n_exp]).astype(BF16))
        sf_ref[...] = e_f[c_len - 1:c_len, :] * sf_ref[...] + jnp.where(smask, ds_f, 0.0)
        dsb_ref[c] = jnp.where(smask, ds_b, 0.0)

    unroll = ce_all.shape[0]

    def fwd(i, carry):
        for k in range(unroll):
            fwd_chunk(i * unroll + k, k)
        return carry

    lax.fori_loop(0, n_chunks // unroll, fwd, 0)

    def bwd(c):
        rows = chunk_rows(c)
        xs_f = xc_ref[rows, 0:D_SSD].astype(F32)
        cm = xc_ref[rows, D_SSD + D_BC:D_XBC]
        e_b = eb_ref[rows, :]
        y = y_ref[rows, :] + e_b * _dot(cm, sb_ref[...].astype(BF16))
        sb_ref[...] = e_b[0:1, :] * sb_ref[...] + dsb_ref[c]
        y = y + dskip_ref[...] * xs_f
        u = y * _silu(z_ref[0, rows, :].astype(F32))
        o_ref[0, rows, :] = _rms(u, ng_ref[...]).astype(BF16)

    def bwd_ctx(i, carry):
        bwd(n_ctx_chunks - 1 - i)
        return carry

    def bwd_lat(i, carry):
        bwd(n_chunks - 1 - i)
        return carry

    lax.fori_loop(0, n_ctx_chunks, bwd_ctx, 0)
    lax.fori_loop(0, n_chunks - n_ctx_chunks, bwd_lat, 0, unroll=2)


def _ssd(z, xbc, dt, conv_w, conv_b, dt_bias, a_log, d_skip, norm_g, n_ctx):
    bsz, t, _ = xbc.shape
    n_chunks = t // CHUNK
    unroll = next(u for u in (3, 2, 1) if n_chunks % u == 0)
    const = lambda shape: pl.BlockSpec(shape, lambda b: (0,) * len(shape))
    seq = lambda w: pl.BlockSpec((1, t, w), lambda b: (b, 0, 0))
    return pl.pallas_call(
        functools.partial(_ssd_kernel, n_ctx=n_ctx),
        grid=(bsz,),
        in_specs=[seq(D_SSD), seq(D_XBC), seq(D_DT),
                  const((8, D_XBC)), const((1, D_XBC)), const((1, D_DT)), const((1, D_DT)),
                  const((1, D_SSD)), const((1, D_SSD))],
        out_specs=seq(D_SSD),
        out_shape=jax.ShapeDtypeStruct((bsz, t, D_SSD), BF16),
        scratch_shapes=[
            pltpu.VMEM((t + 24, D_XBC), F32),
            pltpu.VMEM((t, D_XBC), BF16),
            pltpu.VMEM((t, D_SSD), F32),
            pltpu.VMEM((t, D_SSD), F32),
            pltpu.VMEM((n_chunks, D_BC, D_SSD), F32),
            pltpu.VMEM((D_BC, D_SSD), F32),
            pltpu.VMEM((D_BC, D_SSD), F32),
            pltpu.VMEM((unroll, CHUNK, 4 * D_SSD), F32),
            pltpu.VMEM((unroll, CHUNK, CHUNK), F32),
            pltpu.VMEM((unroll, H_SSD // SSD_GROUPS, CHUNK, CHUNK), BF16),
        ],
        compiler_params=pltpu.CompilerParams(
            dimension_semantics=("arbitrary",), vmem_limit_bytes=VMEM_LIMIT),
        name="ssd",
    )(z, xbc, dt, conv_w, conv_b, dt_bias, a_log, d_skip, norm_g)


def _post_kernel(att_ref, ret_ref, ssd_ref, xc_ref, xl_ref, mod_ref, ng_ref, wo_ref, w1_ref, w2_ref,
                 o_ref, *, ff_chunk, n_ctx_tiles, first, ctx_row):
    is_ctx = pl.program_id(1) + first < n_ctx_tiles
    x = jnp.where(is_ctx, xc_ref[0], xl_ref[0])
    mod = functools.partial(_mod_chunk, mod_ref, jnp.where(is_ctx, ctx_row, pl.program_id(0)))
    mix = jnp.concatenate([att_ref[0], ret_ref[0], ssd_ref[0]], axis=-1)
    o = _dot(mix, wo_ref[...])
    x1 = x + mod(2) * _rms(o, ng_ref[1:2, :])
    h2 = _rms(x1, ng_ref[2:3, :]) * (1.0 + mod(4)) + mod(3)
    h2 = h2.astype(BF16)
    d_ff = w1_ref.shape[1]
    acc = jnp.zeros(x1.shape, F32)
    for j in range(d_ff // ff_chunk):
        sl = slice(j * ff_chunk, (j + 1) * ff_chunk)
        a = jnp.maximum(_dot(h2, w1_ref[:, sl]), 0.0)
        acc = acc + _dot((a * a).astype(BF16), w2_ref[sl, :])
    o_ref[0] = x1 + mod(5) * _rms(acc, ng_ref[3:4, :])


def _post(att, ret, ssd, x_ctx, x_lat, lat_off, mods, layer, ng, w_out, w_ff1, w_ff2, n_ctx_tiles,
          with_ctx):
    bsz, t, _ = ret.shape
    d = x_lat.shape[2]
    d_ff = w_ff1.shape[2]
    first = 0 if with_ctx else n_ctx_tiles
    nt = t // ROW_TILE - first
    assert att.shape[1] == nt * ROW_TILE
    mod_spec = pl.BlockSpec((None,) + mods.shape[1:], lambda b, i: (layer, 0, 0))
    tok = lambda w: pl.BlockSpec((1, ROW_TILE, w), lambda b, i: (b, i + first, 0))
    const = lambda shape: pl.BlockSpec(shape, lambda b, i: (0,) * len(shape),
                                       pipeline_mode=pl.Buffered(1))
    weight = lambda shape: pl.BlockSpec((None,) + shape, lambda b, i: (layer, 0, 0),
                                        pipeline_mode=pl.Buffered(1))
    return pl.pallas_call(
        functools.partial(_post_kernel, ff_chunk=1024, n_ctx_tiles=n_ctx_tiles, first=first,
                          ctx_row=bsz),
        grid=(bsz, nt),
        in_specs=[pl.BlockSpec((1, ROW_TILE, D_ATT), lambda b, i: (b, i, 0)), tok(D_RET), tok(D_SSD),
                  *_token_specs(d, n_ctx_tiles, lat_off, first),
                  mod_spec,
                  const((4, d)), weight((d, d)), weight((d, d_ff)), weight((d_ff, d))],
        out_specs=pl.BlockSpec((1, ROW_TILE, d), lambda b, i: (b, i, 0)),
        out_shape=jax.ShapeDtypeStruct((bsz, nt * ROW_TILE, d), F32),
        compiler_params=pltpu.CompilerParams(
            dimension_semantics=("arbitrary", "arbitrary"), vmem_limit_bytes=VMEM_LIMIT),
        name="post",
    )(att, ret, ssd, x_ctx, x_lat, mods, ng, w_out, w_ff1, w_ff2)


def _rope_tables(n, n_ctx):
    rows = n // GRID_W
    row = jnp.broadcast_to(jnp.arange(rows)[:, None], (rows, GRID_W)).reshape(n)
    col = jnp.broadcast_to(jnp.arange(GRID_W)[None, :], (rows, GRID_W)).reshape(n)
    half = HEAD_DIM // 2
    inv_freq = ROPE_THETA ** (-jnp.arange(0, half, 2, dtype=F32) / half)
    ang = jnp.stack([row, col], axis=-1).astype(F32)[:, :, None] * inv_freq
    ang = jnp.concatenate([ang, ang], axis=-1).reshape(n, HEAD_DIM)
    ang = jnp.tile(ang, (1, LANES // HEAD_DIM))
    cos = jnp.cos(ang)
    sin = jnp.sin(ang)
    upper = (jnp.arange(LANES) % half) >= half // 2
    sin_p = jnp.where(upper, sin, 0.0)
    sin_m = jnp.where(upper, 0.0, -sin)
    ident = lambda v, a: jnp.concatenate([jnp.full((n_ctx, LANES), v, F32), a], axis=0)
    return ident(1.0, cos), ident(0.0, sin_p), ident(0.0, sin_m)


def kernel(x, c, ctx, c_ctx, w_mod, b_mod, norm_g, w_in, w_out, q_norm_g, k_norm_g, ret_decay_logit,
           ret_gn_g, ret_gn_b, ssd_conv_w, ssd_conv_b, ssd_dt_bias, ssd_a_log, ssd_d, ssd_norm_g,
           w_ff1, w_ff2):
    bsz, n, d = x.shape
    n_ctx = ctx.shape[1]
    depth = w_mod.shape[0]
    assert n % ROW_TILE == 0 and n_ctx % ROW_TILE == 0 and n_ctx % CHUNK == 0 and n % CHUNK == 0
    assert w_in.shape[2] == OFF_DT + 2 * H_SSD
    n_ctx_tiles = n_ctx // ROW_TILE

    n_rows = -(-(bsz + 1) // 8) * 8
    cc = jnp.concatenate([c, c_ctx[None, :], jnp.zeros((n_rows - bsz - 1, d), F32)], axis=0)
    mods = _modulation(cc, w_mod, b_mod)

    cos, sin_p, sin_m = _rope_tables(n, n_ctx)
    n_dt = 2 * H_SSD
    pad_dt = lambda a: jnp.pad(a.reshape(1, n_dt), ((0, 0), (0, D_DT - n_dt)))

    x_ctx, x_lat, lat_off = ctx, x, n_ctx_tiles
    wi = jnp.pad(w_in, ((0, 0), (0, 0), (0, D_IN_PAD - w_in.shape[2]))).astype(BF16)
    wo, w1, w2 = w_out.astype(BF16), w_ff1.astype(BF16), w_ff2.astype(BF16)
    for layer in range(depth):
        last = layer == depth - 1
        gqk = jnp.concatenate([jnp.tile(q_norm_g[layer], H_ATT) * (HEAD_DIM ** -0.5 * LOG2_E),
                               jnp.tile(k_norm_g[layer], H_KV)])[None, :]

        q, k, v, ret, z, xbc, dt = _inproj(x_ctx, x_lat, lat_off, n_ctx + n, mods, layer,
                                           norm_g[layer], wi, cos, sin_p, sin_m, gqk, n_ctx_tiles)
        att = _attention(q, k, v, n_ctx, with_ctx=not last)
        lg = jnp.repeat(ret_decay_logit[layer], HEAD_DIM, axis=1)
        ret_o = _retention(ret, lg, ret_gn_g[layer][None, :], ret_gn_b[layer][None, :], n_ctx)
        conv_w = jnp.pad(ssd_conv_w[layer], ((0, 8 - SSD_CONV), (0, 0)))
        ssd_o = _ssd(z, xbc, dt, conv_w, ssd_conv_b[layer][None, :], pad_dt(ssd_dt_bias[layer]),
                     pad_dt(ssd_a_log[layer]), jnp.repeat(ssd_d[layer], HEAD_DIM)[None, :],
                     ssd_norm_g[layer][None, :], n_ctx)
        xall = _post(att, ret_o, ssd_o, x_ctx, x_lat, lat_off, mods, layer, norm_g[layer], wo, w1, w2,
                     n_ctx_tiles, with_ctx=not last)
        x_ctx, x_lat, lat_off = xall, xall, 0
    return xall
```

```python
import functools

import jax
import jax.numpy as jnp
from jax import lax
from jax.experimental import pallas as pl
from jax.experimental.pallas import tpu as pltpu

F32 = jnp.float32
BF16 = jnp.bfloat16

HEAD_DIM = 64
H_ATT = 6
H_KV = 2
H_RET = 4
H_SSD = 6
SSD_GROUPS = 2
SSD_STATE = 128
SSD_CONV = 5
GRID_W = 64
ROPE_THETA = 10000.0
EPS = 1e-6
LOG2_E = 1.4426950408889634

D_ATT = H_ATT * HEAD_DIM
D_KV = H_KV * HEAD_DIM
D_RET = H_RET * HEAD_DIM
D_SSD = H_SSD * HEAD_DIM
D_BC = SSD_GROUPS * SSD_STATE
D_XBC = D_SSD + 2 * D_BC
LANES = 128
D_DT = LANES
OFF_Q = 0
OFF_K = OFF_Q + D_ATT
OFF_V = OFF_K + D_KV
OFF_RET = OFF_V + D_KV
OFF_Z = OFF_RET + 4 * D_RET
OFF_XBC = OFF_Z + D_SSD
OFF_DT = OFF_XBC + D_XBC
D_IN_PAD = OFF_DT + D_DT

ROW_TILE = 256
CHUNK = 256
VMEM_LIMIT = 56 * 1024 * 1024


def _silu(x):
    return x * jax.nn.sigmoid(x)


def _split2(a):
    hi = a.astype(BF16)
    lo = (a - hi.astype(F32)).astype(BF16)
    return hi, lo


def _dot(a, b):
    return jnp.dot(a, b, preferred_element_type=F32)


def _dot_nt(a, b):
    return lax.dot_general(a, b, (((1,), (1,)), ((), ())), preferred_element_type=F32)


def _dot2_right(a, m):
    hi, lo = _split2(a)
    return _dot(hi, m) + _dot(lo, m)


PART_LANES = 16


def _pack3(a, base):
    hi = a.astype(BF16).astype(F32)
    r = a - hi
    mid = r.astype(BF16).astype(F32)
    out = hi + pltpu.roll(mid, PART_LANES, 1) + pltpu.roll(r - mid, 2 * PART_LANES, 1)
    return pltpu.roll(out, base, 1) if base else out


def _unpack3(p):
    return p + pltpu.roll(p, LANES - PART_LANES, 1) + pltpu.roll(p, LANES - 2 * PART_LANES, 1)


def _rms(x, g):
    ms = jnp.mean(x * x, axis=-1, keepdims=True)
    return x * lax.rsqrt(ms + EPS) * g


def _head_avg_matrix(n):
    r = lax.broadcasted_iota(jnp.int32, (n, n), 0) // HEAD_DIM
    c = lax.broadcasted_iota(jnp.int32, (n, n), 1) // HEAD_DIM
    return jnp.where(r == c, 1.0 / HEAD_DIM, 0.0).astype(BF16)


def _mod_kernel(c_ref, w_ref, b_ref, o_ref):
    sc = _silu(c_ref[...]).astype(BF16)
    o_ref[0] = _dot(sc, w_ref[0].astype(BF16)) + b_ref[0]


def _modulation(cc, w_mod, b_mod):
    depth, d, n = w_mod.shape
    rows = cc.shape[0]
    tn = 1536
    return pl.pallas_call(
        _mod_kernel,
        grid=(depth, n // tn),
        in_specs=[
            pl.BlockSpec((rows, d), lambda l, j: (0, 0)),
            pl.BlockSpec((1, d, tn), lambda l, j: (l, 0, j)),
            pl.BlockSpec((1, 1, tn), lambda l, j: (l, 0, j)),
        ],
        out_specs=pl.BlockSpec((1, rows, tn), lambda l, j: (l, 0, j)),
        out_shape=jax.ShapeDtypeStruct((depth, rows, n), F32),
        compiler_params=pltpu.CompilerParams(
            dimension_semantics=("arbitrary", "arbitrary"), vmem_limit_bytes=VMEM_LIMIT),
        name="modulation",
    )(cc, w_mod, b_mod.reshape(depth, 1, n))


def _mod_chunk(mod_ref, row, k):
    d = mod_ref.shape[1] // 6
    return mod_ref[pl.ds(row, 1), k * d:(k + 1) * d]


def _rope(x, cos, sin_p, sin_m):
    return x * cos + pltpu.roll(x, 16, 1) * sin_p + pltpu.roll(x, LANES - 16, 1) * sin_m


def _inproj_kernel(xc_ref, xl_ref, mod_ref, ng_ref, w_ref, cos_ref, sp_ref, sm_ref, gqk_ref,
                   qt_ref, k_ref, v_ref, ret_ref, z_ref, xbc_ref, dt_ref, *, n_ctx_tiles, ctx_row):
    is_ctx = pl.program_id(1) < n_ctx_tiles
    x = jnp.where(is_ctx, xc_ref[0], xl_ref[0])
    mod = functools.partial(_mod_chunk, mod_ref, jnp.where(is_ctx, ctx_row, pl.program_id(0)))
    h = _rms(x, ng_ref[0:1, :])
    h = h * (1.0 + mod(1)) + mod(0)
    acc = _dot(h.astype(BF16), w_ref[...])

    cos = cos_ref[...]
    sin_p = sp_ref[...]
    sin_m = sm_ref[...]

    nqk = D_ATT + D_KV
    qk = acc[:, OFF_Q:OFF_Q + nqk]
    ms = _dot((qk * qk).astype(BF16), _head_avg_matrix(nqk))
    qk = qk * lax.rsqrt(ms + EPS) * gqk_ref[...]
    for i in range(D_ATT // LANES):
        sl = slice(i * LANES, (i + 1) * LANES)
        qt_ref[0, sl, :] = _rope(qk[:, sl], cos, sin_p, sin_m).T.astype(BF16)
    k_ref[0] = _rope(qk[:, D_ATT:nqk], cos, sin_p, sin_m).astype(BF16)
    v_ref[0] = acc[:, OFF_V:OFF_V + D_KV].astype(BF16)

    kscale = HEAD_DIM ** -0.5
    for i in range(2 * D_RET // LANES):
        sl = slice(OFF_RET + i * LANES, OFF_RET + (i + 1) * LANES)
        r = _rope(acc[:, sl], cos, sin_p, sin_m)
        if i >= D_RET // LANES:
            r = r * kscale
        ret_ref[0, :, i * LANES:(i + 1) * LANES] = r.astype(BF16)
    ret_ref[0, :, 2 * D_RET:4 * D_RET] = acc[:, OFF_RET + 2 * D_RET:OFF_RET + 4 * D_RET].astype(BF16)
    z_ref[0] = acc[:, OFF_Z:OFF_Z + D_SSD].astype(BF16)
    xbc_ref[0] = acc[:, OFF_XBC:OFF_XBC + D_XBC].astype(BF16)
    dt_ref[0] = acc[:, OFF_DT:OFF_DT + D_DT]


def _token_specs(d, n_ctx_tiles, lat_off, first=0):
    ctx_spec = pl.BlockSpec((1, ROW_TILE, d),
                            lambda b, i: (b, jnp.minimum(i + first, n_ctx_tiles - 1), 0))
    lat_spec = pl.BlockSpec((1, ROW_TILE, d),
                            lambda b, i: (b, jnp.maximum(i + first, n_ctx_tiles) - lat_off, 0))
    return ctx_spec, lat_spec


def _inproj(x_ctx, x_lat, lat_off, t, mods, layer, ng, w_in, cos, sin_p, sin_m, gqk, n_ctx_tiles):
    bsz, _, d = x_lat.shape
    nt = t // ROW_TILE
    mod_spec = pl.BlockSpec((None,) + mods.shape[1:], lambda b, i: (layer, 0, 0))

    tok = lambda w: pl.BlockSpec((1, ROW_TILE, w), lambda b, i: (b, i, 0))
    rope_spec = pl.BlockSpec((ROW_TILE, LANES), lambda b, i: (i, 0))
    const = lambda shape: pl.BlockSpec(shape, lambda b, i: (0,) * len(shape))
    widths = (D_KV, D_KV, 4 * D_RET, D_SSD, D_XBC)
    qt_spec = pl.BlockSpec((1, D_ATT, ROW_TILE), lambda b, i: (b, 0, i))
    return pl.pallas_call(
        functools.partial(_inproj_kernel, n_ctx_tiles=n_ctx_tiles, ctx_row=bsz),
        grid=(bsz, nt),
        in_specs=[
            *_token_specs(d, n_ctx_tiles, lat_off),
            mod_spec,
            const((4, d)),
            pl.BlockSpec((None, d, D_IN_PAD), lambda b, i: (layer, 0, 0),
                         pipeline_mode=pl.Buffered(1)),
            rope_spec, rope_spec, rope_spec,
            const((1, D_ATT + D_KV)),
        ],
        out_specs=[qt_spec] + [tok(w) for w in widths] + [tok(D_DT)],
        out_shape=[jax.ShapeDtypeStruct((bsz, D_ATT, t), BF16)]
        + [jax.ShapeDtypeStruct((bsz, t, w), BF16) for w in widths]
        + [jax.ShapeDtypeStruct((bsz, t, D_DT), F32)],
        compiler_params=pltpu.CompilerParams(
            dimension_semantics=("arbitrary", "arbitrary"), vmem_limit_bytes=VMEM_LIMIT),
        name="inproj",
    )(x_ctx, x_lat, mods, ng, w_in, cos, sin_p, sin_m, gqk)


V_ROWS = 80
KEY_BLOCK = 256
NEG_BIG = -1e30


def _attn_kernel(qt_ref, k_ref, v_ref, o_ref, km_ref, vt_ref, acc_ref, m_ref, alpha_ref, s_ref, p_ref,
                 *, n_ctx, with_ctx):
    qi = pl.program_id(1)
    per_tile = LANES // HEAD_DIM
    n_tiles = D_ATT // LANES

    @pl.when(qi == 0)
    def _():
        k = k_ref[0].astype(F32)
        kr = pltpu.roll(k, HEAD_DIM, 1)
        low = lax.broadcasted_iota(jnp.int32, k.shape, 1) < HEAD_DIM
        km_ref[0] = jnp.where(low, k, 0.0).astype(BF16)
        km_ref[1] = jnp.where(low, 0.0, kr).astype(BF16)
        km_ref[2] = jnp.where(low, kr, 0.0).astype(BF16)
        km_ref[3] = jnp.where(low, 0.0, k).astype(BF16)
        vt = v_ref[0].astype(F32).T
        row = lax.broadcasted_iota(jnp.int32, (V_ROWS - HEAD_DIM, vt.shape[1]), 0)
        tail = jnp.where(row == 0, 1.0, 0.0)
        for g in range(H_KV):
            vg = jnp.concatenate([vt[g * HEAD_DIM:(g + 1) * HEAD_DIM], tail], axis=0).astype(BF16)
            for blk in range(vt.shape[1] // KEY_BLOCK):
                vt_ref[g, blk] = vg[:, blk * KEY_BLOCK:(blk + 1) * KEY_BLOCK]

    kv_of = lambda h: h // (H_ATT // H_KV)

    def scores(i):
        for j in range(n_tiles):
            ks = jnp.concatenate(
                [km_ref[2 * kv_of(j * per_tile + half) + half, i * KEY_BLOCK:(i + 1) * KEY_BLOCK, :]
                 for half in range(per_tile)], axis=0)
            s_ref[i % 2, j] = _dot(ks, qt_ref[0, j * LANES:(j + 1) * LANES, :])

    def exponentials(i):
        for h in range(H_ATT):
            j, half = divmod(h, per_tile)
            s = s_ref[i % 2, j, half * KEY_BLOCK:(half + 1) * KEY_BLOCK, :]
            m_old = m_ref[h]
            m_new = jnp.maximum(m_old, jnp.max(s, axis=0, keepdims=True))
            m_ref[h] = m_new
            alpha_ref[i % 2, h] = jnp.exp2(m_old - m_new)
            p_ref[i % 2, h] = jnp.exp2(s - m_new).astype(BF16)

    def weighted_values(i):
        for h in range(H_ATT):
            pv = _dot(vt_ref[kv_of(h), i], p_ref[i % 2, h])
            acc_ref[h] = acc_ref[h] * alpha_ref[i % 2, h] + pv

    def run(n_blocks):
        acc_ref[...] = jnp.zeros_like(acc_ref)
        m_ref[...] = jnp.full(m_ref.shape, NEG_BIG, F32)
        for it in range(n_blocks + 2):
            if it >= 2:
                weighted_values(it - 2)
            if it < n_blocks:
                scores(it)
            if 1 <= it <= n_blocks:
                exponentials(it - 1)
        for j in range(n_tiles):
            halves = []
            for half in range(per_tile):
                acc = acc_ref[j * per_tile + half]
                halves.append(acc[0:HEAD_DIM] * (1.0 / acc[HEAD_DIM:HEAD_DIM + 1]))
            ot = jnp.concatenate(halves, axis=0)
            o_ref[0, :, j * LANES:(j + 1) * LANES] = ot.T.astype(BF16)

    n_all = k_ref.shape[1] // KEY_BLOCK
    if with_ctx:
        @pl.when(qi == 0)
        def _():
            run(n_ctx // KEY_BLOCK)

        @pl.when(qi > 0)
        def _():
            run(n_all)
    else:
        run(n_all)


def _attention(qt, k, v, n_ctx, with_ctx):
    bsz, _, t = qt.shape
    tq = ROW_TILE
    assert n_ctx == tq and n_ctx % KEY_BLOCK == 0 and t % KEY_BLOCK == 0
    first = 0 if with_ctx else n_ctx // tq
    nq = t // tq - first
    return pl.pallas_call(
        functools.partial(_attn_kernel, n_ctx=n_ctx, with_ctx=with_ctx),
        grid=(bsz, nq),
        in_specs=[
            pl.BlockSpec((1, D_ATT, tq), lambda b, i: (b, 0, i + first)),
            pl.BlockSpec((1, t, D_KV), lambda b, i: (b, 0, 0)),
            pl.BlockSpec((1, t, D_KV), lambda b, i: (b, 0, 0)),
        ],
        out_specs=pl.BlockSpec((1, tq, D_ATT), lambda b, i: (b, i, 0)),
        out_shape=jax.ShapeDtypeStruct((bsz, nq * tq, D_ATT), BF16),
        scratch_shapes=[pltpu.VMEM((2 * H_KV, t, D_KV), BF16),
                        pltpu.VMEM((H_KV, t // KEY_BLOCK, V_ROWS, KEY_BLOCK), BF16),
                        pltpu.VMEM((H_ATT, V_ROWS, tq), F32),
                        pltpu.VMEM((H_ATT, 1, tq), F32),
                        pltpu.VMEM((2, H_ATT, 1, tq), F32),
                        pltpu.VMEM((2, D_ATT // LANES, (LANES // HEAD_DIM) * KEY_BLOCK, tq), F32),
                        pltpu.VMEM((2, H_ATT, KEY_BLOCK, tq), BF16)],
        compiler_params=pltpu.CompilerParams(
            dimension_semantics=("arbitrary", "arbitrary"), vmem_limit_bytes=VMEM_LIMIT),
        name="attention",
    )(qt, k, v)


def _log_sigmoid(x):
    return jnp.minimum(x, 0.0) - jnp.log1p(jnp.exp(-jnp.abs(x)))


def _ret_kernel(r_ref, lg_ref, gng_ref, gnb_ref, o_ref, y_ref, sf_ref, sb_ref, w_ref, dm_ref,
                *, n_ctx_chunks):
    c_len = CHUNK
    t = r_ref.shape[1]
    n_chunks = t // c_len
    lg = _log_sigmoid(lg_ref[...])
    lgf = lg[0:1, :]
    lgb = lg[1:2, :]
    dec_f = jnp.exp(c_len * lgf)
    dec_b = jnp.exp(c_len * lgb)

    @pl.when(pl.program_id(0) == 0)
    def _():
        tcol = lax.broadcasted_iota(jnp.int32, (c_len, 1), 0).astype(F32)
        w_ref[0] = jnp.exp((tcol + 1.0) * lgf)
        w_ref[1] = jnp.exp((c_len - tcol) * lgb)
        w_ref[2] = jnp.exp((c_len - 1.0 - tcol) * lgf)
        w_ref[3] = jnp.exp(tcol * lgb)
        ti = lax.broadcasted_iota(jnp.int32, (c_len, c_len), 0)
        si = lax.broadcasted_iota(jnp.int32, (c_len, c_len), 1)
        diff = (ti - si).astype(F32)
        for h in range(H_RET):
            lf = lgf[:, h * HEAD_DIM:h * HEAD_DIM + 1]
            lb = lgb[:, h * HEAD_DIM:h * HEAD_DIM + 1]
            dm_ref[h] = jnp.exp(jnp.where(diff >= 0, diff * lf, -diff * lb))

    lane = lax.broadcasted_iota(jnp.int32, (c_len, D_RET), 1)
    srow = lax.broadcasted_iota(jnp.int32, (D_RET, D_RET), 0) // HEAD_DIM
    scol = lax.broadcasted_iota(jnp.int32, (D_RET, D_RET), 1) // HEAD_DIM
    smask = srow == scol
    avg = _head_avg_matrix(D_RET)

    def chunk_rows(c):
        return pl.ds(pl.multiple_of(c * c_len, c_len), c_len)

    def state_delta(k, v, wk):
        kw = (k.astype(F32) * wk).T.astype(BF16)
        return jnp.where(smask, _dot(kw, v), 0.0)

    sf_ref[...] = jnp.zeros_like(sf_ref)
    sb_ref[...] = jnp.zeros_like(sb_ref)

    def fwd(c, carry):
        rows = chunk_rows(c)
        q = r_ref[0, rows, 0:D_RET]
        k = r_ref[0, rows, D_RET:2 * D_RET]
        v = r_ref[0, rows, 2 * D_RET:3 * D_RET]
        zero = jnp.zeros_like(q)
        y = w_ref[0] * _dot(q, sf_ref[...].astype(BF16))
        for h in range(H_RET):
            hm = (lane >= h * HEAD_DIM) & (lane < (h + 1) * HEAD_DIM)
            s = _dot_nt(jnp.where(hm, q, zero), k) * dm_ref[h]
            y = y + _dot(s.astype(BF16), jnp.where(hm, v, zero))
        y_ref[rows, :] = y
        sf_ref[...] = dec_f * sf_ref[...] + state_delta(k, v, w_ref[2])
        return carry

    lax.fori_loop(0, n_chunks, fwd, 0, unroll=3)

    def bwd(c):
        rows = chunk_rows(c)
        q = r_ref[0, rows, 0:D_RET]
        k = r_ref[0, rows, D_RET:2 * D_RET]
        v = r_ref[0, rows, 2 * D_RET:3 * D_RET]
        g = r_ref[0, rows, 3 * D_RET:4 * D_RET].astype(F32)
        y = y_ref[rows, :] + w_ref[1] * _dot(q, sb_ref[...].astype(BF16))
        sb_ref[...] = dec_b * sb_ref[...] + state_delta(k, v, w_ref[3])
        mu = _dot2_right(y, avg)
        d = y - mu
        var = _dot((d * d).astype(BF16), avg)
        yn = d * lax.rsqrt(var + EPS) * gng_ref[...] + gnb_ref[...]
        o_ref[0, rows, :] = (yn * _silu(g)).astype(BF16)

    def bwd_ctx(i, carry):
        bwd(n_ctx_chunks - 1 - i)
        return carry

    def bwd_lat(i, carry):
        bwd(n_chunks - 1 - i)
        return carry

    lax.fori_loop(0, n_ctx_chunks, bwd_ctx, 0)
    lax.fori_loop(0, n_chunks - n_ctx_chunks, bwd_lat, 0, unroll=4)


def _retention(ret, lg, gng, gnb, n_ctx):
    bsz, t, _ = ret.shape
    const = lambda shape: pl.BlockSpec(shape, lambda b: (0,) * len(shape))
    return pl.pallas_call(
        functools.partial(_ret_kernel, n_ctx_chunks=n_ctx // CHUNK),
        grid=(bsz,),
        in_specs=[
            pl.BlockSpec((1, t, 4 * D_RET), lambda b: (b, 0, 0)),
            const((2, D_RET)), const((1, D_RET)), const((1, D_RET)),
        ],
        out_specs=pl.BlockSpec((1, t, D_RET), lambda b: (b, 0, 0)),
        out_shape=jax.ShapeDtypeStruct((bsz, t, D_RET), BF16),
        scratch_shapes=[pltpu.VMEM((t, D_RET), F32), pltpu.VMEM((D_RET, D_RET), F32),
                        pltpu.VMEM((D_RET, D_RET), F32), pltpu.VMEM((4, CHUNK, D_RET), F32),
                        pltpu.VMEM((H_RET, CHUNK, CHUNK), F32)],
        compiler_params=pltpu.CompilerParams(
            dimension_semantics=("arbitrary",), vmem_limit_bytes=VMEM_LIMIT),
        name="retention",
    )(ret, lg, gng, gnb)


def _softplus(x):
    return jnp.maximum(x, 0.0) + jnp.log1p(jnp.exp(-jnp.abs(x)))


def _ssd_kernel(z_ref, xbc_ref, dt_ref, cw_ref, cb_ref, dtb_ref, alog_ref, dskip_ref, ng_ref,
                o_ref, xpad_ref, xc_ref, y_ref, eb_ref, dsb_ref, sf_ref, sb_ref, ce_all, g_all, sc_all,
                *, n_ctx):
    c_len = CHUNK
    t = xbc_ref.shape[1]
    n_chunks = t // c_len
    n_ctx_chunks = n_ctx // c_len
    pad = 8
    half = SSD_CONV // 2

    zpad = jnp.zeros((pad, D_XBC), F32)
    xpad_ref[0:pad, :] = zpad
    xpad_ref[pad + n_ctx:2 * pad + n_ctx, :] = zpad
    xpad_ref[2 * pad + t:3 * pad + t, :] = zpad

    def fill(c, carry):
        src = pl.ds(pl.multiple_of(c * c_len, c_len), c_len)
        off = jnp.where(c < n_ctx_chunks, pad, 2 * pad)
        dst = pl.ds(pl.multiple_of(c * c_len + off, pad), c_len)
        xpad_ref[dst, :] = xbc_ref[0, src, :].astype(F32)
        return carry

    lax.fori_loop(0, n_chunks, fill, 0)

    for c in range(n_chunks):
        base = c * c_len + (pad if c < n_ctx_chunks else 2 * pad)
        acc = jnp.broadcast_to(cb_ref[...], (c_len, D_XBC))
        for j in range(SSD_CONV):
            acc = acc + cw_ref[j:j + 1, :] * xpad_ref[base + j - half:base + j - half + c_len, :]
        xc_ref[c * c_len:(c + 1) * c_len, :] = _silu(acc).astype(BF16)

    nd = 2 * H_SSD
    lane_dt = lax.broadcasted_iota(jnp.int32, (1, D_DT), 1)
    live = lane_dt < nd
    a_vec = jnp.where(live, -jnp.exp(alog_ref[...]), 0.0)
    is_f = lane_dt < H_SSD
    ti = lax.broadcasted_iota(jnp.int32, (c_len, c_len), 0)
    si = lax.broadcasted_iota(jnp.int32, (c_len, c_len), 1)
    causal = si <= ti
    tri_l = jnp.where(causal, 1.0, 0.0).astype(BF16)
    tri_u = jnp.where(si >= ti, 1.0, 0.0).astype(BF16)
    n_exp = 2 * D_SSD
    er = lax.broadcasted_iota(jnp.int32, (D_DT, 2 * n_exp), 0)
    ej = lax.broadcasted_iota(jnp.int32, (D_DT, 2 * n_exp), 1)
    rel = er - jnp.where(ej >= n_exp, 3 * PART_LANES, 0)
    col = jnp.where(ej >= n_exp, ej - n_exp, ej) // HEAD_DIM
    expand = jnp.where((rel == col) | (rel == col + PART_LANES) | (rel == col + 2 * PART_LANES),
                       1.0, 0.0).astype(BF16)
    lane_w = lax.broadcasted_iota(jnp.int32, (c_len, 2 * LANES), 1)
    lane_c = lax.broadcasted_iota(jnp.int32, (c_len, D_BC), 1)
    srow = lax.broadcasted_iota(jnp.int32, (D_BC, D_SSD), 0) // SSD_STATE
    scol = lax.broadcasted_iota(jnp.int32, (D_BC, D_SSD), 1) // (D_SSD // SSD_GROUPS)
    smask = srow == scol
    heads_per_group = H_SSD // SSD_GROUPS

    def chunk_rows(c):
        return pl.ds(pl.multiple_of(c * c_len, c_len), c_len)

    sf_ref[...] = jnp.zeros_like(sf_ref)
    sb_ref[...] = jnp.zeros_like(sb_ref)

    def fwd_chunk(c, slot):
        ce_ref, g_ref, sc_ref = ce_all.at[slot], g_all.at[slot], sc_all.at[slot]
        rows = chunk_rows(c)
        xs = xc_ref[rows, 0:D_SSD]
        bm = xc_ref[rows, D_SSD:D_SSD + D_BC]
        cm = xc_ref[rows, D_SSD + D_BC:D_XBC]
        dt = _softplus(dt_ref[0, rows, :] + dtb_ref[...])
        la = _pack3(dt * a_vec, 0).astype(BF16)
        cum = jnp.where(is_f, _unpack3(_dot(tri_l, la)), _unpack3(_dot(tri_u, la)))
        cum = jnp.where(live, cum, 0.0)
        edge = jnp.where(is_f, cum[c_len - 1:c_len, :], cum[0:1, :])
        wk = jnp.where(live, jnp.exp(edge - cum) * dt, 0.0)
        packed = (_pack3(cum, 0) + _pack3(wk, 3 * PART_LANES)).astype(BF16)
        ce_ref[...] = _dot(packed, expand)
        key_t = (cum - jnp.log(dt)).T

        e_f = jnp.exp(ce_ref[:, 0:D_SSD])
        eb_ref[rows, :] = jnp.exp(ce_ref[:, D_SSD:n_exp])
        y_ref[rows, :] = e_f * _dot(cm, sf_ref[...].astype(BF16))
        zc = jnp.zeros_like(cm)
        for g in range(SSD_GROUPS):
            gm = (lane_c >= g * SSD_STATE) & (lane_c < (g + 1) * SSD_STATE)
            g_ref[...] = _dot_nt(jnp.where(gm, cm, zc), bm)
            win = slice(g * LANES, (g + 2) * LANES)
            xw = xs[:, win]
            zw = jnp.zeros_like(xw)
            yg = None
            for hh in range(heads_per_group):
                h = g * heads_per_group + hh
                d_f = cum[:, h:h + 1] - key_t[h:h + 1, :]
                d_b = cum[:, H_SSD + h:H_SSD + h + 1] - key_t[H_SSD + h:H_SSD + h + 1, :]
                sc_ref[hh] = (jnp.exp(jnp.where(causal, d_f, d_b)) * g_ref[...]).astype(BF16)
                lo = h * HEAD_DIM - g * LANES
                hm = (lane_w >= lo) & (lane_w < lo + HEAD_DIM)
                part = _dot(sc_ref[hh], jnp.where(hm, xw, zw))
                yg = part if yg is None else yg + part
            y_ref[rows, win] = y_ref[rows, win] + yg

        xs_f = xs.astype(F32)
        bm_t = bm.astype(F32).T.astype(BF16)
        ds_f = _dot(bm_t, (xs_f * ce_ref[:, n_exp:n_exp + D_SSD]).astype(BF16))
        ds_b = _dot(bm_t, (xs_f * ce_ref[:, n_exp + D_SSD:2 * n_exp]).astype(BF16))
        sf_ref[...] = e_f[c_len - 1:c_len, :] * sf_ref[...] + jnp.where(smask, ds_f, 0.0)
        dsb_ref[c] = jnp.where(smask, ds_b, 0.0)

    unroll = ce_all.shape[0]

    def fwd(i, carry):
        for k in range(unroll):
            fwd_chunk(i * unroll + k, k)
        return carry

    lax.fori_loop(0, n_chunks // unroll, fwd, 0)

    def bwd(c):
        rows = chunk_rows(c)
        xs_f = xc_ref[rows, 0:D_SSD].astype(F32)
        cm = xc_ref[rows, D_SSD + D_BC:D_XBC]
        e_b = eb_ref[rows, :]
        y = y_ref[rows, :] + e_b * _dot(cm, sb_ref[...].astype(BF16))
        sb_ref[...] = e_b[0:1, :] * sb_ref[...] + dsb_ref[c]
        y = y + dskip_ref[...] * xs_f
        u = y * _silu(z_ref[0, rows, :].astype(F32))
        o_ref[0, rows, :] = _rms(u, ng_ref[...]).astype(BF16)

    def bwd_ctx(i, carry):
        bwd(n_ctx_chunks - 1 - i)
        return carry

    def bwd_lat(i, carry):
        bwd(n_chunks - 1 - i)
        return carry

    lax.fori_loop(0, n_ctx_chunks, bwd_ctx, 0)
    lax.fori_loop(0, n_chunks - n_ctx_chunks, bwd_lat, 0, unroll=2)


def _ssd(z, xbc, dt, conv_w, conv_b, dt_bias, a_log, d_skip, norm_g, n_ctx):
    bsz, t, _ = xbc.shape
    n_chunks = t // CHUNK
    unroll = next(u for u in (3, 2, 1) if n_chunks % u == 0)
    const = lambda shape: pl.BlockSpec(shape, lambda b: (0,) * len(shape))
    seq = lambda w: pl.BlockSpec((1, t, w), lambda b: (b, 0, 0))
    return pl.pallas_call(
        functools.partial(_ssd_kernel, n_ctx=n_ctx),
        grid=(bsz,),
        in_specs=[seq(D_SSD), seq(D_XBC), seq(D_DT),
                  const((8, D_XBC)), const((1, D_XBC)), const((1, D_DT)), const((1, D_DT)),
                  const((1, D_SSD)), const((1, D_SSD))],
        out_specs=seq(D_SSD),
        out_shape=jax.ShapeDtypeStruct((bsz, t, D_SSD), BF16),
        scratch_shapes=[
            pltpu.VMEM((t + 24, D_XBC), F32),
            pltpu.VMEM((t, D_XBC), BF16),
            pltpu.VMEM((t, D_SSD), F32),
            pltpu.VMEM((t, D_SSD), F32),
            pltpu.VMEM((n_chunks, D_BC, D_SSD), F32),
            pltpu.VMEM((D_BC, D_SSD), F32),
            pltpu.VMEM((D_BC, D_SSD), F32),
            pltpu.VMEM((unroll, CHUNK, 4 * D_SSD), F32),
            pltpu.VMEM((unroll, CHUNK, CHUNK), F32),
            pltpu.VMEM((unroll, H_SSD // SSD_GROUPS, CHUNK, CHUNK), BF16),
        ],
        compiler_params=pltpu.CompilerParams(
            dimension_semantics=("arbitrary",), vmem_limit_bytes=VMEM_LIMIT),
        name="ssd",
    )(z, xbc, dt, conv_w, conv_b, dt_bias, a_log, d_skip, norm_g)


def _post_kernel(att_ref, ret_ref, ssd_ref, xc_ref, xl_ref, mod_ref, ng_ref, wo_ref, w1_ref, w2_ref,
                 o_ref, *, ff_chunk, n_ctx_tiles, first, ctx_row):
    is_ctx = pl.program_id(1) + first < n_ctx_tiles
    x = jnp.where(is_ctx, xc_ref[0], xl_ref[0])
    mod = functools.partial(_mod_chunk, mod_ref, jnp.where(is_ctx, ctx_row, pl.program_id(0)))
    mix = jnp.concatenate([att_ref[0], ret_ref[0], ssd_ref[0]], axis=-1)
    o = _dot(mix, wo_ref[...].astype(BF16))
    x1 = x + mod(2) * _rms(o, ng_ref[1:2, :])
    h2 = _rms(x1, ng_ref[2:3, :]) * (1.0 + mod(4)) + mod(3)
    h2 = h2.astype(BF16)
    d_ff = w1_ref.shape[1]
    acc = jnp.zeros(x1.shape, F32)
    for j in range(d_ff // ff_chunk):
        sl = slice(j * ff_chunk, (j + 1) * ff_chunk)
        a = jnp.maximum(_dot(h2, w1_ref[:, sl].astype(BF16)), 0.0)
        acc = acc + _dot((a * a).astype(BF16), w2_ref[sl, :].astype(BF16))
    o_ref[0] = x1 + mod(5) * _rms(acc, ng_ref[3:4, :])


def _post(att, ret, ssd, x_ctx, x_lat, lat_off, mods, layer, ng, w_out, w_ff1, w_ff2, n_ctx_tiles,
          with_ctx):
    bsz, t, _ = ret.shape
    d = x_lat.shape[2]
    d_ff = w_ff1.shape[2]
    first = 0 if with_ctx else n_ctx_tiles
    nt = t // ROW_TILE - first
    assert att.shape[1] == nt * ROW_TILE
    mod_spec = pl.BlockSpec((None,) + mods.shape[1:], lambda b, i: (layer, 0, 0))
    tok = lambda w: pl.BlockSpec((1, ROW_TILE, w), lambda b, i: (b, i + first, 0))
    const = lambda shape: pl.BlockSpec(shape, lambda b, i: (0,) * len(shape),
                                       pipeline_mode=pl.Buffered(1))
    weight = lambda shape: pl.BlockSpec((None,) + shape, lambda b, i: (layer, 0, 0),
                                        pipeline_mode=pl.Buffered(1))
    return pl.pallas_call(
        functools.partial(_post_kernel, ff_chunk=1024, n_ctx_tiles=n_ctx_tiles, first=first,
                          ctx_row=bsz),
        grid=(bsz, nt),
        in_specs=[pl.BlockSpec((1, ROW_TILE, D_ATT), lambda b, i: (b, i, 0)), tok(D_RET), tok(D_SSD),
                  *_token_specs(d, n_ctx_tiles, lat_off, first),
                  mod_spec,
                  const((4, d)), weight((d, d)), weight((d, d_ff)), weight((d_ff, d))],
        out_specs=pl.BlockSpec((1, ROW_TILE, d), lambda b, i: (b, i, 0)),
        out_shape=jax.ShapeDtypeStruct((bsz, nt * ROW_TILE, d), F32),
        compiler_params=pltpu.CompilerParams(
            dimension_semantics=("arbitrary", "arbitrary"), vmem_limit_bytes=VMEM_LIMIT),
        name="post",
    )(att, ret, ssd, x_ctx, x_lat, mods, ng, w_out, w_ff1, w_ff2)


def _rope_tables(n, n_ctx):
    rows = n // GRID_W
    row = jnp.broadcast_to(jnp.arange(rows)[:, None], (rows, GRID_W)).reshape(n)
    col = jnp.broadcast_to(jnp.arange(GRID_W)[None, :], (rows, GRID_W)).reshape(n)
    half = HEAD_DIM // 2
    inv_freq = ROPE_THETA ** (-jnp.arange(0, half, 2, dtype=F32) / half)
    ang = jnp.stack([row, col], axis=-1).astype(F32)[:, :, None] * inv_freq
    ang = jnp.concatenate([ang, ang], axis=-1).reshape(n, HEAD_DIM)
    ang = jnp.tile(ang, (1, LANES // HEAD_DIM))
    cos = jnp.cos(ang)
    sin = jnp.sin(ang)
    upper = (jnp.arange(LANES) % half) >= half // 2
    sin_p = jnp.where(upper, sin, 0.0)
    sin_m = jnp.where(upper, 0.0, -sin)
    ident = lambda v, a: jnp.concatenate([jnp.full((n_ctx, LANES), v, F32), a], axis=0)
    return ident(1.0, cos), ident(0.0, sin_p), ident(0.0, sin_m)


def kernel(x, c, ctx, c_ctx, w_mod, b_mod, norm_g, w_in, w_out, q_norm_g, k_norm_g, ret_decay_logit,
           ret_gn_g, ret_gn_b, ssd_conv_w, ssd_conv_b, ssd_dt_bias, ssd_a_log, ssd_d, ssd_norm_g,
           w_ff1, w_ff2):
    bsz, n, d = x.shape
    n_ctx = ctx.shape[1]
    depth = w_mod.shape[0]
    assert n % ROW_TILE == 0 and n_ctx % ROW_TILE == 0 and n_ctx % CHUNK == 0 and n % CHUNK == 0
    assert w_in.shape[2] == OFF_DT + 2 * H_SSD
    n_ctx_tiles = n_ctx // ROW_TILE

    n_rows = -(-(bsz + 1) // 8) * 8
    cc = jnp.concatenate([c, c_ctx[None, :], jnp.zeros((n_rows - bsz - 1, d), F32)], axis=0)
    mods = _modulation(cc, w_mod, b_mod)

    cos, sin_p, sin_m = _rope_tables(n, n_ctx)
    n_dt = 2 * H_SSD
    pad_dt = lambda a: jnp.pad(a.reshape(1, n_dt), ((0, 0), (0, D_DT - n_dt)))

    x_ctx, x_lat, lat_off = ctx, x, n_ctx_tiles
    wi = jnp.pad(w_in, ((0, 0), (0, 0), (0, D_IN_PAD - w_in.shape[2]))).astype(BF16)
    for layer in range(depth):
        last = layer == depth - 1
        gqk = jnp.concatenate([jnp.tile(q_norm_g[layer], H_ATT) * (HEAD_DIM ** -0.5 * LOG2_E),
                               jnp.tile(k_norm_g[layer], H_KV)])[None, :]

        q, k, v, ret, z, xbc, dt = _inproj(x_ctx, x_lat, lat_off, n_ctx + n, mods, layer,
                                           norm_g[layer], wi, cos, sin_p, sin_m, gqk, n_ctx_tiles)
        att = _attention(q, k, v, n_ctx, with_ctx=not last)
        lg = jnp.repeat(ret_decay_logit[layer], HEAD_DIM, axis=1)
        ret_o = _retention(ret, lg, ret_gn_g[layer][None, :], ret_gn_b[layer][None, :], n_ctx)
        conv_w = jnp.pad(ssd_conv_w[layer], ((0, 8 - SSD_CONV), (0, 0)))
        ssd_o = _ssd(z, xbc, dt, conv_w, ssd_conv_b[layer][None, :], pad_dt(ssd_dt_bias[layer]),
                     pad_dt(ssd_a_log[layer]), jnp.repeat(ssd_d[layer], HEAD_DIM)[None, :],
                     ssd_norm_g[layer][None, :], n_ctx)
        xall = _post(att, ret_o, ssd_o, x_ctx, x_lat, lat_off, mods, layer, norm_g[layer], w_out,
                     w_ff1, w_ff2, n_ctx_tiles, with_ctx=not last)
        x_ctx, x_lat, lat_off = xall, xall, 0
    return xall
```

```python
import functools

import jax
import jax.numpy as jnp
from jax import lax
from jax.experimental import pallas as pl
from jax.experimental.pallas import tpu as pltpu

F32 = jnp.float32
BF16 = jnp.bfloat16

HEAD_DIM = 64
H_ATT = 6
H_KV = 2
H_RET = 4
H_SSD = 6
SSD_GROUPS = 2
SSD_STATE = 128
SSD_CONV = 5
GRID_W = 64
ROPE_THETA = 10000.0
EPS = 1e-6
LOG2_E = 1.4426950408889634

D_ATT = H_ATT * HEAD_DIM
D_KV = H_KV * HEAD_DIM
D_RET = H_RET * HEAD_DIM
D_SSD = H_SSD * HEAD_DIM
D_BC = SSD_GROUPS * SSD_STATE
D_XBC = D_SSD + 2 * D_BC
LANES = 128
D_DT = LANES
OFF_Q = 0
OFF_K = OFF_Q + D_ATT
OFF_V = OFF_K + D_KV
OFF_RET = OFF_V + D_KV
OFF_Z = OFF_RET + 4 * D_RET
OFF_XBC = OFF_Z + D_SSD
OFF_DT = OFF_XBC + D_XBC
D_IN_PAD = OFF_DT + D_DT

ROW_TILE = 256
CHUNK = 256
VMEM_LIMIT = 56 * 1024 * 1024


def _silu(x):
    return x * jax.nn.sigmoid(x)


def _split2(a):
    hi = a.astype(BF16)
    lo = (a - hi.astype(F32)).astype(BF16)
    return hi, lo


def _dot(a, b):
    return jnp.dot(a, b, preferred_element_type=F32)


def _dot_nt(a, b):
    return lax.dot_general(a, b, (((1,), (1,)), ((), ())), preferred_element_type=F32)


def _dot2_right(a, m):
    hi, lo = _split2(a)
    return _dot(hi, m) + _dot(lo, m)


PART_LANES = 16


def _pack3(a, base):
    hi = a.astype(BF16).astype(F32)
    r = a - hi
    mid = r.astype(BF16).astype(F32)
    out = hi + pltpu.roll(mid, PART_LANES, 1) + pltpu.roll(r - mid, 2 * PART_LANES, 1)
    return pltpu.roll(out, base, 1) if base else out


def _unpack3(p):
    return p + pltpu.roll(p, LANES - PART_LANES, 1) + pltpu.roll(p, LANES - 2 * PART_LANES, 1)


def _rms(x, g):
    ms = jnp.mean(x * x, axis=-1, keepdims=True)
    return x * lax.rsqrt(ms + EPS) * g


def _head_avg_matrix(n):
    r = lax.broadcasted_iota(jnp.int32, (n, n), 0) // HEAD_DIM
    c = lax.broadcasted_iota(jnp.int32, (n, n), 1) // HEAD_DIM
    return jnp.where(r == c, 1.0 / HEAD_DIM, 0.0).astype(BF16)


def _mod_kernel(c_ref, w_ref, b_ref, o_ref):
    sc = _silu(c_ref[...]).astype(BF16)
    o_ref[0] = _dot(sc, w_ref[0].astype(BF16)) + b_ref[0]


def _modulation(cc, w_mod, b_mod):
    depth, d, n = w_mod.shape
    rows = cc.shape[0]
    tn = 1536
    return pl.pallas_call(
        _mod_kernel,
        grid=(depth, n // tn),
        in_specs=[
            pl.BlockSpec((rows, d), lambda l, j: (0, 0)),
            pl.BlockSpec((1, d, tn), lambda l, j: (l, 0, j)),
            pl.BlockSpec((1, 1, tn), lambda l, j: (l, 0, j)),
        ],
        out_specs=pl.BlockSpec((1, rows, tn), lambda l, j: (l, 0, j)),
        out_shape=jax.ShapeDtypeStruct((depth, rows, n), F32),
        compiler_params=pltpu.CompilerParams(
            dimension_semantics=("arbitrary", "arbitrary"), vmem_limit_bytes=VMEM_LIMIT),
        name="modulation",
    )(cc, w_mod, b_mod.reshape(depth, 1, n))


def _mod_chunk(mod_ref, row, k):
    d = mod_ref.shape[1] // 6
    return mod_ref[pl.ds(row, 1), k * d:(k + 1) * d]


def _rope(x, cos, sin_p, sin_m):
    return x * cos + pltpu.roll(x, 16, 1) * sin_p + pltpu.roll(x, LANES - 16, 1) * sin_m


def _inproj_kernel(xc_ref, xl_ref, mod_ref, ng_ref, w_ref, cos_ref, sp_ref, sm_ref, gqk_ref,
                   qt_ref, k_ref, v_ref, ret_ref, z_ref, xbc_ref, dt_ref, *, n_ctx_tiles, ctx_row):
    is_ctx = pl.program_id(1) < n_ctx_tiles
    x = jnp.where(is_ctx, xc_ref[0], xl_ref[0])
    mod = functools.partial(_mod_chunk, mod_ref, jnp.where(is_ctx, ctx_row, pl.program_id(0)))
    h = _rms(x, ng_ref[0:1, :])
    h = h * (1.0 + mod(1)) + mod(0)
    hb = h.astype(BF16)
    rope = lambda tile: _rope(tile, cos_ref[...], sp_ref[...], sm_ref[...])
    kscale = HEAD_DIM ** -0.5

    def emit(col, tile):
        sl = lambda off: slice(col - off, col - off + LANES)
        if col < OFF_K:
            qt_ref[0, sl(OFF_Q), :] = rope(tile).T.astype(BF16)
        elif col < OFF_V:
            k_ref[0, :, sl(OFF_K)] = rope(tile).astype(BF16)
        elif col < OFF_RET:
            v_ref[0, :, sl(OFF_V)] = tile.astype(BF16)
        elif col < OFF_RET + D_RET:
            ret_ref[0, :, sl(OFF_RET)] = rope(tile).astype(BF16)
        elif col < OFF_RET + 2 * D_RET:
            ret_ref[0, :, sl(OFF_RET)] = (rope(tile) * kscale).astype(BF16)
        elif col < OFF_Z:
            ret_ref[0, :, sl(OFF_RET)] = tile.astype(BF16)
        elif col < OFF_XBC:
            z_ref[0, :, sl(OFF_Z)] = tile.astype(BF16)
        elif col < OFF_DT:
            xbc_ref[0, :, sl(OFF_XBC)] = tile.astype(BF16)
        else:
            dt_ref[0, :, sl(OFF_DT)] = tile

    group = OFF_V
    for g in range(D_IN_PAD // group):
        acc = _dot(hb, w_ref[:, g * group:(g + 1) * group])
        if g == 0:
            ms = _dot((acc * acc).astype(BF16), _head_avg_matrix(group))
            acc = acc * lax.rsqrt(ms + EPS) * gqk_ref[...]
        for j in range(group // LANES):
            emit(g * group + j * LANES, acc[:, j * LANES:(j + 1) * LANES])


def _token_specs(d, n_ctx_tiles, lat_off, first=0):
    ctx_spec = pl.BlockSpec((1, ROW_TILE, d),
                            lambda b, i: (b, jnp.minimum(i + first, n_ctx_tiles - 1), 0))
    lat_spec = pl.BlockSpec((1, ROW_TILE, d),
                            lambda b, i: (b, jnp.maximum(i + first, n_ctx_tiles) - lat_off, 0))
    return ctx_spec, lat_spec


def _inproj(x_ctx, x_lat, lat_off, t, mods, layer, ng, w_in, cos, sin_p, sin_m, gqk, n_ctx_tiles):
    bsz, _, d = x_lat.shape
    nt = t // ROW_TILE
    mod_spec = pl.BlockSpec((None,) + mods.shape[1:], lambda b, i: (layer, 0, 0))

    tok = lambda w: pl.BlockSpec((1, ROW_TILE, w), lambda b, i: (b, i, 0))
    rope_spec = pl.BlockSpec((ROW_TILE, LANES), lambda b, i: (i, 0))
    const = lambda shape: pl.BlockSpec(shape, lambda b, i: (0,) * len(shape))
    widths = (D_KV, D_KV, 4 * D_RET, D_SSD, D_XBC)
    qt_spec = pl.BlockSpec((1, D_ATT, ROW_TILE), lambda b, i: (b, 0, i))
    return pl.pallas_call(
        functools.partial(_inproj_kernel, n_ctx_tiles=n_ctx_tiles, ctx_row=bsz),
        grid=(bsz, nt),
        in_specs=[
            *_token_specs(d, n_ctx_tiles, lat_off),
            mod_spec,
            const((4, d)),
            pl.BlockSpec((None, d, D_IN_PAD), lambda b, i: (layer, 0, 0),
                         pipeline_mode=pl.Buffered(1)),
            rope_spec, rope_spec, rope_spec,
            const((1, D_ATT + D_KV)),
        ],
        out_specs=[qt_spec] + [tok(w) for w in widths] + [tok(D_DT)],
        out_shape=[jax.ShapeDtypeStruct((bsz, D_ATT, t), BF16)]
        + [jax.ShapeDtypeStruct((bsz, t, w), BF16) for w in widths]
        + [jax.ShapeDtypeStruct((bsz, t, D_DT), F32)],
        compiler_params=pltpu.CompilerParams(
            dimension_semantics=("arbitrary", "arbitrary"), vmem_limit_bytes=VMEM_LIMIT),
        name="inproj",
    )(x_ctx, x_lat, mods, ng, w_in, cos, sin_p, sin_m, gqk)


V_ROWS = 80
KEY_BLOCK = 256
NEG_BIG = -1e30


def _attn_kernel(qt_ref, k_ref, v_ref, o_ref, km_ref, vt_ref, acc_ref, m_ref, alpha_ref, s_ref, p_ref,
                 *, n_ctx, with_ctx):
    qi = pl.program_id(1)
    per_tile = LANES // HEAD_DIM
    n_tiles = D_ATT // LANES

    @pl.when(qi == 0)
    def _():
        k = k_ref[0].astype(F32)
        kr = pltpu.roll(k, HEAD_DIM, 1)
        low = lax.broadcasted_iota(jnp.int32, k.shape, 1) < HEAD_DIM
        km_ref[0] = jnp.where(low, k, 0.0).astype(BF16)
        km_ref[1] = jnp.where(low, 0.0, kr).astype(BF16)
        km_ref[2] = jnp.where(low, kr, 0.0).astype(BF16)
        km_ref[3] = jnp.where(low, 0.0, k).astype(BF16)
        vt = v_ref[0].astype(F32).T
        row = lax.broadcasted_iota(jnp.int32, (V_ROWS - HEAD_DIM, vt.shape[1]), 0)
        tail = jnp.where(row == 0, 1.0, 0.0)
        for g in range(H_KV):
            vg = jnp.concatenate([vt[g * HEAD_DIM:(g + 1) * HEAD_DIM], tail], axis=0).astype(BF16)
            for blk in range(vt.shape[1] // KEY_BLOCK):
                vt_ref[g, blk] = vg[:, blk * KEY_BLOCK:(blk + 1) * KEY_BLOCK]

    kv_of = lambda h: h // (H_ATT // H_KV)

    def scores(i):
        for j in range(n_tiles):
            ks = jnp.concatenate(
                [km_ref[2 * kv_of(j * per_tile + half) + half, i * KEY_BLOCK:(i + 1) * KEY_BLOCK, :]
                 for half in range(per_tile)], axis=0)
            s_ref[i % 2, j] = _dot(ks, qt_ref[0, j * LANES:(j + 1) * LANES, :])

    def exponentials(i):
        for h in range(H_ATT):
            j, half = divmod(h, per_tile)
            s = s_ref[i % 2, j, half * KEY_BLOCK:(half + 1) * KEY_BLOCK, :]
            m_old = m_ref[h]
            m_new = jnp.maximum(m_old, jnp.max(s, axis=0, keepdims=True))
            m_ref[h] = m_new
            alpha_ref[i % 2, h] = jnp.exp2(m_old - m_new)
            p_ref[i % 2, h] = jnp.exp2(s - m_new).astype(BF16)

    def weighted_values(i):
        for h in range(H_ATT):
            pv = _dot(vt_ref[kv_of(h), i], p_ref[i % 2, h])
            acc_ref[h] = acc_ref[h] * alpha_ref[i % 2, h] + pv

    def run(n_blocks):
        acc_ref[...] = jnp.zeros_like(acc_ref)
        m_ref[...] = jnp.full(m_ref.shape, NEG_BIG, F32)
        for it in range(n_blocks + 2):
            if it >= 2:
                weighted_values(it - 2)
            if it < n_blocks:
                scores(it)
            if 1 <= it <= n_blocks:
                exponentials(it - 1)
        for j in range(n_tiles):
            halves = []
            for half in range(per_tile):
                acc = acc_ref[j * per_tile + half]
                halves.append(acc[0:HEAD_DIM] * (1.0 / acc[HEAD_DIM:HEAD_DIM + 1]))
            ot = jnp.concatenate(halves, axis=0)
            o_ref[0, :, j * LANES:(j + 1) * LANES] = ot.T.astype(BF16)

    n_all = k_ref.shape[1] // KEY_BLOCK
    if with_ctx:
        @pl.when(qi == 0)
        def _():
            run(n_ctx // KEY_BLOCK)

        @pl.when(qi > 0)
        def _():
            run(n_all)
    else:
        run(n_all)


def _attention(qt, k, v, n_ctx, with_ctx):
    bsz, _, t = qt.shape
    tq = ROW_TILE
    assert n_ctx == tq and n_ctx % KEY_BLOCK == 0 and t % KEY_BLOCK == 0
    first = 0 if with_ctx else n_ctx // tq
    nq = t // tq - first
    return pl.pallas_call(
        functools.partial(_attn_kernel, n_ctx=n_ctx, with_ctx=with_ctx),
        grid=(bsz, nq),
        in_specs=[
            pl.BlockSpec((1, D_ATT, tq), lambda b, i: (b, 0, i + first)),
            pl.BlockSpec((1, t, D_KV), lambda b, i: (b, 0, 0)),
            pl.BlockSpec((1, t, D_KV), lambda b, i: (b, 0, 0)),
        ],
        out_specs=pl.BlockSpec((1, tq, D_ATT), lambda b, i: (b, i, 0)),
        out_shape=jax.ShapeDtypeStruct((bsz, nq * tq, D_ATT), BF16),
        scratch_shapes=[pltpu.VMEM((2 * H_KV, t, D_KV), BF16),
                        pltpu.VMEM((H_KV, t // KEY_BLOCK, V_ROWS, KEY_BLOCK), BF16),
                        pltpu.VMEM((H_ATT, V_ROWS, tq), F32),
                        pltpu.VMEM((H_ATT, 1, tq), F32),
                        pltpu.VMEM((2, H_ATT, 1, tq), F32),
                        pltpu.VMEM((2, D_ATT // LANES, (LANES // HEAD_DIM) * KEY_BLOCK, tq), F32),
                        pltpu.VMEM((2, H_ATT, KEY_BLOCK, tq), BF16)],
        compiler_params=pltpu.CompilerParams(
            dimension_semantics=("arbitrary", "arbitrary"), vmem_limit_bytes=VMEM_LIMIT),
        name="attention",
    )(qt, k, v)


def _log_sigmoid(x):
    return jnp.minimum(x, 0.0) - jnp.log1p(jnp.exp(-jnp.abs(x)))


def _ret_kernel(r_ref, lg_ref, gng_ref, gnb_ref, o_ref, y_ref, sf_ref, sb_ref, w_ref, dm_ref,
                *, n_ctx_chunks):
    c_len = CHUNK
    t = r_ref.shape[1]
    n_chunks = t // c_len
    lg = _log_sigmoid(lg_ref[...])
    lgf = lg[0:1, :]
    lgb = lg[1:2, :]
    dec_f = jnp.exp(c_len * lgf)
    dec_b = jnp.exp(c_len * lgb)

    @pl.when(pl.program_id(0) == 0)
    def _():
        tcol = lax.broadcasted_iota(jnp.int32, (c_len, 1), 0).astype(F32)
        w_ref[0] = jnp.exp((tcol + 1.0) * lgf)
        w_ref[1] = jnp.exp((c_len - tcol) * lgb)
        w_ref[2] = jnp.exp((c_len - 1.0 - tcol) * lgf)
        w_ref[3] = jnp.exp(tcol * lgb)
        ti = lax.broadcasted_iota(jnp.int32, (c_len, c_len), 0)
        si = lax.broadcasted_iota(jnp.int32, (c_len, c_len), 1)
        diff = (ti - si).astype(F32)
        for h in range(H_RET):
            lf = lgf[:, h * HEAD_DIM:h * HEAD_DIM + 1]
            lb = lgb[:, h * HEAD_DIM:h * HEAD_DIM + 1]
            dm_ref[h] = jnp.exp(jnp.where(diff >= 0, diff * lf, -diff * lb))

    lane = lax.broadcasted_iota(jnp.int32, (c_len, D_RET), 1)
    srow = lax.broadcasted_iota(jnp.int32, (D_RET, D_RET), 0) // HEAD_DIM
    scol = lax.broadcasted_iota(jnp.int32, (D_RET, D_RET), 1) // HEAD_DIM
    smask = srow == scol
    avg = _head_avg_matrix(D_RET)

    def chunk_rows(c):
        return pl.ds(pl.multiple_of(c * c_len, c_len), c_len)

    def state_delta(k, v, wk):
        kw = (k.astype(F32) * wk).T.astype(BF16)
        return jnp.where(smask, _dot(kw, v), 0.0)

    sf_ref[...] = jnp.zeros_like(sf_ref)
    sb_ref[...] = jnp.zeros_like(sb_ref)

    def fwd(c, carry):
        rows = chunk_rows(c)
        q = r_ref[0, rows, 0:D_RET]
        k = r_ref[0, rows, D_RET:2 * D_RET]
        v = r_ref[0, rows, 2 * D_RET:3 * D_RET]
        zero = jnp.zeros_like(q)
        y = w_ref[0] * _dot(q, sf_ref[...].astype(BF16))
        for h in range(H_RET):
            hm = (lane >= h * HEAD_DIM) & (lane < (h + 1) * HEAD_DIM)
            s = _dot_nt(jnp.where(hm, q, zero), k) * dm_ref[h]
            y = y + _dot(s.astype(BF16), jnp.where(hm, v, zero))
        y_ref[rows, :] = y
        sf_ref[...] = dec_f * sf_ref[...] + state_delta(k, v, w_ref[2])
        return carry

    lax.fori_loop(0, n_chunks, fwd, 0, unroll=3)

    def bwd(c):
        rows = chunk_rows(c)
        q = r_ref[0, rows, 0:D_RET]
        k = r_ref[0, rows, D_RET:2 * D_RET]
        v = r_ref[0, rows, 2 * D_RET:3 * D_RET]
        g = r_ref[0, rows, 3 * D_RET:4 * D_RET].astype(F32)
        y = y_ref[rows, :] + w_ref[1] * _dot(q, sb_ref[...].astype(BF16))
        sb_ref[...] = dec_b * sb_ref[...] + state_delta(k, v, w_ref[3])
        mu = _dot2_right(y, avg)
        d = y - mu
        var = _dot((d * d).astype(BF16), avg)
        yn = d * lax.rsqrt(var + EPS) * gng_ref[...] + gnb_ref[...]
        o_ref[0, rows, :] = (yn * _silu(g)).astype(BF16)

    def bwd_ctx(i, carry):
        bwd(n_ctx_chunks - 1 - i)
        return carry

    def bwd_lat(i, carry):
        bwd(n_chunks - 1 - i)
        return carry

    lax.fori_loop(0, n_ctx_chunks, bwd_ctx, 0)
    lax.fori_loop(0, n_chunks - n_ctx_chunks, bwd_lat, 0, unroll=4)


def _retention(ret, lg, gng, gnb, n_ctx):
    bsz, t, _ = ret.shape
    const = lambda shape: pl.BlockSpec(shape, lambda b: (0,) * len(shape))
    return pl.pallas_call(
        functools.partial(_ret_kernel, n_ctx_chunks=n_ctx // CHUNK),
        grid=(bsz,),
        in_specs=[
            pl.BlockSpec((1, t, 4 * D_RET), lambda b: (b, 0, 0)),
            const((2, D_RET)), const((1, D_RET)), const((1, D_RET)),
        ],
        out_specs=pl.BlockSpec((1, t, D_RET), lambda b: (b, 0, 0)),
        out_shape=jax.ShapeDtypeStruct((bsz, t, D_RET), BF16),
        scratch_shapes=[pltpu.VMEM((t, D_RET), F32), pltpu.VMEM((D_RET, D_RET), F32),
                        pltpu.VMEM((D_RET, D_RET), F32), pltpu.VMEM((4, CHUNK, D_RET), F32),
                        pltpu.VMEM((H_RET, CHUNK, CHUNK), F32)],
        compiler_params=pltpu.CompilerParams(
            dimension_semantics=("arbitrary",), vmem_limit_bytes=VMEM_LIMIT),
        name="retention",
    )(ret, lg, gng, gnb)


def _softplus(x):
    return jnp.maximum(x, 0.0) + jnp.log1p(jnp.exp(-jnp.abs(x)))


def _ssd_kernel(z_ref, xbc_ref, dt_ref, cw_ref, cb_ref, dtb_ref, alog_ref, dskip_ref, ng_ref,
                o_ref, xpad_ref, xc_ref, y_ref, eb_ref, dsb_ref, sf_ref, sb_ref, ce_all, g_all, sc_all,
                *, n_ctx):
    c_len = CHUNK
    t = xbc_ref.shape[1]
    n_chunks = t // c_len
    n_ctx_chunks = n_ctx // c_len
    pad = 8
    half = SSD_CONV // 2

    zpad = jnp.zeros((pad, D_XBC), F32)
    xpad_ref[0:pad, :] = zpad
    xpad_ref[pad + n_ctx:2 * pad + n_ctx, :] = zpad
    xpad_ref[2 * pad + t:3 * pad + t, :] = zpad

    def fill(c, carry):
        src = pl.ds(pl.multiple_of(c * c_len, c_len), c_len)
        off = jnp.where(c < n_ctx_chunks, pad, 2 * pad)
        dst = pl.ds(pl.multiple_of(c * c_len + off, pad), c_len)
        xpad_ref[dst, :] = xbc_ref[0, src, :].astype(F32)
        return carry

    lax.fori_loop(0, n_chunks, fill, 0)

    for c in range(n_chunks):
        base = c * c_len + (pad if c < n_ctx_chunks else 2 * pad)
        acc = jnp.broadcast_to(cb_ref[...], (c_len, D_XBC))
        for j in range(SSD_CONV):
            acc = acc + cw_ref[j:j + 1, :] * xpad_ref[base + j - half:base + j - half + c_len, :]
        xc_ref[c * c_len:(c + 1) * c_len, :] = _silu(acc).astype(BF16)

    nd = 2 * H_SSD
    lane_dt = lax.broadcasted_iota(jnp.int32, (1, D_DT), 1)
    live = lane_dt < nd
    a_vec = jnp.where(live, -jnp.exp(alog_ref[...]), 0.0)
    is_f = lane_dt < H_SSD
    ti = lax.broadcasted_iota(jnp.int32, (c_len, c_len), 0)
    si = lax.broadcasted_iota(jnp.int32, (c_len, c_len), 1)
    causal = si <= ti
    tri_l = jnp.where(causal, 1.0, 0.0).astype(BF16)
    tri_u = jnp.where(si >= ti, 1.0, 0.0).astype(BF16)
    n_exp = 2 * D_SSD
    er = lax.broadcasted_iota(jnp.int32, (D_DT, 2 * n_exp), 0)
    ej = lax.broadcasted_iota(jnp.int32, (D_DT, 2 * n_exp), 1)
    rel = er - jnp.where(ej >= n_exp, 3 * PART_LANES, 0)
    col = jnp.where(ej >= n_exp, ej - n_exp, ej) // HEAD_DIM
    expand = jnp.where((rel == col) | (rel == col + PART_LANES) | (rel == col + 2 * PART_LANES),
                       1.0, 0.0).astype(BF16)
    lane_w = lax.broadcasted_iota(jnp.int32, (c_len, 2 * LANES), 1)
    lane_c = lax.broadcasted_iota(jnp.int32, (c_len, D_BC), 1)
    srow = lax.broadcasted_iota(jnp.int32, (D_BC, D_SSD), 0) // SSD_STATE
    scol = lax.broadcasted_iota(jnp.int32, (D_BC, D_SSD), 1) // (D_SSD // SSD_GROUPS)
    smask = srow == scol
    heads_per_group = H_SSD // SSD_GROUPS

    def chunk_rows(c):
        return pl.ds(pl.multiple_of(c * c_len, c_len), c_len)

    sf_ref[...] = jnp.zeros_like(sf_ref)
    sb_ref[...] = jnp.zeros_like(sb_ref)

    def fwd_chunk(c, slot):
        ce_ref, g_ref, sc_ref = ce_all.at[slot], g_all.at[slot], sc_all.at[slot]
        rows = chunk_rows(c)
        xs = xc_ref[rows, 0:D_SSD]
        bm = xc_ref[rows, D_SSD:D_SSD + D_BC]
        cm = xc_ref[rows, D_SSD + D_BC:D_XBC]
        dt = _softplus(dt_ref[0, rows, :] + dtb_ref[...])
        la = _pack3(dt * a_vec, 0).astype(BF16)
        cum = jnp.where(is_f, _unpack3(_dot(tri_l, la)), _unpack3(_dot(tri_u, la)))
        cum = jnp.where(live, cum, 0.0)
        edge = jnp.where(is_f, cum[c_len - 1:c_len, :], cum[0:1, :])
        wk = jnp.where(live, jnp.exp(edge - cum) * dt, 0.0)
        packed = (_pack3(cum, 0) + _pack3(wk, 3 * PART_LANES)).astype(BF16)
        ce_ref[...] = _dot(packed, expand)
        key_t = (cum - jnp.log(dt)).T

        e_f = jnp.exp(ce_ref[:, 0:D_SSD])
        eb_ref[rows, :] = jnp.exp(ce_ref[:, D_SSD:n_exp])
        y_ref[rows, :] = e_f * _dot(cm, sf_ref[...].astype(BF16))
        zc = jnp.zeros_like(cm)
        for g in range(SSD_GROUPS):
            gm = (lane_c >= g * SSD_STATE) & (lane_c < (g + 1) * SSD_STATE)
            g_ref[...] = _dot_nt(jnp.where(gm, cm, zc), bm)
            win = slice(g * LANES, (g + 2) * LANES)
            xw = xs[:, win]
            zw = jnp.zeros_like(xw)
            yg = None
            for hh in range(heads_per_group):
                h = g * heads_per_group + hh
                d_f = cum[:, h:h + 1] - key_t[h:h + 1, :]
                d_b = cum[:, H_SSD + h:H_SSD + h + 1] - key_t[H_SSD + h:H_SSD + h + 1, :]
                sc_ref[hh] = (jnp.exp(jnp.where(causal, d_f, d_b)) * g_ref[...]).astype(BF16)
                lo = h * HEAD_DIM - g * LANES
                hm = (lane_w >= lo) & (lane_w < lo + HEAD_DIM)
                part = _dot(sc_ref[hh], jnp.where(hm, xw, zw))
                yg = part if yg is None else yg + part
            y_ref[rows, win] = y_ref[rows, win] + yg

        xs_f = xs.astype(F32)
        bm_t = bm.astype(F32).T.astype(BF16)
        ds_f = _dot(bm_t, (xs_f * ce_ref[:, n_exp:n_exp + D_SSD]).astype(BF16))
        ds_b = _dot(bm_t, (xs_f * ce_ref[:, n_exp + D_SSD:2 * n_exp]).astype(BF16))
        sf_ref[...] = e_f[c_len - 1:c_len, :] * sf_ref[...] + jnp.where(smask, ds_f, 0.0)
        dsb_ref[c] = jnp.where(smask, ds_b, 0.0)

    unroll = ce_all.shape[0]

    def fwd(i, carry):
        for k in range(unroll):
            fwd_chunk(i * unroll + k, k)
        return carry

    lax.fori_loop(0, n_chunks // unroll, fwd, 0)

    def bwd(c):
        rows = chunk_rows(c)
        xs_f = xc_ref[rows, 0:D_SSD].astype(F32)
        cm = xc_ref[rows, D_SSD + D_BC:D_XBC]
        e_b = eb_ref[rows, :]
        y = y_ref[rows, :] + e_b * _dot(cm, sb_ref[...].astype(BF16))
        sb_ref[...] = e_b[0:1, :] * sb_ref[...] + dsb_ref[c]
        y = y + dskip_ref[...] * xs_f
        u = y * _silu(z_ref[0, rows, :].astype(F32))
        o_ref[0, rows, :] = _rms(u, ng_ref[...]).astype(BF16)

    def bwd_ctx(i, carry):
        bwd(n_ctx_chunks - 1 - i)
        return carry

    def bwd_lat(i, carry):
        bwd(n_chunks - 1 - i)
        return carry

    lax.fori_loop(0, n_ctx_chunks, bwd_ctx, 0)
    lax.fori_loop(0, n_chunks - n_ctx_chunks, bwd_lat, 0, unroll=2)


def _ssd(z, xbc, dt, conv_w, conv_b, dt_bias, a_log, d_skip, norm_g, n_ctx):
    bsz, t, _ = xbc.shape
    n_chunks = t // CHUNK
    unroll = next(u for u in (3, 2, 1) if n_chunks % u == 0)
    const = lambda shape: pl.BlockSpec(shape, lambda b: (0,) * len(shape))
    seq = lambda w: pl.BlockSpec((1, t, w), lambda b: (b, 0, 0))
    return pl.pallas_call(
        functools.partial(_ssd_kernel, n_ctx=n_ctx),
        grid=(bsz,),
        in_specs=[seq(D_SSD), seq(D_XBC), seq(D_DT),
                  const((8, D_XBC)), const((1, D_XBC)), const((1, D_DT)), const((1, D_DT)),
                  const((1, D_SSD)), const((1, D_SSD))],
        out_specs=seq(D_SSD),
        out_shape=jax.ShapeDtypeStruct((bsz, t, D_SSD), BF16),
        scratch_shapes=[
            pltpu.VMEM((t + 24, D_XBC), F32),
            pltpu.VMEM((t, D_XBC), BF16),
            pltpu.VMEM((t, D_SSD), F32),
            pltpu.VMEM((t, D_SSD), F32),
            pltpu.VMEM((n_chunks, D_BC, D_SSD), F32),
            pltpu.VMEM((D_BC, D_SSD), F32),
            pltpu.VMEM((D_BC, D_SSD), F32),
            pltpu.VMEM((unroll, CHUNK, 4 * D_SSD), F32),
            pltpu.VMEM((unroll, CHUNK, CHUNK), F32),
            pltpu.VMEM((unroll, H_SSD // SSD_GROUPS, CHUNK, CHUNK), BF16),
        ],
        compiler_params=pltpu.CompilerParams(
            dimension_semantics=("arbitrary",), vmem_limit_bytes=VMEM_LIMIT),
        name="ssd",
    )(z, xbc, dt, conv_w, conv_b, dt_bias, a_log, d_skip, norm_g)


def _post_kernel(att_ref, ret_ref, ssd_ref, xc_ref, xl_ref, mod_ref, ng_ref, wo_ref, w1_ref, w2_ref,
                 o_ref, *, ff_chunk, n_ctx_tiles, first, ctx_row):
    is_ctx = pl.program_id(1) + first < n_ctx_tiles
    x = jnp.where(is_ctx, xc_ref[0], xl_ref[0])
    mod = functools.partial(_mod_chunk, mod_ref, jnp.where(is_ctx, ctx_row, pl.program_id(0)))
    mix = jnp.concatenate([att_ref[0], ret_ref[0], ssd_ref[0]], axis=-1)
    o = _dot(mix, wo_ref[...].astype(BF16))
    x1 = x + mod(2) * _rms(o, ng_ref[1:2, :])
    h2 = _rms(x1, ng_ref[2:3, :]) * (1.0 + mod(4)) + mod(3)
    h2 = h2.astype(BF16)
    d_ff = w1_ref.shape[1]
    acc = jnp.zeros(x1.shape, F32)
    for j in range(d_ff // ff_chunk):
        sl = slice(j * ff_chunk, (j + 1) * ff_chunk)
        a = jnp.maximum(_dot(h2, w1_ref[:, sl].astype(BF16)), 0.0)
        acc = acc + _dot((a * a).astype(BF16), w2_ref[sl, :].astype(BF16))
    o_ref[0] = x1 + mod(5) * _rms(acc, ng_ref[3:4, :])


def _post(att, ret, ssd, x_ctx, x_lat, lat_off, mods, layer, ng, w_out, w_ff1, w_ff2, n_ctx_tiles,
          with_ctx):
    bsz, t, _ = ret.shape
    d = x_lat.shape[2]
    d_ff = w_ff1.shape[2]
    first = 0 if with_ctx else n_ctx_tiles
    nt = t // ROW_TILE - first
    assert att.shape[1] == nt * ROW_TILE
    mod_spec = pl.BlockSpec((None,) + mods.shape[1:], lambda b, i: (layer, 0, 0))
    tok = lambda w: pl.BlockSpec((1, ROW_TILE, w), lambda b, i: (b, i + first, 0))
    const = lambda shape: pl.BlockSpec(shape, lambda b, i: (0,) * len(shape),
                                       pipeline_mode=pl.Buffered(1))
    weight = lambda shape: pl.BlockSpec((None,) + shape, lambda b, i: (layer, 0, 0),
                                        pipeline_mode=pl.Buffered(1))
    return pl.pallas_call(
        functools.partial(_post_kernel, ff_chunk=1024, n_ctx_tiles=n_ctx_tiles, first=first,
                          ctx_row=bsz),
        grid=(bsz, nt),
        in_specs=[pl.BlockSpec((1, ROW_TILE, D_ATT), lambda b, i: (b, i, 0)), tok(D_RET), tok(D_SSD),
                  *_token_specs(d, n_ctx_tiles, lat_off, first),
                  mod_spec,
                  const((4, d)), weight((d, d)), weight((d, d_ff)), weight((d_ff, d))],
        out_specs=pl.BlockSpec((1, ROW_TILE, d), lambda b, i: (b, i, 0)),
        out_shape=jax.ShapeDtypeStruct((bsz, nt * ROW_TILE, d), F32),
        compiler_params=pltpu.CompilerParams(
            dimension_semantics=("arbitrary", "arbitrary"), vmem_limit_bytes=VMEM_LIMIT),
        name="post",
    )(att, ret, ssd, x_ctx, x_lat, mods, ng, w_out, w_ff1, w_ff2)


def _rope_tables(n, n_ctx):
    rows = n // GRID_W
    row = jnp.broadcast_to(jnp.arange(rows)[:, None], (rows, GRID_W)).reshape(n)
    col = jnp.broadcast_to(jnp.arange(GRID_W)[None, :], (rows, GRID_W)).reshape(n)
    half = HEAD_DIM // 2
    inv_freq = ROPE_THETA ** (-jnp.arange(0, half, 2, dtype=F32) / half)
    ang = jnp.stack([row, col], axis=-1).astype(F32)[:, :, None] * inv_freq
    ang = jnp.concatenate([ang, ang], axis=-1).reshape(n, HEAD_DIM)
    ang = jnp.tile(ang, (1, LANES // HEAD_DIM))
    cos = jnp.cos(ang)
    sin = jnp.sin(ang)
    upper = (jnp.arange(LANES) % half) >= half // 2
    sin_p = jnp.where(upper, sin, 0.0)
    sin_m = jnp.where(upper, 0.0, -sin)
    ident = lambda v, a: jnp.concatenate([jnp.full((n_ctx, LANES), v, F32), a], axis=0)
    return ident(1.0, cos), ident(0.0, sin_p), ident(0.0, sin_m)


def kernel(x, c, ctx, c_ctx, w_mod, b_mod, norm_g, w_in, w_out, q_norm_g, k_norm_g, ret_decay_logit,
           ret_gn_g, ret_gn_b, ssd_conv_w, ssd_conv_b, ssd_dt_bias, ssd_a_log, ssd_d, ssd_norm_g,
           w_ff1, w_ff2):
    bsz, n, d = x.shape
    n_ctx = ctx.shape[1]
    depth = w_mod.shape[0]
    assert n % ROW_TILE == 0 and n_ctx % ROW_TILE == 0 and n_ctx % CHUNK == 0 and n % CHUNK == 0
    assert w_in.shape[2] == OFF_DT + 2 * H_SSD
    n_ctx_tiles = n_ctx // ROW_TILE

    n_rows = -(-(bsz + 1) // 8) * 8
    cc = jnp.concatenate([c, c_ctx[None, :], jnp.zeros((n_rows - bsz - 1, d), F32)], axis=0)
    mods = _modulation(cc, w_mod, b_mod)

    cos, sin_p, sin_m = _rope_tables(n, n_ctx)
    n_dt = 2 * H_SSD
    pad_dt = lambda a: jnp.pad(a.reshape(1, n_dt), ((0, 0), (0, D_DT - n_dt)))

    x_ctx, x_lat, lat_off = ctx, x, n_ctx_tiles
    wi = jnp.pad(w_in, ((0, 0), (0, 0), (0, D_IN_PAD - w_in.shape[2]))).astype(BF16)
    for layer in range(depth):
        last = layer == depth - 1
        gqk = jnp.concatenate([jnp.tile(q_norm_g[layer], H_ATT) * (HEAD_DIM ** -0.5 * LOG2_E),
                               jnp.tile(k_norm_g[layer], H_KV)])[None, :]

        q, k, v, ret, z, xbc, dt = _inproj(x_ctx, x_lat, lat_off, n_ctx + n, mods, layer,
                                           norm_g[layer], wi, cos, sin_p, sin_m, gqk, n_ctx_tiles)
        att = _attention(q, k, v, n_ctx, with_ctx=not last)
        lg = jnp.repeat(ret_decay_logit[layer], HEAD_DIM, axis=1)
        ret_o = _retention(ret, lg, ret_gn_g[layer][None, :], ret_gn_b[layer][None, :], n_ctx)
        conv_w = jnp.pad(ssd_conv_w[layer], ((0, 8 - SSD_CONV), (0, 0)))
        ssd_o = _ssd(z, xbc, dt, conv_w, ssd_conv_b[layer][None, :], pad_dt(ssd_dt_bias[layer]),
                     pad_dt(ssd_a_log[layer]), jnp.repeat(ssd_d[layer], HEAD_DIM)[None, :],
                     ssd_norm_g[layer][None, :], n_ctx)
        xall = _post(att, ret_o, ssd_o, x_ctx, x_lat, lat_off, mods, layer, norm_g[layer], w_out,
                     w_ff1, w_ff2, n_ctx_tiles, with_ctx=not last)
        x_ctx, x_lat, lat_off = xall, xall, 0
    return xall
```

```python
import functools

import jax
import jax.numpy as jnp
from jax import lax
from jax.experimental import pallas as pl
from jax.experimental.pallas import tpu as pltpu

F32 = jnp.float32
BF16 = jnp.bfloat16

HEAD_DIM = 64
H_ATT = 6
H_KV = 2
H_RET = 4
H_SSD = 6
SSD_GROUPS = 2
SSD_STATE = 128
SSD_CONV = 5
GRID_W = 64
ROPE_THETA = 10000.0
EPS = 1e-6
LOG2_E = 1.4426950408889634

D_ATT = H_ATT * HEAD_DIM
D_KV = H_KV * HEAD_DIM
D_RET = H_RET * HEAD_DIM
D_SSD = H_SSD * HEAD_DIM
D_BC = SSD_GROUPS * SSD_STATE
D_XBC = D_SSD + 2 * D_BC
LANES = 128
D_DT = LANES
OFF_Q = 0
OFF_K = OFF_Q + D_ATT
OFF_V = OFF_K + D_KV
OFF_RET = OFF_V + D_KV
OFF_Z = OFF_RET + 4 * D_RET
OFF_XBC = OFF_Z + D_SSD
OFF_DT = OFF_XBC + D_XBC
D_IN_PAD = OFF_DT + D_DT

ROW_TILE = 256
CHUNK = 256
VMEM_LIMIT = 56 * 1024 * 1024


def _silu(x):
    return x * jax.nn.sigmoid(x)


def _split2(a):
    hi = a.astype(BF16)
    lo = (a - hi.astype(F32)).astype(BF16)
    return hi, lo


def _dot(a, b):
    return jnp.dot(a, b, preferred_element_type=F32)


def _dot_nt(a, b):
    return lax.dot_general(a, b, (((1,), (1,)), ((), ())), preferred_element_type=F32)


def _dot2_right(a, m):
    hi, lo = _split2(a)
    return _dot(hi, m) + _dot(lo, m)


PART_LANES = 16


def _pack3(a, base):
    hi = a.astype(BF16).astype(F32)
    r = a - hi
    mid = r.astype(BF16).astype(F32)
    out = hi + pltpu.roll(mid, PART_LANES, 1) + pltpu.roll(r - mid, 2 * PART_LANES, 1)
    return pltpu.roll(out, base, 1) if base else out


def _unpack3(p):
    return p + pltpu.roll(p, LANES - PART_LANES, 1) + pltpu.roll(p, LANES - 2 * PART_LANES, 1)


def _rms(x, g):
    ms = jnp.mean(x * x, axis=-1, keepdims=True)
    return x * lax.rsqrt(ms + EPS) * g


def _head_avg_matrix(n):
    r = lax.broadcasted_iota(jnp.int32, (n, n), 0) // HEAD_DIM
    c = lax.broadcasted_iota(jnp.int32, (n, n), 1) // HEAD_DIM
    return jnp.where(r == c, 1.0 / HEAD_DIM, 0.0).astype(BF16)


def _mod_kernel(c_ref, w_ref, b_ref, o_ref):
    sc = _silu(c_ref[...]).astype(BF16)
    o_ref[0] = _dot(sc, w_ref[0].astype(BF16)) + b_ref[0]


def _modulation(cc, w_mod, b_mod):
    depth, d, n = w_mod.shape
    rows = cc.shape[0]
    tn = 1536
    return pl.pallas_call(
        _mod_kernel,
        grid=(depth, n // tn),
        in_specs=[
            pl.BlockSpec((rows, d), lambda l, j: (0, 0)),
            pl.BlockSpec((1, d, tn), lambda l, j: (l, 0, j)),
            pl.BlockSpec((1, 1, tn), lambda l, j: (l, 0, j)),
        ],
        out_specs=pl.BlockSpec((1, rows, tn), lambda l, j: (l, 0, j)),
        out_shape=jax.ShapeDtypeStruct((depth, rows, n), F32),
        compiler_params=pltpu.CompilerParams(
            dimension_semantics=("arbitrary", "arbitrary"), vmem_limit_bytes=VMEM_LIMIT),
        name="modulation",
    )(cc, w_mod, b_mod.reshape(depth, 1, n))


def _mod_chunk(mod_ref, row, k):
    d = mod_ref.shape[1] // 6
    return mod_ref[pl.ds(row, 1), k * d:(k + 1) * d]


def _rope(x, cos, sin_p, sin_m):
    return x * cos + pltpu.roll(x, 16, 1) * sin_p + pltpu.roll(x, LANES - 16, 1) * sin_m


def _inproj_kernel(xc_ref, xl_ref, mod_ref, ng_ref, w_ref, cos_ref, sp_ref, sm_ref, gqk_ref,
                   qt_ref, k_ref, v_ref, ret_ref, z_ref, xbc_ref, dt_ref, *, n_ctx_tiles, ctx_row):
    is_ctx = pl.program_id(1) < n_ctx_tiles
    x = jnp.where(is_ctx, xc_ref[0], xl_ref[0])
    mod = functools.partial(_mod_chunk, mod_ref, jnp.where(is_ctx, ctx_row, pl.program_id(0)))
    h = _rms(x, ng_ref[0:1, :])
    h = h * (1.0 + mod(1)) + mod(0)
    hb = h.astype(BF16)
    rope = lambda tile: _rope(tile, cos_ref[...], sp_ref[...], sm_ref[...])
    kscale = HEAD_DIM ** -0.5

    def emit(col, tile):
        sl = lambda off: slice(col - off, col - off + LANES)
        if col < OFF_K:
            qt_ref[0, sl(OFF_Q), :] = rope(tile).T.astype(BF16)
        elif col < OFF_V:
            k_ref[0, :, sl(OFF_K)] = rope(tile).astype(BF16)
        elif col < OFF_RET:
            v_ref[0, :, sl(OFF_V)] = tile.astype(BF16)
        elif col < OFF_RET + D_RET:
            ret_ref[0, :, sl(OFF_RET)] = rope(tile).astype(BF16)
        elif col < OFF_RET + 2 * D_RET:
            ret_ref[0, :, sl(OFF_RET)] = (rope(tile) * kscale).astype(BF16)
        elif col < OFF_Z:
            ret_ref[0, :, sl(OFF_RET)] = tile.astype(BF16)
        elif col < OFF_XBC:
            z_ref[0, :, sl(OFF_Z)] = tile.astype(BF16)
        elif col < OFF_DT:
            xbc_ref[0, :, sl(OFF_XBC)] = tile.astype(BF16)
        else:
            dt_ref[0, :, sl(OFF_DT)] = tile

    group = OFF_V
    for g in range(D_IN_PAD // group):
        acc = _dot(hb, w_ref[:, g * group:(g + 1) * group])
        if g == 0:
            ms = _dot((acc * acc).astype(BF16), _head_avg_matrix(group))
            acc = acc * lax.rsqrt(ms + EPS) * gqk_ref[...]
        for j in range(group // LANES):
            emit(g * group + j * LANES, acc[:, j * LANES:(j + 1) * LANES])


def _token_specs(d, n_ctx_tiles, lat_off, first=0):
    ctx_spec = pl.BlockSpec((1, ROW_TILE, d),
                            lambda b, i: (b, jnp.minimum(i + first, n_ctx_tiles - 1), 0))
    lat_spec = pl.BlockSpec((1, ROW_TILE, d),
                            lambda b, i: (b, jnp.maximum(i + first, n_ctx_tiles) - lat_off, 0))
    return ctx_spec, lat_spec


def _inproj(x_ctx, x_lat, lat_off, t, mods, layer, ng, w_in, cos, sin_p, sin_m, gqk, n_ctx_tiles):
    bsz, _, d = x_lat.shape
    nt = t // ROW_TILE
    mod_spec = pl.BlockSpec((None,) + mods.shape[1:], lambda b, i: (layer, 0, 0))

    tok = lambda w: pl.BlockSpec((1, ROW_TILE, w), lambda b, i: (b, i, 0))
    rope_spec = pl.BlockSpec((ROW_TILE, LANES), lambda b, i: (i, 0))
    const = lambda shape: pl.BlockSpec(shape, lambda b, i: (0,) * len(shape))
    widths = (D_KV, D_KV, 4 * D_RET, D_SSD, D_XBC)
    qt_spec = pl.BlockSpec((1, D_ATT, ROW_TILE), lambda b, i: (b, 0, i))
    return pl.pallas_call(
        functools.partial(_inproj_kernel, n_ctx_tiles=n_ctx_tiles, ctx_row=bsz),
        grid=(bsz, nt),
        in_specs=[
            *_token_specs(d, n_ctx_tiles, lat_off),
            mod_spec,
            const((4, d)),
            pl.BlockSpec((None, d, D_IN_PAD), lambda b, i: (layer, 0, 0),
                         pipeline_mode=pl.Buffered(1)),
            rope_spec, rope_spec, rope_spec,
            const((1, D_ATT + D_KV)),
        ],
        out_specs=[qt_spec] + [tok(w) for w in widths] + [tok(D_DT)],
        out_shape=[jax.ShapeDtypeStruct((bsz, D_ATT, t), BF16)]
        + [jax.ShapeDtypeStruct((bsz, t, w), BF16) for w in widths]
        + [jax.ShapeDtypeStruct((bsz, t, D_DT), F32)],
        compiler_params=pltpu.CompilerParams(
            dimension_semantics=("arbitrary", "arbitrary"), vmem_limit_bytes=VMEM_LIMIT),
        name="inproj",
    )(x_ctx, x_lat, mods, ng, w_in, cos, sin_p, sin_m, gqk)


V_ROWS = 80
KEY_BLOCK = 256
NEG_BIG = -1e30


def _attn_kernel(qt_ref, k_ref, v_ref, o_ref, km_ref, vt_ref, acc_ref, m_ref, alpha_ref, s_ref, p_ref,
                 *, n_ctx, with_ctx):
    qi = pl.program_id(1)
    per_tile = LANES // HEAD_DIM
    n_tiles = D_ATT // LANES

    @pl.when(qi == 0)
    def _():
        k = k_ref[0].astype(F32)
        kr = pltpu.roll(k, HEAD_DIM, 1)
        low = lax.broadcasted_iota(jnp.int32, k.shape, 1) < HEAD_DIM
        km_ref[0] = jnp.where(low, k, 0.0).astype(BF16)
        km_ref[1] = jnp.where(low, 0.0, kr).astype(BF16)
        km_ref[2] = jnp.where(low, kr, 0.0).astype(BF16)
        km_ref[3] = jnp.where(low, 0.0, k).astype(BF16)
        vt = v_ref[0].astype(F32).T
        row = lax.broadcasted_iota(jnp.int32, (V_ROWS - HEAD_DIM, vt.shape[1]), 0)
        tail = jnp.where(row == 0, 1.0, 0.0)
        for g in range(H_KV):
            vg = jnp.concatenate([vt[g * HEAD_DIM:(g + 1) * HEAD_DIM], tail], axis=0).astype(BF16)
            for blk in range(vt.shape[1] // KEY_BLOCK):
                vt_ref[g, blk] = vg[:, blk * KEY_BLOCK:(blk + 1) * KEY_BLOCK]

    kv_of = lambda h: h // (H_ATT // H_KV)

    def scores(i):
        for j in range(n_tiles):
            ks = jnp.concatenate(
                [km_ref[2 * kv_of(j * per_tile + half) + half, i * KEY_BLOCK:(i + 1) * KEY_BLOCK, :]
                 for half in range(per_tile)], axis=0)
            s_ref[i % 2, j] = _dot(ks, qt_ref[0, j * LANES:(j + 1) * LANES, :])

    def exponentials(i):
        for h in range(H_ATT):
            j, half = divmod(h, per_tile)
            s = s_ref[i % 2, j, half * KEY_BLOCK:(half + 1) * KEY_BLOCK, :]
            m_old = m_ref[h]
            m_new = jnp.maximum(m_old, jnp.max(s, axis=0, keepdims=True))
            m_ref[h] = m_new
            alpha_ref[i % 2, h] = jnp.exp2(m_old - m_new)
            p_ref[i % 2, h] = jnp.exp2(s - m_new).astype(BF16)

    def weighted_values(i):
        for h in range(H_ATT):
            pv = _dot(vt_ref[kv_of(h), i], p_ref[i % 2, h])
            acc_ref[h] = acc_ref[h] * alpha_ref[i % 2, h] + pv

    def run(n_blocks):
        acc_ref[...] = jnp.zeros_like(acc_ref)
        m_ref[...] = jnp.full(m_ref.shape, NEG_BIG, F32)
        for it in range(n_blocks + 2):
            if it >= 2:
                weighted_values(it - 2)
            if it < n_blocks:
                scores(it)
            if 1 <= it <= n_blocks:
                exponentials(it - 1)
        for j in range(n_tiles):
            halves = []
            for half in range(per_tile):
                acc = acc_ref[j * per_tile + half]
                halves.append(acc[0:HEAD_DIM] * (1.0 / acc[HEAD_DIM:HEAD_DIM + 1]))
            ot = jnp.concatenate(halves, axis=0)
            o_ref[0, :, j * LANES:(j + 1) * LANES] = ot.T.astype(BF16)

    n_all = k_ref.shape[1] // KEY_BLOCK
    if with_ctx:
        @pl.when(qi == 0)
        def _():
            run(n_ctx // KEY_BLOCK)

        @pl.when(qi > 0)
        def _():
            run(n_all)
    else:
        run(n_all)


def _attention(qt, k, v, n_ctx, with_ctx):
    bsz, _, t = qt.shape
    tq = ROW_TILE
    assert n_ctx == tq and n_ctx % KEY_BLOCK == 0 and t % KEY_BLOCK == 0
    first = 0 if with_ctx else n_ctx // tq
    nq = t // tq - first
    return pl.pallas_call(
        functools.partial(_attn_kernel, n_ctx=n_ctx, with_ctx=with_ctx),
        grid=(bsz, nq),
        in_specs=[
            pl.BlockSpec((1, D_ATT, tq), lambda b, i: (b, 0, i + first)),
            pl.BlockSpec((1, t, D_KV), lambda b, i: (b, 0, 0)),
            pl.BlockSpec((1, t, D_KV), lambda b, i: (b, 0, 0)),
        ],
        out_specs=pl.BlockSpec((1, tq, D_ATT), lambda b, i: (b, i, 0)),
        out_shape=jax.ShapeDtypeStruct((bsz, nq * tq, D_ATT), BF16),
        scratch_shapes=[pltpu.VMEM((2 * H_KV, t, D_KV), BF16),
                        pltpu.VMEM((H_KV, t // KEY_BLOCK, V_ROWS, KEY_BLOCK), BF16),
                        pltpu.VMEM((H_ATT, V_ROWS, tq), F32),
                        pltpu.VMEM((H_ATT, 1, tq), F32),
                        pltpu.VMEM((2, H_ATT, 1, tq), F32),
                        pltpu.VMEM((2, D_ATT // LANES, (LANES // HEAD_DIM) * KEY_BLOCK, tq), F32),
                        pltpu.VMEM((2, H_ATT, KEY_BLOCK, tq), BF16)],
        compiler_params=pltpu.CompilerParams(
            dimension_semantics=("arbitrary", "arbitrary"), vmem_limit_bytes=VMEM_LIMIT),
        name="attention",
    )(qt, k, v)


def _log_sigmoid(x):
    return jnp.minimum(x, 0.0) - jnp.log1p(jnp.exp(-jnp.abs(x)))


def _ret_kernel(r_ref, lg_ref, gng_ref, gnb_ref, o_ref, y_ref, sf_ref, sb_ref, w_ref, dm_ref,
                *, n_ctx_chunks):
    c_len = CHUNK
    t = r_ref.shape[1]
    n_chunks = t // c_len
    lg = _log_sigmoid(lg_ref[...])
    lgf = lg[0:1, :]
    lgb = lg[1:2, :]
    dec_f = jnp.exp(c_len * lgf)
    dec_b = jnp.exp(c_len * lgb)

    @pl.when(pl.program_id(0) == 0)
    def _():
        tcol = lax.broadcasted_iota(jnp.int32, (c_len, 1), 0).astype(F32)
        w_ref[0] = jnp.exp((tcol + 1.0) * lgf)
        w_ref[1] = jnp.exp((c_len - tcol) * lgb)
        w_ref[2] = jnp.exp((c_len - 1.0 - tcol) * lgf)
        w_ref[3] = jnp.exp(tcol * lgb)
        ti = lax.broadcasted_iota(jnp.int32, (c_len, c_len), 0)
        si = lax.broadcasted_iota(jnp.int32, (c_len, c_len), 1)
        diff = (ti - si).astype(F32)
        for h in range(H_RET):
            lf = lgf[:, h * HEAD_DIM:h * HEAD_DIM + 1]
            lb = lgb[:, h * HEAD_DIM:h * HEAD_DIM + 1]
            dm_ref[h] = jnp.exp(jnp.where(diff >= 0, diff * lf, -diff * lb))

    lane = lax.broadcasted_iota(jnp.int32, (c_len, D_RET), 1)
    srow = lax.broadcasted_iota(jnp.int32, (D_RET, D_RET), 0) // HEAD_DIM
    scol = lax.broadcasted_iota(jnp.int32, (D_RET, D_RET), 1) // HEAD_DIM
    smask = srow == scol
    avg = _head_avg_matrix(D_RET)

    def chunk_rows(c):
        return pl.ds(pl.multiple_of(c * c_len, c_len), c_len)

    def state_delta(k, v, wk):
        kw = (k.astype(F32) * wk).T.astype(BF16)
        return jnp.where(smask, _dot(kw, v), 0.0)

    sf_ref[...] = jnp.zeros_like(sf_ref)
    sb_ref[...] = jnp.zeros_like(sb_ref)

    def fwd(c, carry):
        rows = chunk_rows(c)
        q = r_ref[0, rows, 0:D_RET]
        k = r_ref[0, rows, D_RET:2 * D_RET]
        v = r_ref[0, rows, 2 * D_RET:3 * D_RET]
        zero = jnp.zeros_like(q)
        y = w_ref[0] * _dot(q, sf_ref[...].astype(BF16))
        for h in range(H_RET):
            hm = (lane >= h * HEAD_DIM) & (lane < (h + 1) * HEAD_DIM)
            s = _dot_nt(jnp.where(hm, q, zero), k) * dm_ref[h]
            y = y + _dot(s.astype(BF16), jnp.where(hm, v, zero))
        y_ref[rows, :] = y
        sf_ref[...] = dec_f * sf_ref[...] + state_delta(k, v, w_ref[2])
        return carry

    lax.fori_loop(0, n_chunks, fwd, 0, unroll=3)

    def bwd(c):
        rows = chunk_rows(c)
        q = r_ref[0, rows, 0:D_RET]
        k = r_ref[0, rows, D_RET:2 * D_RET]
        v = r_ref[0, rows, 2 * D_RET:3 * D_RET]
        g = r_ref[0, rows, 3 * D_RET:4 * D_RET].astype(F32)
        y = y_ref[rows, :] + w_ref[1] * _dot(q, sb_ref[...].astype(BF16))
        sb_ref[...] = dec_b * sb_ref[...] + state_delta(k, v, w_ref[3])
        mu = _dot2_right(y, avg)
        d = y - mu
        var = _dot((d * d).astype(BF16), avg)
        yn = d * lax.rsqrt(var + EPS) * gng_ref[...] + gnb_ref[...]
        o_ref[0, rows, :] = (yn * _silu(g)).astype(BF16)

    def bwd_ctx(i, carry):
        bwd(n_ctx_chunks - 1 - i)
        return carry

    def bwd_lat(i, carry):
        bwd(n_chunks - 1 - i)
        return carry

    lax.fori_loop(0, n_ctx_chunks, bwd_ctx, 0)
    lax.fori_loop(0, n_chunks - n_ctx_chunks, bwd_lat, 0, unroll=4)


def _retention(ret, lg, gng, gnb, n_ctx):
    bsz, t, _ = ret.shape
    const = lambda shape: pl.BlockSpec(shape, lambda b: (0,) * len(shape))
    return pl.pallas_call(
        functools.partial(_ret_kernel, n_ctx_chunks=n_ctx // CHUNK),
        grid=(bsz,),
        in_specs=[
            pl.BlockSpec((1, t, 4 * D_RET), lambda b: (b, 0, 0)),
            const((2, D_RET)), const((1, D_RET)), const((1, D_RET)),
        ],
        out_specs=pl.BlockSpec((1, t, D_RET), lambda b: (b, 0, 0)),
        out_shape=jax.ShapeDtypeStruct((bsz, t, D_RET), BF16),
        scratch_shapes=[pltpu.VMEM((t, D_RET), F32), pltpu.VMEM((D_RET, D_RET), F32),
                        pltpu.VMEM((D_RET, D_RET), F32), pltpu.VMEM((4, CHUNK, D_RET), F32),
                        pltpu.VMEM((H_RET, CHUNK, CHUNK), F32)],
        compiler_params=pltpu.CompilerParams(
            dimension_semantics=("arbitrary",), vmem_limit_bytes=VMEM_LIMIT),
        name="retention",
    )(ret, lg, gng, gnb)


def _softplus(x):
    return jnp.maximum(x, 0.0) + jnp.log1p(jnp.exp(-jnp.abs(x)))


def _ssd_kernel(z_ref, xbc_ref, dt_ref, cw_ref, cb_ref, dtb_ref, alog_ref, dskip_ref, ng_ref,
                o_ref, xc_ref, y_ref, eb_ref, dsb_ref, sf_ref, sb_ref, ce_all, g_all, sc_all,
                *, n_ctx):
    c_len = CHUNK
    t = xbc_ref.shape[1]
    n_chunks = t // c_len
    n_ctx_chunks = n_ctx // c_len
    edge = 16
    half = SSD_CONV // 2

    taps = [k for k in range(-half, half + 1) if k]
    ri = lax.broadcasted_iota(jnp.int32, (c_len, c_len), 0)
    ci = lax.broadcasted_iota(jnp.int32, (c_len, c_len), 1)
    shift = {k: jnp.where(ci == ri + k, 1.0, 0.0).astype(BF16) for k in taps}
    er = lax.broadcasted_iota(jnp.int32, (edge, edge), 0)
    ec = lax.broadcasted_iota(jnp.int32, (edge, edge), 1)
    prev_rows = {k: jnp.where(ec == er + k + edge, 1.0, 0.0).astype(BF16) for k in taps if k < 0}
    next_rows = {k: jnp.where(ec == er + k - edge, 1.0, 0.0).astype(BF16) for k in taps if k > 0}
    tap_w = lambda k, lanes: cw_ref[k + half:k + half + 1, lanes]
    lane_groups = [slice(lo, min(lo + 2 * LANES, D_XBC)) for lo in range(0, D_XBC, 2 * LANES)]
    for c in range(n_chunks):
        r0 = c * c_len
        has_prev = c not in (0, n_ctx_chunks)
        has_next = c not in (n_ctx_chunks - 1, n_chunks - 1)
        for lanes in lane_groups:
            xb = xbc_ref[0, r0:r0 + c_len, lanes]
            acc = cb_ref[:, lanes] + tap_w(0, lanes) * xb.astype(F32)
            for k in taps:
                acc = acc + tap_w(k, lanes) * _dot(shift[k], xb)
            head, tail = acc[0:edge], acc[c_len - edge:c_len]
            if has_prev:
                p = xbc_ref[0, r0 - edge:r0, lanes]
                for k in prev_rows:
                    head = head + tap_w(k, lanes) * _dot(prev_rows[k], p)
            if has_next:
                nx = xbc_ref[0, r0 + c_len:r0 + c_len + edge, lanes]
                for k in next_rows:
                    tail = tail + tap_w(k, lanes) * _dot(next_rows[k], nx)
            xc_ref[r0:r0 + edge, lanes] = _silu(head).astype(BF16)
            xc_ref[r0 + edge:r0 + c_len - edge, lanes] = _silu(acc[edge:c_len - edge]).astype(BF16)
            xc_ref[r0 + c_len - edge:r0 + c_len, lanes] = _silu(tail).astype(BF16)

    nd = 2 * H_SSD
    lane_dt = lax.broadcasted_iota(jnp.int32, (1, D_DT), 1)
    live = lane_dt < nd
    a_vec = jnp.where(live, -jnp.exp(alog_ref[...]), 0.0)
    is_f = lane_dt < H_SSD
    ti = lax.broadcasted_iota(jnp.int32, (c_len, c_len), 0)
    si = lax.broadcasted_iota(jnp.int32, (c_len, c_len), 1)
    causal = si <= ti
    tri_l = jnp.where(causal, 1.0, 0.0).astype(BF16)
    tri_u = jnp.where(si >= ti, 1.0, 0.0).astype(BF16)
    n_exp = 2 * D_SSD
    er = lax.broadcasted_iota(jnp.int32, (D_DT, 2 * n_exp), 0)
    ej = lax.broadcasted_iota(jnp.int32, (D_DT, 2 * n_exp), 1)
    rel = er - jnp.where(ej >= n_exp, 3 * PART_LANES, 0)
    col = jnp.where(ej >= n_exp, ej - n_exp, ej) // HEAD_DIM
    expand = jnp.where((rel == col) | (rel == col + PART_LANES) | (rel == col + 2 * PART_LANES),
                       1.0, 0.0).astype(BF16)
    lane_w = lax.broadcasted_iota(jnp.int32, (c_len, 2 * LANES), 1)
    lane_c = lax.broadcasted_iota(jnp.int32, (c_len, D_BC), 1)
    srow = lax.broadcasted_iota(jnp.int32, (D_BC, D_SSD), 0) // SSD_STATE
    scol = lax.broadcasted_iota(jnp.int32, (D_BC, D_SSD), 1) // (D_SSD // SSD_GROUPS)
    smask = srow == scol
    heads_per_group = H_SSD // SSD_GROUPS

    def chunk_rows(c):
        return pl.ds(pl.multiple_of(c * c_len, c_len), c_len)

    sf_ref[...] = jnp.zeros_like(sf_ref)
    sb_ref[...] = jnp.zeros_like(sb_ref)

    def fwd_chunk(c, slot):
        ce_ref, g_ref, sc_ref = ce_all.at[slot], g_all.at[slot], sc_all.at[slot]
        rows = chunk_rows(c)
        xs = xc_ref[rows, 0:D_SSD]
        bm = xc_ref[rows, D_SSD:D_SSD + D_BC]
        cm = xc_ref[rows, D_SSD + D_BC:D_XBC]
        dt = _softplus(dt_ref[0, rows, :] + dtb_ref[...])
        la = _pack3(dt * a_vec, 0).astype(BF16)
        cum = jnp.where(is_f, _unpack3(_dot(tri_l, la)), _unpack3(_dot(tri_u, la)))
        cum = jnp.where(live, cum, 0.0)
        edge = jnp.where(is_f, cum[c_len - 1:c_len, :], cum[0:1, :])
        wk = jnp.where(live, jnp.exp(edge - cum) * dt, 0.0)
        packed = (_pack3(cum, 0) + _pack3(wk, 3 * PART_LANES)).astype(BF16)
        ce_ref[...] = _dot(packed, expand)
        key_t = (cum - jnp.log(dt)).T

        e_f = jnp.exp(ce_ref[:, 0:D_SSD])
        eb_ref[rows, :] = jnp.exp(ce_ref[:, D_SSD:n_exp])
        y_ref[rows, :] = e_f * _dot(cm, sf_ref[...].astype(BF16))
        zc = jnp.zeros_like(cm)
        for g in range(SSD_GROUPS):
            gm = (lane_c >= g * SSD_STATE) & (lane_c < (g + 1) * SSD_STATE)
            g_ref[...] = _dot_nt(jnp.where(gm, cm, zc), bm)
            win = slice(g * LANES, (g + 2) * LANES)
            xw = xs[:, win]
            zw = jnp.zeros_like(xw)
            yg = None
            for hh in range(heads_per_group):
                h = g * heads_per_group + hh
                d_f = cum[:, h:h + 1] - key_t[h:h + 1, :]
                d_b = cum[:, H_SSD + h:H_SSD + h + 1] - key_t[H_SSD + h:H_SSD + h + 1, :]
                sc_ref[hh] = (jnp.exp(jnp.where(causal, d_f, d_b)) * g_ref[...]).astype(BF16)
                lo = h * HEAD_DIM - g * LANES
                hm = (lane_w >= lo) & (lane_w < lo + HEAD_DIM)
                part = _dot(sc_ref[hh], jnp.where(hm, xw, zw))
                yg = part if yg is None else yg + part
            y_ref[rows, win] = y_ref[rows, win] + yg

        xs_f = xs.astype(F32)
        bm_t = bm.astype(F32).T.astype(BF16)
        ds_f = _dot(bm_t, (xs_f * ce_ref[:, n_exp:n_exp + D_SSD]).astype(BF16))
        ds_b = _dot(bm_t, (xs_f * ce_ref[:, n_exp + D_SSD:2 * n_exp]).astype(BF16))
        sf_ref[...] = e_f[c_len - 1:c_len, :] * sf_ref[...] + jnp.where(smask, ds_f, 0.0)
        dsb_ref[c] = jnp.where(smask, ds_b, 0.0)

    unroll = ce_all.shape[0]

    def fwd(i, carry):
        for k in range(unroll):
            fwd_chunk(i * unroll + k, k)
        return carry

    lax.fori_loop(0, n_chunks // unroll, fwd, 0)

    def bwd(c):
        rows = chunk_rows(c)
        xs_f = xc_ref[rows, 0:D_SSD].astype(F32)
        cm = xc_ref[rows, D_SSD + D_BC:D_XBC]
        e_b = eb_ref[rows, :]
        y = y_ref[rows, :] + e_b * _dot(cm, sb_ref[...].astype(BF16))
        sb_ref[...] = e_b[0:1, :] * sb_ref[...] + dsb_ref[c]
        y = y + dskip_ref[...] * xs_f
        u = y * _silu(z_ref[0, rows, :].astype(F32))
        o_ref[0, rows, :] = _rms(u, ng_ref[...]).astype(BF16)

    def bwd_ctx(i, carry):
        bwd(n_ctx_chunks - 1 - i)
        return carry

    def bwd_lat(i, carry):
        bwd(n_chunks - 1 - i)
        return carry

    lax.fori_loop(0, n_ctx_chunks, bwd_ctx, 0)
    lax.fori_loop(0, n_chunks - n_ctx_chunks, bwd_lat, 0, unroll=2)


def _ssd(z, xbc, dt, conv_w, conv_b, dt_bias, a_log, d_skip, norm_g, n_ctx):
    bsz, t, _ = xbc.shape
    n_chunks = t // CHUNK
    unroll = next(u for u in (3, 2, 1) if n_chunks % u == 0)
    const = lambda shape: pl.BlockSpec(shape, lambda b: (0,) * len(shape))
    seq = lambda w: pl.BlockSpec((1, t, w), lambda b: (b, 0, 0))
    return pl.pallas_call(
        functools.partial(_ssd_kernel, n_ctx=n_ctx),
        grid=(bsz,),
        in_specs=[seq(D_SSD), seq(D_XBC), seq(D_DT),
                  const((8, D_XBC)), const((1, D_XBC)), const((1, D_DT)), const((1, D_DT)),
                  const((1, D_SSD)), const((1, D_SSD))],
        out_specs=seq(D_SSD),
        out_shape=jax.ShapeDtypeStruct((bsz, t, D_SSD), BF16),
        scratch_shapes=[
            pltpu.VMEM((t, D_XBC), BF16),
            pltpu.VMEM((t, D_SSD), F32),
            pltpu.VMEM((t, D_SSD), F32),
            pltpu.VMEM((n_chunks, D_BC, D_SSD), F32),
            pltpu.VMEM((D_BC, D_SSD), F32),
            pltpu.VMEM((D_BC, D_SSD), F32),
            pltpu.VMEM((unroll, CHUNK, 4 * D_SSD), F32),
            pltpu.VMEM((unroll, CHUNK, CHUNK), F32),
            pltpu.VMEM((unroll, H_SSD // SSD_GROUPS, CHUNK, CHUNK), BF16),
        ],
        compiler_params=pltpu.CompilerParams(
            dimension_semantics=("arbitrary",), vmem_limit_bytes=VMEM_LIMIT),
        name="ssd",
    )(z, xbc, dt, conv_w, conv_b, dt_bias, a_log, d_skip, norm_g)


def _post_kernel(att_ref, ret_ref, ssd_ref, xc_ref, xl_ref, mod_ref, ng_ref, wo_ref, w1_ref, w2_ref,
                 o_ref, *, ff_chunk, n_ctx_tiles, first, ctx_row):
    is_ctx = pl.program_id(1) + first < n_ctx_tiles
    x = jnp.where(is_ctx, xc_ref[0], xl_ref[0])
    mod = functools.partial(_mod_chunk, mod_ref, jnp.where(is_ctx, ctx_row, pl.program_id(0)))
    mix = jnp.concatenate([att_ref[0], ret_ref[0], ssd_ref[0]], axis=-1)
    o = _dot(mix, wo_ref[...].astype(BF16))
    x1 = x + mod(2) * _rms(o, ng_ref[1:2, :])
    h2 = _rms(x1, ng_ref[2:3, :]) * (1.0 + mod(4)) + mod(3)
    h2 = h2.astype(BF16)
    d_ff = w1_ref.shape[1]
    acc = jnp.zeros(x1.shape, F32)
    for j in range(d_ff // ff_chunk):
        sl = slice(j * ff_chunk, (j + 1) * ff_chunk)
        a = jnp.maximum(_dot(h2, w1_ref[:, sl].astype(BF16)), 0.0)
        acc = acc + _dot((a * a).astype(BF16), w2_ref[sl, :].astype(BF16))
    o_ref[0] = x1 + mod(5) * _rms(acc, ng_ref[3:4, :])


def _post(att, ret, ssd, x_ctx, x_lat, lat_off, mods, layer, ng, w_out, w_ff1, w_ff2, n_ctx_tiles,
          with_ctx):
    bsz, t, _ = ret.shape
    d = x_lat.shape[2]
    d_ff = w_ff1.shape[2]
    first = 0 if with_ctx else n_ctx_tiles
    nt = t // ROW_TILE - first
    assert att.shape[1] == nt * ROW_TILE
    mod_spec = pl.BlockSpec((None,) + mods.shape[1:], lambda b, i: (layer, 0, 0))
    tok = lambda w: pl.BlockSpec((1, ROW_TILE, w), lambda b, i: (b, i + first, 0))
    const = lambda shape: pl.BlockSpec(shape, lambda b, i: (0,) * len(shape),
                                       pipeline_mode=pl.Buffered(1))
    weight = lambda shape: pl.BlockSpec((None,) + shape, lambda b, i: (layer, 0, 0),
                                        pipeline_mode=pl.Buffered(1))
    return pl.pallas_call(
        functools.partial(_post_kernel, ff_chunk=1024, n_ctx_tiles=n_ctx_tiles, first=first,
                          ctx_row=bsz),
        grid=(bsz, nt),
        in_specs=[pl.BlockSpec((1, ROW_TILE, D_ATT), lambda b, i: (b, i, 0)), tok(D_RET), tok(D_SSD),
                  *_token_specs(d, n_ctx_tiles, lat_off, first),
                  mod_spec,
                  const((4, d)), weight((d, d)), weight((d, d_ff)), weight((d_ff, d))],
        out_specs=pl.BlockSpec((1, ROW_TILE, d), lambda b, i: (b, i, 0)),
        out_shape=jax.ShapeDtypeStruct((bsz, nt * ROW_TILE, d), F32),
        compiler_params=pltpu.CompilerParams(
            dimension_semantics=("arbitrary", "arbitrary"), vmem_limit_bytes=VMEM_LIMIT),
        name="post",
    )(att, ret, ssd, x_ctx, x_lat, mods, ng, w_out, w_ff1, w_ff2)


def _rope_tables(n, n_ctx):
    rows = n // GRID_W
    row = jnp.broadcast_to(jnp.arange(rows)[:, None], (rows, GRID_W)).reshape(n)
    col = jnp.broadcast_to(jnp.arange(GRID_W)[None, :], (rows, GRID_W)).reshape(n)
    half = HEAD_DIM // 2
    inv_freq = ROPE_THETA ** (-jnp.arange(0, half, 2, dtype=F32) / half)
    ang = jnp.stack([row, col], axis=-1).astype(F32)[:, :, None] * inv_freq
    ang = jnp.concatenate([ang, ang], axis=-1).reshape(n, HEAD_DIM)
    ang = jnp.tile(ang, (1, LANES // HEAD_DIM))
    cos = jnp.cos(ang)
    sin = jnp.sin(ang)
    upper = (jnp.arange(LANES) % half) >= half // 2
    sin_p = jnp.where(upper, sin, 0.0)
    sin_m = jnp.where(upper, 0.0, -sin)
    ident = lambda v, a: jnp.concatenate([jnp.full((n_ctx, LANES), v, F32), a], axis=0)
    return ident(1.0, cos), ident(0.0, sin_p), ident(0.0, sin_m)


def kernel(x, c, ctx, c_ctx, w_mod, b_mod, norm_g, w_in, w_out, q_norm_g, k_norm_g, ret_decay_logit,
           ret_gn_g, ret_gn_b, ssd_conv_w, ssd_conv_b, ssd_dt_bias, ssd_a_log, ssd_d, ssd_norm_g,
           w_ff1, w_ff2):
    bsz, n, d = x.shape
    n_ctx = ctx.shape[1]
    depth = w_mod.shape[0]
    assert n % ROW_TILE == 0 and n_ctx % ROW_TILE == 0 and n_ctx % CHUNK == 0 and n % CHUNK == 0
    assert w_in.shape[2] == OFF_DT + 2 * H_SSD
    n_ctx_tiles = n_ctx // ROW_TILE

    n_rows = -(-(bsz + 1) // 8) * 8
    cc = jnp.concatenate([c, c_ctx[None, :], jnp.zeros((n_rows - bsz - 1, d), F32)], axis=0)
    mods = _modulation(cc, w_mod, b_mod)

    cos, sin_p, sin_m = _rope_tables(n, n_ctx)
    n_dt = 2 * H_SSD
    pad_dt = lambda a: jnp.pad(a.reshape(1, n_dt), ((0, 0), (0, D_DT - n_dt)))

    x_ctx, x_lat, lat_off = ctx, x, n_ctx_tiles
    wi = jnp.pad(w_in, ((0, 0), (0, 0), (0, D_IN_PAD - w_in.shape[2]))).astype(BF16)
    for layer in range(depth):
        last = layer == depth - 1
        gqk = jnp.concatenate([jnp.tile(q_norm_g[layer], H_ATT) * (HEAD_DIM ** -0.5 * LOG2_E),
                               jnp.tile(k_norm_g[layer], H_KV)])[None, :]

        q, k, v, ret, z, xbc, dt = _inproj(x_ctx, x_lat, lat_off, n_ctx + n, mods, layer,
                                           norm_g[layer], wi, cos, sin_p, sin_m, gqk, n_ctx_tiles)
        att = _attention(q, k, v, n_ctx, with_ctx=not last)
        lg = jnp.repeat(ret_decay_logit[layer], HEAD_DIM, axis=1)
        ret_o = _retention(ret, lg, ret_gn_g[layer][None, :], ret_gn_b[layer][None, :], n_ctx)
        conv_w = jnp.pad(ssd_conv_w[layer], ((0, 8 - SSD_CONV), (0, 0)))
        ssd_o = _ssd(z, xbc, dt, conv_w, ssd_conv_b[layer][None, :], pad_dt(ssd_dt_bias[layer]),
                     pad_dt(ssd_a_log[layer]), jnp.repeat(ssd_d[layer], HEAD_DIM)[None, :],
                     ssd_norm_g[layer][None, :], n_ctx)
        xall = _post(att, ret_o, ssd_o, x_ctx, x_lat, lat_off, mods, layer, norm_g[layer], w_out,
                     w_ff1, w_ff2, n_ctx_tiles, with_ctx=not last)
        x_ctx, x_lat, lat_off = xall, xall, 0
    return xall
```

```python
import functools

import jax
import jax.numpy as jnp
from jax import lax
from jax.experimental import pallas as pl
from jax.experimental.pallas import tpu as pltpu

F32 = jnp.float32
BF16 = jnp.bfloat16

HEAD_DIM = 64
H_ATT = 6
H_KV = 2
H_RET = 4
H_SSD = 6
SSD_GROUPS = 2
SSD_STATE = 128
SSD_CONV = 5
GRID_W = 64
ROPE_THETA = 10000.0
EPS = 1e-6
LOG2_E = 1.4426950408889634

D_ATT = H_ATT * HEAD_DIM
D_KV = H_KV * HEAD_DIM
D_RET = H_RET * HEAD_DIM
D_SSD = H_SSD * HEAD_DIM
D_BC = SSD_GROUPS * SSD_STATE
D_XBC = D_SSD + 2 * D_BC
LANES = 128
D_DT = LANES
OFF_Q = 0
OFF_K = OFF_Q + D_ATT
OFF_V = OFF_K + D_KV
OFF_RET = OFF_V + D_KV
OFF_Z = OFF_RET + 4 * D_RET
OFF_XBC = OFF_Z + D_SSD
OFF_DT = OFF_XBC + D_XBC
D_IN_PAD = OFF_DT + D_DT

ROW_TILE = 256
CHUNK = 256
VMEM_LIMIT = 56 * 1024 * 1024


def _silu(x):
    return x * jax.nn.sigmoid(x)


def _split2(a):
    hi = a.astype(BF16)
    lo = (a - hi.astype(F32)).astype(BF16)
    return hi, lo


def _dot(a, b):
    return jnp.dot(a, b, preferred_element_type=F32)


def _dot_nt(a, b):
    return lax.dot_general(a, b, (((1,), (1,)), ((), ())), preferred_element_type=F32)


def _dot2_right(a, m):
    hi, lo = _split2(a)
    return _dot(hi, m) + _dot(lo, m)


PART_LANES = 16


def _pack3(a, base):
    hi = a.astype(BF16).astype(F32)
    r = a - hi
    mid = r.astype(BF16).astype(F32)
    out = hi + pltpu.roll(mid, PART_LANES, 1) + pltpu.roll(r - mid, 2 * PART_LANES, 1)
    return pltpu.roll(out, base, 1) if base else out


def _unpack3(p):
    return p + pltpu.roll(p, LANES - PART_LANES, 1) + pltpu.roll(p, LANES - 2 * PART_LANES, 1)


def _rms(x, g):
    ms = jnp.mean(x * x, axis=-1, keepdims=True)
    return x * lax.rsqrt(ms + EPS) * g


def _head_avg_matrix(n):
    r = lax.broadcasted_iota(jnp.int32, (n, n), 0) // HEAD_DIM
    c = lax.broadcasted_iota(jnp.int32, (n, n), 1) // HEAD_DIM
    return jnp.where(r == c, 1.0 / HEAD_DIM, 0.0).astype(BF16)


def _mod_kernel(c_ref, w_ref, b_ref, o_ref):
    sc = _silu(c_ref[...]).astype(BF16)
    o_ref[0] = _dot(sc, w_ref[0].astype(BF16)) + b_ref[0]


def _modulation(cc, w_mod, b_mod):
    depth, d, n = w_mod.shape
    rows = cc.shape[0]
    tn = n // 4
    return pl.pallas_call(
        _mod_kernel,
        grid=(depth, n // tn),
        in_specs=[
            pl.BlockSpec((rows, d), lambda l, j: (0, 0)),
            pl.BlockSpec((1, d, tn), lambda l, j: (l, 0, j)),
            pl.BlockSpec((1, 1, tn), lambda l, j: (l, 0, j)),
        ],
        out_specs=pl.BlockSpec((1, rows, tn), lambda l, j: (l, 0, j)),
        out_shape=jax.ShapeDtypeStruct((depth, rows, n), F32),
        compiler_params=pltpu.CompilerParams(
            dimension_semantics=("arbitrary", "arbitrary"), vmem_limit_bytes=VMEM_LIMIT),
        name="modulation",
    )(cc, w_mod, b_mod.reshape(depth, 1, n))


def _mod_chunk(mod_ref, row, k):
    d = mod_ref.shape[1] // 6
    return mod_ref[pl.ds(row, 1), k * d:(k + 1) * d]


ROT_HALF = HEAD_DIM // 4


def _rope(x, cos, sin_p, sin_m):
    return (x * cos + pltpu.roll(x, ROT_HALF, 1) * sin_p
            + pltpu.roll(x, LANES - ROT_HALF, 1) * sin_m)


def _inproj_kernel(xc_ref, xl_ref, mod_ref, ng_ref, w_ref, cos_ref, sp_ref, sm_ref, gqk_ref,
                   qt_ref, k_ref, v_ref, ret_ref, z_ref, xbc_ref, dt_ref, *, n_ctx_tiles, ctx_row):
    is_ctx = pl.program_id(1) < n_ctx_tiles
    x = jnp.where(is_ctx, xc_ref[0], xl_ref[0])
    mod = functools.partial(_mod_chunk, mod_ref, jnp.where(is_ctx, ctx_row, pl.program_id(0)))
    h = _rms(x, ng_ref[0:1, :])
    h = h * (1.0 + mod(1)) + mod(0)
    hb = h.astype(BF16)
    rope = lambda tile: _rope(tile, cos_ref[...], sp_ref[...], sm_ref[...])
    kscale = HEAD_DIM ** -0.5

    def emit(col, tile):
        sl = lambda off: slice(col - off, col - off + LANES)
        if col < OFF_K:
            qt_ref[0, sl(OFF_Q), :] = rope(tile).T.astype(BF16)
        elif col < OFF_V:
            k_ref[0, :, sl(OFF_K)] = rope(tile).astype(BF16)
        elif col < OFF_RET:
            v_ref[0, :, sl(OFF_V)] = tile.astype(BF16)
        elif col < OFF_RET + D_RET:
            ret_ref[0, :, sl(OFF_RET)] = rope(tile).astype(BF16)
        elif col < OFF_RET + 2 * D_RET:
            ret_ref[0, :, sl(OFF_RET)] = (rope(tile) * kscale).astype(BF16)
        elif col < OFF_Z:
            ret_ref[0, :, sl(OFF_RET)] = tile.astype(BF16)
        elif col < OFF_XBC:
            z_ref[0, :, sl(OFF_Z)] = tile.astype(BF16)
        elif col < OFF_DT:
            xbc_ref[0, :, sl(OFF_XBC)] = tile.astype(BF16)
        else:
            dt_ref[0, :, sl(OFF_DT)] = tile

    group = OFF_V
    for g in range(D_IN_PAD // group):
        acc = _dot(hb, w_ref[:, g * group:(g + 1) * group])
        if g == 0:
            ms = _dot((acc * acc).astype(BF16), _head_avg_matrix(group))
            acc = acc * lax.rsqrt(ms + EPS) * gqk_ref[...]
        for j in range(group // LANES):
            emit(g * group + j * LANES, acc[:, j * LANES:(j + 1) * LANES])


def _token_specs(d, n_ctx_tiles, lat_off, first=0):
    ctx_spec = pl.BlockSpec((1, ROW_TILE, d),
                            lambda b, i: (b, jnp.minimum(i + first, n_ctx_tiles - 1), 0))
    lat_spec = pl.BlockSpec((1, ROW_TILE, d),
                            lambda b, i: (b, jnp.maximum(i + first, n_ctx_tiles) - lat_off, 0))
    return ctx_spec, lat_spec


def _inproj(x_ctx, x_lat, lat_off, t, mods, layer, ng, w_in, cos, sin_p, sin_m, gqk, n_ctx_tiles):
    bsz, _, d = x_lat.shape
    nt = t // ROW_TILE
    mod_spec = pl.BlockSpec((None,) + mods.shape[1:], lambda b, i: (layer, 0, 0))

    tok = lambda w: pl.BlockSpec((1, ROW_TILE, w), lambda b, i: (b, i, 0))
    rope_spec = pl.BlockSpec((ROW_TILE, LANES), lambda b, i: (i, 0))
    const = lambda shape: pl.BlockSpec(shape, lambda b, i: (0,) * len(shape))
    widths = (D_KV, D_KV, 4 * D_RET, D_SSD, D_XBC)
    qt_spec = pl.BlockSpec((1, D_ATT, ROW_TILE), lambda b, i: (b, 0, i))
    return pl.pallas_call(
        functools.partial(_inproj_kernel, n_ctx_tiles=n_ctx_tiles, ctx_row=bsz),
        grid=(bsz, nt),
        in_specs=[
            *_token_specs(d, n_ctx_tiles, lat_off),
            mod_spec,
            const((4, d)),
            pl.BlockSpec((None, d, D_IN_PAD), lambda b, i: (layer, 0, 0),
                         pipeline_mode=pl.Buffered(1)),
            rope_spec, rope_spec, rope_spec,
            const((1, D_ATT + D_KV)),
        ],
        out_specs=[qt_spec] + [tok(w) for w in widths] + [tok(D_DT)],
        out_shape=[jax.ShapeDtypeStruct((bsz, D_ATT, t), BF16)]
        + [jax.ShapeDtypeStruct((bsz, t, w), BF16) for w in widths]
        + [jax.ShapeDtypeStruct((bsz, t, D_DT), F32)],
        compiler_params=pltpu.CompilerParams(
            dimension_semantics=("arbitrary", "arbitrary"), vmem_limit_bytes=VMEM_LIMIT),
        name="inproj",
    )(x_ctx, x_lat, mods, ng, w_in, cos, sin_p, sin_m, gqk)


V_ROWS = 80
KEY_BLOCK = 256
NEG_BIG = -1e30


def _attn_kernel(qt_ref, k_ref, v_ref, o_ref, km_ref, vt_ref, acc_ref, m_ref, alpha_ref, s_ref, p_ref,
                 *, n_ctx, with_ctx):
    qi = pl.program_id(1)
    per_tile = LANES // HEAD_DIM
    n_tiles = D_ATT // LANES

    @pl.when(qi == 0)
    def _():
        k = k_ref[0].astype(F32)
        kr = pltpu.roll(k, HEAD_DIM, 1)
        low = lax.broadcasted_iota(jnp.int32, k.shape, 1) < HEAD_DIM
        km_ref[0] = jnp.where(low, k, 0.0).astype(BF16)
        km_ref[1] = jnp.where(low, 0.0, kr).astype(BF16)
        km_ref[2] = jnp.where(low, kr, 0.0).astype(BF16)
        km_ref[3] = jnp.where(low, 0.0, k).astype(BF16)
        vt = v_ref[0].astype(F32).T
        row = lax.broadcasted_iota(jnp.int32, (V_ROWS - HEAD_DIM, vt.shape[1]), 0)
        tail = jnp.where(row == 0, 1.0, 0.0)
        for g in range(H_KV):
            vg = jnp.concatenate([vt[g * HEAD_DIM:(g + 1) * HEAD_DIM], tail], axis=0).astype(BF16)
            for blk in range(vt.shape[1] // KEY_BLOCK):
                vt_ref[g, blk] = vg[:, blk * KEY_BLOCK:(blk + 1) * KEY_BLOCK]

    kv_of = lambda h: h // (H_ATT // H_KV)

    def scores(i):
        for j in range(n_tiles):
            ks = jnp.concatenate(
                [km_ref[2 * kv_of(j * per_tile + half) + half, i * KEY_BLOCK:(i + 1) * KEY_BLOCK, :]
                 for half in range(per_tile)], axis=0)
            s_ref[i % 2, j] = _dot(ks, qt_ref[0, j * LANES:(j + 1) * LANES, :])

    def exponentials(i):
        for h in range(H_ATT):
            j, half = divmod(h, per_tile)
            s = s_ref[i % 2, j, half * KEY_BLOCK:(half + 1) * KEY_BLOCK, :]
            m_old = m_ref[h]
            m_new = jnp.maximum(m_old, jnp.max(s, axis=0, keepdims=True))
            m_ref[h] = m_new
            alpha_ref[i % 2, h] = jnp.exp2(m_old - m_new)
            p_ref[i % 2, h] = jnp.exp2(s - m_new).astype(BF16)

    def weighted_values(i):
        for h in range(H_ATT):
            pv = _dot(vt_ref[kv_of(h), i], p_ref[i % 2, h])
            acc_ref[h] = acc_ref[h] * alpha_ref[i % 2, h] + pv

    def run(n_blocks):
        acc_ref[...] = jnp.zeros_like(acc_ref)
        m_ref[...] = jnp.full(m_ref.shape, NEG_BIG, F32)
        for it in range(n_blocks + 2):
            if it >= 2:
                weighted_values(it - 2)
            if it < n_blocks:
                scores(it)
            if 1 <= it <= n_blocks:
                exponentials(it - 1)
        for j in range(n_tiles):
            halves = []
            for half in range(per_tile):
                acc = acc_ref[j * per_tile + half]
                halves.append(acc[0:HEAD_DIM] * (1.0 / acc[HEAD_DIM:HEAD_DIM + 1]))
            ot = jnp.concatenate(halves, axis=0)
            o_ref[0, :, j * LANES:(j + 1) * LANES] = ot.T.astype(BF16)

    n_all = k_ref.shape[1] // KEY_BLOCK
    if with_ctx:
        @pl.when(qi == 0)
        def _():
            run(n_ctx // KEY_BLOCK)

        @pl.when(qi > 0)
        def _():
            run(n_all)
    else:
        run(n_all)


def _attention(qt, k, v, n_ctx, with_ctx):
    bsz, _, t = qt.shape
    tq = ROW_TILE
    assert n_ctx == tq and n_ctx % KEY_BLOCK == 0 and t % KEY_BLOCK == 0
    first = 0 if with_ctx else n_ctx // tq
    nq = t // tq - first
    return pl.pallas_call(
        functools.partial(_attn_kernel, n_ctx=n_ctx, with_ctx=with_ctx),
        grid=(bsz, nq),
        in_specs=[
            pl.BlockSpec((1, D_ATT, tq), lambda b, i: (b, 0, i + first)),
            pl.BlockSpec((1, t, D_KV), lambda b, i: (b, 0, 0)),
            pl.BlockSpec((1, t, D_KV), lambda b, i: (b, 0, 0)),
        ],
        out_specs=pl.BlockSpec((1, tq, D_ATT), lambda b, i: (b, i, 0)),
        out_shape=jax.ShapeDtypeStruct((bsz, nq * tq, D_ATT), BF16),
        scratch_shapes=[pltpu.VMEM((2 * H_KV, t, D_KV), BF16),
                        pltpu.VMEM((H_KV, t // KEY_BLOCK, V_ROWS, KEY_BLOCK), BF16),
                        pltpu.VMEM((H_ATT, V_ROWS, tq), F32),
                        pltpu.VMEM((H_ATT, 1, tq), F32),
                        pltpu.VMEM((2, H_ATT, 1, tq), F32),
                        pltpu.VMEM((2, D_ATT // LANES, (LANES // HEAD_DIM) * KEY_BLOCK, tq), F32),
                        pltpu.VMEM((2, H_ATT, KEY_BLOCK, tq), BF16)],
        compiler_params=pltpu.CompilerParams(
            dimension_semantics=("arbitrary", "arbitrary"), vmem_limit_bytes=VMEM_LIMIT),
        name="attention",
    )(qt, k, v)


def _log_sigmoid(x):
    return jnp.minimum(x, 0.0) - jnp.log1p(jnp.exp(-jnp.abs(x)))


def _ret_kernel(r_ref, lg_ref, gng_ref, gnb_ref, o_ref, y_ref, sf_ref, sb_ref, w_ref, dm_ref,
                *, n_ctx_chunks):
    c_len = CHUNK
    t = r_ref.shape[1]
    n_chunks = t // c_len
    lg = _log_sigmoid(lg_ref[...])
    lgf = lg[0:1, :]
    lgb = lg[1:2, :]
    dec_f = jnp.exp(c_len * lgf)
    dec_b = jnp.exp(c_len * lgb)

    @pl.when(pl.program_id(0) == 0)
    def _():
        tcol = lax.broadcasted_iota(jnp.int32, (c_len, 1), 0).astype(F32)
        w_ref[0] = jnp.exp((tcol + 1.0) * lgf)
        w_ref[1] = jnp.exp((c_len - tcol) * lgb)
        w_ref[2] = jnp.exp((c_len - 1.0 - tcol) * lgf)
        w_ref[3] = jnp.exp(tcol * lgb)
        ti = lax.broadcasted_iota(jnp.int32, (c_len, c_len), 0)
        si = lax.broadcasted_iota(jnp.int32, (c_len, c_len), 1)
        diff = (ti - si).astype(F32)
        for h in range(H_RET):
            lf = lgf[:, h * HEAD_DIM:h * HEAD_DIM + 1]
            lb = lgb[:, h * HEAD_DIM:h * HEAD_DIM + 1]
            dm_ref[h] = jnp.exp(jnp.where(diff >= 0, diff * lf, -diff * lb))

    lane = lax.broadcasted_iota(jnp.int32, (c_len, D_RET), 1)
    srow = lax.broadcasted_iota(jnp.int32, (D_RET, D_RET), 0) // HEAD_DIM
    scol = lax.broadcasted_iota(jnp.int32, (D_RET, D_RET), 1) // HEAD_DIM
    smask = srow == scol
    avg = _head_avg_matrix(D_RET)

    def chunk_rows(c):
        return pl.ds(pl.multiple_of(c * c_len, c_len), c_len)

    def state_delta(k, v, wk):
        kw = (k.astype(F32) * wk).T.astype(BF16)
        return jnp.where(smask, _dot(kw, v), 0.0)

    sf_ref[...] = jnp.zeros_like(sf_ref)
    sb_ref[...] = jnp.zeros_like(sb_ref)

    def fwd(c, carry):
        rows = chunk_rows(c)
        q = r_ref[0, rows, 0:D_RET]
        k = r_ref[0, rows, D_RET:2 * D_RET]
        v = r_ref[0, rows, 2 * D_RET:3 * D_RET]
        zero = jnp.zeros_like(q)
        y = w_ref[0] * _dot(q, sf_ref[...].astype(BF16))
        for h in range(H_RET):
            hm = (lane >= h * HEAD_DIM) & (lane < (h + 1) * HEAD_DIM)
            s = _dot_nt(jnp.where(hm, q, zero), k) * dm_ref[h]
            y = y + _dot(s.astype(BF16), jnp.where(hm, v, zero))
        y_ref[rows, :] = y
        sf_ref[...] = dec_f * sf_ref[...] + state_delta(k, v, w_ref[2])
        return carry

    lax.fori_loop(0, n_chunks, fwd, 0, unroll=3)

    def bwd(c):
        rows = chunk_rows(c)
        q = r_ref[0, rows, 0:D_RET]
        k = r_ref[0, rows, D_RET:2 * D_RET]
        v = r_ref[0, rows, 2 * D_RET:3 * D_RET]
        g = r_ref[0, rows, 3 * D_RET:4 * D_RET].astype(F32)
        y = y_ref[rows, :] + w_ref[1] * _dot(q, sb_ref[...].astype(BF16))
        sb_ref[...] = dec_b * sb_ref[...] + state_delta(k, v, w_ref[3])
        mu = _dot2_right(y, avg)
        d = y - mu
        var = _dot((d * d).astype(BF16), avg)
        yn = d * lax.rsqrt(var + EPS) * gng_ref[...] + gnb_ref[...]
        o_ref[0, rows, :] = (yn * _silu(g)).astype(BF16)

    def bwd_ctx(i, carry):
        bwd(n_ctx_chunks - 1 - i)
        return carry

    def bwd_lat(i, carry):
        bwd(n_chunks - 1 - i)
        return carry

    lax.fori_loop(0, n_ctx_chunks, bwd_ctx, 0)
    lax.fori_loop(0, n_chunks - n_ctx_chunks, bwd_lat, 0, unroll=4)


def _retention(ret, lg, gng, gnb, n_ctx):
    bsz, t, _ = ret.shape
    const = lambda shape: pl.BlockSpec(shape, lambda b: (0,) * len(shape))
    return pl.pallas_call(
        functools.partial(_ret_kernel, n_ctx_chunks=n_ctx // CHUNK),
        grid=(bsz,),
        in_specs=[
            pl.BlockSpec((1, t, 4 * D_RET), lambda b: (b, 0, 0)),
            const((2, D_RET)), const((1, D_RET)), const((1, D_RET)),
        ],
        out_specs=pl.BlockSpec((1, t, D_RET), lambda b: (b, 0, 0)),
        out_shape=jax.ShapeDtypeStruct((bsz, t, D_RET), BF16),
        scratch_shapes=[pltpu.VMEM((t, D_RET), F32), pltpu.VMEM((D_RET, D_RET), F32),
                        pltpu.VMEM((D_RET, D_RET), F32), pltpu.VMEM((4, CHUNK, D_RET), F32),
                        pltpu.VMEM((H_RET, CHUNK, CHUNK), F32)],
        compiler_params=pltpu.CompilerParams(
            dimension_semantics=("arbitrary",), vmem_limit_bytes=VMEM_LIMIT),
        name="retention",
    )(ret, lg, gng, gnb)


def _softplus(x):
    return jnp.maximum(x, 0.0) + jnp.log1p(jnp.exp(-jnp.abs(x)))


def _ssd_kernel(z_ref, xbc_ref, dt_ref, cw_ref, cb_ref, dtb_ref, alog_ref, dskip_ref, ng_ref,
                o_ref, xc_ref, y_ref, eb_ref, dsb_ref, sf_ref, sb_ref, ce_all, g_all, sc_all,
                *, n_ctx):
    c_len = CHUNK
    t = xbc_ref.shape[1]
    n_chunks = t // c_len
    n_ctx_chunks = n_ctx // c_len
    edge = 16
    half = SSD_CONV // 2

    taps = [k for k in range(-half, half + 1) if k]
    ri = lax.broadcasted_iota(jnp.int32, (c_len, c_len), 0)
    ci = lax.broadcasted_iota(jnp.int32, (c_len, c_len), 1)
    shift = {k: jnp.where(ci == ri + k, 1.0, 0.0).astype(BF16) for k in taps}
    er = lax.broadcasted_iota(jnp.int32, (edge, edge), 0)
    ec = lax.broadcasted_iota(jnp.int32, (edge, edge), 1)
    prev_rows = {k: jnp.where(ec == er + k + edge, 1.0, 0.0).astype(BF16) for k in taps if k < 0}
    next_rows = {k: jnp.where(ec == er + k - edge, 1.0, 0.0).astype(BF16) for k in taps if k > 0}
    tap_w = lambda k, lanes: cw_ref[k + half:k + half + 1, lanes]
    lane_groups = [slice(lo, min(lo + 2 * LANES, D_XBC)) for lo in range(0, D_XBC, 2 * LANES)]
    for c in range(n_chunks):
        r0 = c * c_len
        has_prev = c not in (0, n_ctx_chunks)
        has_next = c not in (n_ctx_chunks - 1, n_chunks - 1)
        for lanes in lane_groups:
            xb = xbc_ref[0, r0:r0 + c_len, lanes]
            acc = cb_ref[:, lanes] + tap_w(0, lanes) * xb.astype(F32)
            for k in taps:
                acc = acc + tap_w(k, lanes) * _dot(shift[k], xb)
            head, tail = acc[0:edge], acc[c_len - edge:c_len]
            if has_prev:
                p = xbc_ref[0, r0 - edge:r0, lanes]
                for k in prev_rows:
                    head = head + tap_w(k, lanes) * _dot(prev_rows[k], p)
            if has_next:
                nx = xbc_ref[0, r0 + c_len:r0 + c_len + edge, lanes]
                for k in next_rows:
                    tail = tail + tap_w(k, lanes) * _dot(next_rows[k], nx)
            xc_ref[r0:r0 + edge, lanes] = _silu(head).astype(BF16)
            xc_ref[r0 + edge:r0 + c_len - edge, lanes] = _silu(acc[edge:c_len - edge]).astype(BF16)
            xc_ref[r0 + c_len - edge:r0 + c_len, lanes] = _silu(tail).astype(BF16)

    nd = 2 * H_SSD
    lane_dt = lax.broadcasted_iota(jnp.int32, (1, D_DT), 1)
    live = lane_dt < nd
    a_vec = jnp.where(live, -jnp.exp(alog_ref[...]), 0.0)
    is_f = lane_dt < H_SSD
    ti = lax.broadcasted_iota(jnp.int32, (c_len, c_len), 0)
    si = lax.broadcasted_iota(jnp.int32, (c_len, c_len), 1)
    causal = si <= ti
    tri_l = jnp.where(causal, 1.0, 0.0).astype(BF16)
    tri_u = jnp.where(si >= ti, 1.0, 0.0).astype(BF16)
    n_exp = 2 * D_SSD
    er = lax.broadcasted_iota(jnp.int32, (D_DT, 2 * n_exp), 0)
    ej = lax.broadcasted_iota(jnp.int32, (D_DT, 2 * n_exp), 1)
    rel = er - jnp.where(ej >= n_exp, 3 * PART_LANES, 0)
    col = jnp.where(ej >= n_exp, ej - n_exp, ej) // HEAD_DIM
    expand = jnp.where((rel == col) | (rel == col + PART_LANES) | (rel == col + 2 * PART_LANES),
                       1.0, 0.0).astype(BF16)
    lane_w = lax.broadcasted_iota(jnp.int32, (c_len, 2 * LANES), 1)
    lane_c = lax.broadcasted_iota(jnp.int32, (c_len, D_BC), 1)
    srow = lax.broadcasted_iota(jnp.int32, (D_BC, D_SSD), 0) // SSD_STATE
    scol = lax.broadcasted_iota(jnp.int32, (D_BC, D_SSD), 1) // (D_SSD // SSD_GROUPS)
    smask = srow == scol
    heads_per_group = H_SSD // SSD_GROUPS

    def chunk_rows(c):
        return pl.ds(pl.multiple_of(c * c_len, c_len), c_len)

    sf_ref[...] = jnp.zeros_like(sf_ref)
    sb_ref[...] = jnp.zeros_like(sb_ref)

    def fwd_chunk(c, slot):
        ce_ref, g_ref, sc_ref = ce_all.at[slot], g_all.at[slot], sc_all.at[slot]
        rows = chunk_rows(c)
        xs = xc_ref[rows, 0:D_SSD]
        bm = xc_ref[rows, D_SSD:D_SSD + D_BC]
        cm = xc_ref[rows, D_SSD + D_BC:D_XBC]
        dt = _softplus(dt_ref[0, rows, :] + dtb_ref[...])
        la = _pack3(dt * a_vec, 0).astype(BF16)
        cum = jnp.where(is_f, _unpack3(_dot(tri_l, la)), _unpack3(_dot(tri_u, la)))
        cum = jnp.where(live, cum, 0.0)
        edge = jnp.where(is_f, cum[c_len - 1:c_len, :], cum[0:1, :])
        wk = jnp.where(live, jnp.exp(edge - cum) * dt, 0.0)
        packed = (_pack3(cum, 0) + _pack3(wk, 3 * PART_LANES)).astype(BF16)
        ce_ref[...] = _dot(packed, expand)
        key_t = (cum - jnp.log(dt)).T

        e_f = jnp.exp(ce_ref[:, 0:D_SSD])
        eb_ref[rows, :] = jnp.exp(ce_ref[:, D_SSD:n_exp])
        y_ref[rows, :] = e_f * _dot(cm, sf_ref[...].astype(BF16))
        zc = jnp.zeros_like(cm)
        for g in range(SSD_GROUPS):
            gm = (lane_c >= g * SSD_STATE) & (lane_c < (g + 1) * SSD_STATE)
            g_ref[...] = _dot_nt(jnp.where(gm, cm, zc), bm)
            win = slice(g * LANES, (g + 2) * LANES)
            xw = xs[:, win]
            zw = jnp.zeros_like(xw)
            yg = None
            for hh in range(heads_per_group):
                h = g * heads_per_group + hh
                d_f = cum[:, h:h + 1] - key_t[h:h + 1, :]
                d_b = cum[:, H_SSD + h:H_SSD + h + 1] - key_t[H_SSD + h:H_SSD + h + 1, :]
                sc_ref[hh] = (jnp.exp(jnp.where(causal, d_f, d_b)) * g_ref[...]).astype(BF16)
                lo = h * HEAD_DIM - g * LANES
                hm = (lane_w >= lo) & (lane_w < lo + HEAD_DIM)
                part = _dot(sc_ref[hh], jnp.where(hm, xw, zw))
                yg = part if yg is None else yg + part
            y_ref[rows, win] = y_ref[rows, win] + yg

        xs_f = xs.astype(F32)
        bm_t = bm.astype(F32).T.astype(BF16)
        ds_f = _dot(bm_t, (xs_f * ce_ref[:, n_exp:n_exp + D_SSD]).astype(BF16))
        ds_b = _dot(bm_t, (xs_f * ce_ref[:, n_exp + D_SSD:2 * n_exp]).astype(BF16))
        sf_ref[...] = e_f[c_len - 1:c_len, :] * sf_ref[...] + jnp.where(smask, ds_f, 0.0)
        dsb_ref[c] = jnp.where(smask, ds_b, 0.0)

    unroll = ce_all.shape[0]

    def fwd(i, carry):
        for k in range(unroll):
            fwd_chunk(i * unroll + k, k)
        return carry

    lax.fori_loop(0, n_chunks // unroll, fwd, 0)

    def bwd(c):
        rows = chunk_rows(c)
        xs_f = xc_ref[rows, 0:D_SSD].astype(F32)
        cm = xc_ref[rows, D_SSD + D_BC:D_XBC]
        e_b = eb_ref[rows, :]
        y = y_ref[rows, :] + e_b * _dot(cm, sb_ref[...].astype(BF16))
        sb_ref[...] = e_b[0:1, :] * sb_ref[...] + dsb_ref[c]
        y = y + dskip_ref[...] * xs_f
        u = y * _silu(z_ref[0, rows, :].astype(F32))
        o_ref[0, rows, :] = _rms(u, ng_ref[...]).astype(BF16)

    def bwd_ctx(i, carry):
        bwd(n_ctx_chunks - 1 - i)
        return carry

    def bwd_lat(i, carry):
        bwd(n_chunks - 1 - i)
        return carry

    lax.fori_loop(0, n_ctx_chunks, bwd_ctx, 0)
    lax.fori_loop(0, n_chunks - n_ctx_chunks, bwd_lat, 0, unroll=2)


def _ssd(z, xbc, dt, conv_w, conv_b, dt_bias, a_log, d_skip, norm_g, n_ctx):
    bsz, t, _ = xbc.shape
    n_chunks = t // CHUNK
    unroll = next(u for u in (3, 2, 1) if n_chunks % u == 0)
    const = lambda shape: pl.BlockSpec(shape, lambda b: (0,) * len(shape))
    seq = lambda w: pl.BlockSpec((1, t, w), lambda b: (b, 0, 0))
    return pl.pallas_call(
        functools.partial(_ssd_kernel, n_ctx=n_ctx),
        grid=(bsz,),
        in_specs=[seq(D_SSD), seq(D_XBC), seq(D_DT),
                  const((8, D_XBC)), const((1, D_XBC)), const((1, D_DT)), const((1, D_DT)),
                  const((1, D_SSD)), const((1, D_SSD))],
        out_specs=seq(D_SSD),
        out_shape=jax.ShapeDtypeStruct((bsz, t, D_SSD), BF16),
        scratch_shapes=[
            pltpu.VMEM((t, D_XBC), BF16),
            pltpu.VMEM((t, D_SSD), F32),
            pltpu.VMEM((t, D_SSD), F32),
            pltpu.VMEM((n_chunks, D_BC, D_SSD), F32),
            pltpu.VMEM((D_BC, D_SSD), F32),
            pltpu.VMEM((D_BC, D_SSD), F32),
            pltpu.VMEM((unroll, CHUNK, 4 * D_SSD), F32),
            pltpu.VMEM((unroll, CHUNK, CHUNK), F32),
            pltpu.VMEM((unroll, H_SSD // SSD_GROUPS, CHUNK, CHUNK), BF16),
        ],
        compiler_params=pltpu.CompilerParams(
            dimension_semantics=("arbitrary",), vmem_limit_bytes=VMEM_LIMIT),
        name="ssd",
    )(z, xbc, dt, conv_w, conv_b, dt_bias, a_log, d_skip, norm_g)


def _post_kernel(att_ref, ret_ref, ssd_ref, xc_ref, xl_ref, mod_ref, ng_ref, wo_ref, w1_ref, w2_ref,
                 o_ref, *, ff_chunk, n_ctx_tiles, first, ctx_row):
    is_ctx = pl.program_id(1) + first < n_ctx_tiles
    x = jnp.where(is_ctx, xc_ref[0], xl_ref[0])
    mod = functools.partial(_mod_chunk, mod_ref, jnp.where(is_ctx, ctx_row, pl.program_id(0)))
    mix = jnp.concatenate([att_ref[0], ret_ref[0], ssd_ref[0]], axis=-1)
    o = _dot(mix, wo_ref[...].astype(BF16))
    x1 = x + mod(2) * _rms(o, ng_ref[1:2, :])
    h2 = _rms(x1, ng_ref[2:3, :]) * (1.0 + mod(4)) + mod(3)
    h2 = h2.astype(BF16)
    d_ff = w1_ref.shape[1]
    acc = jnp.zeros(x1.shape, F32)
    for j in range(d_ff // ff_chunk):
        sl = slice(j * ff_chunk, (j + 1) * ff_chunk)
        a = jnp.maximum(_dot(h2, w1_ref[:, sl].astype(BF16)), 0.0)
        acc = acc + _dot((a * a).astype(BF16), w2_ref[sl, :].astype(BF16))
    o_ref[0] = x1 + mod(5) * _rms(acc, ng_ref[3:4, :])


def _post(att, ret, ssd, x_ctx, x_lat, lat_off, mods, layer, ng, w_out, w_ff1, w_ff2, n_ctx_tiles,
          with_ctx):
    bsz, t, _ = ret.shape
    d = x_lat.shape[2]
    d_ff = w_ff1.shape[2]
    first = 0 if with_ctx else n_ctx_tiles
    nt = t // ROW_TILE - first
    assert att.shape[1] == nt * ROW_TILE
    mod_spec = pl.BlockSpec((None,) + mods.shape[1:], lambda b, i: (layer, 0, 0))
    tok = lambda w: pl.BlockSpec((1, ROW_TILE, w), lambda b, i: (b, i + first, 0))
    const = lambda shape: pl.BlockSpec(shape, lambda b, i: (0,) * len(shape),
                                       pipeline_mode=pl.Buffered(1))
    weight = lambda shape: pl.BlockSpec((None,) + shape, lambda b, i: (layer, 0, 0),
                                        pipeline_mode=pl.Buffered(1))
    return pl.pallas_call(
        functools.partial(_post_kernel, ff_chunk=1024, n_ctx_tiles=n_ctx_tiles, first=first,
                          ctx_row=bsz),
        grid=(bsz, nt),
        in_specs=[pl.BlockSpec((1, ROW_TILE, D_ATT), lambda b, i: (b, i, 0)), tok(D_RET), tok(D_SSD),
                  *_token_specs(d, n_ctx_tiles, lat_off, first),
                  mod_spec,
                  const((4, d)), weight((d, d)), weight((d, d_ff)), weight((d_ff, d))],
        out_specs=pl.BlockSpec((1, ROW_TILE, d), lambda b, i: (b, i, 0)),
        out_shape=jax.ShapeDtypeStruct((bsz, nt * ROW_TILE, d), F32),
        compiler_params=pltpu.CompilerParams(
            dimension_semantics=("arbitrary", "arbitrary"), vmem_limit_bytes=VMEM_LIMIT),
        name="post",
    )(att, ret, ssd, x_ctx, x_lat, mods, ng, w_out, w_ff1, w_ff2)


def _rope_tables(n, n_ctx):
    rows = n // GRID_W
    row = jnp.broadcast_to(jnp.arange(rows)[:, None], (rows, GRID_W)).reshape(n)
    col = jnp.broadcast_to(jnp.arange(GRID_W)[None, :], (rows, GRID_W)).reshape(n)
    half = HEAD_DIM // 2
    inv_freq = ROPE_THETA ** (-jnp.arange(0, half, 2, dtype=F32) / half)
    ang = jnp.stack([row, col], axis=-1).astype(F32)[:, :, None] * inv_freq
    ang = jnp.concatenate([ang, ang], axis=-1).reshape(n, HEAD_DIM)
    ang = jnp.tile(ang, (1, LANES // HEAD_DIM))
    cos = jnp.cos(ang)
    sin = jnp.sin(ang)
    upper = (jnp.arange(LANES) % half) >= half // 2
    sin_p = jnp.where(upper, sin, 0.0)
    sin_m = jnp.where(upper, 0.0, -sin)
    ident = lambda v, a: jnp.concatenate([jnp.full((n_ctx, LANES), v, F32), a], axis=0)
    return ident(1.0, cos), ident(0.0, sin_p), ident(0.0, sin_m)


def kernel(x, c, ctx, c_ctx, w_mod, b_mod, norm_g, w_in, w_out, q_norm_g, k_norm_g, ret_decay_logit,
           ret_gn_g, ret_gn_b, ssd_conv_w, ssd_conv_b, ssd_dt_bias, ssd_a_log, ssd_d, ssd_norm_g,
           w_ff1, w_ff2):
    bsz, n, d = x.shape
    n_ctx = ctx.shape[1]
    depth = w_mod.shape[0]
    assert n % ROW_TILE == 0 and n_ctx % ROW_TILE == 0 and n_ctx % CHUNK == 0 and n % CHUNK == 0
    assert w_in.shape[2] == OFF_DT + 2 * H_SSD
    n_ctx_tiles = n_ctx // ROW_TILE

    n_rows = -(-(bsz + 1) // 8) * 8
    cc = jnp.concatenate([c, c_ctx[None, :], jnp.zeros((n_rows - bsz - 1, d), F32)], axis=0)
    mods = _modulation(cc, w_mod, b_mod)

    cos, sin_p, sin_m = _rope_tables(n, n_ctx)
    n_dt = 2 * H_SSD
    pad_dt = lambda a: jnp.pad(a.reshape(1, n_dt), ((0, 0), (0, D_DT - n_dt)))

    x_ctx, x_lat, lat_off = ctx, x, n_ctx_tiles
    wi = jnp.pad(w_in, ((0, 0), (0, 0), (0, D_IN_PAD - w_in.shape[2]))).astype(BF16)
    for layer in range(depth):
        last = layer == depth - 1
        gqk = jnp.concatenate([jnp.tile(q_norm_g[layer], H_ATT) * (HEAD_DIM ** -0.5 * LOG2_E),
                               jnp.tile(k_norm_g[layer], H_KV)])[None, :]

        q, k, v, ret, z, xbc, dt = _inproj(x_ctx, x_lat, lat_off, n_ctx + n, mods, layer,
                                           norm_g[layer], wi, cos, sin_p, sin_m, gqk, n_ctx_tiles)
        att = _attention(q, k, v, n_ctx, with_ctx=not last)
        lg = jnp.repeat(ret_decay_logit[layer], HEAD_DIM, axis=1)
        ret_o = _retention(ret, lg, ret_gn_g[layer][None, :], ret_gn_b[layer][None, :], n_ctx)
        conv_w = jnp.pad(ssd_conv_w[layer], ((0, 8 - SSD_CONV), (0, 0)))
        ssd_o = _ssd(z, xbc, dt, conv_w, ssd_conv_b[layer][None, :], pad_dt(ssd_dt_bias[layer]),
                     pad_dt(ssd_a_log[layer]), jnp.repeat(ssd_d[layer], HEAD_DIM)[None, :],
                     ssd_norm_g[layer][None, :], n_ctx)
        xall = _post(att, ret_o, ssd_o, x_ctx, x_lat, lat_off, mods, layer, norm_g[layer], w_out,
                     w_ff1, w_ff2, n_ctx_tiles, with_ctx=not last)
        x_ctx, x_lat, lat_off = xall, xall, 0
    return xall
```

```python
import functools

import jax
import jax.numpy as jnp
from jax import lax
from jax.experimental import pallas as pl
from jax.experimental.pallas import tpu as pltpu

F32 = jnp.float32
BF16 = jnp.bfloat16

HEAD_DIM = 64
H_ATT = 6
H_KV = 2
H_RET = 4
H_SSD = 6
SSD_GROUPS = 2
SSD_STATE = 128
SSD_CONV = 5
GRID_W = 64
ROPE_THETA = 10000.0
EPS = 1e-6
LOG2_E = 1.4426950408889634

D_ATT = H_ATT * HEAD_DIM
D_KV = H_KV * HEAD_DIM
D_RET = H_RET * HEAD_DIM
D_SSD = H_SSD * HEAD_DIM
D_BC = SSD_GROUPS * SSD_STATE
D_XBC = D_SSD + 2 * D_BC
LANES = 128
D_DT = LANES
OFF_Q = 0
OFF_K = OFF_Q + D_ATT
OFF_V = OFF_K + D_KV
OFF_RET = OFF_V + D_KV
OFF_Z = OFF_RET + 4 * D_RET
OFF_XBC = OFF_Z + D_SSD
OFF_DT = OFF_XBC + D_XBC
D_IN_PAD = OFF_DT + D_DT

ROW_TILE = 256
CHUNK = 256
VMEM_LIMIT = 56 * 1024 * 1024


def _silu(x):
    return x * jax.nn.sigmoid(x)


def _split2(a):
    hi = a.astype(BF16)
    lo = (a - hi.astype(F32)).astype(BF16)
    return hi, lo


def _dot(a, b):
    return jnp.dot(a, b, preferred_element_type=F32)


def _dot_nt(a, b):
    return lax.dot_general(a, b, (((1,), (1,)), ((), ())), preferred_element_type=F32)


def _dot2_right(a, m):
    hi, lo = _split2(a)
    return _dot(hi, m) + _dot(lo, m)


PART_LANES = 16


def _pack3(a, base):
    hi = a.astype(BF16).astype(F32)
    r = a - hi
    mid = r.astype(BF16).astype(F32)
    out = hi + pltpu.roll(mid, PART_LANES, 1) + pltpu.roll(r - mid, 2 * PART_LANES, 1)
    return pltpu.roll(out, base, 1) if base else out


def _unpack3(p):
    return p + pltpu.roll(p, LANES - PART_LANES, 1) + pltpu.roll(p, LANES - 2 * PART_LANES, 1)


def _rms(x, g):
    ms = jnp.mean(x * x, axis=-1, keepdims=True)
    return x * lax.rsqrt(ms + EPS) * g


def _head_avg_matrix(n):
    r = lax.broadcasted_iota(jnp.int32, (n, n), 0) // HEAD_DIM
    c = lax.broadcasted_iota(jnp.int32, (n, n), 1) // HEAD_DIM
    return jnp.where(r == c, 1.0 / HEAD_DIM, 0.0).astype(BF16)


def _mod_kernel(c_ref, w_ref, b_ref, o_ref):
    sc = _silu(c_ref[...]).astype(BF16)
    o_ref[0] = _dot(sc, w_ref[0].astype(BF16)) + b_ref[0]


def _modulation(cc, w_mod, b_mod):
    depth, d, n = w_mod.shape
    rows = cc.shape[0]
    tn = n // 4
    return pl.pallas_call(
        _mod_kernel,
        grid=(depth, n // tn),
        in_specs=[
            pl.BlockSpec((rows, d), lambda l, j: (0, 0)),
            pl.BlockSpec((1, d, tn), lambda l, j: (l, 0, j)),
            pl.BlockSpec((1, 1, tn), lambda l, j: (l, 0, j)),
        ],
        out_specs=pl.BlockSpec((1, rows, tn), lambda l, j: (l, 0, j)),
        out_shape=jax.ShapeDtypeStruct((depth, rows, n), F32),
        compiler_params=pltpu.CompilerParams(
            dimension_semantics=("arbitrary", "arbitrary"), vmem_limit_bytes=VMEM_LIMIT),
        name="modulation",
    )(cc, w_mod, b_mod.reshape(depth, 1, n))


def _mod_chunk(mod_ref, row, k):
    d = mod_ref.shape[1] // 6
    return mod_ref[pl.ds(row, 1), k * d:(k + 1) * d]


ROT_HALF = HEAD_DIM // 4


def _rope(x, cos, sin_p, sin_m):
    return (x * cos + pltpu.roll(x, ROT_HALF, 1) * sin_p
            + pltpu.roll(x, LANES - ROT_HALF, 1) * sin_m)


def _inproj_kernel(xc_ref, xl_ref, mod_ref, ng_ref, w_ref, cos_ref, sp_ref, sm_ref, gqk_ref,
                   qt_ref, k_ref, v_ref, ret_ref, z_ref, xbc_ref, dt_ref, *, n_ctx_tiles, ctx_row):
    is_ctx = pl.program_id(1) < n_ctx_tiles
    x = jnp.where(is_ctx, xc_ref[0], xl_ref[0])
    mod = functools.partial(_mod_chunk, mod_ref, jnp.where(is_ctx, ctx_row, pl.program_id(0)))
    h = _rms(x, ng_ref[0:1, :])
    h = h * (1.0 + mod(1)) + mod(0)
    hb = h.astype(BF16)
    rope = lambda tile: _rope(tile, cos_ref[...], sp_ref[...], sm_ref[...])
    kscale = HEAD_DIM ** -0.5

    def emit(col, tile):
        sl = lambda off: slice(col - off, col - off + LANES)
        if col < OFF_K:
            qt_ref[0, sl(OFF_Q), :] = rope(tile).T.astype(BF16)
        elif col < OFF_V:
            k_ref[0, :, sl(OFF_K)] = rope(tile).astype(BF16)
        elif col < OFF_RET:
            v_ref[0, :, sl(OFF_V)] = tile.astype(BF16)
        elif col < OFF_RET + D_RET:
            ret_ref[0, :, sl(OFF_RET)] = rope(tile).astype(BF16)
        elif col < OFF_RET + 2 * D_RET:
            ret_ref[0, :, sl(OFF_RET)] = (rope(tile) * kscale).astype(BF16)
        elif col < OFF_Z:
            ret_ref[0, :, sl(OFF_RET)] = tile.astype(BF16)
        elif col < OFF_XBC:
            z_ref[0, :, sl(OFF_Z)] = tile.astype(BF16)
        elif col < OFF_DT:
            xbc_ref[0, :, sl(OFF_XBC)] = tile.astype(BF16)
        else:
            dt_ref[0, :, sl(OFF_DT)] = tile

    group = OFF_V
    for g in range(D_IN_PAD // group):
        acc = _dot(hb, w_ref[:, g * group:(g + 1) * group])
        if g == 0:
            ms = _dot((acc * acc).astype(BF16), _head_avg_matrix(group))
            acc = acc * lax.rsqrt(ms + EPS) * gqk_ref[...]
        for j in range(group // LANES):
            emit(g * group + j * LANES, acc[:, j * LANES:(j + 1) * LANES])


def _token_specs(d, n_ctx_tiles, lat_off, first=0):
    ctx_spec = pl.BlockSpec((1, ROW_TILE, d),
                            lambda b, i: (b, jnp.minimum(i + first, n_ctx_tiles - 1), 0))
    lat_spec = pl.BlockSpec((1, ROW_TILE, d),
                            lambda b, i: (b, jnp.maximum(i + first, n_ctx_tiles) - lat_off, 0))
    return ctx_spec, lat_spec


def _inproj(x_ctx, x_lat, lat_off, t, mods, layer, ng, w_in, cos, sin_p, sin_m, gqk, n_ctx_tiles):
    bsz, _, d = x_lat.shape
    nt = t // ROW_TILE
    mod_spec = pl.BlockSpec((None,) + mods.shape[1:], lambda b, i: (layer, 0, 0))

    tok = lambda w: pl.BlockSpec((1, ROW_TILE, w), lambda b, i: (b, i, 0))
    rope_spec = pl.BlockSpec((ROW_TILE, LANES), lambda b, i: (i, 0))
    const = lambda shape: pl.BlockSpec(shape, lambda b, i: (0,) * len(shape))
    widths = (D_KV, D_KV, 4 * D_RET, D_SSD, D_XBC)
    qt_spec = pl.BlockSpec((1, D_ATT, ROW_TILE), lambda b, i: (b, 0, i))
    return pl.pallas_call(
        functools.partial(_inproj_kernel, n_ctx_tiles=n_ctx_tiles, ctx_row=bsz),
        grid=(bsz, nt),
        in_specs=[
            *_token_specs(d, n_ctx_tiles, lat_off),
            mod_spec,
            const((4, d)),
            pl.BlockSpec((None, d, D_IN_PAD), lambda b, i: (layer, 0, 0),
                         pipeline_mode=pl.Buffered(1)),
            rope_spec, rope_spec, rope_spec,
            const((1, D_ATT + D_KV)),
        ],
        out_specs=[qt_spec] + [tok(w) for w in widths] + [tok(D_DT)],
        out_shape=[jax.ShapeDtypeStruct((bsz, D_ATT, t), BF16)]
        + [jax.ShapeDtypeStruct((bsz, t, w), BF16) for w in widths]
        + [jax.ShapeDtypeStruct((bsz, t, D_DT), F32)],
        compiler_params=pltpu.CompilerParams(
            dimension_semantics=("arbitrary", "arbitrary"), vmem_limit_bytes=VMEM_LIMIT),
        name="inproj",
    )(x_ctx, x_lat, mods, ng, w_in, cos, sin_p, sin_m, gqk)


V_ROWS = 80
KEY_BLOCK = 256
NEG_BIG = -1e30


def _attn_kernel(*refs, n_q, n_blocks):
    q_refs = refs[:n_q]
    k_ref, v_ref, o_ref, km_ref, vt_ref, acc_ref, m_ref, alpha_ref, s_ref, p_ref = refs[n_q:]
    per_tile = LANES // HEAD_DIM
    n_tiles = D_ATT // LANES
    tq = q_refs[0].shape[2]

    @pl.when(pl.program_id(1) == 0)
    def _():
        k = k_ref[0].astype(F32)
        kr = pltpu.roll(k, HEAD_DIM, 1)
        low = lax.broadcasted_iota(jnp.int32, k.shape, 1) < HEAD_DIM
        km_ref[0] = jnp.where(low, k, 0.0).astype(BF16)
        km_ref[1] = jnp.where(low, 0.0, kr).astype(BF16)
        km_ref[2] = jnp.where(low, kr, 0.0).astype(BF16)
        km_ref[3] = jnp.where(low, 0.0, k).astype(BF16)
        vt = v_ref[0].astype(F32).T
        row = lax.broadcasted_iota(jnp.int32, (V_ROWS - HEAD_DIM, vt.shape[1]), 0)
        tail = jnp.where(row == 0, 1.0, 0.0)
        for g in range(H_KV):
            vg = jnp.concatenate([vt[g * HEAD_DIM:(g + 1) * HEAD_DIM], tail], axis=0).astype(BF16)
            for blk in range(n_blocks):
                vt_ref[g, blk] = vg[:, blk * KEY_BLOCK:(blk + 1) * KEY_BLOCK]

    kv_of = lambda h: h // (H_ATT // H_KV)

    def scores(n):
        t, i = divmod(n, n_blocks)
        for j in range(n_tiles):
            ks = jnp.concatenate(
                [km_ref[2 * kv_of(j * per_tile + half) + half, i * KEY_BLOCK:(i + 1) * KEY_BLOCK, :]
                 for half in range(per_tile)], axis=0)
            s_ref[n % 2, j] = _dot(ks, q_refs[t][0, j * LANES:(j + 1) * LANES, :])

    def exponentials(n):
        t = n // n_blocks
        for h in range(H_ATT):
            j, half = divmod(h, per_tile)
            s = s_ref[n % 2, j, half * KEY_BLOCK:(half + 1) * KEY_BLOCK, :]
            m_old = m_ref[t, h]
            m_new = jnp.maximum(m_old, jnp.max(s, axis=0, keepdims=True))
            m_ref[t, h] = m_new
            alpha_ref[n % 2, h] = jnp.exp2(m_old - m_new)
            p_ref[n % 2, h] = jnp.exp2(s - m_new).astype(BF16)

    def weighted_values(n):
        t, i = divmod(n, n_blocks)
        for h in range(H_ATT):
            pv = _dot(vt_ref[kv_of(h), i], p_ref[n % 2, h])
            acc_ref[t, h] = acc_ref[t, h] * alpha_ref[n % 2, h] + pv

    def finish(t):
        for j in range(n_tiles):
            halves = []
            for half in range(per_tile):
                acc = acc_ref[t, j * per_tile + half]
                halves.append(acc[0:HEAD_DIM] * (1.0 / acc[HEAD_DIM:HEAD_DIM + 1]))
            ot = jnp.concatenate(halves, axis=0)
            o_ref[0, t * tq:(t + 1) * tq, j * LANES:(j + 1) * LANES] = ot.T.astype(BF16)

    acc_ref[...] = jnp.zeros_like(acc_ref)
    m_ref[...] = jnp.full(m_ref.shape, NEG_BIG, F32)
    n_items = n_q * n_blocks
    for it in range(n_items + 2):
        if it >= 2:
            weighted_values(it - 2)
            if (it - 1) % n_blocks == 0:
                finish((it - 2) // n_blocks)
        if it < n_items:
            scores(it)
        if 1 <= it <= n_items:
            exponentials(it - 1)


def _attention(qt, k, v, first_row, n_rows, n_keys, n_q):
    bsz = qt.shape[0]
    tq = ROW_TILE
    assert first_row % tq == 0 and n_rows % (n_q * tq) == 0 and n_keys % KEY_BLOCK == 0
    n_blocks = n_keys // KEY_BLOCK
    q_spec = lambda j: pl.BlockSpec((1, D_ATT, tq),
                                    lambda b, i: (b, 0, first_row // tq + n_q * i + j))
    kv_spec = pl.BlockSpec((1, n_keys, D_KV), lambda b, i: (b, 0, 0))
    return pl.pallas_call(
        functools.partial(_attn_kernel, n_q=n_q, n_blocks=n_blocks),
        grid=(bsz, n_rows // (n_q * tq)),
        in_specs=[q_spec(j) for j in range(n_q)] + [kv_spec, kv_spec],
        out_specs=pl.BlockSpec((1, n_q * tq, D_ATT), lambda b, i: (b, i, 0)),
        out_shape=jax.ShapeDtypeStruct((bsz, n_rows, D_ATT), BF16),
        scratch_shapes=[pltpu.VMEM((2 * H_KV, n_keys, D_KV), BF16),
                        pltpu.VMEM((H_KV, n_blocks, V_ROWS, KEY_BLOCK), BF16),
                        pltpu.VMEM((n_q, H_ATT, V_ROWS, tq), F32),
                        pltpu.VMEM((n_q, H_ATT, 1, tq), F32),
                        pltpu.VMEM((2, H_ATT, 1, tq), F32),
                        pltpu.VMEM((2, D_ATT // LANES, (LANES // HEAD_DIM) * KEY_BLOCK, tq), F32),
                        pltpu.VMEM((2, H_ATT, KEY_BLOCK, tq), BF16)],
        compiler_params=pltpu.CompilerParams(
            dimension_semantics=("arbitrary", "arbitrary"), vmem_limit_bytes=VMEM_LIMIT),
        name="attention",
    )(*([qt] * n_q), k, v)


def _log_sigmoid(x):
    return jnp.minimum(x, 0.0) - jnp.log1p(jnp.exp(-jnp.abs(x)))


def _ret_kernel(r_ref, lg_ref, gng_ref, gnb_ref, o_ref, y_ref, sf_ref, sb_ref, w_ref, dm_ref,
                *, n_ctx_chunks):
    c_len = CHUNK
    t = r_ref.shape[1]
    n_chunks = t // c_len
    lg = _log_sigmoid(lg_ref[...])
    lgf = lg[0:1, :]
    lgb = lg[1:2, :]
    dec_f = jnp.exp(c_len * lgf)
    dec_b = jnp.exp(c_len * lgb)

    @pl.when(pl.program_id(0) == 0)
    def _():
        tcol = lax.broadcasted_iota(jnp.int32, (c_len, 1), 0).astype(F32)
        w_ref[0] = jnp.exp((tcol + 1.0) * lgf)
        w_ref[1] = jnp.exp((c_len - tcol) * lgb)
        w_ref[2] = jnp.exp((c_len - 1.0 - tcol) * lgf)
        w_ref[3] = jnp.exp(tcol * lgb)
        ti = lax.broadcasted_iota(jnp.int32, (c_len, c_len), 0)
        si = lax.broadcasted_iota(jnp.int32, (c_len, c_len), 1)
        diff = (ti - si).astype(F32)
        for h in range(H_RET):
            lf = lgf[:, h * HEAD_DIM:h * HEAD_DIM + 1]
            lb = lgb[:, h * HEAD_DIM:h * HEAD_DIM + 1]
            dm_ref[h] = jnp.exp(jnp.where(diff >= 0, diff * lf, -diff * lb))

    lane = lax.broadcasted_iota(jnp.int32, (c_len, D_RET), 1)
    srow = lax.broadcasted_iota(jnp.int32, (D_RET, D_RET), 0) // HEAD_DIM
    scol = lax.broadcasted_iota(jnp.int32, (D_RET, D_RET), 1) // HEAD_DIM
    smask = srow == scol
    avg = _head_avg_matrix(D_RET)

    def chunk_rows(c):
        return pl.ds(pl.multiple_of(c * c_len, c_len), c_len)

    def state_delta(k, v, wk):
        kw = (k.astype(F32) * wk).T.astype(BF16)
        return jnp.where(smask, _dot(kw, v), 0.0)

    sf_ref[...] = jnp.zeros_like(sf_ref)
    sb_ref[...] = jnp.zeros_like(sb_ref)

    def fwd(c, carry):
        rows = chunk_rows(c)
        q = r_ref[0, rows, 0:D_RET]
        k = r_ref[0, rows, D_RET:2 * D_RET]
        v = r_ref[0, rows, 2 * D_RET:3 * D_RET]
        zero = jnp.zeros_like(q)
        y = w_ref[0] * _dot(q, sf_ref[...].astype(BF16))
        for h in range(H_RET):
            hm = (lane >= h * HEAD_DIM) & (lane < (h + 1) * HEAD_DIM)
            s = _dot_nt(jnp.where(hm, q, zero), k) * dm_ref[h]
            y = y + _dot(s.astype(BF16), jnp.where(hm, v, zero))
        y_ref[rows, :] = y
        sf_ref[...] = dec_f * sf_ref[...] + state_delta(k, v, w_ref[2])
        return carry

    lax.fori_loop(0, n_chunks, fwd, 0, unroll=3)

    def bwd(c):
        rows = chunk_rows(c)
        q = r_ref[0, rows, 0:D_RET]
        k = r_ref[0, rows, D_RET:2 * D_RET]
        v = r_ref[0, rows, 2 * D_RET:3 * D_RET]
        g = r_ref[0, rows, 3 * D_RET:4 * D_RET].astype(F32)
        y = y_ref[rows, :] + w_ref[1] * _dot(q, sb_ref[...].astype(BF16))
        sb_ref[...] = dec_b * sb_ref[...] + state_delta(k, v, w_ref[3])
        mu = _dot2_right(y, avg)
        d = y - mu
        var = _dot((d * d).astype(BF16), avg)
        yn = d * lax.rsqrt(var + EPS) * gng_ref[...] + gnb_ref[...]
        o_ref[0, rows, :] = (yn * _silu(g)).astype(BF16)

    def bwd_ctx(i, carry):
        bwd(n_ctx_chunks - 1 - i)
        return carry

    def bwd_lat(i, carry):
        bwd(n_chunks - 1 - i)
        return carry

    lax.fori_loop(0, n_ctx_chunks, bwd_ctx, 0)
    lax.fori_loop(0, n_chunks - n_ctx_chunks, bwd_lat, 0, unroll=4)


def _retention(ret, lg, gng, gnb, n_ctx):
    bsz, t, _ = ret.shape
    const = lambda shape: pl.BlockSpec(shape, lambda b: (0,) * len(shape))
    return pl.pallas_call(
        functools.partial(_ret_kernel, n_ctx_chunks=n_ctx // CHUNK),
        grid=(bsz,),
        in_specs=[
            pl.BlockSpec((1, t, 4 * D_RET), lambda b: (b, 0, 0)),
            const((2, D_RET)), const((1, D_RET)), const((1, D_RET)),
        ],
        out_specs=pl.BlockSpec((1, t, D_RET), lambda b: (b, 0, 0)),
        out_shape=jax.ShapeDtypeStruct((bsz, t, D_RET), BF16),
        scratch_shapes=[pltpu.VMEM((t, D_RET), F32), pltpu.VMEM((D_RET, D_RET), F32),
                        pltpu.VMEM((D_RET, D_RET), F32), pltpu.VMEM((4, CHUNK, D_RET), F32),
                        pltpu.VMEM((H_RET, CHUNK, CHUNK), F32)],
        compiler_params=pltpu.CompilerParams(
            dimension_semantics=("arbitrary",), vmem_limit_bytes=VMEM_LIMIT),
        name="retention",
    )(ret, lg, gng, gnb)


def _softplus(x):
    return jnp.maximum(x, 0.0) + jnp.log1p(jnp.exp(-jnp.abs(x)))


def _ssd_kernel(z_ref, xbc_ref, dt_ref, cw_ref, cb_ref, dtb_ref, alog_ref, dskip_ref, ng_ref,
                o_ref, xc_ref, y_ref, eb_ref, dsb_ref, sf_ref, sb_ref, ce_all, g_all, sc_all,
                *, n_ctx):
    c_len = CHUNK
    t = xbc_ref.shape[1]
    n_chunks = t // c_len
    n_ctx_chunks = n_ctx // c_len
    edge = 16
    half = SSD_CONV // 2

    taps = [k for k in range(-half, half + 1) if k]
    ri = lax.broadcasted_iota(jnp.int32, (c_len, c_len), 0)
    ci = lax.broadcasted_iota(jnp.int32, (c_len, c_len), 1)
    shift = {k: jnp.where(ci == ri + k, 1.0, 0.0).astype(BF16) for k in taps}
    er = lax.broadcasted_iota(jnp.int32, (edge, edge), 0)
    ec = lax.broadcasted_iota(jnp.int32, (edge, edge), 1)
    prev_rows = {k: jnp.where(ec == er + k + edge, 1.0, 0.0).astype(BF16) for k in taps if k < 0}
    next_rows = {k: jnp.where(ec == er + k - edge, 1.0, 0.0).astype(BF16) for k in taps if k > 0}
    tap_w = lambda k, lanes: cw_ref[k + half:k + half + 1, lanes]
    lane_groups = [slice(lo, min(lo + 2 * LANES, D_XBC)) for lo in range(0, D_XBC, 2 * LANES)]
    for c in range(n_chunks):
        r0 = c * c_len
        has_prev = c not in (0, n_ctx_chunks)
        has_next = c not in (n_ctx_chunks - 1, n_chunks - 1)
        for lanes in lane_groups:
            xb = xbc_ref[0, r0:r0 + c_len, lanes]
            acc = cb_ref[:, lanes] + tap_w(0, lanes) * xb.astype(F32)
            for k in taps:
                acc = acc + tap_w(k, lanes) * _dot(shift[k], xb)
            head, tail = acc[0:edge], acc[c_len - edge:c_len]
            if has_prev:
                p = xbc_ref[0, r0 - edge:r0, lanes]
                for k in prev_rows:
                    head = head + tap_w(k, lanes) * _dot(prev_rows[k], p)
            if has_next:
                nx = xbc_ref[0, r0 + c_len:r0 + c_len + edge, lanes]
                for k in next_rows:
                    tail = tail + tap_w(k, lanes) * _dot(next_rows[k], nx)
            xc_ref[r0:r0 + edge, lanes] = _silu(head).astype(BF16)
            xc_ref[r0 + edge:r0 + c_len - edge, lanes] = _silu(acc[edge:c_len - edge]).astype(BF16)
            xc_ref[r0 + c_len - edge:r0 + c_len, lanes] = _silu(tail).astype(BF16)

    nd = 2 * H_SSD
    lane_dt = lax.broadcasted_iota(jnp.int32, (1, D_DT), 1)
    live = lane_dt < nd
    a_vec = jnp.where(live, -jnp.exp(alog_ref[...]), 0.0)
    is_f = lane_dt < H_SSD
    ti = lax.broadcasted_iota(jnp.int32, (c_len, c_len), 0)
    si = lax.broadcasted_iota(jnp.int32, (c_len, c_len), 1)
    causal = si <= ti
    tri_l = jnp.where(causal, 1.0, 0.0).astype(BF16)
    tri_u = jnp.where(si >= ti, 1.0, 0.0).astype(BF16)
    n_exp = 2 * D_SSD
    er = lax.broadcasted_iota(jnp.int32, (D_DT, 2 * n_exp), 0)
    ej = lax.broadcasted_iota(jnp.int32, (D_DT, 2 * n_exp), 1)
    rel = er - jnp.where(ej >= n_exp, 3 * PART_LANES, 0)
    col = jnp.where(ej >= n_exp, ej - n_exp, ej) // HEAD_DIM
    expand = jnp.where((rel == col) | (rel == col + PART_LANES) | (rel == col + 2 * PART_LANES),
                       1.0, 0.0).astype(BF16)
    lane_w = lax.broadcasted_iota(jnp.int32, (c_len, 2 * LANES), 1)
    lane_c = lax.broadcasted_iota(jnp.int32, (c_len, D_BC), 1)
    srow = lax.broadcasted_iota(jnp.int32, (D_BC, D_SSD), 0) // SSD_STATE
    scol = lax.broadcasted_iota(jnp.int32, (D_BC, D_SSD), 1) // (D_SSD // SSD_GROUPS)
    smask = srow == scol
    heads_per_group = H_SSD // SSD_GROUPS

    def chunk_rows(c):
        return pl.ds(pl.multiple_of(c * c_len, c_len), c_len)

    sf_ref[...] = jnp.zeros_like(sf_ref)
    sb_ref[...] = jnp.zeros_like(sb_ref)

    def fwd_chunk(c, slot):
        ce_ref, g_ref, sc_ref = ce_all.at[slot], g_all.at[slot], sc_all.at[slot]
        rows = chunk_rows(c)
        xs = xc_ref[rows, 0:D_SSD]
        bm = xc_ref[rows, D_SSD:D_SSD + D_BC]
        cm = xc_ref[rows, D_SSD + D_BC:D_XBC]
        dt = _softplus(dt_ref[0, rows, :] + dtb_ref[...])
        la = _pack3(dt * a_vec, 0).astype(BF16)
        cum = jnp.where(is_f, _unpack3(_dot(tri_l, la)), _unpack3(_dot(tri_u, la)))
        cum = jnp.where(live, cum, 0.0)
        edge = jnp.where(is_f, cum[c_len - 1:c_len, :], cum[0:1, :])
        wk = jnp.where(live, jnp.exp(edge - cum) * dt, 0.0)
        packed = (_pack3(cum, 0) + _pack3(wk, 3 * PART_LANES)).astype(BF16)
        ce_ref[...] = _dot(packed, expand)
        key_t = (cum - jnp.log(dt)).T

        e_f = jnp.exp(ce_ref[:, 0:D_SSD])
        eb_ref[rows, :] = jnp.exp(ce_ref[:, D_SSD:n_exp])
        y_ref[rows, :] = e_f * _dot(cm, sf_ref[...].astype(BF16))
        zc = jnp.zeros_like(cm)
        for g in range(SSD_GROUPS):
            gm = (lane_c >= g * SSD_STATE) & (lane_c < (g + 1) * SSD_STATE)
            g_ref[...] = _dot_nt(jnp.where(gm, cm, zc), bm)
            win = slice(g * LANES, (g + 2) * LANES)
            xw = xs[:, win]
            zw = jnp.zeros_like(xw)
            yg = None
            for hh in range(heads_per_group):
                h = g * heads_per_group + hh
                d_f = cum[:, h:h + 1] - key_t[h:h + 1, :]
                d_b = cum[:, H_SSD + h:H_SSD + h + 1] - key_t[H_SSD + h:H_SSD + h + 1, :]
                sc_ref[hh] = (jnp.exp(jnp.where(causal, d_f, d_b)) * g_ref[...]).astype(BF16)
                lo = h * HEAD_DIM - g * LANES
                hm = (lane_w >= lo) & (lane_w < lo + HEAD_DIM)
                part = _dot(sc_ref[hh], jnp.where(hm, xw, zw))
                yg = part if yg is None else yg + part
            y_ref[rows, win] = y_ref[rows, win] + yg

        xs_f = xs.astype(F32)
        bm_t = bm.astype(F32).T.astype(BF16)
        ds_f = _dot(bm_t, (xs_f * ce_ref[:, n_exp:n_exp + D_SSD]).astype(BF16))
        ds_b = _dot(bm_t, (xs_f * ce_ref[:, n_exp + D_SSD:2 * n_exp]).astype(BF16))
        sf_ref[...] = e_f[c_len - 1:c_len, :] * sf_ref[...] + jnp.where(smask, ds_f, 0.0)
        dsb_ref[c] = jnp.where(smask, ds_b, 0.0)

    unroll = ce_all.shape[0]

    def fwd(i, carry):
        for k in range(unroll):
            fwd_chunk(i * unroll + k, k)
        return carry

    lax.fori_loop(0, n_chunks // unroll, fwd, 0)

    def bwd(c):
        rows = chunk_rows(c)
        xs_f = xc_ref[rows, 0:D_SSD].astype(F32)
        cm = xc_ref[rows, D_SSD + D_BC:D_XBC]
        e_b = eb_ref[rows, :]
        y = y_ref[rows, :] + e_b * _dot(cm, sb_ref[...].astype(BF16))
        sb_ref[...] = e_b[0:1, :] * sb_ref[...] + dsb_ref[c]
        y = y + dskip_ref[...] * xs_f
        u = y * _silu(z_ref[0, rows, :].astype(F32))
        o_ref[0, rows, :] = _rms(u, ng_ref[...]).astype(BF16)

    def bwd_ctx(i, carry):
        bwd(n_ctx_chunks - 1 - i)
        return carry

    def bwd_lat(i, carry):
        bwd(n_chunks - 1 - i)
        return carry

    lax.fori_loop(0, n_ctx_chunks, bwd_ctx, 0)
    lax.fori_loop(0, n_chunks - n_ctx_chunks, bwd_lat, 0, unroll=2)


def _ssd(z, xbc, dt, conv_w, conv_b, dt_bias, a_log, d_skip, norm_g, n_ctx):
    bsz, t, _ = xbc.shape
    n_chunks = t // CHUNK
    unroll = next(u for u in (3, 2, 1) if n_chunks % u == 0)
    const = lambda shape: pl.BlockSpec(shape, lambda b: (0,) * len(shape))
    seq = lambda w: pl.BlockSpec((1, t, w), lambda b: (b, 0, 0))
    return pl.pallas_call(
        functools.partial(_ssd_kernel, n_ctx=n_ctx),
        grid=(bsz,),
        in_specs=[seq(D_SSD), seq(D_XBC), seq(D_DT),
                  const((8, D_XBC)), const((1, D_XBC)), const((1, D_DT)), const((1, D_DT)),
                  const((1, D_SSD)), const((1, D_SSD))],
        out_specs=seq(D_SSD),
        out_shape=jax.ShapeDtypeStruct((bsz, t, D_SSD), BF16),
        scratch_shapes=[
            pltpu.VMEM((t, D_XBC), BF16),
            pltpu.VMEM((t, D_SSD), F32),
            pltpu.VMEM((t, D_SSD), F32),
            pltpu.VMEM((n_chunks, D_BC, D_SSD), F32),
            pltpu.VMEM((D_BC, D_SSD), F32),
            pltpu.VMEM((D_BC, D_SSD), F32),
            pltpu.VMEM((unroll, CHUNK, 4 * D_SSD), F32),
            pltpu.VMEM((unroll, CHUNK, CHUNK), F32),
            pltpu.VMEM((unroll, H_SSD // SSD_GROUPS, CHUNK, CHUNK), BF16),
        ],
        compiler_params=pltpu.CompilerParams(
            dimension_semantics=("arbitrary",), vmem_limit_bytes=VMEM_LIMIT),
        name="ssd",
    )(z, xbc, dt, conv_w, conv_b, dt_bias, a_log, d_skip, norm_g)


def _post_kernel(attc_ref, attl_ref, ret_ref, ssd_ref, xc_ref, xl_ref, mod_ref, ng_ref, wo_ref, w1_ref,
                 w2_ref, o_ref, *, ff_chunk, n_ctx_tiles, first, ctx_row):
    is_ctx = pl.program_id(1) + first < n_ctx_tiles
    x = jnp.where(is_ctx, xc_ref[0], xl_ref[0])
    att = jnp.where(is_ctx, attc_ref[0], attl_ref[0])
    mod = functools.partial(_mod_chunk, mod_ref, jnp.where(is_ctx, ctx_row, pl.program_id(0)))
    mix = jnp.concatenate([att, ret_ref[0], ssd_ref[0]], axis=-1)
    o = _dot(mix, wo_ref[...].astype(BF16))
    x1 = x + mod(2) * _rms(o, ng_ref[1:2, :])
    h2 = _rms(x1, ng_ref[2:3, :]) * (1.0 + mod(4)) + mod(3)
    h2 = h2.astype(BF16)
    d_ff = w1_ref.shape[1]
    acc = jnp.zeros(x1.shape, F32)
    for j in range(d_ff // ff_chunk):
        sl = slice(j * ff_chunk, (j + 1) * ff_chunk)
        a = jnp.maximum(_dot(h2, w1_ref[:, sl].astype(BF16)), 0.0)
        acc = acc + _dot((a * a).astype(BF16), w2_ref[sl, :].astype(BF16))
    o_ref[0] = x1 + mod(5) * _rms(acc, ng_ref[3:4, :])


def _post(att_ctx, att_lat, ret, ssd, x_ctx, x_lat, lat_off, mods, layer, ng, w_out, w_ff1, w_ff2,
          n_ctx_tiles, with_ctx):
    bsz, t, _ = ret.shape
    d = x_lat.shape[2]
    d_ff = w_ff1.shape[2]
    first = 0 if with_ctx else n_ctx_tiles
    nt = t // ROW_TILE - first
    mod_spec = pl.BlockSpec((None,) + mods.shape[1:], lambda b, i: (layer, 0, 0))
    tok = lambda w: pl.BlockSpec((1, ROW_TILE, w), lambda b, i: (b, i + first, 0))
    const = lambda shape: pl.BlockSpec(shape, lambda b, i: (0,) * len(shape),
                                       pipeline_mode=pl.Buffered(1))
    weight = lambda shape: pl.BlockSpec((None,) + shape, lambda b, i: (layer, 0, 0),
                                        pipeline_mode=pl.Buffered(1))
    return pl.pallas_call(
        functools.partial(_post_kernel, ff_chunk=1024, n_ctx_tiles=n_ctx_tiles, first=first,
                          ctx_row=bsz),
        grid=(bsz, nt),
        in_specs=[*_token_specs(D_ATT, n_ctx_tiles, n_ctx_tiles, first), tok(D_RET), tok(D_SSD),
                  *_token_specs(d, n_ctx_tiles, lat_off, first),
                  mod_spec,
                  const((4, d)), weight((d, d)), weight((d, d_ff)), weight((d_ff, d))],
        out_specs=pl.BlockSpec((1, ROW_TILE, d), lambda b, i: (b, i, 0)),
        out_shape=jax.ShapeDtypeStruct((bsz, nt * ROW_TILE, d), F32),
        compiler_params=pltpu.CompilerParams(
            dimension_semantics=("arbitrary", "arbitrary"), vmem_limit_bytes=VMEM_LIMIT),
        name="post",
    )(att_ctx, att_lat, ret, ssd, x_ctx, x_lat, mods, ng, w_out, w_ff1, w_ff2)


def _rope_tables(n, n_ctx):
    rows = n // GRID_W
    row = jnp.broadcast_to(jnp.arange(rows)[:, None], (rows, GRID_W)).reshape(n)
    col = jnp.broadcast_to(jnp.arange(GRID_W)[None, :], (rows, GRID_W)).reshape(n)
    half = HEAD_DIM // 2
    inv_freq = ROPE_THETA ** (-jnp.arange(0, half, 2, dtype=F32) / half)
    ang = jnp.stack([row, col], axis=-1).astype(F32)[:, :, None] * inv_freq
    ang = jnp.concatenate([ang, ang], axis=-1).reshape(n, HEAD_DIM)
    ang = jnp.tile(ang, (1, LANES // HEAD_DIM))
    cos = jnp.cos(ang)
    sin = jnp.sin(ang)
    upper = (jnp.arange(LANES) % half) >= half // 2
    sin_p = jnp.where(upper, sin, 0.0)
    sin_m = jnp.where(upper, 0.0, -sin)
    ident = lambda v, a: jnp.concatenate([jnp.full((n_ctx, LANES), v, F32), a], axis=0)
    return ident(1.0, cos), ident(0.0, sin_p), ident(0.0, sin_m)


def kernel(x, c, ctx, c_ctx, w_mod, b_mod, norm_g, w_in, w_out, q_norm_g, k_norm_g, ret_decay_logit,
           ret_gn_g, ret_gn_b, ssd_conv_w, ssd_conv_b, ssd_dt_bias, ssd_a_log, ssd_d, ssd_norm_g,
           w_ff1, w_ff2):
    bsz, n, d = x.shape
    n_ctx = ctx.shape[1]
    depth = w_mod.shape[0]
    assert n % ROW_TILE == 0 and n_ctx % ROW_TILE == 0 and n_ctx % CHUNK == 0 and n % CHUNK == 0
    assert w_in.shape[2] == OFF_DT + 2 * H_SSD
    n_ctx_tiles = n_ctx // ROW_TILE

    n_rows = -(-(bsz + 1) // 8) * 8
    cc = jnp.concatenate([c, c_ctx[None, :], jnp.zeros((n_rows - bsz - 1, d), F32)], axis=0)
    mods = _modulation(cc, w_mod, b_mod)

    cos, sin_p, sin_m = _rope_tables(n, n_ctx)
    n_dt = 2 * H_SSD
    pad_dt = lambda a: jnp.pad(a.reshape(1, n_dt), ((0, 0), (0, D_DT - n_dt)))

    x_ctx, x_lat, lat_off = ctx, x, n_ctx_tiles
    wi = jnp.pad(w_in, ((0, 0), (0, 0), (0, D_IN_PAD - w_in.shape[2]))).astype(BF16)
    for layer in range(depth):
        last = layer == depth - 1
        gqk = jnp.concatenate([jnp.tile(q_norm_g[layer], H_ATT) * (HEAD_DIM ** -0.5 * LOG2_E),
                               jnp.tile(k_norm_g[layer], H_KV)])[None, :]

        q, k, v, ret, z, xbc, dt = _inproj(x_ctx, x_lat, lat_off, n_ctx + n, mods, layer,
                                           norm_g[layer], wi, cos, sin_p, sin_m, gqk, n_ctx_tiles)
        att_lat = _attention(q, k, v, n_ctx, n, n_ctx + n, n_q=2 if n % (2 * ROW_TILE) == 0 else 1)
        att_ctx = att_lat if last else _attention(q, k, v, 0, n_ctx, n_ctx, n_q=1)
        lg = jnp.repeat(ret_decay_logit[layer], HEAD_DIM, axis=1)
        ret_o = _retention(ret, lg, ret_gn_g[layer][None, :], ret_gn_b[layer][None, :], n_ctx)
        conv_w = jnp.pad(ssd_conv_w[layer], ((0, 8 - SSD_CONV), (0, 0)))
        ssd_o = _ssd(z, xbc, dt, conv_w, ssd_conv_b[layer][None, :], pad_dt(ssd_dt_bias[layer]),
                     pad_dt(ssd_a_log[layer]), jnp.repeat(ssd_d[layer], HEAD_DIM)[None, :],
                     ssd_norm_g[layer][None, :], n_ctx)
        xall = _post(att_ctx, att_lat, ret_o, ssd_o, x_ctx, x_lat, lat_off, mods, layer, norm_g[layer], w_out,
                     w_ff1, w_ff2, n_ctx_tiles, with_ctx=not last)
        x_ctx, x_lat, lat_off = xall, xall, 0
    return xall
```

```python
import functools

import jax
import jax.numpy as jnp
from jax import lax
from jax.experimental import pallas as pl
from jax.experimental.pallas import tpu as pltpu

F32 = jnp.float32
BF16 = jnp.bfloat16

HEAD_DIM = 64
H_ATT = 6
H_KV = 2
H_RET = 4
H_SSD = 6
SSD_GROUPS = 2
SSD_STATE = 128
SSD_CONV = 5
GRID_W = 64
ROPE_THETA = 10000.0
EPS = 1e-6
LOG2_E = 1.4426950408889634

D_ATT = H_ATT * HEAD_DIM
D_KV = H_KV * HEAD_DIM
D_RET = H_RET * HEAD_DIM
D_SSD = H_SSD * HEAD_DIM
D_BC = SSD_GROUPS * SSD_STATE
D_XBC = D_SSD + 2 * D_BC
LANES = 128
D_DT = LANES
OFF_Q = 0
OFF_K = OFF_Q + D_ATT
OFF_V = OFF_K + D_KV
OFF_RET = OFF_V + D_KV
OFF_Z = OFF_RET + 4 * D_RET
OFF_XBC = OFF_Z + D_SSD
OFF_DT = OFF_XBC + D_XBC
D_IN_PAD = OFF_DT + D_DT

ROW_TILE = 256
CHUNK = 256
VMEM_LIMIT = 56 * 1024 * 1024


def _silu(x):
    return x * jax.nn.sigmoid(x)


def _split2(a):
    hi = a.astype(BF16)
    lo = (a - hi.astype(F32)).astype(BF16)
    return hi, lo


def _dot(a, b):
    return jnp.dot(a, b, preferred_element_type=F32)


def _dot_nt(a, b):
    return lax.dot_general(a, b, (((1,), (1,)), ((), ())), preferred_element_type=F32)


def _dot2_right(a, m):
    hi, lo = _split2(a)
    return _dot(hi, m) + _dot(lo, m)


PART_LANES = 16


def _pack3(a, base):
    hi = a.astype(BF16).astype(F32)
    r = a - hi
    mid = r.astype(BF16).astype(F32)
    out = hi + pltpu.roll(mid, PART_LANES, 1) + pltpu.roll(r - mid, 2 * PART_LANES, 1)
    return pltpu.roll(out, base, 1) if base else out


def _unpack3(p):
    return p + pltpu.roll(p, LANES - PART_LANES, 1) + pltpu.roll(p, LANES - 2 * PART_LANES, 1)


def _rms(x, g):
    ms = jnp.mean(x * x, axis=-1, keepdims=True)
    return x * lax.rsqrt(ms + EPS) * g


def _head_avg_matrix(n):
    r = lax.broadcasted_iota(jnp.int32, (n, n), 0) // HEAD_DIM
    c = lax.broadcasted_iota(jnp.int32, (n, n), 1) // HEAD_DIM
    return jnp.where(r == c, 1.0 / HEAD_DIM, 0.0).astype(BF16)


def _mod_kernel(c_ref, w_ref, b_ref, o_ref):
    sc = _silu(c_ref[...]).astype(BF16)
    o_ref[0] = _dot(sc, w_ref[0].astype(BF16)) + b_ref[0]


def _modulation(cc, w_mod, b_mod):
    depth, d, n = w_mod.shape
    rows = cc.shape[0]
    tn = n // 4
    return pl.pallas_call(
        _mod_kernel,
        grid=(depth, n // tn),
        in_specs=[
            pl.BlockSpec((rows, d), lambda l, j: (0, 0)),
            pl.BlockSpec((1, d, tn), lambda l, j: (l, 0, j)),
            pl.BlockSpec((1, 1, tn), lambda l, j: (l, 0, j)),
        ],
        out_specs=pl.BlockSpec((1, rows, tn), lambda l, j: (l, 0, j)),
        out_shape=jax.ShapeDtypeStruct((depth, rows, n), F32),
        compiler_params=pltpu.CompilerParams(
            dimension_semantics=("arbitrary", "arbitrary"), vmem_limit_bytes=VMEM_LIMIT),
        name="modulation",
    )(cc, w_mod, b_mod.reshape(depth, 1, n))


def _mod_chunk(mod_ref, row, k):
    d = mod_ref.shape[1] // 6
    return mod_ref[pl.ds(row, 1), k * d:(k + 1) * d]


ROT_HALF = HEAD_DIM // 4


def _rope(x, cos, sin_p, sin_m):
    return (x * cos + pltpu.roll(x, ROT_HALF, 1) * sin_p
            + pltpu.roll(x, LANES - ROT_HALF, 1) * sin_m)


def _inproj_kernel(xc_ref, xl_ref, mod_ref, ng_ref, w_ref, cos_ref, sp_ref, sm_ref, gqk_ref,
                   qt_ref, k_ref, v_ref, ret_ref, z_ref, xbc_ref, dt_ref, *, n_ctx_tiles, ctx_row):
    is_ctx = pl.program_id(1) < n_ctx_tiles
    x = jnp.where(is_ctx, xc_ref[0], xl_ref[0])
    mod = functools.partial(_mod_chunk, mod_ref, jnp.where(is_ctx, ctx_row, pl.program_id(0)))
    h = _rms(x, ng_ref[0:1, :])
    h = h * (1.0 + mod(1)) + mod(0)
    hb = h.astype(BF16)
    rope = lambda tile: _rope(tile, cos_ref[...], sp_ref[...], sm_ref[...])
    kscale = HEAD_DIM ** -0.5

    def emit(col, tile):
        sl = lambda off: slice(col - off, col - off + LANES)
        if col < OFF_K:
            qt_ref[0, sl(OFF_Q), :] = rope(tile).T.astype(BF16)
        elif col < OFF_V:
            k_ref[0, :, sl(OFF_K)] = rope(tile).astype(BF16)
        elif col < OFF_RET:
            v_ref[0, :, sl(OFF_V)] = tile.astype(BF16)
        elif col < OFF_RET + D_RET:
            ret_ref[0, :, sl(OFF_RET)] = rope(tile).astype(BF16)
        elif col < OFF_RET + 2 * D_RET:
            ret_ref[0, :, sl(OFF_RET)] = (rope(tile) * kscale).astype(BF16)
        elif col < OFF_Z:
            ret_ref[0, :, sl(OFF_RET)] = tile.astype(BF16)
        elif col < OFF_XBC:
            z_ref[0, :, sl(OFF_Z)] = tile.astype(BF16)
        elif col < OFF_DT:
            xbc_ref[0, :, sl(OFF_XBC)] = tile.astype(BF16)
        else:
            dt_ref[0, :, sl(OFF_DT)] = tile

    group = OFF_V
    for g in range(D_IN_PAD // group):
        acc = _dot(hb, w_ref[:, g * group:(g + 1) * group])
        if g == 0:
            ms = _dot((acc * acc).astype(BF16), _head_avg_matrix(group))
            acc = acc * lax.rsqrt(ms + EPS) * gqk_ref[...]
        for j in range(group // LANES):
            emit(g * group + j * LANES, acc[:, j * LANES:(j + 1) * LANES])


def _token_specs(d, n_ctx_tiles, lat_off, first=0):
    ctx_spec = pl.BlockSpec((1, ROW_TILE, d),
                            lambda b, i: (b, jnp.minimum(i + first, n_ctx_tiles - 1), 0))
    lat_spec = pl.BlockSpec((1, ROW_TILE, d),
                            lambda b, i: (b, jnp.maximum(i + first, n_ctx_tiles) - lat_off, 0))
    return ctx_spec, lat_spec


def _inproj(x_ctx, x_lat, lat_off, t, mods, layer, ng, w_in, cos, sin_p, sin_m, gqk, n_ctx_tiles):
    bsz, _, d = x_lat.shape
    nt = t // ROW_TILE
    mod_spec = pl.BlockSpec((None,) + mods.shape[1:], lambda b, i: (layer, 0, 0))

    tok = lambda w: pl.BlockSpec((1, ROW_TILE, w), lambda b, i: (b, i, 0))
    rope_spec = pl.BlockSpec((ROW_TILE, LANES), lambda b, i: (i, 0))
    const = lambda shape: pl.BlockSpec(shape, lambda b, i: (0,) * len(shape))
    widths = (D_KV, D_KV, 4 * D_RET, D_SSD, D_XBC)
    qt_spec = pl.BlockSpec((1, D_ATT, ROW_TILE), lambda b, i: (b, 0, i))
    return pl.pallas_call(
        functools.partial(_inproj_kernel, n_ctx_tiles=n_ctx_tiles, ctx_row=bsz),
        grid=(bsz, nt),
        in_specs=[
            *_token_specs(d, n_ctx_tiles, lat_off),
            mod_spec,
            const((4, d)),
            pl.BlockSpec((None, d, D_IN_PAD), lambda b, i: (layer, 0, 0),
                         pipeline_mode=pl.Buffered(1)),
            rope_spec, rope_spec, rope_spec,
            const((1, D_ATT + D_KV)),
        ],
        out_specs=[qt_spec] + [tok(w) for w in widths] + [tok(D_DT)],
        out_shape=[jax.ShapeDtypeStruct((bsz, D_ATT, t), BF16)]
        + [jax.ShapeDtypeStruct((bsz, t, w), BF16) for w in widths]
        + [jax.ShapeDtypeStruct((bsz, t, D_DT), F32)],
        compiler_params=pltpu.CompilerParams(
            dimension_semantics=("arbitrary", "arbitrary"), vmem_limit_bytes=VMEM_LIMIT),
        name="inproj",
    )(x_ctx, x_lat, mods, ng, w_in, cos, sin_p, sin_m, gqk)


V_ROWS = 80
KEY_BLOCK = 256
NEG_BIG = -1e30


def _attn_kernel(*refs, n_q, n_blocks):
    q_refs = refs[:n_q]
    k_ref, v_ref, o_ref, km_ref, vt_ref, acc_ref, m_ref, alpha_ref, s_ref, p_ref = refs[n_q:]
    per_tile = LANES // HEAD_DIM
    n_tiles = D_ATT // LANES
    tq = q_refs[0].shape[2]

    @pl.when(pl.program_id(1) == 0)
    def _():
        k = k_ref[0].astype(F32)
        kr = pltpu.roll(k, HEAD_DIM, 1)
        low = lax.broadcasted_iota(jnp.int32, k.shape, 1) < HEAD_DIM
        km_ref[0] = jnp.where(low, k, 0.0).astype(BF16)
        km_ref[1] = jnp.where(low, 0.0, kr).astype(BF16)
        km_ref[2] = jnp.where(low, kr, 0.0).astype(BF16)
        km_ref[3] = jnp.where(low, 0.0, k).astype(BF16)
        vt = v_ref[0].astype(F32).T
        row = lax.broadcasted_iota(jnp.int32, (V_ROWS - HEAD_DIM, vt.shape[1]), 0)
        tail = jnp.where(row == 0, 1.0, 0.0)
        for g in range(H_KV):
            vg = jnp.concatenate([vt[g * HEAD_DIM:(g + 1) * HEAD_DIM], tail], axis=0).astype(BF16)
            for blk in range(n_blocks):
                vt_ref[g, blk] = vg[:, blk * KEY_BLOCK:(blk + 1) * KEY_BLOCK]

    kv_of = lambda h: h // (H_ATT // H_KV)

    def scores(n):
        t, i = divmod(n, n_blocks)
        for j in range(n_tiles):
            ks = jnp.concatenate(
                [km_ref[2 * kv_of(j * per_tile + half) + half, i * KEY_BLOCK:(i + 1) * KEY_BLOCK, :]
                 for half in range(per_tile)], axis=0)
            s_ref[n % 2, j] = _dot(ks, q_refs[t][0, j * LANES:(j + 1) * LANES, :])

    def exponentials(n):
        t = n // n_blocks
        for h in range(H_ATT):
            j, half = divmod(h, per_tile)
            s = s_ref[n % 2, j, half * KEY_BLOCK:(half + 1) * KEY_BLOCK, :]
            m_old = m_ref[t, h]
            m_new = jnp.maximum(m_old, jnp.max(s, axis=0, keepdims=True))
            m_ref[t, h] = m_new
            alpha_ref[n % 2, h] = jnp.exp2(m_old - m_new)
            p_ref[n % 2, h] = jnp.exp2(s - m_new).astype(BF16)

    def weighted_values(n):
        t, i = divmod(n, n_blocks)
        for h in range(H_ATT):
            pv = _dot(vt_ref[kv_of(h), i], p_ref[n % 2, h])
            acc_ref[t, h] = acc_ref[t, h] * alpha_ref[n % 2, h] + pv

    def finish(t):
        for j in range(n_tiles):
            halves = []
            for half in range(per_tile):
                acc = acc_ref[t, j * per_tile + half]
                halves.append(acc[0:HEAD_DIM] * (1.0 / acc[HEAD_DIM:HEAD_DIM + 1]))
            ot = jnp.concatenate(halves, axis=0)
            o_ref[0, t * tq:(t + 1) * tq, j * LANES:(j + 1) * LANES] = ot.T.astype(BF16)

    acc_ref[...] = jnp.zeros_like(acc_ref)
    m_ref[...] = jnp.full(m_ref.shape, NEG_BIG, F32)
    n_items = n_q * n_blocks
    for it in range(n_items + 2):
        if it >= 2:
            weighted_values(it - 2)
            if (it - 1) % n_blocks == 0:
                finish((it - 2) // n_blocks)
        if it < n_items:
            scores(it)
        if 1 <= it <= n_items:
            exponentials(it - 1)


def _attention(qt, k, v, first_row, n_rows, n_keys, n_q):
    bsz = qt.shape[0]
    tq = ROW_TILE
    assert first_row % tq == 0 and n_rows % (n_q * tq) == 0 and n_keys % KEY_BLOCK == 0
    n_blocks = n_keys // KEY_BLOCK
    q_spec = lambda j: pl.BlockSpec((1, D_ATT, tq),
                                    lambda b, i: (b, 0, first_row // tq + n_q * i + j))
    kv_spec = pl.BlockSpec((1, n_keys, D_KV), lambda b, i: (b, 0, 0))
    return pl.pallas_call(
        functools.partial(_attn_kernel, n_q=n_q, n_blocks=n_blocks),
        grid=(bsz, n_rows // (n_q * tq)),
        in_specs=[q_spec(j) for j in range(n_q)] + [kv_spec, kv_spec],
        out_specs=pl.BlockSpec((1, n_q * tq, D_ATT), lambda b, i: (b, i, 0)),
        out_shape=jax.ShapeDtypeStruct((bsz, n_rows, D_ATT), BF16),
        scratch_shapes=[pltpu.VMEM((2 * H_KV, n_keys, D_KV), BF16),
                        pltpu.VMEM((H_KV, n_blocks, V_ROWS, KEY_BLOCK), BF16),
                        pltpu.VMEM((n_q, H_ATT, V_ROWS, tq), F32),
                        pltpu.VMEM((n_q, H_ATT, 1, tq), F32),
                        pltpu.VMEM((2, H_ATT, 1, tq), F32),
                        pltpu.VMEM((2, D_ATT // LANES, (LANES // HEAD_DIM) * KEY_BLOCK, tq), F32),
                        pltpu.VMEM((2, H_ATT, KEY_BLOCK, tq), BF16)],
        compiler_params=pltpu.CompilerParams(
            dimension_semantics=("arbitrary", "arbitrary"), vmem_limit_bytes=VMEM_LIMIT),
        name="attention",
    )(*([qt] * n_q), k, v)


def _log_sigmoid(x):
    return jnp.minimum(x, 0.0) - jnp.log1p(jnp.exp(-jnp.abs(x)))


def _ret_kernel(r_ref, lg_ref, gng_ref, gnb_ref, o_ref, y_ref, sf_ref, sb_ref, w_ref, dm_ref,
                *, n_ctx_chunks):
    c_len = CHUNK
    t = r_ref.shape[1]
    n_chunks = t // c_len
    lg = _log_sigmoid(lg_ref[...])
    lgf = lg[0:1, :]
    lgb = lg[1:2, :]
    dec_f = jnp.exp(c_len * lgf)
    dec_b = jnp.exp(c_len * lgb)

    @pl.when(pl.program_id(0) == 0)
    def _():
        tcol = lax.broadcasted_iota(jnp.int32, (c_len, 1), 0).astype(F32)
        w_ref[0] = jnp.exp((tcol + 1.0) * lgf)
        w_ref[1] = jnp.exp((c_len - tcol) * lgb)
        w_ref[2] = jnp.exp((c_len - 1.0 - tcol) * lgf)
        w_ref[3] = jnp.exp(tcol * lgb)
        ti = lax.broadcasted_iota(jnp.int32, (c_len, c_len), 0)
        si = lax.broadcasted_iota(jnp.int32, (c_len, c_len), 1)
        diff = (ti - si).astype(F32)
        for h in range(H_RET):
            lf = lgf[:, h * HEAD_DIM:h * HEAD_DIM + 1]
            lb = lgb[:, h * HEAD_DIM:h * HEAD_DIM + 1]
            dm_ref[h] = jnp.exp(jnp.where(diff >= 0, diff * lf, -diff * lb))

    lane = lax.broadcasted_iota(jnp.int32, (c_len, D_RET), 1)
    srow = lax.broadcasted_iota(jnp.int32, (D_RET, D_RET), 0) // HEAD_DIM
    scol = lax.broadcasted_iota(jnp.int32, (D_RET, D_RET), 1) // HEAD_DIM
    smask = srow == scol
    avg = _head_avg_matrix(D_RET)

    def chunk_rows(c):
        return pl.ds(pl.multiple_of(c * c_len, c_len), c_len)

    def state_delta(k, v, wk):
        kw = (k.astype(F32) * wk).T.astype(BF16)
        return jnp.where(smask, _dot(kw, v), 0.0)

    sf_ref[...] = jnp.zeros_like(sf_ref)
    sb_ref[...] = jnp.zeros_like(sb_ref)

    def fwd(c, carry):
        rows = chunk_rows(c)
        q = r_ref[0, rows, 0:D_RET]
        k = r_ref[0, rows, D_RET:2 * D_RET]
        v = r_ref[0, rows, 2 * D_RET:3 * D_RET]
        zero = jnp.zeros_like(q)
        y = w_ref[0] * _dot(q, sf_ref[...].astype(BF16))
        for h in range(H_RET):
            hm = (lane >= h * HEAD_DIM) & (lane < (h + 1) * HEAD_DIM)
            s = _dot_nt(jnp.where(hm, q, zero), k) * dm_ref[h]
            y = y + _dot(s.astype(BF16), jnp.where(hm, v, zero))
        y_ref[rows, :] = y
        sf_ref[...] = dec_f * sf_ref[...] + state_delta(k, v, w_ref[2])
        return carry

    lax.fori_loop(0, n_chunks, fwd, 0, unroll=3)

    def bwd(c):
        rows = chunk_rows(c)
        q = r_ref[0, rows, 0:D_RET]
        k = r_ref[0, rows, D_RET:2 * D_RET]
        v = r_ref[0, rows, 2 * D_RET:3 * D_RET]
        g = r_ref[0, rows, 3 * D_RET:4 * D_RET].astype(F32)
        y = y_ref[rows, :] + w_ref[1] * _dot(q, sb_ref[...].astype(BF16))
        sb_ref[...] = dec_b * sb_ref[...] + state_delta(k, v, w_ref[3])
        mu = _dot2_right(y, avg)
        d = y - mu
        var = _dot((d * d).astype(BF16), avg)
        yn = d * lax.rsqrt(var + EPS) * gng_ref[...] + gnb_ref[...]
        o_ref[0, rows, :] = (yn * _silu(g)).astype(BF16)

    def bwd_ctx(i, carry):
        bwd(n_ctx_chunks - 1 - i)
        return carry

    def bwd_lat(i, carry):
        bwd(n_chunks - 1 - i)
        return carry

    lax.fori_loop(0, n_ctx_chunks, bwd_ctx, 0)
    lax.fori_loop(0, n_chunks - n_ctx_chunks, bwd_lat, 0, unroll=4)


def _retention(ret, lg, gng, gnb, n_ctx):
    bsz, t, _ = ret.shape
    const = lambda shape: pl.BlockSpec(shape, lambda b: (0,) * len(shape))
    return pl.pallas_call(
        functools.partial(_ret_kernel, n_ctx_chunks=n_ctx // CHUNK),
        grid=(bsz,),
        in_specs=[
            pl.BlockSpec((1, t, 4 * D_RET), lambda b: (b, 0, 0)),
            const((2, D_RET)), const((1, D_RET)), const((1, D_RET)),
        ],
        out_specs=pl.BlockSpec((1, t, D_RET), lambda b: (b, 0, 0)),
        out_shape=jax.ShapeDtypeStruct((bsz, t, D_RET), BF16),
        scratch_shapes=[pltpu.VMEM((t, D_RET), F32), pltpu.VMEM((D_RET, D_RET), F32),
                        pltpu.VMEM((D_RET, D_RET), F32), pltpu.VMEM((4, CHUNK, D_RET), F32),
                        pltpu.VMEM((H_RET, CHUNK, CHUNK), F32)],
        compiler_params=pltpu.CompilerParams(
            dimension_semantics=("arbitrary",), vmem_limit_bytes=VMEM_LIMIT),
        name="retention",
    )(ret, lg, gng, gnb)


def _softplus(x):
    return jnp.maximum(x, 0.0) + jnp.log1p(jnp.exp(-jnp.abs(x)))


def _ssd_kernel(z_ref, xbc_ref, dt_ref, cw_ref, cb_ref, dtb_ref, alog_ref, dskip_ref, ng_ref,
                o_ref, xc_ref, y_ref, eb_ref, dsb_ref, sf_ref, sb_ref, ce_all, g_all, sc_all,
                *, n_ctx):
    c_len = CHUNK
    t = xbc_ref.shape[1]
    n_chunks = t // c_len
    n_ctx_chunks = n_ctx // c_len
    edge = 16
    half = SSD_CONV // 2

    taps = [k for k in range(-half, half + 1) if k]
    ri = lax.broadcasted_iota(jnp.int32, (c_len, c_len), 0)
    ci = lax.broadcasted_iota(jnp.int32, (c_len, c_len), 1)
    shift = {k: jnp.where(ci == ri + k, 1.0, 0.0).astype(BF16) for k in taps}
    er = lax.broadcasted_iota(jnp.int32, (edge, edge), 0)
    ec = lax.broadcasted_iota(jnp.int32, (edge, edge), 1)
    prev_rows = {k: jnp.where(ec == er + k + edge, 1.0, 0.0).astype(BF16) for k in taps if k < 0}
    next_rows = {k: jnp.where(ec == er + k - edge, 1.0, 0.0).astype(BF16) for k in taps if k > 0}
    tap_w = lambda k, lanes: cw_ref[k + half:k + half + 1, lanes]
    lane_groups = [slice(lo, min(lo + 2 * LANES, D_XBC)) for lo in range(0, D_XBC, 2 * LANES)]
    for c in range(n_chunks):
        r0 = c * c_len
        has_prev = c not in (0, n_ctx_chunks)
        has_next = c not in (n_ctx_chunks - 1, n_chunks - 1)
        for lanes in lane_groups:
            xb = xbc_ref[0, r0:r0 + c_len, lanes]
            acc = cb_ref[:, lanes] + tap_w(0, lanes) * xb.astype(F32)
            for k in taps:
                acc = acc + tap_w(k, lanes) * _dot(shift[k], xb)
            head, tail = acc[0:edge], acc[c_len - edge:c_len]
            if has_prev:
                p = xbc_ref[0, r0 - edge:r0, lanes]
                for k in prev_rows:
                    head = head + tap_w(k, lanes) * _dot(prev_rows[k], p)
            if has_next:
                nx = xbc_ref[0, r0 + c_len:r0 + c_len + edge, lanes]
                for k in next_rows:
                    tail = tail + tap_w(k, lanes) * _dot(next_rows[k], nx)
            xc_ref[r0:r0 + edge, lanes] = _silu(head).astype(BF16)
            xc_ref[r0 + edge:r0 + c_len - edge, lanes] = _silu(acc[edge:c_len - edge]).astype(BF16)
            xc_ref[r0 + c_len - edge:r0 + c_len, lanes] = _silu(tail).astype(BF16)

    nd = 2 * H_SSD
    lane_dt = lax.broadcasted_iota(jnp.int32, (1, D_DT), 1)
    live = lane_dt < nd
    a_vec = jnp.where(live, -jnp.exp(alog_ref[...]), 0.0)
    is_f = lane_dt < H_SSD
    ti = lax.broadcasted_iota(jnp.int32, (c_len, c_len), 0)
    si = lax.broadcasted_iota(jnp.int32, (c_len, c_len), 1)
    causal = si <= ti
    tri_l = jnp.where(causal, 1.0, 0.0).astype(BF16)
    tri_u = jnp.where(si >= ti, 1.0, 0.0).astype(BF16)
    n_exp = 2 * D_SSD
    er = lax.broadcasted_iota(jnp.int32, (D_DT, 2 * n_exp), 0)
    ej = lax.broadcasted_iota(jnp.int32, (D_DT, 2 * n_exp), 1)
    rel = er - jnp.where(ej >= n_exp, 3 * PART_LANES, 0)
    col = jnp.where(ej >= n_exp, ej - n_exp, ej) // HEAD_DIM
    expand = jnp.where((rel == col) | (rel == col + PART_LANES) | (rel == col + 2 * PART_LANES),
                       1.0, 0.0).astype(BF16)
    lane_w = lax.broadcasted_iota(jnp.int32, (c_len, 2 * LANES), 1)
    lane_c = lax.broadcasted_iota(jnp.int32, (c_len, D_BC), 1)
    srow = lax.broadcasted_iota(jnp.int32, (D_BC, D_SSD), 0) // SSD_STATE
    scol = lax.broadcasted_iota(jnp.int32, (D_BC, D_SSD), 1) // (D_SSD // SSD_GROUPS)
    smask = srow == scol
    heads_per_group = H_SSD // SSD_GROUPS

    def chunk_rows(c):
        return pl.ds(pl.multiple_of(c * c_len, c_len), c_len)

    sf_ref[...] = jnp.zeros_like(sf_ref)
    sb_ref[...] = jnp.zeros_like(sb_ref)

    def fwd_chunk(c, slot):
        ce_ref, g_ref, sc_ref = ce_all.at[slot], g_all.at[slot], sc_all.at[slot]
        rows = chunk_rows(c)
        xs = xc_ref[rows, 0:D_SSD]
        bm = xc_ref[rows, D_SSD:D_SSD + D_BC]
        cm = xc_ref[rows, D_SSD + D_BC:D_XBC]
        dt = _softplus(dt_ref[0, rows, :] + dtb_ref[...])
        la = _pack3(dt * a_vec, 0).astype(BF16)
        cum = jnp.where(is_f, _unpack3(_dot(tri_l, la)), _unpack3(_dot(tri_u, la)))
        cum = jnp.where(live, cum, 0.0)
        edge = jnp.where(is_f, cum[c_len - 1:c_len, :], cum[0:1, :])
        wk = jnp.where(live, jnp.exp(edge - cum) * dt, 0.0)
        packed = (_pack3(cum, 0) + _pack3(wk, 3 * PART_LANES)).astype(BF16)
        ce_ref[...] = _dot(packed, expand)
        key_t = (cum - jnp.log(dt)).T

        e_f = jnp.exp(ce_ref[:, 0:D_SSD])
        eb_ref[rows, :] = jnp.exp(ce_ref[:, D_SSD:n_exp])
        y_ref[rows, :] = e_f * _dot(cm, sf_ref[...].astype(BF16))
        zc = jnp.zeros_like(cm)
        for g in range(SSD_GROUPS):
            gm = (lane_c >= g * SSD_STATE) & (lane_c < (g + 1) * SSD_STATE)
            g_ref[...] = _dot_nt(jnp.where(gm, cm, zc), bm)
            win = slice(g * LANES, (g + 2) * LANES)
            xw = xs[:, win]
            zw = jnp.zeros_like(xw)
            yg = None
            for hh in range(heads_per_group):
                h = g * heads_per_group + hh
                d_f = cum[:, h:h + 1] - key_t[h:h + 1, :]
                d_b = cum[:, H_SSD + h:H_SSD + h + 1] - key_t[H_SSD + h:H_SSD + h + 1, :]
                sc_ref[hh] = (jnp.exp(jnp.where(causal, d_f, d_b)) * g_ref[...]).astype(BF16)
                lo = h * HEAD_DIM - g * LANES
                hm = (lane_w >= lo) & (lane_w < lo + HEAD_DIM)
                part = _dot(sc_ref[hh], jnp.where(hm, xw, zw))
                yg = part if yg is None else yg + part
            y_ref[rows, win] = y_ref[rows, win] + yg

        xs_f = xs.astype(F32)
        bm_t = bm.astype(F32).T.astype(BF16)
        ds_f = _dot(bm_t, (xs_f * ce_ref[:, n_exp:n_exp + D_SSD]).astype(BF16))
        ds_b = _dot(bm_t, (xs_f * ce_ref[:, n_exp + D_SSD:2 * n_exp]).astype(BF16))
        sf_ref[...] = e_f[c_len - 1:c_len, :] * sf_ref[...] + jnp.where(smask, ds_f, 0.0)
        dsb_ref[c] = jnp.where(smask, ds_b, 0.0)

    unroll = ce_all.shape[0]

    def fwd(i, carry):
        for k in range(unroll):
            fwd_chunk(i * unroll + k, k)
        return carry

    lax.fori_loop(0, n_chunks // unroll, fwd, 0)

    def bwd(c):
        rows = chunk_rows(c)
        xs_f = xc_ref[rows, 0:D_SSD].astype(F32)
        cm = xc_ref[rows, D_SSD + D_BC:D_XBC]
        e_b = eb_ref[rows, :]
        y = y_ref[rows, :] + e_b * _dot(cm, sb_ref[...].astype(BF16))
        sb_ref[...] = e_b[0:1, :] * sb_ref[...] + dsb_ref[c]
        y = y + dskip_ref[...] * xs_f
        u = y * _silu(z_ref[0, rows, :].astype(F32))
        o_ref[0, rows, :] = _rms(u, ng_ref[...]).astype(BF16)

    def bwd_ctx(i, carry):
        bwd(n_ctx_chunks - 1 - i)
        return carry

    def bwd_lat(i, carry):
        bwd(n_chunks - 1 - i)
        return carry

    lax.fori_loop(0, n_ctx_chunks, bwd_ctx, 0)
    lax.fori_loop(0, n_chunks - n_ctx_chunks, bwd_lat, 0, unroll=2)


def _ssd(z, xbc, dt, conv_w, conv_b, dt_bias, a_log, d_skip, norm_g, n_ctx):
    bsz, t, _ = xbc.shape
    n_chunks = t // CHUNK
    unroll = next(u for u in (3, 2, 1) if n_chunks % u == 0)
    const = lambda shape: pl.BlockSpec(shape, lambda b: (0,) * len(shape))
    seq = lambda w: pl.BlockSpec((1, t, w), lambda b: (b, 0, 0))
    return pl.pallas_call(
        functools.partial(_ssd_kernel, n_ctx=n_ctx),
        grid=(bsz,),
        in_specs=[seq(D_SSD), seq(D_XBC), seq(D_DT),
                  const((8, D_XBC)), const((1, D_XBC)), const((1, D_DT)), const((1, D_DT)),
                  const((1, D_SSD)), const((1, D_SSD))],
        out_specs=seq(D_SSD),
        out_shape=jax.ShapeDtypeStruct((bsz, t, D_SSD), BF16),
        scratch_shapes=[
            pltpu.VMEM((t, D_XBC), BF16),
            pltpu.VMEM((t, D_SSD), F32),
            pltpu.VMEM((t, D_SSD), F32),
            pltpu.VMEM((n_chunks, D_BC, D_SSD), F32),
            pltpu.VMEM((D_BC, D_SSD), F32),
            pltpu.VMEM((D_BC, D_SSD), F32),
            pltpu.VMEM((unroll, CHUNK, 4 * D_SSD), F32),
            pltpu.VMEM((unroll, CHUNK, CHUNK), F32),
            pltpu.VMEM((unroll, H_SSD // SSD_GROUPS, CHUNK, CHUNK), BF16),
        ],
        compiler_params=pltpu.CompilerParams(
            dimension_semantics=("arbitrary",), vmem_limit_bytes=VMEM_LIMIT),
        name="ssd",
    )(z, xbc, dt, conv_w, conv_b, dt_bias, a_log, d_skip, norm_g)


def _post_kernel(*refs, n_tiles, ff_chunk, mod_row):
    att_refs, ret_refs, ssd_refs, x_refs = (refs[k * n_tiles:(k + 1) * n_tiles] for k in range(4))
    mod_ref, ng_ref, wo_ref, w1_ref, w2_ref, o_ref = refs[4 * n_tiles:]
    row = pl.program_id(0) if mod_row is None else mod_row
    mod = functools.partial(_mod_chunk, mod_ref, row)
    rows = x_refs[0].shape[1]
    n_chunk = w1_ref.shape[1] // ff_chunk

    def project(t):
        mix = jnp.concatenate([att_refs[t][0], ret_refs[t][0], ssd_refs[t][0]], axis=-1)
        return _dot(mix, wo_ref[...].astype(BF16))

    def mlp_input(t, o):
        x1 = x_refs[t][0] + mod(2) * _rms(o, ng_ref[1:2, :])
        h2 = _rms(x1, ng_ref[2:3, :]) * (1.0 + mod(4)) + mod(3)
        return x1, h2.astype(BF16)

    def mlp_chunk(h2, acc, j):
        sl = slice(j * ff_chunk, (j + 1) * ff_chunk)
        a = jnp.maximum(_dot(h2, w1_ref[:, sl].astype(BF16)), 0.0)
        part = _dot((a * a).astype(BF16), w2_ref[sl, :].astype(BF16))
        return part if acc is None else acc + part

    def finish(t, x1, acc):
        o_ref[0, t * rows:(t + 1) * rows, :] = x1 + mod(5) * _rms(acc, ng_ref[3:4, :])

    proj = [project(t) for t in range(n_tiles)]
    ready = mlp_input(0, proj[0])
    pending = None
    for t in range(n_tiles):
        x1, h2 = ready
        acc = mlp_chunk(h2, None, 0)
        if t + 1 < n_tiles:
            ready = mlp_input(t + 1, proj[t + 1])
        if pending is not None:
            finish(*pending)
        for j in range(1, n_chunk):
            acc = mlp_chunk(h2, acc, j)
        pending = (t, x1, acc)
    finish(*pending)


def _post(att, ret, ssd, x, ret_off, mods, layer, ng, w_out, w_ff1, w_ff2, mod_row, n_tiles):
    bsz, seg, d = x.shape
    assert seg % (n_tiles * ROW_TILE) == 0 and att.shape[1] == seg
    tile = lambda w, off, j: pl.BlockSpec((1, ROW_TILE, w),
                                          lambda b, i: (b, off + n_tiles * i + j, 0))
    per_tile = lambda w, off: [tile(w, off, j) for j in range(n_tiles)]
    const = lambda shape: pl.BlockSpec(shape, lambda b, i: (0,) * len(shape),
                                       pipeline_mode=pl.Buffered(1))
    slab = lambda a, **kw: pl.BlockSpec((None,) + a.shape[1:], lambda b, i: (layer, 0, 0), **kw)
    return pl.pallas_call(
        functools.partial(_post_kernel, n_tiles=n_tiles, ff_chunk=1024, mod_row=mod_row),
        grid=(bsz, seg // (n_tiles * ROW_TILE)),
        in_specs=[*per_tile(D_ATT, 0), *per_tile(D_RET, ret_off), *per_tile(D_SSD, ret_off),
                  *per_tile(d, 0),
                  slab(mods), const((4, d)),
                  slab(w_out, pipeline_mode=pl.Buffered(1)),
                  slab(w_ff1, pipeline_mode=pl.Buffered(1)),
                  slab(w_ff2, pipeline_mode=pl.Buffered(1))],
        out_specs=pl.BlockSpec((1, n_tiles * ROW_TILE, d), lambda b, i: (b, i, 0)),
        out_shape=jax.ShapeDtypeStruct((bsz, seg, d), F32),
        compiler_params=pltpu.CompilerParams(
            dimension_semantics=("arbitrary", "arbitrary"), vmem_limit_bytes=VMEM_LIMIT),
        name="post",
    )(*([att] * n_tiles), *([ret] * n_tiles), *([ssd] * n_tiles), *([x] * n_tiles),
      mods, ng, w_out, w_ff1, w_ff2)


def _rope_tables(n, n_ctx):
    rows = n // GRID_W
    row = jnp.broadcast_to(jnp.arange(rows)[:, None], (rows, GRID_W)).reshape(n)
    col = jnp.broadcast_to(jnp.arange(GRID_W)[None, :], (rows, GRID_W)).reshape(n)
    half = HEAD_DIM // 2
    inv_freq = ROPE_THETA ** (-jnp.arange(0, half, 2, dtype=F32) / half)
    ang = jnp.stack([row, col], axis=-1).astype(F32)[:, :, None] * inv_freq
    ang = jnp.concatenate([ang, ang], axis=-1).reshape(n, HEAD_DIM)
    ang = jnp.tile(ang, (1, LANES // HEAD_DIM))
    cos = jnp.cos(ang)
    sin = jnp.sin(ang)
    upper = (jnp.arange(LANES) % half) >= half // 2
    sin_p = jnp.where(upper, sin, 0.0)
    sin_m = jnp.where(upper, 0.0, -sin)
    ident = lambda v, a: jnp.concatenate([jnp.full((n_ctx, LANES), v, F32), a], axis=0)
    return ident(1.0, cos), ident(0.0, sin_p), ident(0.0, sin_m)


def kernel(x, c, ctx, c_ctx, w_mod, b_mod, norm_g, w_in, w_out, q_norm_g, k_norm_g, ret_decay_logit,
           ret_gn_g, ret_gn_b, ssd_conv_w, ssd_conv_b, ssd_dt_bias, ssd_a_log, ssd_d, ssd_norm_g,
           w_ff1, w_ff2):
    bsz, n, d = x.shape
    n_ctx = ctx.shape[1]
    depth = w_mod.shape[0]
    assert n % ROW_TILE == 0 and n_ctx % ROW_TILE == 0 and n_ctx % CHUNK == 0 and n % CHUNK == 0
    assert w_in.shape[2] == OFF_DT + 2 * H_SSD
    n_ctx_tiles = n_ctx // ROW_TILE

    n_rows = -(-(bsz + 1) // 8) * 8
    cc = jnp.concatenate([c, c_ctx[None, :], jnp.zeros((n_rows - bsz - 1, d), F32)], axis=0)
    mods = _modulation(cc, w_mod, b_mod)

    cos, sin_p, sin_m = _rope_tables(n, n_ctx)
    n_dt = 2 * H_SSD
    pad_dt = lambda a: jnp.pad(a.reshape(1, n_dt), ((0, 0), (0, D_DT - n_dt)))

    x_ctx, x_lat, lat_off = ctx, x, n_ctx_tiles
    pair = 2 if n % (2 * ROW_TILE) == 0 else 1
    wi = jnp.pad(w_in, ((0, 0), (0, 0), (0, D_IN_PAD - w_in.shape[2]))).astype(BF16)
    for layer in range(depth):
        last = layer == depth - 1
        gqk = jnp.concatenate([jnp.tile(q_norm_g[layer], H_ATT) * (HEAD_DIM ** -0.5 * LOG2_E),
                               jnp.tile(k_norm_g[layer], H_KV)])[None, :]

        q, k, v, ret, z, xbc, dt = _inproj(x_ctx, x_lat, lat_off, n_ctx + n, mods, layer,
                                           norm_g[layer], wi, cos, sin_p, sin_m, gqk, n_ctx_tiles)
        att_lat = _attention(q, k, v, n_ctx, n, n_ctx + n, n_q=pair)
        lg = jnp.repeat(ret_decay_logit[layer], HEAD_DIM, axis=1)
        ret_o = _retention(ret, lg, ret_gn_g[layer][None, :], ret_gn_b[layer][None, :], n_ctx)
        conv_w = jnp.pad(ssd_conv_w[layer], ((0, 8 - SSD_CONV), (0, 0)))
        ssd_o = _ssd(z, xbc, dt, conv_w, ssd_conv_b[layer][None, :], pad_dt(ssd_dt_bias[layer]),
                     pad_dt(ssd_a_log[layer]), jnp.repeat(ssd_d[layer], HEAD_DIM)[None, :],
                     ssd_norm_g[layer][None, :], n_ctx)
        post = functools.partial(_post, mods=mods, layer=layer, ng=norm_g[layer], w_out=w_out,
                                 w_ff1=w_ff1, w_ff2=w_ff2)
        new_lat = post(att_lat, ret_o, ssd_o, x_lat, n_ctx_tiles, mod_row=None, n_tiles=pair)
        if not last:
            att_ctx = _attention(q, k, v, 0, n_ctx, n_ctx, n_q=1)
            x_ctx = post(att_ctx, ret_o, ssd_o, x_ctx, 0, mod_row=bsz, n_tiles=1)
        x_lat = new_lat
    return x_lat
```

```python
import functools

import jax
import jax.numpy as jnp
from jax import lax
from jax.experimental import pallas as pl
from jax.experimental.pallas import tpu as pltpu

F32 = jnp.float32
BF16 = jnp.bfloat16

HEAD_DIM = 64
H_ATT = 6
H_KV = 2
H_RET = 4
H_SSD = 6
SSD_GROUPS = 2
SSD_STATE = 128
SSD_CONV = 5
GRID_W = 64
ROPE_THETA = 10000.0
EPS = 1e-6
LOG2_E = 1.4426950408889634

D_ATT = H_ATT * HEAD_DIM
D_KV = H_KV * HEAD_DIM
D_RET = H_RET * HEAD_DIM
D_SSD = H_SSD * HEAD_DIM
D_BC = SSD_GROUPS * SSD_STATE
D_XBC = D_SSD + 2 * D_BC
LANES = 128
D_DT = LANES
OFF_Q = 0
OFF_K = OFF_Q + D_ATT
OFF_V = OFF_K + D_KV
OFF_RET = OFF_V + D_KV
OFF_Z = OFF_RET + 4 * D_RET
OFF_XBC = OFF_Z + D_SSD
OFF_DT = OFF_XBC + D_XBC
D_IN_PAD = OFF_DT + D_DT

ROW_TILE = 256
CHUNK = 256
VMEM_LIMIT = 56 * 1024 * 1024


def _silu(x):
    return x * jax.nn.sigmoid(x)


def _split2(a):
    hi = a.astype(BF16)
    lo = (a - hi.astype(F32)).astype(BF16)
    return hi, lo


def _dot(a, b):
    return jnp.dot(a, b, preferred_element_type=F32)


def _dot_nt(a, b):
    return lax.dot_general(a, b, (((1,), (1,)), ((), ())), preferred_element_type=F32)


def _dot2_right(a, m):
    hi, lo = _split2(a)
    return _dot(hi, m) + _dot(lo, m)


PART_LANES = 16


def _pack3(a, base):
    hi = a.astype(BF16).astype(F32)
    r = a - hi
    mid = r.astype(BF16).astype(F32)
    out = hi + pltpu.roll(mid, PART_LANES, 1) + pltpu.roll(r - mid, 2 * PART_LANES, 1)
    return pltpu.roll(out, base, 1) if base else out


def _unpack3(p):
    return p + pltpu.roll(p, LANES - PART_LANES, 1) + pltpu.roll(p, LANES - 2 * PART_LANES, 1)


def _rms(x, g):
    ms = jnp.mean(x * x, axis=-1, keepdims=True)
    return x * lax.rsqrt(ms + EPS) * g


def _head_avg_matrix(n):
    r = lax.broadcasted_iota(jnp.int32, (n, n), 0) // HEAD_DIM
    c = lax.broadcasted_iota(jnp.int32, (n, n), 1) // HEAD_DIM
    return jnp.where(r == c, 1.0 / HEAD_DIM, 0.0).astype(BF16)


def _mod_kernel(c_ref, w_ref, b_ref, o_ref):
    sc = _silu(c_ref[...]).astype(BF16)
    o_ref[0] = _dot(sc, w_ref[0].astype(BF16)) + b_ref[0]


def _modulation(cc, w_mod, b_mod):
    depth, d, n = w_mod.shape
    rows = cc.shape[0]
    tn = n // 4
    return pl.pallas_call(
        _mod_kernel,
        grid=(depth, n // tn),
        in_specs=[
            pl.BlockSpec((rows, d), lambda l, j: (0, 0)),
            pl.BlockSpec((1, d, tn), lambda l, j: (l, 0, j)),
            pl.BlockSpec((1, 1, tn), lambda l, j: (l, 0, j)),
        ],
        out_specs=pl.BlockSpec((1, rows, tn), lambda l, j: (l, 0, j)),
        out_shape=jax.ShapeDtypeStruct((depth, rows, n), F32),
        compiler_params=pltpu.CompilerParams(
            dimension_semantics=("arbitrary", "arbitrary"), vmem_limit_bytes=VMEM_LIMIT),
        name="modulation",
    )(cc, w_mod, b_mod.reshape(depth, 1, n))


def _mod_chunk(mod_ref, row, k):
    d = mod_ref.shape[1] // 6
    return mod_ref[pl.ds(row, 1), k * d:(k + 1) * d]


ROT_HALF = HEAD_DIM // 4


def _rope(x, cos, sin_p, sin_m):
    return (x * cos + pltpu.roll(x, ROT_HALF, 1) * sin_p
            + pltpu.roll(x, LANES - ROT_HALF, 1) * sin_m)


def _inproj_kernel(xc_ref, xl_ref, mod_ref, ng_ref, w_ref, cos_ref, sp_ref, sm_ref, gqk_ref,
                   qt_ref, k_ref, v_ref, ret_ref, z_ref, xbc_ref, dt_ref, *, n_ctx_tiles, ctx_row):
    is_ctx = pl.program_id(1) < n_ctx_tiles
    x = jnp.where(is_ctx, xc_ref[0], xl_ref[0])
    mod = functools.partial(_mod_chunk, mod_ref, jnp.where(is_ctx, ctx_row, pl.program_id(0)))
    h = _rms(x, ng_ref[0:1, :])
    h = h * (1.0 + mod(1)) + mod(0)
    hb = h.astype(BF16)
    rope = lambda tile: _rope(tile, cos_ref[...], sp_ref[...], sm_ref[...])
    kscale = HEAD_DIM ** -0.5

    def emit(col, tile):
        sl = lambda off: slice(col - off, col - off + LANES)
        if col < OFF_K:
            qt_ref[0, sl(OFF_Q), :] = rope(tile).T.astype(BF16)
        elif col < OFF_V:
            k_ref[0, :, sl(OFF_K)] = rope(tile).astype(BF16)
        elif col < OFF_RET:
            v_ref[0, :, sl(OFF_V)] = tile.astype(BF16)
        elif col < OFF_RET + D_RET:
            ret_ref[0, :, sl(OFF_RET)] = rope(tile).astype(BF16)
        elif col < OFF_RET + 2 * D_RET:
            ret_ref[0, :, sl(OFF_RET)] = (rope(tile) * kscale).astype(BF16)
        elif col < OFF_Z:
            ret_ref[0, :, sl(OFF_RET)] = tile.astype(BF16)
        elif col < OFF_XBC:
            z_ref[0, :, sl(OFF_Z)] = tile.astype(BF16)
        elif col < OFF_DT:
            xbc_ref[0, :, sl(OFF_XBC)] = tile.astype(BF16)
        else:
            dt_ref[0, :, sl(OFF_DT)] = tile

    group = OFF_V
    for g in range(D_IN_PAD // group):
        acc = _dot(hb, w_ref[:, g * group:(g + 1) * group])
        if g == 0:
            ms = _dot((acc * acc).astype(BF16), _head_avg_matrix(group))
            acc = acc * lax.rsqrt(ms + EPS) * gqk_ref[...]
        for j in range(group // LANES):
            emit(g * group + j * LANES, acc[:, j * LANES:(j + 1) * LANES])


def _token_specs(d, n_ctx_tiles):
    ctx_spec = pl.BlockSpec((1, ROW_TILE, d),
                            lambda b, i: (b, jnp.minimum(i, n_ctx_tiles - 1), 0))
    lat_spec = pl.BlockSpec((1, ROW_TILE, d),
                            lambda b, i: (b, jnp.maximum(i, n_ctx_tiles) - n_ctx_tiles, 0))
    return ctx_spec, lat_spec


def _inproj(x_ctx, x_lat, t, mods, layer, ng, w_in, cos, sin_p, sin_m, gqk, n_ctx_tiles):
    bsz, _, d = x_lat.shape
    nt = t // ROW_TILE
    mod_spec = pl.BlockSpec((None,) + mods.shape[1:], lambda b, i: (layer, 0, 0))

    tok = lambda w: pl.BlockSpec((1, ROW_TILE, w), lambda b, i: (b, i, 0))
    rope_spec = pl.BlockSpec((ROW_TILE, LANES), lambda b, i: (i, 0))
    const = lambda shape: pl.BlockSpec(shape, lambda b, i: (0,) * len(shape))
    widths = (D_KV, D_KV, 4 * D_RET, D_SSD, D_XBC)
    qt_spec = pl.BlockSpec((1, D_ATT, ROW_TILE), lambda b, i: (b, 0, i))
    return pl.pallas_call(
        functools.partial(_inproj_kernel, n_ctx_tiles=n_ctx_tiles, ctx_row=bsz),
        grid=(bsz, nt),
        in_specs=[
            *_token_specs(d, n_ctx_tiles),
            mod_spec,
            const((4, d)),
            pl.BlockSpec((None, d, D_IN_PAD), lambda b, i: (layer, 0, 0),
                         pipeline_mode=pl.Buffered(1)),
            rope_spec, rope_spec, rope_spec,
            const((1, D_ATT + D_KV)),
        ],
        out_specs=[qt_spec] + [tok(w) for w in widths] + [tok(D_DT)],
        out_shape=[jax.ShapeDtypeStruct((bsz, D_ATT, t), BF16)]
        + [jax.ShapeDtypeStruct((bsz, t, w), BF16) for w in widths]
        + [jax.ShapeDtypeStruct((bsz, t, D_DT), F32)],
        compiler_params=pltpu.CompilerParams(
            dimension_semantics=("arbitrary", "arbitrary"), vmem_limit_bytes=VMEM_LIMIT),
        name="inproj",
    )(x_ctx, x_lat, mods, ng, w_in, cos, sin_p, sin_m, gqk)


V_ROWS = 80
KEY_BLOCK = 256
NEG_BIG = -1e30


def _attn_kernel(*refs, n_q, n_blocks):
    q_refs = refs[:n_q]
    k_ref, v_ref, o_ref, km_ref, vt_ref, acc_ref, m_ref, alpha_ref, s_ref, p_ref = refs[n_q:]
    per_tile = LANES // HEAD_DIM
    n_tiles = D_ATT // LANES
    tq = q_refs[0].shape[2]

    @pl.when(pl.program_id(1) == 0)
    def _():
        k = k_ref[0].astype(F32)
        kr = pltpu.roll(k, HEAD_DIM, 1)
        low = lax.broadcasted_iota(jnp.int32, k.shape, 1) < HEAD_DIM
        km_ref[0] = jnp.where(low, k, 0.0).astype(BF16)
        km_ref[1] = jnp.where(low, 0.0, kr).astype(BF16)
        km_ref[2] = jnp.where(low, kr, 0.0).astype(BF16)
        km_ref[3] = jnp.where(low, 0.0, k).astype(BF16)
        vt = v_ref[0].astype(F32).T
        row = lax.broadcasted_iota(jnp.int32, (V_ROWS - HEAD_DIM, vt.shape[1]), 0)
        tail = jnp.where(row == 0, 1.0, 0.0)
        for g in range(H_KV):
            vg = jnp.concatenate([vt[g * HEAD_DIM:(g + 1) * HEAD_DIM], tail], axis=0).astype(BF16)
            for blk in range(n_blocks):
                vt_ref[g, blk] = vg[:, blk * KEY_BLOCK:(blk + 1) * KEY_BLOCK]

    kv_of = lambda h: h // (H_ATT // H_KV)

    def scores(n):
        t, i = divmod(n, n_blocks)
        for j in range(n_tiles):
            ks = jnp.concatenate(
                [km_ref[2 * kv_of(j * per_tile + half) + half, i * KEY_BLOCK:(i + 1) * KEY_BLOCK, :]
                 for half in range(per_tile)], axis=0)
            s_ref[n % 2, j] = _dot(ks, q_refs[t][0, j * LANES:(j + 1) * LANES, :])

    def exponentials(n):
        t = n // n_blocks
        for h in range(H_ATT):
            j, half = divmod(h, per_tile)
            s = s_ref[n % 2, j, half * KEY_BLOCK:(half + 1) * KEY_BLOCK, :]
            m_old = m_ref[t, h]
            m_new = jnp.maximum(m_old, jnp.max(s, axis=0, keepdims=True))
            m_ref[t, h] = m_new
            alpha_ref[n % 2, h] = jnp.exp2(m_old - m_new)
            p_ref[n % 2, h] = jnp.exp2(s - m_new).astype(BF16)

    def weighted_values(n):
        t, i = divmod(n, n_blocks)
        for h in range(H_ATT):
            pv = _dot(vt_ref[kv_of(h), i], p_ref[n % 2, h])
            acc_ref[t, h] = acc_ref[t, h] * alpha_ref[n % 2, h] + pv

    def finish(t):
        for j in range(n_tiles):
            halves = []
            for half in range(per_tile):
                acc = acc_ref[t, j * per_tile + half]
                halves.append(acc[0:HEAD_DIM] * (1.0 / acc[HEAD_DIM:HEAD_DIM + 1]))
            ot = jnp.concatenate(halves, axis=0)
            o_ref[0, t * tq:(t + 1) * tq, j * LANES:(j + 1) * LANES] = ot.T.astype(BF16)

    acc_ref[...] = jnp.zeros_like(acc_ref)
    m_ref[...] = jnp.full(m_ref.shape, NEG_BIG, F32)
    n_items = n_q * n_blocks
    for it in range(n_items + 2):
        if it >= 2:
            weighted_values(it - 2)
            if (it - 1) % n_blocks == 0:
                finish((it - 2) // n_blocks)
        if it < n_items:
            scores(it)
        if 1 <= it <= n_items:
            exponentials(it - 1)


def _attention(qt, k, v, first_row, n_rows, n_keys, n_q):
    bsz = qt.shape[0]
    tq = ROW_TILE
    assert first_row % tq == 0 and n_rows % (n_q * tq) == 0 and n_keys % KEY_BLOCK == 0
    n_blocks = n_keys // KEY_BLOCK
    q_spec = lambda j: pl.BlockSpec((1, D_ATT, tq),
                                    lambda b, i: (b, 0, first_row // tq + n_q * i + j))
    kv_spec = pl.BlockSpec((1, n_keys, D_KV), lambda b, i: (b, 0, 0))
    return pl.pallas_call(
        functools.partial(_attn_kernel, n_q=n_q, n_blocks=n_blocks),
        grid=(bsz, n_rows // (n_q * tq)),
        in_specs=[q_spec(j) for j in range(n_q)] + [kv_spec, kv_spec],
        out_specs=pl.BlockSpec((1, n_q * tq, D_ATT), lambda b, i: (b, i, 0)),
        out_shape=jax.ShapeDtypeStruct((bsz, n_rows, D_ATT), BF16),
        scratch_shapes=[pltpu.VMEM((2 * H_KV, n_keys, D_KV), BF16),
                        pltpu.VMEM((H_KV, n_blocks, V_ROWS, KEY_BLOCK), BF16),
                        pltpu.VMEM((n_q, H_ATT, V_ROWS, tq), F32),
                        pltpu.VMEM((n_q, H_ATT, 1, tq), F32),
                        pltpu.VMEM((2, H_ATT, 1, tq), F32),
                        pltpu.VMEM((2, D_ATT // LANES, (LANES // HEAD_DIM) * KEY_BLOCK, tq), F32),
                        pltpu.VMEM((2, H_ATT, KEY_BLOCK, tq), BF16)],
        compiler_params=pltpu.CompilerParams(
            dimension_semantics=("arbitrary", "arbitrary"), vmem_limit_bytes=VMEM_LIMIT),
        name="attention",
    )(*([qt] * n_q), k, v)


def _log_sigmoid(x):
    return jnp.minimum(x, 0.0) - jnp.log1p(jnp.exp(-jnp.abs(x)))


def _ret_kernel(r_ref, lg_ref, gng_ref, gnb_ref, o_ref, y_ref, sf_ref, sb_ref, w_ref, dm_ref,
                *, n_ctx_chunks):
    c_len = CHUNK
    t = r_ref.shape[1]
    n_chunks = t // c_len
    lg = _log_sigmoid(lg_ref[...])
    lgf = lg[0:1, :]
    lgb = lg[1:2, :]
    dec_f = jnp.exp(c_len * lgf)
    dec_b = jnp.exp(c_len * lgb)

    @pl.when(pl.program_id(0) == 0)
    def _():
        tcol = lax.broadcasted_iota(jnp.int32, (c_len, 1), 0).astype(F32)
        w_ref[0] = jnp.exp((tcol + 1.0) * lgf)
        w_ref[1] = jnp.exp((c_len - tcol) * lgb)
        w_ref[2] = jnp.exp((c_len - 1.0 - tcol) * lgf)
        w_ref[3] = jnp.exp(tcol * lgb)
        ti = lax.broadcasted_iota(jnp.int32, (c_len, c_len), 0)
        si = lax.broadcasted_iota(jnp.int32, (c_len, c_len), 1)
        diff = (ti - si).astype(F32)
        for h in range(H_RET):
            lf = lgf[:, h * HEAD_DIM:h * HEAD_DIM + 1]
            lb = lgb[:, h * HEAD_DIM:h * HEAD_DIM + 1]
            dm_ref[h] = jnp.exp(jnp.where(diff >= 0, diff * lf, -diff * lb))

    lane = lax.broadcasted_iota(jnp.int32, (c_len, D_RET), 1)
    srow = lax.broadcasted_iota(jnp.int32, (D_RET, D_RET), 0) // HEAD_DIM
    scol = lax.broadcasted_iota(jnp.int32, (D_RET, D_RET), 1) // HEAD_DIM
    smask = srow == scol
    avg = _head_avg_matrix(D_RET)

    def chunk_rows(c):
        return pl.ds(pl.multiple_of(c * c_len, c_len), c_len)

    def state_delta(k, v, wk):
        kw = (k.astype(F32) * wk).T.astype(BF16)
        return jnp.where(smask, _dot(kw, v), 0.0)

    sf_ref[...] = jnp.zeros_like(sf_ref)
    sb_ref[...] = jnp.zeros_like(sb_ref)

    def fwd(c, carry):
        rows = chunk_rows(c)
        q = r_ref[0, rows, 0:D_RET]
        k = r_ref[0, rows, D_RET:2 * D_RET]
        v = r_ref[0, rows, 2 * D_RET:3 * D_RET]
        zero = jnp.zeros_like(q)
        y = w_ref[0] * _dot(q, sf_ref[...].astype(BF16))
        for h in range(H_RET):
            hm = (lane >= h * HEAD_DIM) & (lane < (h + 1) * HEAD_DIM)
            s = _dot_nt(jnp.where(hm, q, zero), k) * dm_ref[h]
            y = y + _dot(s.astype(BF16), jnp.where(hm, v, zero))
        y_ref[rows, :] = y
        sf_ref[...] = dec_f * sf_ref[...] + state_delta(k, v, w_ref[2])
        return carry

    lax.fori_loop(0, n_chunks, fwd, 0, unroll=3)

    def bwd(c):
        rows = chunk_rows(c)
        q = r_ref[0, rows, 0:D_RET]
        k = r_ref[0, rows, D_RET:2 * D_RET]
        v = r_ref[0, rows, 2 * D_RET:3 * D_RET]
        g = r_ref[0, rows, 3 * D_RET:4 * D_RET].astype(F32)
        y = y_ref[rows, :] + w_ref[1] * _dot(q, sb_ref[...].astype(BF16))
        sb_ref[...] = dec_b * sb_ref[...] + state_delta(k, v, w_ref[3])
        mu = _dot2_right(y, avg)
        d = y - mu
        var = _dot((d * d).astype(BF16), avg)
        yn = d * lax.rsqrt(var + EPS) * gng_ref[...] + gnb_ref[...]
        o_ref[0, rows, :] = (yn * _silu(g)).astype(BF16)

    def bwd_ctx(i, carry):
        bwd(n_ctx_chunks - 1 - i)
        return carry

    def bwd_lat(i, carry):
        bwd(n_chunks - 1 - i)
        return carry

    lax.fori_loop(0, n_ctx_chunks, bwd_ctx, 0)
    lax.fori_loop(0, n_chunks - n_ctx_chunks, bwd_lat, 0, unroll=4)


def _retention(ret, lg, gng, gnb, n_ctx):
    bsz, t, _ = ret.shape
    const = lambda shape: pl.BlockSpec(shape, lambda b: (0,) * len(shape))
    return pl.pallas_call(
        functools.partial(_ret_kernel, n_ctx_chunks=n_ctx // CHUNK),
        grid=(bsz,),
        in_specs=[
            pl.BlockSpec((1, t, 4 * D_RET), lambda b: (b, 0, 0)),
            const((2, D_RET)), const((1, D_RET)), const((1, D_RET)),
        ],
        out_specs=pl.BlockSpec((1, t, D_RET), lambda b: (b, 0, 0)),
        out_shape=jax.ShapeDtypeStruct((bsz, t, D_RET), BF16),
        scratch_shapes=[pltpu.VMEM((t, D_RET), F32), pltpu.VMEM((D_RET, D_RET), F32),
                        pltpu.VMEM((D_RET, D_RET), F32), pltpu.VMEM((4, CHUNK, D_RET), F32),
                        pltpu.VMEM((H_RET, CHUNK, CHUNK), F32)],
        compiler_params=pltpu.CompilerParams(
            dimension_semantics=("arbitrary",), vmem_limit_bytes=VMEM_LIMIT),
        name="retention",
    )(ret, lg, gng, gnb)


def _softplus(x):
    return jnp.maximum(x, 0.0) + jnp.log1p(jnp.exp(-jnp.abs(x)))


def _ssd_kernel(z_ref, xbc_ref, dt_ref, cw_ref, cb_ref, dtb_ref, alog_ref, dskip_ref, ng_ref,
                o_ref, xc_ref, y_ref, eb_ref, dsb_ref, sf_ref, sb_ref, ce_all, g_all, sc_all,
                *, n_ctx):
    c_len = CHUNK
    t = xbc_ref.shape[1]
    n_chunks = t // c_len
    n_ctx_chunks = n_ctx // c_len
    edge = 16
    half = SSD_CONV // 2

    taps = [k for k in range(-half, half + 1) if k]
    ri = lax.broadcasted_iota(jnp.int32, (c_len, c_len), 0)
    ci = lax.broadcasted_iota(jnp.int32, (c_len, c_len), 1)
    shift = {k: jnp.where(ci == ri + k, 1.0, 0.0).astype(BF16) for k in taps}
    er = lax.broadcasted_iota(jnp.int32, (edge, edge), 0)
    ec = lax.broadcasted_iota(jnp.int32, (edge, edge), 1)
    prev_rows = {k: jnp.where(ec == er + k + edge, 1.0, 0.0).astype(BF16) for k in taps if k < 0}
    next_rows = {k: jnp.where(ec == er + k - edge, 1.0, 0.0).astype(BF16) for k in taps if k > 0}
    tap_w = lambda k, lanes: cw_ref[k + half:k + half + 1, lanes]
    lane_groups = [slice(lo, min(lo + 2 * LANES, D_XBC)) for lo in range(0, D_XBC, 2 * LANES)]
    for c in range(n_chunks):
        r0 = c * c_len
        has_prev = c not in (0, n_ctx_chunks)
        has_next = c not in (n_ctx_chunks - 1, n_chunks - 1)
        for lanes in lane_groups:
            xb = xbc_ref[0, r0:r0 + c_len, lanes]
            acc = cb_ref[:, lanes] + tap_w(0, lanes) * xb.astype(F32)
            for k in taps:
                acc = acc + tap_w(k, lanes) * _dot(shift[k], xb)
            head, tail = acc[0:edge], acc[c_len - edge:c_len]
            if has_prev:
                p = xbc_ref[0, r0 - edge:r0, lanes]
                for k in prev_rows:
                    head = head + tap_w(k, lanes) * _dot(prev_rows[k], p)
            if has_next:
                nx = xbc_ref[0, r0 + c_len:r0 + c_len + edge, lanes]
                for k in next_rows:
                    tail = tail + tap_w(k, lanes) * _dot(next_rows[k], nx)
            xc_ref[r0:r0 + edge, lanes] = _silu(head).astype(BF16)
            xc_ref[r0 + edge:r0 + c_len - edge, lanes] = _silu(acc[edge:c_len - edge]).astype(BF16)
            xc_ref[r0 + c_len - edge:r0 + c_len, lanes] = _silu(tail).astype(BF16)

    nd = 2 * H_SSD
    lane_dt = lax.broadcasted_iota(jnp.int32, (1, D_DT), 1)
    live = lane_dt < nd
    a_vec = jnp.where(live, -jnp.exp(alog_ref[...]), 0.0)
    is_f = lane_dt < H_SSD
    ti = lax.broadcasted_iota(jnp.int32, (c_len, c_len), 0)
    si = lax.broadcasted_iota(jnp.int32, (c_len, c_len), 1)
    causal = si <= ti
    tri_l = jnp.where(causal, 1.0, 0.0).astype(BF16)
    tri_u = jnp.where(si >= ti, 1.0, 0.0).astype(BF16)
    n_exp = 2 * D_SSD
    er = lax.broadcasted_iota(jnp.int32, (D_DT, 2 * n_exp), 0)
    ej = lax.broadcasted_iota(jnp.int32, (D_DT, 2 * n_exp), 1)
    rel = er - jnp.where(ej >= n_exp, 3 * PART_LANES, 0)
    col = jnp.where(ej >= n_exp, ej - n_exp, ej) // HEAD_DIM
    expand = jnp.where((rel == col) | (rel == col + PART_LANES) | (rel == col + 2 * PART_LANES),
                       1.0, 0.0).astype(BF16)
    lane_w = lax.broadcasted_iota(jnp.int32, (c_len, 2 * LANES), 1)
    lane_c = lax.broadcasted_iota(jnp.int32, (c_len, D_BC), 1)
    srow = lax.broadcasted_iota(jnp.int32, (D_BC, D_SSD), 0) // SSD_STATE
    scol = lax.broadcasted_iota(jnp.int32, (D_BC, D_SSD), 1) // (D_SSD // SSD_GROUPS)
    smask = srow == scol
    heads_per_group = H_SSD // SSD_GROUPS

    def chunk_rows(c):
        return pl.ds(pl.multiple_of(c * c_len, c_len), c_len)

    sf_ref[...] = jnp.zeros_like(sf_ref)
    sb_ref[...] = jnp.zeros_like(sb_ref)

    def fwd_chunk(c, slot):
        ce_ref, g_ref, sc_ref = ce_all.at[slot], g_all.at[slot], sc_all.at[slot]
        rows = chunk_rows(c)
        xs = xc_ref[rows, 0:D_SSD]
        bm = xc_ref[rows, D_SSD:D_SSD + D_BC]
        cm = xc_ref[rows, D_SSD + D_BC:D_XBC]
        dt = _softplus(dt_ref[0, rows, :] + dtb_ref[...])
        la = _pack3(dt * a_vec, 0).astype(BF16)
        cum = jnp.where(is_f, _unpack3(_dot(tri_l, la)), _unpack3(_dot(tri_u, la)))
        cum = jnp.where(live, cum, 0.0)
        edge = jnp.where(is_f, cum[c_len - 1:c_len, :], cum[0:1, :])
        wk = jnp.where(live, jnp.exp(edge - cum) * dt, 0.0)
        packed = (_pack3(cum, 0) + _pack3(wk, 3 * PART_LANES)).astype(BF16)
        ce_ref[...] = _dot(packed, expand)
        key_t = (cum - jnp.log(dt)).T

        e_f = jnp.exp(ce_ref[:, 0:D_SSD])
        eb_ref[rows, :] = jnp.exp(ce_ref[:, D_SSD:n_exp])
        y_ref[rows, :] = e_f * _dot(cm, sf_ref[...].astype(BF16))
        zc = jnp.zeros_like(cm)
        for g in range(SSD_GROUPS):
            gm = (lane_c >= g * SSD_STATE) & (lane_c < (g + 1) * SSD_STATE)
            g_ref[...] = _dot_nt(jnp.where(gm, cm, zc), bm)
            win = slice(g * LANES, (g + 2) * LANES)
            xw = xs[:, win]
            zw = jnp.zeros_like(xw)
            yg = None
            for hh in range(heads_per_group):
                h = g * heads_per_group + hh
                d_f = cum[:, h:h + 1] - key_t[h:h + 1, :]
                d_b = cum[:, H_SSD + h:H_SSD + h + 1] - key_t[H_SSD + h:H_SSD + h + 1, :]
                sc_ref[hh] = (jnp.exp(jnp.where(causal, d_f, d_b)) * g_ref[...]).astype(BF16)
                lo = h * HEAD_DIM - g * LANES
                hm = (lane_w >= lo) & (lane_w < lo + HEAD_DIM)
                part = _dot(sc_ref[hh], jnp.where(hm, xw, zw))
                yg = part if yg is None else yg + part
            y_ref[rows, win] = y_ref[rows, win] + yg

        xs_f = xs.astype(F32)
        bm_t = bm.astype(F32).T.astype(BF16)
        ds_f = _dot(bm_t, (xs_f * ce_ref[:, n_exp:n_exp + D_SSD]).astype(BF16))
        ds_b = _dot(bm_t, (xs_f * ce_ref[:, n_exp + D_SSD:2 * n_exp]).astype(BF16))
        sf_ref[...] = e_f[c_len - 1:c_len, :] * sf_ref[...] + jnp.where(smask, ds_f, 0.0)
        dsb_ref[c] = jnp.where(smask, ds_b, 0.0)

    unroll = ce_all.shape[0]

    def fwd(i, carry):
        for k in range(unroll):
            fwd_chunk(i * unroll + k, k)
        return carry

    lax.fori_loop(0, n_chunks // unroll, fwd, 0)

    def bwd(c):
        rows = chunk_rows(c)
        xs_f = xc_ref[rows, 0:D_SSD].astype(F32)
        cm = xc_ref[rows, D_SSD + D_BC:D_XBC]
        e_b = eb_ref[rows, :]
        y = y_ref[rows, :] + e_b * _dot(cm, sb_ref[...].astype(BF16))
        sb_ref[...] = e_b[0:1, :] * sb_ref[...] + dsb_ref[c]
        y = y + dskip_ref[...] * xs_f
        u = y * _silu(z_ref[0, rows, :].astype(F32))
        o_ref[0, rows, :] = _rms(u, ng_ref[...]).astype(BF16)

    def bwd_ctx(i, carry):
        bwd(n_ctx_chunks - 1 - i)
        return carry

    def bwd_lat(i, carry):
        bwd(n_chunks - 1 - i)
        return carry

    lax.fori_loop(0, n_ctx_chunks, bwd_ctx, 0)
    lax.fori_loop(0, n_chunks - n_ctx_chunks, bwd_lat, 0, unroll=2)


def _ssd(z, xbc, dt, conv_w, conv_b, dt_bias, a_log, d_skip, norm_g, n_ctx):
    bsz, t, _ = xbc.shape
    n_chunks = t // CHUNK
    unroll = next(u for u in (3, 2, 1) if n_chunks % u == 0)
    const = lambda shape: pl.BlockSpec(shape, lambda b: (0,) * len(shape))
    seq = lambda w: pl.BlockSpec((1, t, w), lambda b: (b, 0, 0))
    return pl.pallas_call(
        functools.partial(_ssd_kernel, n_ctx=n_ctx),
        grid=(bsz,),
        in_specs=[seq(D_SSD), seq(D_XBC), seq(D_DT),
                  const((8, D_XBC)), const((1, D_XBC)), const((1, D_DT)), const((1, D_DT)),
                  const((1, D_SSD)), const((1, D_SSD))],
        out_specs=seq(D_SSD),
        out_shape=jax.ShapeDtypeStruct((bsz, t, D_SSD), BF16),
        scratch_shapes=[
            pltpu.VMEM((t, D_XBC), BF16),
            pltpu.VMEM((t, D_SSD), F32),
            pltpu.VMEM((t, D_SSD), F32),
            pltpu.VMEM((n_chunks, D_BC, D_SSD), F32),
            pltpu.VMEM((D_BC, D_SSD), F32),
            pltpu.VMEM((D_BC, D_SSD), F32),
            pltpu.VMEM((unroll, CHUNK, 4 * D_SSD), F32),
            pltpu.VMEM((unroll, CHUNK, CHUNK), F32),
            pltpu.VMEM((unroll, H_SSD // SSD_GROUPS, CHUNK, CHUNK), BF16),
        ],
        compiler_params=pltpu.CompilerParams(
            dimension_semantics=("arbitrary",), vmem_limit_bytes=VMEM_LIMIT),
        name="ssd",
    )(z, xbc, dt, conv_w, conv_b, dt_bias, a_log, d_skip, norm_g)


def _post_kernel(*refs, n_tiles, ff_chunk, mod_row):
    att_refs, ret_refs, ssd_refs, x_refs = (refs[k * n_tiles:(k + 1) * n_tiles] for k in range(4))
    mod_ref, ng_ref, wo_ref, w1_ref, w2_ref, o_ref = refs[4 * n_tiles:]
    row = pl.program_id(0) if mod_row is None else mod_row
    mod = functools.partial(_mod_chunk, mod_ref, row)
    rows = x_refs[0].shape[1]
    n_chunk = w1_ref.shape[1] // ff_chunk

    def project(t):
        mix = jnp.concatenate([att_refs[t][0], ret_refs[t][0], ssd_refs[t][0]], axis=-1)
        return _dot(mix, wo_ref[...].astype(BF16))

    def mlp_input(t, o):
        x1 = x_refs[t][0] + mod(2) * _rms(o, ng_ref[1:2, :])
        h2 = _rms(x1, ng_ref[2:3, :]) * (1.0 + mod(4)) + mod(3)
        return x1, h2.astype(BF16)

    def mlp_chunk(h2, acc, j):
        sl = slice(j * ff_chunk, (j + 1) * ff_chunk)
        a = jnp.maximum(_dot(h2, w1_ref[:, sl].astype(BF16)), 0.0)
        part = _dot((a * a).astype(BF16), w2_ref[sl, :].astype(BF16))
        return part if acc is None else acc + part

    def finish(t, x1, acc):
        o_ref[0, t * rows:(t + 1) * rows, :] = x1 + mod(5) * _rms(acc, ng_ref[3:4, :])

    proj = [project(t) for t in range(n_tiles)]
    ready = mlp_input(0, proj[0])
    pending = None
    for t in range(n_tiles):
        x1, h2 = ready
        acc = mlp_chunk(h2, None, 0)
        if t + 1 < n_tiles:
            ready = mlp_input(t + 1, proj[t + 1])
        if pending is not None:
            finish(*pending)
        for j in range(1, n_chunk):
            acc = mlp_chunk(h2, acc, j)
        pending = (t, x1, acc)
    finish(*pending)


def _post(att, ret, ssd, x, ret_off, mods, layer, ng, w_out, w_ff1, w_ff2, mod_row, n_tiles):
    bsz, seg, d = x.shape
    assert seg % (n_tiles * ROW_TILE) == 0 and att.shape[1] == seg
    tile = lambda w, off, j: pl.BlockSpec((1, ROW_TILE, w),
                                          lambda b, i: (b, off + n_tiles * i + j, 0))
    per_tile = lambda w, off: [tile(w, off, j) for j in range(n_tiles)]
    const = lambda shape: pl.BlockSpec(shape, lambda b, i: (0,) * len(shape),
                                       pipeline_mode=pl.Buffered(1))
    slab = lambda a, **kw: pl.BlockSpec((None,) + a.shape[1:], lambda b, i: (layer, 0, 0), **kw)
    return pl.pallas_call(
        functools.partial(_post_kernel, n_tiles=n_tiles, ff_chunk=1024, mod_row=mod_row),
        grid=(bsz, seg // (n_tiles * ROW_TILE)),
        in_specs=[*per_tile(D_ATT, 0), *per_tile(D_RET, ret_off), *per_tile(D_SSD, ret_off),
                  *per_tile(d, 0),
                  slab(mods), const((4, d)),
                  slab(w_out, pipeline_mode=pl.Buffered(1)),
                  slab(w_ff1, pipeline_mode=pl.Buffered(1)),
                  slab(w_ff2, pipeline_mode=pl.Buffered(1))],
        out_specs=pl.BlockSpec((1, n_tiles * ROW_TILE, d), lambda b, i: (b, i, 0)),
        out_shape=jax.ShapeDtypeStruct((bsz, seg, d), F32),
        compiler_params=pltpu.CompilerParams(
            dimension_semantics=("arbitrary", "arbitrary"), vmem_limit_bytes=VMEM_LIMIT),
        name="post",
    )(*([att] * n_tiles), *([ret] * n_tiles), *([ssd] * n_tiles), *([x] * n_tiles),
      mods, ng, w_out, w_ff1, w_ff2)


def _rope_tables(n, n_ctx):
    rows = n // GRID_W
    row = jnp.broadcast_to(jnp.arange(rows)[:, None], (rows, GRID_W)).reshape(n)
    col = jnp.broadcast_to(jnp.arange(GRID_W)[None, :], (rows, GRID_W)).reshape(n)
    half = HEAD_DIM // 2
    inv_freq = ROPE_THETA ** (-jnp.arange(0, half, 2, dtype=F32) / half)
    ang = jnp.stack([row, col], axis=-1).astype(F32)[:, :, None] * inv_freq
    ang = jnp.concatenate([ang, ang], axis=-1).reshape(n, HEAD_DIM)
    ang = jnp.tile(ang, (1, LANES // HEAD_DIM))
    cos = jnp.cos(ang)
    sin = jnp.sin(ang)
    upper = (jnp.arange(LANES) % half) >= half // 2
    sin_p = jnp.where(upper, sin, 0.0)
    sin_m = jnp.where(upper, 0.0, -sin)
    ident = lambda v, a: jnp.concatenate([jnp.full((n_ctx, LANES), v, F32), a], axis=0)
    return ident(1.0, cos), ident(0.0, sin_p), ident(0.0, sin_m)


def kernel(x, c, ctx, c_ctx, w_mod, b_mod, norm_g, w_in, w_out, q_norm_g, k_norm_g, ret_decay_logit,
           ret_gn_g, ret_gn_b, ssd_conv_w, ssd_conv_b, ssd_dt_bias, ssd_a_log, ssd_d, ssd_norm_g,
           w_ff1, w_ff2):
    bsz, n, d = x.shape
    n_ctx = ctx.shape[1]
    depth = w_mod.shape[0]
    assert n % ROW_TILE == 0 and n_ctx % ROW_TILE == 0 and n_ctx % CHUNK == 0 and n % CHUNK == 0
    assert w_in.shape[2] == OFF_DT + 2 * H_SSD
    n_ctx_tiles = n_ctx // ROW_TILE

    n_rows = -(-(bsz + 1) // 8) * 8
    cc = jnp.concatenate([c, c_ctx[None, :], jnp.zeros((n_rows - bsz - 1, d), F32)], axis=0)
    mods = _modulation(cc, w_mod, b_mod)

    cos, sin_p, sin_m = _rope_tables(n, n_ctx)
    n_dt = 2 * H_SSD
    pad_dt = lambda a: jnp.pad(a.reshape(1, n_dt), ((0, 0), (0, D_DT - n_dt)))

    x_ctx, x_lat = ctx, x
    pair = 2 if n % (2 * ROW_TILE) == 0 else 1
    wi = jnp.pad(w_in, ((0, 0), (0, 0), (0, D_IN_PAD - w_in.shape[2]))).astype(BF16)
    for layer in range(depth):
        last = layer == depth - 1
        gqk = jnp.concatenate([jnp.tile(q_norm_g[layer], H_ATT) * (HEAD_DIM ** -0.5 * LOG2_E),
                               jnp.tile(k_norm_g[layer], H_KV)])[None, :]

        q, k, v, ret, z, xbc, dt = _inproj(x_ctx, x_lat, n_ctx + n, mods, layer,
                                           norm_g[layer], wi, cos, sin_p, sin_m, gqk, n_ctx_tiles)
        att_lat = _attention(q, k, v, n_ctx, n, n_ctx + n, n_q=pair)
        lg = jnp.repeat(ret_decay_logit[layer], HEAD_DIM, axis=1)
        ret_o = _retention(ret, lg, ret_gn_g[layer][None, :], ret_gn_b[layer][None, :], n_ctx)
        conv_w = jnp.pad(ssd_conv_w[layer], ((0, 8 - SSD_CONV), (0, 0)))
        ssd_o = _ssd(z, xbc, dt, conv_w, ssd_conv_b[layer][None, :], pad_dt(ssd_dt_bias[layer]),
                     pad_dt(ssd_a_log[layer]), jnp.repeat(ssd_d[layer], HEAD_DIM)[None, :],
                     ssd_norm_g[layer][None, :], n_ctx)
        post = functools.partial(_post, mods=mods, layer=layer, ng=norm_g[layer], w_out=w_out,
                                 w_ff1=w_ff1, w_ff2=w_ff2)
        new_lat = post(att_lat, ret_o, ssd_o, x_lat, n_ctx_tiles, mod_row=None, n_tiles=pair)
        if not last:
            att_ctx = _attention(q, k, v, 0, n_ctx, n_ctx, n_q=1)
            x_ctx = post(att_ctx, ret_o, ssd_o, x_ctx, 0, mod_row=bsz, n_tiles=1)
        x_lat = new_lat
    return x_lat
```

```python
import functools

import jax
import jax.numpy as jnp
from jax import lax
from jax.experimental import pallas as pl
from jax.experimental.pallas import tpu as pltpu

F32 = jnp.float32
BF16 = jnp.bfloat16

HEAD_DIM = 64
H_ATT = 6
H_KV = 2
H_RET = 4
H_SSD = 6
SSD_GROUPS = 2
SSD_STATE = 128
SSD_CONV = 5
GRID_W = 64
ROPE_THETA = 10000.0
EPS = 1e-6
LOG2_E = 1.4426950408889634

D_ATT = H_ATT * HEAD_DIM
D_KV = H_KV * HEAD_DIM
D_RET = H_RET * HEAD_DIM
D_SSD = H_SSD * HEAD_DIM
D_BC = SSD_GROUPS * SSD_STATE
D_XBC = D_SSD + 2 * D_BC
LANES = 128
D_DT = LANES
OFF_Q = 0
OFF_K = OFF_Q + D_ATT
OFF_V = OFF_K + D_KV
OFF_RET = OFF_V + D_KV
OFF_Z = OFF_RET + 4 * D_RET
OFF_XBC = OFF_Z + D_SSD
OFF_DT = OFF_XBC + D_XBC
D_IN_PAD = OFF_DT + D_DT

ROW_TILE = 256
CHUNK = 256
VMEM_LIMIT = 56 * 1024 * 1024


def _silu(x):
    return x * jax.nn.sigmoid(x)


def _split2(a):
    hi = a.astype(BF16)
    lo = (a - hi.astype(F32)).astype(BF16)
    return hi, lo


def _dot(a, b):
    return jnp.dot(a, b, preferred_element_type=F32)


def _dot_nt(a, b):
    return lax.dot_general(a, b, (((1,), (1,)), ((), ())), preferred_element_type=F32)


def _dot2_right(a, m):
    hi, lo = _split2(a)
    return _dot(hi, m) + _dot(lo, m)


PART_LANES = 16


def _pack3(a, base):
    hi = a.astype(BF16).astype(F32)
    r = a - hi
    mid = r.astype(BF16).astype(F32)
    out = hi + pltpu.roll(mid, PART_LANES, 1) + pltpu.roll(r - mid, 2 * PART_LANES, 1)
    return pltpu.roll(out, base, 1) if base else out


def _unpack3(p):
    return p + pltpu.roll(p, LANES - PART_LANES, 1) + pltpu.roll(p, LANES - 2 * PART_LANES, 1)


def _rms(x, g):
    ms = jnp.mean(x * x, axis=-1, keepdims=True)
    return x * lax.rsqrt(ms + EPS) * g


def _head_avg_matrix(n):
    r = lax.broadcasted_iota(jnp.int32, (n, n), 0) // HEAD_DIM
    c = lax.broadcasted_iota(jnp.int32, (n, n), 1) // HEAD_DIM
    return jnp.where(r == c, 1.0 / HEAD_DIM, 0.0).astype(BF16)


def _mod_kernel(c_ref, w_ref, b_ref, o_ref):
    sc = _silu(c_ref[...]).astype(BF16)
    o_ref[0] = _dot(sc, w_ref[0].astype(BF16)) + b_ref[0]


def _modulation(cc, w_mod, b_mod):
    depth, d, n = w_mod.shape
    rows = cc.shape[0]
    tn = n // 4
    return pl.pallas_call(
        _mod_kernel,
        grid=(depth, n // tn),
        in_specs=[
            pl.BlockSpec((rows, d), lambda l, j: (0, 0)),
            pl.BlockSpec((1, d, tn), lambda l, j: (l, 0, j)),
            pl.BlockSpec((1, 1, tn), lambda l, j: (l, 0, j)),
        ],
        out_specs=pl.BlockSpec((1, rows, tn), lambda l, j: (l, 0, j)),
        out_shape=jax.ShapeDtypeStruct((depth, rows, n), F32),
        compiler_params=pltpu.CompilerParams(
            dimension_semantics=("arbitrary", "arbitrary"), vmem_limit_bytes=VMEM_LIMIT),
        name="modulation",
    )(cc, w_mod, b_mod.reshape(depth, 1, n))


def _mod_chunk(mod_ref, row, k):
    d = mod_ref.shape[1] // 6
    return mod_ref[pl.ds(row, 1), k * d:(k + 1) * d]


ROT_HALF = HEAD_DIM // 4


def _rope(x, cos, sin_p, sin_m):
    return (x * cos + pltpu.roll(x, ROT_HALF, 1) * sin_p
            + pltpu.roll(x, LANES - ROT_HALF, 1) * sin_m)


def _inproj_kernel(xc_ref, xl_ref, mod_ref, ng_ref, w_ref, cos_ref, sp_ref, sm_ref, gqk_ref,
                   qt_ref, k_ref, v_ref, ret_ref, z_ref, xbc_ref, dt_ref, *, n_ctx_tiles, ctx_row):
    is_ctx = pl.program_id(1) < n_ctx_tiles
    x = jnp.where(is_ctx, xc_ref[0], xl_ref[0])
    mod = functools.partial(_mod_chunk, mod_ref, jnp.where(is_ctx, ctx_row, pl.program_id(0)))
    h = _rms(x, ng_ref[0:1, :])
    h = h * (1.0 + mod(1)) + mod(0)
    hb = h.astype(BF16)
    rope = lambda tile: _rope(tile, cos_ref[...], sp_ref[...], sm_ref[...])
    kscale = HEAD_DIM ** -0.5

    def emit(col, tile):
        sl = lambda off: slice(col - off, col - off + LANES)
        if col < OFF_K:
            qt_ref[0, sl(OFF_Q), :] = rope(tile).T.astype(BF16)
        elif col < OFF_V:
            k_ref[0, :, sl(OFF_K)] = rope(tile).astype(BF16)
        elif col < OFF_RET:
            v_ref[0, :, sl(OFF_V)] = tile.astype(BF16)
        elif col < OFF_RET + D_RET:
            ret_ref[0, :, sl(OFF_RET)] = rope(tile).astype(BF16)
        elif col < OFF_RET + 2 * D_RET:
            ret_ref[0, :, sl(OFF_RET)] = (rope(tile) * kscale).astype(BF16)
        elif col < OFF_Z:
            ret_ref[0, :, sl(OFF_RET)] = tile.astype(BF16)
        elif col < OFF_XBC:
            z_ref[0, :, sl(OFF_Z)] = tile.astype(BF16)
        elif col < OFF_DT:
            xbc_ref[0, :, sl(OFF_XBC)] = tile.astype(BF16)
        else:
            dt_ref[0, :, sl(OFF_DT)] = tile

    group = OFF_V
    for g in range(D_IN_PAD // group):
        acc = _dot(hb, w_ref[:, g * group:(g + 1) * group])
        if g == 0:
            ms = _dot((acc * acc).astype(BF16), _head_avg_matrix(group))
            acc = acc * lax.rsqrt(ms + EPS) * gqk_ref[...]
        for j in range(group // LANES):
            emit(g * group + j * LANES, acc[:, j * LANES:(j + 1) * LANES])


def _token_specs(d, n_ctx_tiles):
    ctx_spec = pl.BlockSpec((1, ROW_TILE, d),
                            lambda b, i: (b, jnp.minimum(i, n_ctx_tiles - 1), 0))
    lat_spec = pl.BlockSpec((1, ROW_TILE, d),
                            lambda b, i: (b, jnp.maximum(i, n_ctx_tiles) - n_ctx_tiles, 0))
    return ctx_spec, lat_spec


def _inproj(x_ctx, x_lat, t, mods, layer, ng, w_in, cos, sin_p, sin_m, gqk, n_ctx_tiles):
    bsz, _, d = x_lat.shape
    nt = t // ROW_TILE
    mod_spec = pl.BlockSpec((None,) + mods.shape[1:], lambda b, i: (layer, 0, 0))

    tok = lambda w: pl.BlockSpec((1, ROW_TILE, w), lambda b, i: (b, i, 0))
    rope_spec = pl.BlockSpec((ROW_TILE, LANES), lambda b, i: (i, 0))
    const = lambda shape: pl.BlockSpec(shape, lambda b, i: (0,) * len(shape))
    widths = (D_KV, D_KV, 4 * D_RET, D_SSD, D_XBC)
    qt_spec = pl.BlockSpec((1, D_ATT, ROW_TILE), lambda b, i: (b, 0, i))
    return pl.pallas_call(
        functools.partial(_inproj_kernel, n_ctx_tiles=n_ctx_tiles, ctx_row=bsz),
        grid=(bsz, nt),
        in_specs=[
            *_token_specs(d, n_ctx_tiles),
            mod_spec,
            const((4, d)),
            pl.BlockSpec((None, d, D_IN_PAD), lambda b, i: (layer, 0, 0),
                         pipeline_mode=pl.Buffered(1)),
            rope_spec, rope_spec, rope_spec,
            const((1, D_ATT + D_KV)),
        ],
        out_specs=[qt_spec] + [tok(w) for w in widths] + [tok(D_DT)],
        out_shape=[jax.ShapeDtypeStruct((bsz, D_ATT, t), BF16)]
        + [jax.ShapeDtypeStruct((bsz, t, w), BF16) for w in widths]
        + [jax.ShapeDtypeStruct((bsz, t, D_DT), F32)],
        compiler_params=pltpu.CompilerParams(
            dimension_semantics=("arbitrary", "arbitrary"), vmem_limit_bytes=VMEM_LIMIT),
        name="inproj",
    )(x_ctx, x_lat, mods, ng, w_in, cos, sin_p, sin_m, gqk)


V_ROWS = 80
KEY_BLOCK = 256
NEG_BIG = -1e30


def _attn_kernel(*refs, n_q, n_blocks):
    q_refs = refs[:n_q]
    k_ref, v_ref, o_ref, km_ref, vt_ref, acc_ref, m_ref, alpha_ref, s_ref, p_ref = refs[n_q:]
    per_tile = LANES // HEAD_DIM
    n_tiles = D_ATT // LANES
    tq = q_refs[0].shape[2]

    @pl.when(pl.program_id(1) == 0)
    def _():
        k = k_ref[0].astype(F32)
        kr = pltpu.roll(k, HEAD_DIM, 1)
        low = lax.broadcasted_iota(jnp.int32, k.shape, 1) < HEAD_DIM
        km_ref[0] = jnp.where(low, k, 0.0).astype(BF16)
        km_ref[1] = jnp.where(low, 0.0, kr).astype(BF16)
        km_ref[2] = jnp.where(low, kr, 0.0).astype(BF16)
        km_ref[3] = jnp.where(low, 0.0, k).astype(BF16)
        vt = v_ref[0].astype(F32).T
        row = lax.broadcasted_iota(jnp.int32, (V_ROWS - HEAD_DIM, vt.shape[1]), 0)
        tail = jnp.where(row == 0, 1.0, 0.0)
        for g in range(H_KV):
            vg = jnp.concatenate([vt[g * HEAD_DIM:(g + 1) * HEAD_DIM], tail], axis=0).astype(BF16)
            for blk in range(n_blocks):
                vt_ref[g, blk] = vg[:, blk * KEY_BLOCK:(blk + 1) * KEY_BLOCK]

    kv_of = lambda h: h // (H_ATT // H_KV)

    def scores(n):
        t, i = divmod(n, n_blocks)
        for j in range(n_tiles):
            ks = jnp.concatenate(
                [km_ref[2 * kv_of(j * per_tile + half) + half, i * KEY_BLOCK:(i + 1) * KEY_BLOCK, :]
                 for half in range(per_tile)], axis=0)
            s_ref[n % 2, j] = _dot(ks, q_refs[t][0, j * LANES:(j + 1) * LANES, :])

    def exponentials(n):
        t = n // n_blocks
        for h in range(H_ATT):
            j, half = divmod(h, per_tile)
            s = s_ref[n % 2, j, half * KEY_BLOCK:(half + 1) * KEY_BLOCK, :]
            m_old = m_ref[t, h]
            m_new = jnp.maximum(m_old, jnp.max(s, axis=0, keepdims=True))
            m_ref[t, h] = m_new
            alpha_ref[n % 2, h] = jnp.exp2(m_old - m_new)
            p_ref[n % 2, h] = jnp.exp2(s - m_new).astype(BF16)

    def weighted_values(n):
        t, i = divmod(n, n_blocks)
        for h in range(H_ATT):
            pv = _dot(vt_ref[kv_of(h), i], p_ref[n % 2, h])
            acc_ref[t, h] = acc_ref[t, h] * alpha_ref[n % 2, h] + pv

    def finish(t):
        for j in range(n_tiles):
            halves = []
            for half in range(per_tile):
                acc = acc_ref[t, j * per_tile + half]
                halves.append(acc[0:HEAD_DIM] * (1.0 / acc[HEAD_DIM:HEAD_DIM + 1]))
            ot = jnp.concatenate(halves, axis=0)
            o_ref[0, t * tq:(t + 1) * tq, j * LANES:(j + 1) * LANES] = ot.T.astype(BF16)

    acc_ref[...] = jnp.zeros_like(acc_ref)
    m_ref[...] = jnp.full(m_ref.shape, NEG_BIG, F32)
    n_items = n_q * n_blocks
    for it in range(n_items + 2):
        if it >= 2:
            weighted_values(it - 2)
            if (it - 1) % n_blocks == 0:
                finish((it - 2) // n_blocks)
        if it < n_items:
            scores(it)
        if 1 <= it <= n_items:
            exponentials(it - 1)


def _attention(qt, k, v, first_row, n_rows, n_keys, n_q):
    bsz = qt.shape[0]
    tq = ROW_TILE
    assert first_row % tq == 0 and n_rows % (n_q * tq) == 0 and n_keys % KEY_BLOCK == 0
    n_blocks = n_keys // KEY_BLOCK
    q_spec = lambda j: pl.BlockSpec((1, D_ATT, tq),
                                    lambda b, i: (b, 0, first_row // tq + n_q * i + j))
    kv_spec = pl.BlockSpec((1, n_keys, D_KV), lambda b, i: (b, 0, 0))
    return pl.pallas_call(
        functools.partial(_attn_kernel, n_q=n_q, n_blocks=n_blocks),
        grid=(bsz, n_rows // (n_q * tq)),
        in_specs=[q_spec(j) for j in range(n_q)] + [kv_spec, kv_spec],
        out_specs=pl.BlockSpec((1, n_q * tq, D_ATT), lambda b, i: (b, i, 0)),
        out_shape=jax.ShapeDtypeStruct((bsz, n_rows, D_ATT), BF16),
        scratch_shapes=[pltpu.VMEM((2 * H_KV, n_keys, D_KV), BF16),
                        pltpu.VMEM((H_KV, n_blocks, V_ROWS, KEY_BLOCK), BF16),
                        pltpu.VMEM((n_q, H_ATT, V_ROWS, tq), F32),
                        pltpu.VMEM((n_q, H_ATT, 1, tq), F32),
                        pltpu.VMEM((2, H_ATT, 1, tq), F32),
                        pltpu.VMEM((2, D_ATT // LANES, (LANES // HEAD_DIM) * KEY_BLOCK, tq), F32),
                        pltpu.VMEM((2, H_ATT, KEY_BLOCK, tq), BF16)],
        compiler_params=pltpu.CompilerParams(
            dimension_semantics=("arbitrary", "arbitrary"), vmem_limit_bytes=VMEM_LIMIT),
        name="attention",
    )(*([qt] * n_q), k, v)


def _log_sigmoid(x):
    return jnp.minimum(x, 0.0) - jnp.log1p(jnp.exp(-jnp.abs(x)))


def _ret_kernel(r_ref, lg_ref, gng_ref, gnb_ref, o_ref, y_ref, sf_ref, sb_ref, w_ref, dm_ref,
                *, n_ctx_chunks):
    c_len = CHUNK
    t = r_ref.shape[1]
    n_chunks = t // c_len
    lg = _log_sigmoid(lg_ref[...])
    lgf = lg[0:1, :]
    lgb = lg[1:2, :]
    dec_f = jnp.exp(c_len * lgf)
    dec_b = jnp.exp(c_len * lgb)

    @pl.when(pl.program_id(0) == 0)
    def _():
        tcol = lax.broadcasted_iota(jnp.int32, (c_len, 1), 0).astype(F32)
        w_ref[0] = jnp.exp((tcol + 1.0) * lgf)
        w_ref[1] = jnp.exp((c_len - tcol) * lgb)
        w_ref[2] = jnp.exp((c_len - 1.0 - tcol) * lgf)
        w_ref[3] = jnp.exp(tcol * lgb)
        ti = lax.broadcasted_iota(jnp.int32, (c_len, c_len), 0)
        si = lax.broadcasted_iota(jnp.int32, (c_len, c_len), 1)
        diff = (ti - si).astype(F32)
        for h in range(H_RET):
            lf = lgf[:, h * HEAD_DIM:h * HEAD_DIM + 1]
            lb = lgb[:, h * HEAD_DIM:h * HEAD_DIM + 1]
            dm_ref[h] = jnp.exp(jnp.where(diff >= 0, diff * lf, -diff * lb))

    lane = lax.broadcasted_iota(jnp.int32, (c_len, D_RET), 1)
    srow = lax.broadcasted_iota(jnp.int32, (D_RET, D_RET), 0) // HEAD_DIM
    scol = lax.broadcasted_iota(jnp.int32, (D_RET, D_RET), 1) // HEAD_DIM
    smask = srow == scol
    avg = _head_avg_matrix(D_RET)

    def chunk_rows(c):
        return pl.ds(pl.multiple_of(c * c_len, c_len), c_len)

    def state_delta(k, v, wk):
        kw = (k.astype(F32) * wk).T.astype(BF16)
        return jnp.where(smask, _dot(kw, v), 0.0)

    sf_ref[...] = jnp.zeros_like(sf_ref)
    sb_ref[...] = jnp.zeros_like(sb_ref)

    def fwd(c, carry):
        rows = chunk_rows(c)
        q = r_ref[0, rows, 0:D_RET]
        k = r_ref[0, rows, D_RET:2 * D_RET]
        v = r_ref[0, rows, 2 * D_RET:3 * D_RET]
        zero = jnp.zeros_like(q)
        y = w_ref[0] * _dot(q, sf_ref[...].astype(BF16))
        for h in range(H_RET):
            hm = (lane >= h * HEAD_DIM) & (lane < (h + 1) * HEAD_DIM)
            s = _dot_nt(jnp.where(hm, q, zero), k) * dm_ref[h]
            y = y + _dot(s.astype(BF16), jnp.where(hm, v, zero))
        y_ref[rows, :] = y
        sf_ref[...] = dec_f * sf_ref[...] + state_delta(k, v, w_ref[2])
        return carry

    lax.fori_loop(0, n_chunks, fwd, 0, unroll=3)

    def bwd(c):
        rows = chunk_rows(c)
        q = r_ref[0, rows, 0:D_RET]
        k = r_ref[0, rows, D_RET:2 * D_RET]
        v = r_ref[0, rows, 2 * D_RET:3 * D_RET]
        g = r_ref[0, rows, 3 * D_RET:4 * D_RET].astype(F32)
        y = y_ref[rows, :] + w_ref[1] * _dot(q, sb_ref[...].astype(BF16))
        sb_ref[...] = dec_b * sb_ref[...] + state_delta(k, v, w_ref[3])
        mu = _dot2_right(y, avg)
        d = y - mu
        var = _dot((d * d).astype(BF16), avg)
        yn = d * lax.rsqrt(var + EPS) * gng_ref[...] + gnb_ref[...]
        o_ref[0, rows, :] = (yn * _silu(g)).astype(BF16)

    def bwd_ctx(i, carry):
        bwd(n_ctx_chunks - 1 - i)
        return carry

    def bwd_lat(i, carry):
        bwd(n_chunks - 1 - i)
        return carry

    lax.fori_loop(0, n_ctx_chunks, bwd_ctx, 0)
    lax.fori_loop(0, n_chunks - n_ctx_chunks, bwd_lat, 0, unroll=4)


def _retention(ret, lg, gng, gnb, n_ctx):
    bsz, t, _ = ret.shape
    const = lambda shape: pl.BlockSpec(shape, lambda b: (0,) * len(shape))
    return pl.pallas_call(
        functools.partial(_ret_kernel, n_ctx_chunks=n_ctx // CHUNK),
        grid=(bsz,),
        in_specs=[
            pl.BlockSpec((1, t, 4 * D_RET), lambda b: (b, 0, 0)),
            const((2, D_RET)), const((1, D_RET)), const((1, D_RET)),
        ],
        out_specs=pl.BlockSpec((1, t, D_RET), lambda b: (b, 0, 0)),
        out_shape=jax.ShapeDtypeStruct((bsz, t, D_RET), BF16),
        scratch_shapes=[pltpu.VMEM((t, D_RET), F32), pltpu.VMEM((D_RET, D_RET), F32),
                        pltpu.VMEM((D_RET, D_RET), F32), pltpu.VMEM((4, CHUNK, D_RET), F32),
                        pltpu.VMEM((H_RET, CHUNK, CHUNK), F32)],
        compiler_params=pltpu.CompilerParams(
            dimension_semantics=("arbitrary",), vmem_limit_bytes=VMEM_LIMIT),
        name="retention",
    )(ret, lg, gng, gnb)


def _softplus(x):
    return jnp.maximum(x, 0.0) + jnp.log1p(jnp.exp(-jnp.abs(x)))


def _ssd_kernel(z_ref, xbc_ref, dt_ref, cw_ref, cb_ref, dtb_ref, alog_ref, dskip_ref, ng_ref,
                o_ref, xc_ref, y_ref, eb_ref, dsb_ref, sf_ref, sb_ref, ce_all, g_all, sc_all,
                *, n_ctx):
    c_len = CHUNK
    t = xbc_ref.shape[1]
    n_chunks = t // c_len
    n_ctx_chunks = n_ctx // c_len
    edge = 16
    half = SSD_CONV // 2

    taps = [k for k in range(-half, half + 1) if k]
    ri = lax.broadcasted_iota(jnp.int32, (c_len, c_len), 0)
    ci = lax.broadcasted_iota(jnp.int32, (c_len, c_len), 1)
    shift = {k: jnp.where(ci == ri + k, 1.0, 0.0).astype(BF16) for k in taps}
    er = lax.broadcasted_iota(jnp.int32, (edge, edge), 0)
    ec = lax.broadcasted_iota(jnp.int32, (edge, edge), 1)
    prev_rows = {k: jnp.where(ec == er + k + edge, 1.0, 0.0).astype(BF16) for k in taps if k < 0}
    next_rows = {k: jnp.where(ec == er + k - edge, 1.0, 0.0).astype(BF16) for k in taps if k > 0}
    tap_w = lambda k, lanes: cw_ref[k + half:k + half + 1, lanes]
    lane_groups = [slice(lo, min(lo + 2 * LANES, D_XBC)) for lo in range(0, D_XBC, 2 * LANES)]
    for c in range(n_chunks):
        r0 = c * c_len
        has_prev = c not in (0, n_ctx_chunks)
        has_next = c not in (n_ctx_chunks - 1, n_chunks - 1)
        for lanes in lane_groups:
            xb = xbc_ref[0, r0:r0 + c_len, lanes]
            acc = cb_ref[:, lanes] + tap_w(0, lanes) * xb.astype(F32)
            for k in taps:
                acc = acc + tap_w(k, lanes) * _dot(shift[k], xb)
            head, tail = acc[0:edge], acc[c_len - edge:c_len]
            if has_prev:
                p = xbc_ref[0, r0 - edge:r0, lanes]
                for k in prev_rows:
                    head = head + tap_w(k, lanes) * _dot(prev_rows[k], p)
            if has_next:
                nx = xbc_ref[0, r0 + c_len:r0 + c_len + edge, lanes]
                for k in next_rows:
                    tail = tail + tap_w(k, lanes) * _dot(next_rows[k], nx)
            xc_ref[r0:r0 + edge, lanes] = _silu(head).astype(BF16)
            xc_ref[r0 + edge:r0 + c_len - edge, lanes] = _silu(acc[edge:c_len - edge]).astype(BF16)
            xc_ref[r0 + c_len - edge:r0 + c_len, lanes] = _silu(tail).astype(BF16)

    nd = 2 * H_SSD
    lane_dt = lax.broadcasted_iota(jnp.int32, (1, D_DT), 1)
    live = lane_dt < nd
    a_vec = jnp.where(live, -jnp.exp(alog_ref[...]), 0.0)
    is_f = lane_dt < H_SSD
    ti = lax.broadcasted_iota(jnp.int32, (c_len, c_len), 0)
    si = lax.broadcasted_iota(jnp.int32, (c_len, c_len), 1)
    causal = si <= ti
    tri_l = jnp.where(causal, 1.0, 0.0).astype(BF16)
    tri_u = jnp.where(si >= ti, 1.0, 0.0).astype(BF16)
    n_exp = 2 * D_SSD
    er = lax.broadcasted_iota(jnp.int32, (D_DT, 2 * n_exp), 0)
    ej = lax.broadcasted_iota(jnp.int32, (D_DT, 2 * n_exp), 1)
    rel = er - jnp.where(ej >= n_exp, 3 * PART_LANES, 0)
    col = jnp.where(ej >= n_exp, ej - n_exp, ej) // HEAD_DIM
    expand = jnp.where((rel == col) | (rel == col + PART_LANES) | (rel == col + 2 * PART_LANES),
                       1.0, 0.0).astype(BF16)
    lane_w = lax.broadcasted_iota(jnp.int32, (c_len, 2 * LANES), 1)
    lane_c = lax.broadcasted_iota(jnp.int32, (c_len, D_BC), 1)
    srow = lax.broadcasted_iota(jnp.int32, (D_BC, D_SSD), 0) // SSD_STATE
    scol = lax.broadcasted_iota(jnp.int32, (D_BC, D_SSD), 1) // (D_SSD // SSD_GROUPS)
    smask = srow == scol
    heads_per_group = H_SSD // SSD_GROUPS

    def chunk_rows(c):
        return pl.ds(pl.multiple_of(c * c_len, c_len), c_len)

    sf_ref[...] = jnp.zeros_like(sf_ref)
    sb_ref[...] = jnp.zeros_like(sb_ref)

    def fwd_chunk(c, slot):
        ce_ref, g_ref, sc_ref = ce_all.at[slot], g_all.at[slot], sc_all.at[slot]
        rows = chunk_rows(c)
        xs = xc_ref[rows, 0:D_SSD]
        bm = xc_ref[rows, D_SSD:D_SSD + D_BC]
        cm = xc_ref[rows, D_SSD + D_BC:D_XBC]
        dt = _softplus(dt_ref[0, rows, :] + dtb_ref[...])
        la = _pack3(dt * a_vec, 0).astype(BF16)
        cum = jnp.where(is_f, _unpack3(_dot(tri_l, la)), _unpack3(_dot(tri_u, la)))
        cum = jnp.where(live, cum, 0.0)
        edge = jnp.where(is_f, cum[c_len - 1:c_len, :], cum[0:1, :])
        wk = jnp.where(live, jnp.exp(edge - cum) * dt, 0.0)
        packed = (_pack3(cum, 0) + _pack3(wk, 3 * PART_LANES)).astype(BF16)
        ce_ref[...] = _dot(packed, expand)
        key_t = (cum - jnp.log(dt)).T

        e_f = jnp.exp(ce_ref[:, 0:D_SSD])
        eb_ref[rows, :] = jnp.exp(ce_ref[:, D_SSD:n_exp])
        y_ref[rows, :] = e_f * _dot(cm, sf_ref[...].astype(BF16))
        zc = jnp.zeros_like(cm)
        for g in range(SSD_GROUPS):
            gm = (lane_c >= g * SSD_STATE) & (lane_c < (g + 1) * SSD_STATE)
            g_ref[...] = _dot_nt(jnp.where(gm, cm, zc), bm)
            win = slice(g * LANES, (g + 2) * LANES)
            xw = xs[:, win]
            zw = jnp.zeros_like(xw)
            yg = None
            for hh in range(heads_per_group):
                h = g * heads_per_group + hh
                d_f = cum[:, h:h + 1] - key_t[h:h + 1, :]
                d_b = cum[:, H_SSD + h:H_SSD + h + 1] - key_t[H_SSD + h:H_SSD + h + 1, :]
                sc_ref[hh] = (jnp.exp(jnp.where(causal, d_f, d_b)) * g_ref[...]).astype(BF16)
                lo = h * HEAD_DIM - g * LANES
                hm = (lane_w >= lo) & (lane_w < lo + HEAD_DIM)
                part = _dot(sc_ref[hh], jnp.where(hm, xw, zw))
                yg = part if yg is None else yg + part
            y_ref[rows, win] = y_ref[rows, win] + yg

        xs_f = xs.astype(F32)
        bm_t = bm.astype(F32).T.astype(BF16)
        ds_f = _dot(bm_t, (xs_f * ce_ref[:, n_exp:n_exp + D_SSD]).astype(BF16))
        ds_b = _dot(bm_t, (xs_f * ce_ref[:, n_exp + D_SSD:2 * n_exp]).astype(BF16))
        sf_ref[...] = e_f[c_len - 1:c_len, :] * sf_ref[...] + jnp.where(smask, ds_f, 0.0)
        dsb_ref[c] = jnp.where(smask, ds_b, 0.0)

    unroll = ce_all.shape[0]

    def fwd(i, carry):
        for k in range(unroll):
            fwd_chunk(i * unroll + k, k)
        return carry

    lax.fori_loop(0, n_chunks // unroll, fwd, 0)

    def bwd(c):
        rows = chunk_rows(c)
        xs_f = xc_ref[rows, 0:D_SSD].astype(F32)
        cm = xc_ref[rows, D_SSD + D_BC:D_XBC]
        e_b = eb_ref[rows, :]
        y = y_ref[rows, :] + e_b * _dot(cm, sb_ref[...].astype(BF16))
        sb_ref[...] = e_b[0:1, :] * sb_ref[...] + dsb_ref[c]
        y = y + dskip_ref[...] * xs_f
        u = y * _silu(z_ref[0, rows, :].astype(F32))
        o_ref[0, rows, :] = _rms(u, ng_ref[...]).astype(BF16)

    def bwd_ctx(i, carry):
        bwd(n_ctx_chunks - 1 - i)
        return carry

    def bwd_lat(i, carry):
        bwd(n_chunks - 1 - i)
        return carry

    lax.fori_loop(0, n_ctx_chunks, bwd_ctx, 0)
    lax.fori_loop(0, n_chunks - n_ctx_chunks, bwd_lat, 0, unroll=2)


def _ssd(z, xbc, dt, conv_w, conv_b, dt_bias, a_log, d_skip, norm_g, n_ctx):
    bsz, t, _ = xbc.shape
    n_chunks = t // CHUNK
    unroll = next(u for u in (3, 2, 1) if n_chunks % u == 0)
    const = lambda shape: pl.BlockSpec(shape, lambda b: (0,) * len(shape))
    seq = lambda w: pl.BlockSpec((1, t, w), lambda b: (b, 0, 0))
    return pl.pallas_call(
        functools.partial(_ssd_kernel, n_ctx=n_ctx),
        grid=(bsz,),
        in_specs=[seq(D_SSD), seq(D_XBC), seq(D_DT),
                  const((8, D_XBC)), const((1, D_XBC)), const((1, D_DT)), const((1, D_DT)),
                  const((1, D_SSD)), const((1, D_SSD))],
        out_specs=seq(D_SSD),
        out_shape=jax.ShapeDtypeStruct((bsz, t, D_SSD), BF16),
        scratch_shapes=[
            pltpu.VMEM((t, D_XBC), BF16),
            pltpu.VMEM((t, D_SSD), F32),
            pltpu.VMEM((t, D_SSD), F32),
            pltpu.VMEM((n_chunks, D_BC, D_SSD), F32),
            pltpu.VMEM((D_BC, D_SSD), F32),
            pltpu.VMEM((D_BC, D_SSD), F32),
            pltpu.VMEM((unroll, CHUNK, 4 * D_SSD), F32),
            pltpu.VMEM((unroll, CHUNK, CHUNK), F32),
            pltpu.VMEM((unroll, H_SSD // SSD_GROUPS, CHUNK, CHUNK), BF16),
        ],
        compiler_params=pltpu.CompilerParams(
            dimension_semantics=("arbitrary",), vmem_limit_bytes=VMEM_LIMIT),
        name="ssd",
    )(z, xbc, dt, conv_w, conv_b, dt_bias, a_log, d_skip, norm_g)


def _post_kernel(*refs, n_tiles, ff_chunk, mod_row):
    att_refs, ret_refs, ssd_refs, x_refs = (refs[k * n_tiles:(k + 1) * n_tiles] for k in range(4))
    mod_ref, ng_ref, wo_ref, w1_ref, w2_ref, o_ref = refs[4 * n_tiles:]
    row = pl.program_id(0) if mod_row is None else mod_row
    mod = functools.partial(_mod_chunk, mod_ref, row)
    rows = x_refs[0].shape[1]
    n_chunk = w1_ref.shape[1] // ff_chunk

    def project(t):
        mix = jnp.concatenate([att_refs[t][0], ret_refs[t][0], ssd_refs[t][0]], axis=-1)
        return _dot(mix, wo_ref[...].astype(BF16))

    def mlp_input(t, o):
        x1 = x_refs[t][0] + mod(2) * _rms(o, ng_ref[1:2, :])
        h2 = _rms(x1, ng_ref[2:3, :]) * (1.0 + mod(4)) + mod(3)
        return x1, h2.astype(BF16)

    def mlp_chunk(h2, acc, j):
        sl = slice(j * ff_chunk, (j + 1) * ff_chunk)
        a = jnp.maximum(_dot(h2, w1_ref[:, sl].astype(BF16)), 0.0)
        part = _dot((a * a).astype(BF16), w2_ref[sl, :].astype(BF16))
        return part if acc is None else acc + part

    def finish(t, x1, acc):
        o_ref[0, t * rows:(t + 1) * rows, :] = x1 + mod(5) * _rms(acc, ng_ref[3:4, :])

    proj = [project(t) for t in range(n_tiles)]
    ready = mlp_input(0, proj[0])
    pending = None
    for t in range(n_tiles):
        x1, h2 = ready
        acc = mlp_chunk(h2, None, 0)
        if t + 1 < n_tiles:
            ready = mlp_input(t + 1, proj[t + 1])
        if pending is not None:
            finish(*pending)
        for j in range(1, n_chunk):
            acc = mlp_chunk(h2, acc, j)
        pending = (t, x1, acc)
    finish(*pending)


def _post(att, ret, ssd, x, ret_off, mods, layer, ng, w_out, w_ff1, w_ff2, mod_row, n_tiles):
    bsz, seg, d = x.shape
    assert seg % (n_tiles * ROW_TILE) == 0 and att.shape[1] == seg
    tile = lambda w, off, j: pl.BlockSpec((1, ROW_TILE, w),
                                          lambda b, i: (b, off + n_tiles * i + j, 0))
    per_tile = lambda w, off: [tile(w, off, j) for j in range(n_tiles)]
    const = lambda shape: pl.BlockSpec(shape, lambda b, i: (0,) * len(shape),
                                       pipeline_mode=pl.Buffered(1))
    slab = lambda a, **kw: pl.BlockSpec((None,) + a.shape[1:], lambda b, i: (layer, 0, 0), **kw)
    return pl.pallas_call(
        functools.partial(_post_kernel, n_tiles=n_tiles, ff_chunk=1024, mod_row=mod_row),
        grid=(bsz, seg // (n_tiles * ROW_TILE)),
        in_specs=[*per_tile(D_ATT, 0), *per_tile(D_RET, ret_off), *per_tile(D_SSD, ret_off),
                  *per_tile(d, 0),
                  slab(mods), const((4, d)),
                  slab(w_out, pipeline_mode=pl.Buffered(1)),
                  slab(w_ff1, pipeline_mode=pl.Buffered(1)),
                  slab(w_ff2, pipeline_mode=pl.Buffered(1))],
        out_specs=pl.BlockSpec((1, n_tiles * ROW_TILE, d), lambda b, i: (b, i, 0)),
        out_shape=jax.ShapeDtypeStruct((bsz, seg, d), F32),
        compiler_params=pltpu.CompilerParams(
            dimension_semantics=("arbitrary", "arbitrary"), vmem_limit_bytes=VMEM_LIMIT),
        name="post",
    )(*([att] * n_tiles), *([ret] * n_tiles), *([ssd] * n_tiles), *([x] * n_tiles),
      mods, ng, w_out, w_ff1, w_ff2)


def _rope_tables(n, n_ctx):
    rows = n // GRID_W
    row = jnp.broadcast_to(jnp.arange(rows)[:, None], (rows, GRID_W)).reshape(n)
    col = jnp.broadcast_to(jnp.arange(GRID_W)[None, :], (rows, GRID_W)).reshape(n)
    half = HEAD_DIM // 2
    inv_freq = ROPE_THETA ** (-jnp.arange(0, half, 2, dtype=F32) / half)
    ang = jnp.stack([row, col], axis=-1).astype(F32)[:, :, None] * inv_freq
    ang = jnp.concatenate([ang, ang], axis=-1).reshape(n, HEAD_DIM)
    ang = jnp.tile(ang, (1, LANES // HEAD_DIM))
    cos = jnp.cos(ang)
    sin = jnp.sin(ang)
    upper = (jnp.arange(LANES) % half) >= half // 2
    sin_p = jnp.where(upper, sin, 0.0)
    sin_m = jnp.where(upper, 0.0, -sin)
    ident = lambda v, a: jnp.concatenate([jnp.full((n_ctx, LANES), v, F32), a], axis=0)
    return ident(1.0, cos), ident(0.0, sin_p), ident(0.0, sin_m)


def kernel(x, c, ctx, c_ctx, w_mod, b_mod, norm_g, w_in, w_out, q_norm_g, k_norm_g, ret_decay_logit,
           ret_gn_g, ret_gn_b, ssd_conv_w, ssd_conv_b, ssd_dt_bias, ssd_a_log, ssd_d, ssd_norm_g,
           w_ff1, w_ff2):
    bsz, n, d = x.shape
    n_ctx = ctx.shape[1]
    depth = w_mod.shape[0]
    assert n % ROW_TILE == 0 and n_ctx % ROW_TILE == 0 and n_ctx % CHUNK == 0 and n % CHUNK == 0
    assert w_in.shape[2] == OFF_DT + 2 * H_SSD
    n_ctx_tiles = n_ctx // ROW_TILE

    n_rows = -(-(bsz + 1) // 8) * 8
    cc = jnp.concatenate([c, c_ctx[None, :], jnp.zeros((n_rows - bsz - 1, d), F32)], axis=0)
    mods = _modulation(cc, w_mod, b_mod)

    cos, sin_p, sin_m = _rope_tables(n, n_ctx)
    n_dt = 2 * H_SSD
    pad_dt = lambda a: jnp.pad(a.reshape(1, n_dt), ((0, 0), (0, D_DT - n_dt)))

    x_ctx, x_lat = ctx, x
    pair = 2 if n % (2 * ROW_TILE) == 0 else 1
    wi = jnp.pad(w_in, ((0, 0), (0, 0), (0, D_IN_PAD - w_in.shape[2]))).astype(BF16)
    for layer in range(depth):
        last = layer == depth - 1
        gqk = jnp.concatenate([jnp.tile(q_norm_g[layer], H_ATT) * (HEAD_DIM ** -0.5 * LOG2_E),
                               jnp.tile(k_norm_g[layer], H_KV)])[None, :]

        q, k, v, ret, z, xbc, dt = _inproj(x_ctx, x_lat, n_ctx + n, mods, layer,
                                           norm_g[layer], wi, cos, sin_p, sin_m, gqk, n_ctx_tiles)
        att_lat = _attention(q, k, v, n_ctx, n, n_ctx + n, n_q=2 * pair if n % (4 * ROW_TILE) == 0 else pair)
        lg = jnp.repeat(ret_decay_logit[layer], HEAD_DIM, axis=1)
        ret_o = _retention(ret, lg, ret_gn_g[layer][None, :], ret_gn_b[layer][None, :], n_ctx)
        conv_w = jnp.pad(ssd_conv_w[layer], ((0, 8 - SSD_CONV), (0, 0)))
        ssd_o = _ssd(z, xbc, dt, conv_w, ssd_conv_b[layer][None, :], pad_dt(ssd_dt_bias[layer]),
                     pad_dt(ssd_a_log[layer]), jnp.repeat(ssd_d[layer], HEAD_DIM)[None, :],
                     ssd_norm_g[layer][None, :], n_ctx)
        post = functools.partial(_post, mods=mods, layer=layer, ng=norm_g[layer], w_out=w_out,
                                 w_ff1=w_ff1, w_ff2=w_ff2)
        new_lat = post(att_lat, ret_o, ssd_o, x_lat, n_ctx_tiles, mod_row=None, n_tiles=pair)
        if not last:
            att_ctx = _attention(q, k, v, 0, n_ctx, n_ctx, n_q=1)
            x_ctx = post(att_ctx, ret_o, ssd_o, x_ctx, 0, mod_row=bsz, n_tiles=1)
        x_lat = new_lat
    return x_lat
```

```python
import functools

import jax
import jax.numpy as jnp
from jax import lax
from jax.experimental import pallas as pl
from jax.experimental.pallas import tpu as pltpu

F32 = jnp.float32
BF16 = jnp.bfloat16

HEAD_DIM = 64
H_ATT = 6
H_KV = 2
H_RET = 4
H_SSD = 6
SSD_GROUPS = 2
SSD_STATE = 128
SSD_CONV = 5
GRID_W = 64
ROPE_THETA = 10000.0
EPS = 1e-6
LOG2_E = 1.4426950408889634

D_ATT = H_ATT * HEAD_DIM
D_KV = H_KV * HEAD_DIM
D_RET = H_RET * HEAD_DIM
D_SSD = H_SSD * HEAD_DIM
D_BC = SSD_GROUPS * SSD_STATE
D_XBC = D_SSD + 2 * D_BC
LANES = 128
D_DT = LANES
OFF_Q = 0
OFF_K = OFF_Q + D_ATT
OFF_V = OFF_K + D_KV
OFF_RET = OFF_V + D_KV
OFF_Z = OFF_RET + 4 * D_RET
OFF_XBC = OFF_Z + D_SSD
OFF_DT = OFF_XBC + D_XBC
D_IN_PAD = OFF_DT + D_DT

ROW_TILE = 256
CHUNK = 256
VMEM_LIMIT = 56 * 1024 * 1024


def _silu(x):
    return x * jax.nn.sigmoid(x)


def _split2(a):
    hi = a.astype(BF16)
    lo = (a - hi.astype(F32)).astype(BF16)
    return hi, lo


def _dot(a, b):
    return jnp.dot(a, b, preferred_element_type=F32)


def _dot_nt(a, b):
    return lax.dot_general(a, b, (((1,), (1,)), ((), ())), preferred_element_type=F32)


def _dot2_right(a, m):
    hi, lo = _split2(a)
    return _dot(hi, m) + _dot(lo, m)


PART_LANES = 16


def _pack3(a, base):
    hi = a.astype(BF16).astype(F32)
    r = a - hi
    mid = r.astype(BF16).astype(F32)
    out = hi + pltpu.roll(mid, PART_LANES, 1) + pltpu.roll(r - mid, 2 * PART_LANES, 1)
    return pltpu.roll(out, base, 1) if base else out


def _unpack3(p):
    return p + pltpu.roll(p, LANES - PART_LANES, 1) + pltpu.roll(p, LANES - 2 * PART_LANES, 1)


def _rms(x, g):
    ms = jnp.mean(x * x, axis=-1, keepdims=True)
    return x * lax.rsqrt(ms + EPS) * g


def _head_avg_matrix(n):
    r = lax.broadcasted_iota(jnp.int32, (n, n), 0) // HEAD_DIM
    c = lax.broadcasted_iota(jnp.int32, (n, n), 1) // HEAD_DIM
    return jnp.where(r == c, 1.0 / HEAD_DIM, 0.0).astype(BF16)


def _mod_kernel(c_ref, w_ref, b_ref, o_ref):
    sc = _silu(c_ref[...]).astype(BF16)
    o_ref[0] = _dot(sc, w_ref[0].astype(BF16)) + b_ref[0]


def _modulation(cc, w_mod, b_mod):
    depth, d, n = w_mod.shape
    rows = cc.shape[0]
    tn = n // 4
    return pl.pallas_call(
        _mod_kernel,
        grid=(depth, n // tn),
        in_specs=[
            pl.BlockSpec((rows, d), lambda l, j: (0, 0)),
            pl.BlockSpec((1, d, tn), lambda l, j: (l, 0, j)),
            pl.BlockSpec((1, 1, tn), lambda l, j: (l, 0, j)),
        ],
        out_specs=pl.BlockSpec((1, rows, tn), lambda l, j: (l, 0, j)),
        out_shape=jax.ShapeDtypeStruct((depth, rows, n), F32),
        compiler_params=pltpu.CompilerParams(
            dimension_semantics=("arbitrary", "arbitrary"), vmem_limit_bytes=VMEM_LIMIT),
        name="modulation",
    )(cc, w_mod, b_mod.reshape(depth, 1, n))


def _mod_chunk(mod_ref, row, k):
    d = mod_ref.shape[1] // 6
    return mod_ref[pl.ds(row, 1), k * d:(k + 1) * d]


ROT_HALF = HEAD_DIM // 4


def _rope(x, cos, sin_p, sin_m):
    return (x * cos + pltpu.roll(x, ROT_HALF, 1) * sin_p
            + pltpu.roll(x, LANES - ROT_HALF, 1) * sin_m)


def _inproj_kernel(xc_ref, xl_ref, mod_ref, ng_ref, w_ref, cos_ref, sp_ref, sm_ref, gqk_ref,
                   qt_ref, k_ref, v_ref, ret_ref, z_ref, xbc_ref, dt_ref, *, n_ctx_tiles, ctx_row):
    is_ctx = pl.program_id(1) < n_ctx_tiles
    x = jnp.where(is_ctx, xc_ref[0], xl_ref[0])
    mod = functools.partial(_mod_chunk, mod_ref, jnp.where(is_ctx, ctx_row, pl.program_id(0)))
    h = _rms(x, ng_ref[0:1, :])
    h = h * (1.0 + mod(1)) + mod(0)
    hb = h.astype(BF16)
    rope = lambda tile: _rope(tile, cos_ref[...], sp_ref[...], sm_ref[...])
    kscale = HEAD_DIM ** -0.5

    def emit(col, tile):
        sl = lambda off: slice(col - off, col - off + LANES)
        if col < OFF_K:
            qt_ref[0, sl(OFF_Q), :] = rope(tile).T.astype(BF16)
        elif col < OFF_V:
            k_ref[0, :, sl(OFF_K)] = rope(tile).astype(BF16)
        elif col < OFF_RET:
            v_ref[0, :, sl(OFF_V)] = tile.astype(BF16)
        elif col < OFF_RET + D_RET:
            ret_ref[0, :, sl(OFF_RET)] = rope(tile).astype(BF16)
        elif col < OFF_RET + 2 * D_RET:
            ret_ref[0, :, sl(OFF_RET)] = (rope(tile) * kscale).astype(BF16)
        elif col < OFF_Z:
            ret_ref[0, :, sl(OFF_RET)] = tile.astype(BF16)
        elif col < OFF_XBC:
            z_ref[0, :, sl(OFF_Z)] = tile.astype(BF16)
        elif col < OFF_DT:
            xbc_ref[0, :, sl(OFF_XBC)] = tile.astype(BF16)
        else:
            dt_ref[0, :, sl(OFF_DT)] = tile

    group = OFF_V
    for g in range(D_IN_PAD // group):
        acc = _dot(hb, w_ref[:, g * group:(g + 1) * group])
        if g == 0:
            ms = _dot((acc * acc).astype(BF16), _head_avg_matrix(group))
            acc = acc * lax.rsqrt(ms + EPS) * gqk_ref[...]
        for j in range(group // LANES):
            emit(g * group + j * LANES, acc[:, j * LANES:(j + 1) * LANES])


def _token_specs(d, n_ctx_tiles):
    ctx_spec = pl.BlockSpec((1, ROW_TILE, d),
                            lambda b, i: (b, jnp.minimum(i, n_ctx_tiles - 1), 0))
    lat_spec = pl.BlockSpec((1, ROW_TILE, d),
                            lambda b, i: (b, jnp.maximum(i, n_ctx_tiles) - n_ctx_tiles, 0))
    return ctx_spec, lat_spec


def _inproj(x_ctx, x_lat, t, mods, layer, ng, w_in, cos, sin_p, sin_m, gqk, n_ctx_tiles):
    bsz, _, d = x_lat.shape
    nt = t // ROW_TILE
    mod_spec = pl.BlockSpec((None,) + mods.shape[1:], lambda b, i: (layer, 0, 0))

    tok = lambda w: pl.BlockSpec((1, ROW_TILE, w), lambda b, i: (b, i, 0))
    rope_spec = pl.BlockSpec((ROW_TILE, LANES), lambda b, i: (i, 0))
    const = lambda shape: pl.BlockSpec(shape, lambda b, i: (0,) * len(shape))
    widths = (D_KV, D_KV, 4 * D_RET, D_SSD, D_XBC)
    qt_spec = pl.BlockSpec((1, D_ATT, ROW_TILE), lambda b, i: (b, 0, i))
    return pl.pallas_call(
        functools.partial(_inproj_kernel, n_ctx_tiles=n_ctx_tiles, ctx_row=bsz),
        grid=(bsz, nt),
        in_specs=[
            *_token_specs(d, n_ctx_tiles),
            mod_spec,
            const((4, d)),
            pl.BlockSpec((None, d, D_IN_PAD), lambda b, i: (layer, 0, 0),
                         pipeline_mode=pl.Buffered(1)),
            rope_spec, rope_spec, rope_spec,
            const((1, D_ATT + D_KV)),
        ],
        out_specs=[qt_spec] + [tok(w) for w in widths] + [tok(D_DT)],
        out_shape=[jax.ShapeDtypeStruct((bsz, D_ATT, t), BF16)]
        + [jax.ShapeDtypeStruct((bsz, t, w), BF16) for w in widths]
        + [jax.ShapeDtypeStruct((bsz, t, D_DT), F32)],
        compiler_params=pltpu.CompilerParams(
            dimension_semantics=("arbitrary", "arbitrary"), vmem_limit_bytes=VMEM_LIMIT),
        name="inproj",
    )(x_ctx, x_lat, mods, ng, w_in, cos, sin_p, sin_m, gqk)


V_ROWS = 80
KEY_BLOCK = 256
NEG_BIG = -1e30


def _attn_kernel(*refs, n_q, n_blocks):
    q_refs = refs[:n_q]
    k_ref, v_ref, o_ref, km_ref, vt_ref, acc_ref, m_ref, alpha_ref, s_ref, p_ref = refs[n_q:]
    per_tile = LANES // HEAD_DIM
    n_tiles = D_ATT // LANES
    tq = q_refs[0].shape[2]

    @pl.when(pl.program_id(1) == 0)
    def _():
        k = k_ref[0].astype(F32)
        kr = pltpu.roll(k, HEAD_DIM, 1)
        low = lax.broadcasted_iota(jnp.int32, k.shape, 1) < HEAD_DIM
        km_ref[0] = jnp.where(low, k, 0.0).astype(BF16)
        km_ref[1] = jnp.where(low, 0.0, kr).astype(BF16)
        km_ref[2] = jnp.where(low, kr, 0.0).astype(BF16)
        km_ref[3] = jnp.where(low, 0.0, k).astype(BF16)
        vt = v_ref[0].astype(F32).T
        row = lax.broadcasted_iota(jnp.int32, (V_ROWS - HEAD_DIM, vt.shape[1]), 0)
        tail = jnp.where(row == 0, 1.0, 0.0)
        for g in range(H_KV):
            vg = jnp.concatenate([vt[g * HEAD_DIM:(g + 1) * HEAD_DIM], tail], axis=0).astype(BF16)
            for blk in range(n_blocks):
                vt_ref[g, blk] = vg[:, blk * KEY_BLOCK:(blk + 1) * KEY_BLOCK]

    kv_of = lambda h: h // (H_ATT // H_KV)

    def scores(n):
        t, i = divmod(n, n_blocks)
        for j in range(n_tiles):
            ks = jnp.concatenate(
                [km_ref[2 * kv_of(j * per_tile + half) + half, i * KEY_BLOCK:(i + 1) * KEY_BLOCK, :]
                 for half in range(per_tile)], axis=0)
            s_ref[n % 2, j] = _dot(ks, q_refs[t][0, j * LANES:(j + 1) * LANES, :])

    def exponentials(n):
        t = n // n_blocks
        for h in range(H_ATT):
            j, half = divmod(h, per_tile)
            s = s_ref[n % 2, j, half * KEY_BLOCK:(half + 1) * KEY_BLOCK, :]
            m_old = m_ref[t, h]
            m_new = jnp.maximum(m_old, jnp.max(s, axis=0, keepdims=True))
            m_ref[t, h] = m_new
            alpha_ref[n % 2, h] = jnp.exp2(m_old - m_new)
            p_ref[n % 2, h] = jnp.exp2(s - m_new).astype(BF16)

    def weighted_values(n):
        t, i = divmod(n, n_blocks)
        for h in range(H_ATT):
            pv = _dot(vt_ref[kv_of(h), i], p_ref[n % 2, h])
            acc_ref[t, h] = acc_ref[t, h] * alpha_ref[n % 2, h] + pv

    def finish(t):
        for j in range(n_tiles):
            halves = []
            for half in range(per_tile):
                acc = acc_ref[t, j * per_tile + half]
                halves.append(acc[0:HEAD_DIM] * (1.0 / acc[HEAD_DIM:HEAD_DIM + 1]))
            ot = jnp.concatenate(halves, axis=0)
            o_ref[0, t * tq:(t + 1) * tq, j * LANES:(j + 1) * LANES] = ot.T.astype(BF16)

    acc_ref[...] = jnp.zeros_like(acc_ref)
    m_ref[...] = jnp.full(m_ref.shape, NEG_BIG, F32)
    n_items = n_q * n_blocks
    for it in range(n_items + 2):
        if it >= 2:
            weighted_values(it - 2)
            if (it - 1) % n_blocks == 0:
                finish((it - 2) // n_blocks)
        if it < n_items:
            scores(it)
        if 1 <= it <= n_items:
            exponentials(it - 1)


def _attention(qt, k, v, first_row, n_rows, n_keys, n_q):
    bsz = qt.shape[0]
    tq = ROW_TILE
    assert first_row % tq == 0 and n_rows % (n_q * tq) == 0 and n_keys % KEY_BLOCK == 0
    n_blocks = n_keys // KEY_BLOCK
    q_spec = lambda j: pl.BlockSpec((1, D_ATT, tq),
                                    lambda b, i: (b, 0, first_row // tq + n_q * i + j))
    kv_spec = pl.BlockSpec((1, n_keys, D_KV), lambda b, i: (b, 0, 0))
    return pl.pallas_call(
        functools.partial(_attn_kernel, n_q=n_q, n_blocks=n_blocks),
        grid=(bsz, n_rows // (n_q * tq)),
        in_specs=[q_spec(j) for j in range(n_q)] + [kv_spec, kv_spec],
        out_specs=pl.BlockSpec((1, n_q * tq, D_ATT), lambda b, i: (b, i, 0)),
        out_shape=jax.ShapeDtypeStruct((bsz, n_rows, D_ATT), BF16),
        scratch_shapes=[pltpu.VMEM((2 * H_KV, n_keys, D_KV), BF16),
                        pltpu.VMEM((H_KV, n_blocks, V_ROWS, KEY_BLOCK), BF16),
                        pltpu.VMEM((n_q, H_ATT, V_ROWS, tq), F32),
                        pltpu.VMEM((n_q, H_ATT, 1, tq), F32),
                        pltpu.VMEM((2, H_ATT, 1, tq), F32),
                        pltpu.VMEM((2, D_ATT // LANES, (LANES // HEAD_DIM) * KEY_BLOCK, tq), F32),
                        pltpu.VMEM((2, H_ATT, KEY_BLOCK, tq), BF16)],
        compiler_params=pltpu.CompilerParams(
            dimension_semantics=("arbitrary", "arbitrary"), vmem_limit_bytes=VMEM_LIMIT),
        name="attention",
    )(*([qt] * n_q), k, v)


def _log_sigmoid(x):
    return jnp.minimum(x, 0.0) - jnp.log1p(jnp.exp(-jnp.abs(x)))


def _ret_kernel(r_ref, lg_ref, gng_ref, gnb_ref, o_ref, y_ref, sf_ref, sb_ref, w_ref, dm_ref,
                *, n_ctx_chunks):
    c_len = CHUNK
    t = r_ref.shape[1]
    n_chunks = t // c_len
    lg = _log_sigmoid(lg_ref[...])
    lgf = lg[0:1, :]
    lgb = lg[1:2, :]
    dec_f = jnp.exp(c_len * lgf)
    dec_b = jnp.exp(c_len * lgb)

    @pl.when(pl.program_id(0) == 0)
    def _():
        tcol = lax.broadcasted_iota(jnp.int32, (c_len, 1), 0).astype(F32)
        w_ref[0] = jnp.exp((tcol + 1.0) * lgf)
        w_ref[1] = jnp.exp((c_len - tcol) * lgb)
        w_ref[2] = jnp.exp((c_len - 1.0 - tcol) * lgf)
        w_ref[3] = jnp.exp(tcol * lgb)
        ti = lax.broadcasted_iota(jnp.int32, (c_len, c_len), 0)
        si = lax.broadcasted_iota(jnp.int32, (c_len, c_len), 1)
        diff = (ti - si).astype(F32)
        for h in range(H_RET):
            lf = lgf[:, h * HEAD_DIM:h * HEAD_DIM + 1]
            lb = lgb[:, h * HEAD_DIM:h * HEAD_DIM + 1]
            dm_ref[h] = jnp.exp(jnp.where(diff >= 0, diff * lf, -diff * lb))

    lane = lax.broadcasted_iota(jnp.int32, (c_len, D_RET), 1)
    srow = lax.broadcasted_iota(jnp.int32, (D_RET, D_RET), 0) // HEAD_DIM
    scol = lax.broadcasted_iota(jnp.int32, (D_RET, D_RET), 1) // HEAD_DIM
    smask = srow == scol
    avg = _head_avg_matrix(D_RET)

    def chunk_rows(c):
        return pl.ds(pl.multiple_of(c * c_len, c_len), c_len)

    def state_delta(k, v, wk):
        kw = (k.astype(F32) * wk).T.astype(BF16)
        return jnp.where(smask, _dot(kw, v), 0.0)

    sf_ref[...] = jnp.zeros_like(sf_ref)
    sb_ref[...] = jnp.zeros_like(sb_ref)

    def fwd(c, carry):
        rows = chunk_rows(c)
        q = r_ref[0, rows, 0:D_RET]
        k = r_ref[0, rows, D_RET:2 * D_RET]
        v = r_ref[0, rows, 2 * D_RET:3 * D_RET]
        zero = jnp.zeros_like(q)
        y = w_ref[0] * _dot(q, sf_ref[...].astype(BF16))
        for h in range(H_RET):
            hm = (lane >= h * HEAD_DIM) & (lane < (h + 1) * HEAD_DIM)
            s = _dot_nt(jnp.where(hm, q, zero), k) * dm_ref[h]
            y = y + _dot(s.astype(BF16), jnp.where(hm, v, zero))
        y_ref[rows, :] = y
        sf_ref[...] = dec_f * sf_ref[...] + state_delta(k, v, w_ref[2])
        return carry

    lax.fori_loop(0, n_chunks, fwd, 0, unroll=3)

    def bwd(c):
        rows = chunk_rows(c)
        q = r_ref[0, rows, 0:D_RET]
        k = r_ref[0, rows, D_RET:2 * D_RET]
        v = r_ref[0, rows, 2 * D_RET:3 * D_RET]
        g = r_ref[0, rows, 3 * D_RET:4 * D_RET].astype(F32)
        y = y_ref[rows, :] + w_ref[1] * _dot(q, sb_ref[...].astype(BF16))
        sb_ref[...] = dec_b * sb_ref[...] + state_delta(k, v, w_ref[3])
        mu = _dot2_right(y, avg)
        d = y - mu
        var = _dot((d * d).astype(BF16), avg)
        yn = d * lax.rsqrt(var + EPS) * gng_ref[...] + gnb_ref[...]
        o_ref[0, rows, :] = (yn * _silu(g)).astype(BF16)

    def bwd_ctx(i, carry):
        bwd(n_ctx_chunks - 1 - i)
        return carry

    def bwd_lat(i, carry):
        bwd(n_chunks - 1 - i)
        return carry

    lax.fori_loop(0, n_ctx_chunks, bwd_ctx, 0)
    lax.fori_loop(0, n_chunks - n_ctx_chunks, bwd_lat, 0, unroll=4)


def _retention(ret, lg, gng, gnb, n_ctx):
    bsz, t, _ = ret.shape
    const = lambda shape: pl.BlockSpec(shape, lambda b: (0,) * len(shape))
    return pl.pallas_call(
        functools.partial(_ret_kernel, n_ctx_chunks=n_ctx // CHUNK),
        grid=(bsz,),
        in_specs=[
            pl.BlockSpec((1, t, 4 * D_RET), lambda b: (b, 0, 0)),
            const((2, D_RET)), const((1, D_RET)), const((1, D_RET)),
        ],
        out_specs=pl.BlockSpec((1, t, D_RET), lambda b: (b, 0, 0)),
        out_shape=jax.ShapeDtypeStruct((bsz, t, D_RET), BF16),
        scratch_shapes=[pltpu.VMEM((t, D_RET), F32), pltpu.VMEM((D_RET, D_RET), F32),
                        pltpu.VMEM((D_RET, D_RET), F32), pltpu.VMEM((4, CHUNK, D_RET), F32),
                        pltpu.VMEM((H_RET, CHUNK, CHUNK), F32)],
        compiler_params=pltpu.CompilerParams(
            dimension_semantics=("arbitrary",), vmem_limit_bytes=VMEM_LIMIT),
        name="retention",
    )(ret, lg, gng, gnb)


def _softplus(x):
    return jnp.maximum(x, 0.0) + jnp.log1p(jnp.exp(-jnp.abs(x)))


def _ssd_kernel(z_ref, xbc_ref, dt_ref, cw_ref, cb_ref, dtb_ref, alog_ref, dskip_ref, ng_ref,
                o_ref, xc_ref, y_ref, eb_ref, dsb_ref, sf_ref, sb_ref, ce_all, g_all, sc_all,
                *, n_ctx):
    c_len = CHUNK
    t = xbc_ref.shape[1]
    n_chunks = t // c_len
    n_ctx_chunks = n_ctx // c_len
    edge = 16
    half = SSD_CONV // 2

    taps = [k for k in range(-half, half + 1) if k]
    ri = lax.broadcasted_iota(jnp.int32, (c_len, c_len), 0)
    ci = lax.broadcasted_iota(jnp.int32, (c_len, c_len), 1)
    shift = {k: jnp.where(ci == ri + k, 1.0, 0.0).astype(BF16) for k in taps}
    er = lax.broadcasted_iota(jnp.int32, (edge, edge), 0)
    ec = lax.broadcasted_iota(jnp.int32, (edge, edge), 1)
    prev_rows = {k: jnp.where(ec == er + k + edge, 1.0, 0.0).astype(BF16) for k in taps if k < 0}
    next_rows = {k: jnp.where(ec == er + k - edge, 1.0, 0.0).astype(BF16) for k in taps if k > 0}
    tap_w = lambda k, lanes: cw_ref[k + half:k + half + 1, lanes]
    lane_groups = [slice(lo, min(lo + 2 * LANES, D_XBC)) for lo in range(0, D_XBC, 2 * LANES)]
    for c in range(n_chunks):
        r0 = c * c_len
        has_prev = c not in (0, n_ctx_chunks)
        has_next = c not in (n_ctx_chunks - 1, n_chunks - 1)
        for lanes in lane_groups:
            xb = xbc_ref[0, r0:r0 + c_len, lanes]
            acc = cb_ref[:, lanes] + tap_w(0, lanes) * xb.astype(F32)
            for k in taps:
                acc = acc + tap_w(k, lanes) * _dot(shift[k], xb)
            head, tail = acc[0:edge], acc[c_len - edge:c_len]
            if has_prev:
                p = xbc_ref[0, r0 - edge:r0, lanes]
                for k in prev_rows:
                    head = head + tap_w(k, lanes) * _dot(prev_rows[k], p)
            if has_next:
                nx = xbc_ref[0, r0 + c_len:r0 + c_len + edge, lanes]
                for k in next_rows:
                    tail = tail + tap_w(k, lanes) * _dot(next_rows[k], nx)
            xc_ref[r0:r0 + edge, lanes] = _silu(head).astype(BF16)
            xc_ref[r0 + edge:r0 + c_len - edge, lanes] = _silu(acc[edge:c_len - edge]).astype(BF16)
            xc_ref[r0 + c_len - edge:r0 + c_len, lanes] = _silu(tail).astype(BF16)

    nd = 2 * H_SSD
    lane_dt = lax.broadcasted_iota(jnp.int32, (1, D_DT), 1)
    live = lane_dt < nd
    a_vec = jnp.where(live, -jnp.exp(alog_ref[...]), 0.0)
    is_f = lane_dt < H_SSD
    ti = lax.broadcasted_iota(jnp.int32, (c_len, c_len), 0)
    si = lax.broadcasted_iota(jnp.int32, (c_len, c_len), 1)
    causal = si <= ti
    tri_l = jnp.where(causal, 1.0, 0.0).astype(BF16)
    tri_u = jnp.where(si >= ti, 1.0, 0.0).astype(BF16)
    n_exp = 2 * D_SSD
    er = lax.broadcasted_iota(jnp.int32, (D_DT, 2 * n_exp), 0)
    ej = lax.broadcasted_iota(jnp.int32, (D_DT, 2 * n_exp), 1)
    rel = er - jnp.where(ej >= n_exp, 3 * PART_LANES, 0)
    col = jnp.where(ej >= n_exp, ej - n_exp, ej) // HEAD_DIM
    expand = jnp.where((rel == col) | (rel == col + PART_LANES) | (rel == col + 2 * PART_LANES),
                       1.0, 0.0).astype(BF16)
    lane_w = lax.broadcasted_iota(jnp.int32, (c_len, 2 * LANES), 1)
    lane_c = lax.broadcasted_iota(jnp.int32, (c_len, D_BC), 1)
    srow = lax.broadcasted_iota(jnp.int32, (D_BC, D_SSD), 0) // SSD_STATE
    scol = lax.broadcasted_iota(jnp.int32, (D_BC, D_SSD), 1) // (D_SSD // SSD_GROUPS)
    smask = srow == scol
    heads_per_group = H_SSD // SSD_GROUPS

    def chunk_rows(c):
        return pl.ds(pl.multiple_of(c * c_len, c_len), c_len)

    sf_ref[...] = jnp.zeros_like(sf_ref)
    sb_ref[...] = jnp.zeros_like(sb_ref)

    def fwd_chunk(c, slot):
        ce_ref, g_ref, sc_ref = ce_all.at[slot], g_all.at[slot], sc_all.at[slot]
        rows = chunk_rows(c)
        xs = xc_ref[rows, 0:D_SSD]
        bm = xc_ref[rows, D_SSD:D_SSD + D_BC]
        cm = xc_ref[rows, D_SSD + D_BC:D_XBC]
        dt = _softplus(dt_ref[0, rows, :] + dtb_ref[...])
        la = _pack3(dt * a_vec, 0).astype(BF16)
        cum = jnp.where(is_f, _unpack3(_dot(tri_l, la)), _unpack3(_dot(tri_u, la)))
        cum = jnp.where(live, cum, 0.0)
        edge = jnp.where(is_f, cum[c_len - 1:c_len, :], cum[0:1, :])
        wk = jnp.where(live, jnp.exp(edge - cum) * dt, 0.0)
        packed = (_pack3(cum, 0) + _pack3(wk, 3 * PART_LANES)).astype(BF16)
        ce_ref[...] = _dot(packed, expand)
        key_t = (cum - jnp.log(dt)).T

        e_f = jnp.exp(ce_ref[:, 0:D_SSD])
        eb_ref[rows, :] = jnp.exp(ce_ref[:, D_SSD:n_exp])
        y_ref[rows, :] = e_f * _dot(cm, sf_ref[...].astype(BF16))
        zc = jnp.zeros_like(cm)
        for g in range(SSD_GROUPS):
            gm = (lane_c >= g * SSD_STATE) & (lane_c < (g + 1) * SSD_STATE)
            g_ref[...] = _dot_nt(jnp.where(gm, cm, zc), bm)
            win = slice(g * LANES, (g + 2) * LANES)
            xw = xs[:, win]
            zw = jnp.zeros_like(xw)
            yg = None
            for hh in range(heads_per_group):
                h = g * heads_per_group + hh
                for r in (slice(0, c_len // 2), slice(c_len // 2, c_len)):
                    d_f = cum[r, h:h + 1] - key_t[h:h + 1, :]
                    d_b = cum[r, H_SSD + h:H_SSD + h + 1] - key_t[H_SSD + h:H_SSD + h + 1, :]
                    decay = jnp.exp(jnp.where(causal[r], d_f, d_b))
                    sc_ref[hh, r, :] = (decay * g_ref[r, :]).astype(BF16)
                lo = h * HEAD_DIM - g * LANES
                hm = (lane_w >= lo) & (lane_w < lo + HEAD_DIM)
                part = _dot(sc_ref[hh], jnp.where(hm, xw, zw))
                yg = part if yg is None else yg + part
            y_ref[rows, win] = y_ref[rows, win] + yg

        xs_f = xs.astype(F32)
        bm_t = bm.astype(F32).T.astype(BF16)
        ds_f = _dot(bm_t, (xs_f * ce_ref[:, n_exp:n_exp + D_SSD]).astype(BF16))
        ds_b = _dot(bm_t, (xs_f * ce_ref[:, n_exp + D_SSD:2 * n_exp]).astype(BF16))
        sf_ref[...] = e_f[c_len - 1:c_len, :] * sf_ref[...] + jnp.where(smask, ds_f, 0.0)
        dsb_ref[c] = jnp.where(smask, ds_b, 0.0)

    unroll = ce_all.shape[0]

    def fwd(i, carry):
        for k in range(unroll):
            fwd_chunk(i * unroll + k, k)
        return carry

    lax.fori_loop(0, n_chunks // unroll, fwd, 0)

    def bwd(c):
        rows = chunk_rows(c)
        xs_f = xc_ref[rows, 0:D_SSD].astype(F32)
        cm = xc_ref[rows, D_SSD + D_BC:D_XBC]
        e_b = eb_ref[rows, :]
        y = y_ref[rows, :] + e_b * _dot(cm, sb_ref[...].astype(BF16))
        sb_ref[...] = e_b[0:1, :] * sb_ref[...] + dsb_ref[c]
        y = y + dskip_ref[...] * xs_f
        u = y * _silu(z_ref[0, rows, :].astype(F32))
        o_ref[0, rows, :] = _rms(u, ng_ref[...]).astype(BF16)

    def bwd_ctx(i, carry):
        bwd(n_ctx_chunks - 1 - i)
        return carry

    def bwd_lat(i, carry):
        bwd(n_chunks - 1 - i)
        return carry

    lax.fori_loop(0, n_ctx_chunks, bwd_ctx, 0)
    lax.fori_loop(0, n_chunks - n_ctx_chunks, bwd_lat, 0, unroll=2)


def _ssd(z, xbc, dt, conv_w, conv_b, dt_bias, a_log, d_skip, norm_g, n_ctx):
    bsz, t, _ = xbc.shape
    n_chunks = t // CHUNK
    unroll = next(u for u in (3, 2, 1) if n_chunks % u == 0)
    const = lambda shape: pl.BlockSpec(shape, lambda b: (0,) * len(shape))
    seq = lambda w: pl.BlockSpec((1, t, w), lambda b: (b, 0, 0))
    return pl.pallas_call(
        functools.partial(_ssd_kernel, n_ctx=n_ctx),
        grid=(bsz,),
        in_specs=[seq(D_SSD), seq(D_XBC), seq(D_DT),
                  const((8, D_XBC)), const((1, D_XBC)), const((1, D_DT)), const((1, D_DT)),
                  const((1, D_SSD)), const((1, D_SSD))],
        out_specs=seq(D_SSD),
        out_shape=jax.ShapeDtypeStruct((bsz, t, D_SSD), BF16),
        scratch_shapes=[
            pltpu.VMEM((t, D_XBC), BF16),
            pltpu.VMEM((t, D_SSD), F32),
            pltpu.VMEM((t, D_SSD), F32),
            pltpu.VMEM((n_chunks, D_BC, D_SSD), F32),
            pltpu.VMEM((D_BC, D_SSD), F32),
            pltpu.VMEM((D_BC, D_SSD), F32),
            pltpu.VMEM((unroll, CHUNK, 4 * D_SSD), F32),
            pltpu.VMEM((unroll, CHUNK, CHUNK), F32),
            pltpu.VMEM((unroll, H_SSD // SSD_GROUPS, CHUNK, CHUNK), BF16),
        ],
        compiler_params=pltpu.CompilerParams(
            dimension_semantics=("arbitrary",), vmem_limit_bytes=VMEM_LIMIT),
        name="ssd",
    )(z, xbc, dt, conv_w, conv_b, dt_bias, a_log, d_skip, norm_g)


def _post_kernel(*refs, n_tiles, ff_chunk, mod_row):
    att_refs, ret_refs, ssd_refs, x_refs = (refs[k * n_tiles:(k + 1) * n_tiles] for k in range(4))
    mod_ref, ng_ref, wo_ref, w1_ref, w2_ref, o_ref = refs[4 * n_tiles:]
    row = pl.program_id(0) if mod_row is None else mod_row
    mod = functools.partial(_mod_chunk, mod_ref, row)
    rows = x_refs[0].shape[1]
    n_chunk = w1_ref.shape[1] // ff_chunk

    def project(t):
        mix = jnp.concatenate([att_refs[t][0], ret_refs[t][0], ssd_refs[t][0]], axis=-1)
        return _dot(mix, wo_ref[...].astype(BF16))

    def mlp_input(t, o):
        x1 = x_refs[t][0] + mod(2) * _rms(o, ng_ref[1:2, :])
        h2 = _rms(x1, ng_ref[2:3, :]) * (1.0 + mod(4)) + mod(3)
        return x1, h2.astype(BF16)

    def mlp_chunk(h2, acc, j):
        sl = slice(j * ff_chunk, (j + 1) * ff_chunk)
        a = jnp.maximum(_dot(h2, w1_ref[:, sl].astype(BF16)), 0.0)
        part = _dot((a * a).astype(BF16), w2_ref[sl, :].astype(BF16))
        return part if acc is None else acc + part

    def finish(t, x1, acc):
        o_ref[0, t * rows:(t + 1) * rows, :] = x1 + mod(5) * _rms(acc, ng_ref[3:4, :])

    proj = [project(t) for t in range(n_tiles)]
    ready = mlp_input(0, proj[0])
    pending = None
    for t in range(n_tiles):
        x1, h2 = ready
        acc = mlp_chunk(h2, None, 0)
        if t + 1 < n_tiles:
            ready = mlp_input(t + 1, proj[t + 1])
        if pending is not None:
            finish(*pending)
        for j in range(1, n_chunk):
            acc = mlp_chunk(h2, acc, j)
        pending = (t, x1, acc)
    finish(*pending)


def _post(att, ret, ssd, x, ret_off, mods, layer, ng, w_out, w_ff1, w_ff2, mod_row, n_tiles):
    bsz, seg, d = x.shape
    assert seg % (n_tiles * ROW_TILE) == 0 and att.shape[1] == seg
    tile = lambda w, off, j: pl.BlockSpec((1, ROW_TILE, w),
                                          lambda b, i: (b, off + n_tiles * i + j, 0))
    per_tile = lambda w, off: [tile(w, off, j) for j in range(n_tiles)]
    const = lambda shape: pl.BlockSpec(shape, lambda b, i: (0,) * len(shape),
                                       pipeline_mode=pl.Buffered(1))
    slab = lambda a, **kw: pl.BlockSpec((None,) + a.shape[1:], lambda b, i: (layer, 0, 0), **kw)
    return pl.pallas_call(
        functools.partial(_post_kernel, n_tiles=n_tiles, ff_chunk=1024, mod_row=mod_row),
        grid=(bsz, seg // (n_tiles * ROW_TILE)),
        in_specs=[*per_tile(D_ATT, 0), *per_tile(D_RET, ret_off), *per_tile(D_SSD, ret_off),
                  *per_tile(d, 0),
                  slab(mods), const((4, d)),
                  slab(w_out, pipeline_mode=pl.Buffered(1)),
                  slab(w_ff1, pipeline_mode=pl.Buffered(1)),
                  slab(w_ff2, pipeline_mode=pl.Buffered(1))],
        out_specs=pl.BlockSpec((1, n_tiles * ROW_TILE, d), lambda b, i: (b, i, 0)),
        out_shape=jax.ShapeDtypeStruct((bsz, seg, d), F32),
        compiler_params=pltpu.CompilerParams(
            dimension_semantics=("arbitrary", "arbitrary"), vmem_limit_bytes=VMEM_LIMIT),
        name="post",
    )(*([att] * n_tiles), *([ret] * n_tiles), *([ssd] * n_tiles), *([x] * n_tiles),
      mods, ng, w_out, w_ff1, w_ff2)


def _rope_tables(n, n_ctx):
    rows = n // GRID_W
    row = jnp.broadcast_to(jnp.arange(rows)[:, None], (rows, GRID_W)).reshape(n)
    col = jnp.broadcast_to(jnp.arange(GRID_W)[None, :], (rows, GRID_W)).reshape(n)
    half = HEAD_DIM // 2
    inv_freq = ROPE_THETA ** (-jnp.arange(0, half, 2, dtype=F32) / half)
    ang = jnp.stack([row, col], axis=-1).astype(F32)[:, :, None] * inv_freq
    ang = jnp.concatenate([ang, ang], axis=-1).reshape(n, HEAD_DIM)
    ang = jnp.tile(ang, (1, LANES // HEAD_DIM))
    cos = jnp.cos(ang)
    sin = jnp.sin(ang)
    upper = (jnp.arange(LANES) % half) >= half // 2
    sin_p = jnp.where(upper, sin, 0.0)
    sin_m = jnp.where(upper, 0.0, -sin)
    ident = lambda v, a: jnp.concatenate([jnp.full((n_ctx, LANES), v, F32), a], axis=0)
    return ident(1.0, cos), ident(0.0, sin_p), ident(0.0, sin_m)


def kernel(x, c, ctx, c_ctx, w_mod, b_mod, norm_g, w_in, w_out, q_norm_g, k_norm_g, ret_decay_logit,
           ret_gn_g, ret_gn_b, ssd_conv_w, ssd_conv_b, ssd_dt_bias, ssd_a_log, ssd_d, ssd_norm_g,
           w_ff1, w_ff2):
    bsz, n, d = x.shape
    n_ctx = ctx.shape[1]
    depth = w_mod.shape[0]
    assert n % ROW_TILE == 0 and n_ctx % ROW_TILE == 0 and n_ctx % CHUNK == 0 and n % CHUNK == 0
    assert w_in.shape[2] == OFF_DT + 2 * H_SSD
    n_ctx_tiles = n_ctx // ROW_TILE

    n_rows = -(-(bsz + 1) // 8) * 8
    cc = jnp.concatenate([c, c_ctx[None, :], jnp.zeros((n_rows - bsz - 1, d), F32)], axis=0)
    mods = _modulation(cc, w_mod, b_mod)

    cos, sin_p, sin_m = _rope_tables(n, n_ctx)
    n_dt = 2 * H_SSD
    pad_dt = lambda a: jnp.pad(a.reshape(1, n_dt), ((0, 0), (0, D_DT - n_dt)))

    x_ctx, x_lat = ctx, x
    pair = 2 if n % (2 * ROW_TILE) == 0 else 1
    wi = jnp.pad(w_in, ((0, 0), (0, 0), (0, D_IN_PAD - w_in.shape[2]))).astype(BF16)
    for layer in range(depth):
        last = layer == depth - 1
        gqk = jnp.concatenate([jnp.tile(q_norm_g[layer], H_ATT) * (HEAD_DIM ** -0.5 * LOG2_E),
                               jnp.tile(k_norm_g[layer], H_KV)])[None, :]

        q, k, v, ret, z, xbc, dt = _inproj(x_ctx, x_lat, n_ctx + n, mods, layer,
                                           norm_g[layer], wi, cos, sin_p, sin_m, gqk, n_ctx_tiles)
        att_lat = _attention(q, k, v, n_ctx, n, n_ctx + n, n_q=pair)
        lg = jnp.repeat(ret_decay_logit[layer], HEAD_DIM, axis=1)
        ret_o = _retention(ret, lg, ret_gn_g[layer][None, :], ret_gn_b[layer][None, :], n_ctx)
        conv_w = jnp.pad(ssd_conv_w[layer], ((0, 8 - SSD_CONV), (0, 0)))
        ssd_o = _ssd(z, xbc, dt, conv_w, ssd_conv_b[layer][None, :], pad_dt(ssd_dt_bias[layer]),
                     pad_dt(ssd_a_log[layer]), jnp.repeat(ssd_d[layer], HEAD_DIM)[None, :],
                     ssd_norm_g[layer][None, :], n_ctx)
        post = functools.partial(_post, mods=mods, layer=layer, ng=norm_g[layer], w_out=w_out,
                                 w_ff1=w_ff1, w_ff2=w_ff2)
        new_lat = post(att_lat, ret_o, ssd_o, x_lat, n_ctx_tiles, mod_row=None, n_tiles=pair)
        if not last:
            att_ctx = _attention(q, k, v, 0, n_ctx, n_ctx, n_q=1)
            x_ctx = post(att_ctx, ret_o, ssd_o, x_ctx, 0, mod_row=bsz, n_tiles=1)
        x_lat = new_lat
    return x_lat
```
